```python
import jax
import jax.numpy as jnp
from jax import lax
import numpy as np

D_MODEL = 1024
BATCH = 16
SEQ = 4096
DEPTH = 4

GRID_W = 64
CTX_LEN = 256
EPS = 1e-6

DN_HEADS = 4
DN_DK = 128
DN_DV = 128
DN_CONV = 4
DN_CHUNK = 64
HG_HEADS = 4
HG_DK = 128
HG_DV = 128
HG_CHUNK = 64
LRU_HEADS = 4
LRU_WIDTH = 512
LRU_CONV = 4
LRU_C = 8.0
ATT_Q_HEADS = 8
ATT_KV_HEADS = 2
ATT_HEAD_DIM = 64
ATT_BLOCK = 128
ROPE_THETA = 10000.0

N_BRANCH = 4
BRANCH_WIDTH = 512
D_FF = 4 * D_MODEL

IN_SPLITS = (
    DN_HEADS * DN_DK, DN_HEADS * DN_DK, DN_HEADS * DN_DV, DN_HEADS * DN_DV, 2 * DN_HEADS, 2 * DN_HEADS,
    HG_HEADS * HG_DK, HG_HEADS * HG_DK, HG_HEADS * HG_DK, HG_HEADS * HG_DV, HG_HEADS * HG_DV,
    LRU_WIDTH, LRU_WIDTH,
    ATT_Q_HEADS * ATT_HEAD_DIM, ATT_KV_HEADS * ATT_HEAD_DIM, ATT_KV_HEADS * ATT_HEAD_DIM,
    N_BRANCH * D_MODEL,
)
IN_WIDTH = sum(IN_SPLITS)

kernel_name = 'hybrid_dit_deltanet_hgrn2_rglru_gqa'


def _rms(x, w):
    xf = x.astype(jnp.float32)
    y = xf * lax.rsqrt(jnp.mean(xf * xf, axis=-1, keepdims=True) + EPS)
    return (y * w.astype(jnp.float32)).astype(x.dtype)


def _l2n(x):
    return x * lax.rsqrt(jnp.sum(x * x, axis=-1, keepdims=True) + EPS)


def _modulate(h, shift, scale):
    return h * (1.0 + scale) + shift


def _split_cols(u):
    idx = [int(i) for i in np.cumsum(IN_SPLITS)[:-1]]
    return jnp.split(u, idx, axis=-1)


def _dwconv(x, w):
    k = w.shape[0]
    t = x.shape[1]
    xp = jnp.pad(x, ((0, 0), ((k - 1) // 2, k // 2), (0, 0)))
    return sum(xp[:, j:j + t] * w[j] for j in range(k))


def _tflip(a, rev):
    return jnp.flip(a, axis=1) if rev else a


def _to_chunks(a, cs):
    bsz, t, h = a.shape[:3]
    a = a.reshape((bsz, t // cs, cs, h) + a.shape[3:])
    return jnp.moveaxis(a, (1, 3), (0, 2))


def _from_chunks(o):
    n, bsz, h, cs = o.shape[:4]
    return jnp.moveaxis(o, (0, 2), (1, 3)).reshape((bsz, n * cs, h) + o.shape[4:])


def _bidirectional(run, ctx_dirs, lat_dirs, s0):
    o_ctx, o_lat = [], []
    for d in range(2):
        rev = d == 1
        oc, s_ctx = run(tuple(_tflip(a, rev) for a in ctx_dirs[d]), s0)
        ol, _ = run(tuple(_tflip(a, rev) for a in lat_dirs[d]), s_ctx)
        o_ctx.append(_tflip(oc, rev))
        o_lat.append(_tflip(ol, rev))
    return o_ctx[0] + o_ctx[1], o_lat[0] + o_lat[1]


def _gdn_chunked(q, k, v, g, beta, s0):
    dv = v.shape[-1]
    q, k, v, g, beta = (_to_chunks(a, DN_CHUNK) for a in (q, k, v, g, beta))
    gam = jnp.cumsum(g, axis=-1)
    diff = gam[..., :, None] - gam[..., None, :]
    pos = jnp.arange(DN_CHUNK)
    incl = pos[:, None] >= pos[None, :]
    strict = pos[:, None] > pos[None, :]
    kb = k * beta[..., None]
    a_low = jnp.einsum('nbhid,nbhjd->nbhij', kb, k) * jnp.exp(jnp.where(strict, diff, -jnp.inf))
    eye = jnp.eye(DN_CHUNK, dtype=a_low.dtype)
    rhs = jnp.concatenate([v * beta[..., None], kb * jnp.exp(gam)[..., None]], axis=-1)
    sol = lax.linalg.triangular_solve(a_low + eye, rhs, left_side=True, lower=True, unit_diagonal=True)
    u, w = sol[..., :dv], sol[..., dv:]
    a_qk = jnp.einsum('nbhid,nbhjd->nbhij', q, k) * jnp.exp(jnp.where(incl, diff, -jnp.inf))
    q_dec = q * jnp.exp(gam)[..., None]
    k_dec = k * jnp.exp(gam[..., -1:] - gam)[..., None]
    c_dec = jnp.exp(gam[..., -1])[..., None, None]

    def step(s, xs):
        u_i, w_i, aqk_i, qd_i, kd_i, cd_i = xs
        v_new = u_i - jnp.einsum('bhck,bhkv->bhcv', w_i, s)
        o_i = jnp.einsum('bhck,bhkv->bhcv', qd_i, s) + jnp.einsum('bhcj,bhjv->bhcv', aqk_i, v_new)
        s = s * cd_i + jnp.einsum('bhck,bhcv->bhkv', kd_i, v_new)
        return s, o_i

    s_fin, o = lax.scan(step, s0, (u, w, a_qk, q_dec, k_dec, c_dec))
    return _from_chunks(o), s_fin


def _dn_prep(parts, conv_w, a_log, dt_bias):
    q, k, v, _, beta_raw, alpha_raw = parts
    bsz, t, _ = q.shape
    qkv = jax.nn.silu(_dwconv(jnp.concatenate([q, k, v], axis=-1), conv_w)).astype(jnp.float32)
    wk = DN_HEADS * DN_DK
    q = _l2n(qkv[..., :wk].reshape(bsz, t, DN_HEADS, DN_DK)) * (DN_DK ** -0.5)
    k = _l2n(qkv[..., wk:2 * wk].reshape(bsz, t, DN_HEADS, DN_DK))
    v = qkv[..., 2 * wk:].reshape(bsz, t, DN_HEADS, DN_DV)
    beta = jax.nn.sigmoid(beta_raw.astype(jnp.float32)).reshape(bsz, t, 2, DN_HEADS)
    g = -jnp.exp(a_log.astype(jnp.float32)) * jax.nn.softplus(
        alpha_raw.astype(jnp.float32).reshape(bsz, t, 2, DN_HEADS) + dt_bias.astype(jnp.float32))
    return [(q, k, v, g[:, :, d], beta[:, :, d]) for d in range(2)]


def _gated_out(o, gate, norm_w):
    bsz, t, h, dv = o.shape
    y = _rms(o, norm_w) * jax.nn.silu(gate.astype(jnp.float32)).reshape(bsz, t, h, dv)
    return y.reshape(bsz, t, h * dv).astype(gate.dtype)


def _deltanet_mixer(pc, pl, conv_w, a_log, dt_bias, norm_w, need_ctx):
    s0 = jnp.zeros((pc[0].shape[0], DN_HEADS, DN_DK, DN_DV), jnp.float32)
    oc, ol = _bidirectional(lambda xs, s: _gdn_chunked(*xs, s),
                            _dn_prep(pc, conv_w, a_log, dt_bias), _dn_prep(pl, conv_w, a_log, dt_bias), s0)
    y_ctx = _gated_out(oc, pc[3], norm_w) if need_ctx else None
    return y_ctx, _gated_out(ol, pl[3], norm_w)


def _hgrn2_chunked(q, k, v, logf, s0):
    qs, ks, vs = _to_chunks(q, HG_CHUNK), _to_chunks(k, HG_CHUNK), _to_chunks(v, HG_CHUNK)
    gs = jnp.cumsum(_to_chunks(logf, HG_CHUNK), axis=3)
    pos = jnp.arange(HG_CHUNK)
    incl = (pos[:, None] >= pos[None, :])[:, :, None]

    def step(s, xs):
        qc, kc, vc, gc = xs
        diff = gc[:, :, :, None, :] - gc[:, :, None, :, :]
        dec = jnp.exp(jnp.where(incl, diff, -jnp.inf))
        scores = jnp.einsum('bhtd,bhsd,bhtsd->bhts', qc, kc, dec)
        o = jnp.einsum('bhtd,bhde->bhte', qc * jnp.exp(gc), s) + jnp.einsum('bhts,bhse->bhte', scores, vc)
        g_last = gc[:, :, -1:, :]
        s = s * jnp.exp(g_last[:, :, 0, :])[..., None] + jnp.einsum(
            'bhsd,bhse->bhde', kc * jnp.exp(g_last - gc), vc)
        return s, o

    s_fin, o = lax.scan(step, s0, (qs, ks, vs, gs))
    return _from_chunks(o), s_fin


def _hgrn2_lower_bounds(p):
    sm = jax.nn.softmax(p.astype(jnp.float32), axis=1)
    cs = jnp.cumsum(sm, axis=1)
    return cs - cs[:, :1]


def _hgrn2_prep(parts, lb):
    q, f_fwd, f_bwd, i, _ = parts
    bsz, t, _ = q.shape
    shp = (bsz, t, HG_HEADS, HG_DK)
    q = jax.nn.silu(q.astype(jnp.float32)).reshape(shp)
    v = i.astype(jnp.float32).reshape(bsz, t, HG_HEADS, HG_DV)
    dirs = []
    for d, fr in enumerate((f_fwd, f_bwd)):
        f = lb[d] + (1.0 - lb[d]) * jax.nn.sigmoid(fr.astype(jnp.float32))
        dirs.append((q, (1.0 - f).reshape(shp), v, jnp.log(f).reshape(shp)))
    return dirs


def _hgrn2_mixer(pc, pl, lb, norm_w, need_ctx):
    s0 = jnp.zeros((pc[0].shape[0], HG_HEADS, HG_DK, HG_DV), jnp.float32)
    oc, ol = _bidirectional(lambda xs, s: _hgrn2_chunked(*xs, s), _hgrn2_prep(pc, lb), _hgrn2_prep(pl, lb), s0)
    y_ctx = _gated_out(oc, pc[4], norm_w) if need_ctx else None
    return y_ctx, _gated_out(ol, pl[4], norm_w)


def _linear_scan(xs, h0):
    a, b = xs

    def comb(l, r):
        return l[0] * r[0], r[0] * l[1] + r[1]

    a_cum, b_cum = lax.associative_scan(comb, (a, b), axis=1)
    h = a_cum * h0[:, None, :] + b_cum
    return h, h[:, -1]


def _rglru_prep(parts, conv_w, conv_b, w_a, b_a, w_x, b_x, lam):
    xb = parts[0]
    bsz, t, _ = xb.shape
    xc = (_dwconv(xb, conv_w) + conv_b).astype(jnp.float32)
    xh = xc.reshape(bsz, t, LRU_HEADS, LRU_WIDTH // LRU_HEADS)
    dirs = []
    for d in range(2):
        r = jax.nn.sigmoid(jnp.einsum('bthi,hij->bthj', xh, w_a[d].astype(jnp.float32)).reshape(bsz, t, LRU_WIDTH) + b_a[d])
        ig = jax.nn.sigmoid(jnp.einsum('bthi,hij->bthj', xh, w_x[d].astype(jnp.float32)).reshape(bsz, t, LRU_WIDTH) + b_x[d])
        log_a = -LRU_C * r * jax.nn.softplus(-lam[d].astype(jnp.float32))
        dirs.append((jnp.exp(log_a), jnp.sqrt(-jnp.expm1(2.0 * log_a)) * ig * xc))
    return dirs


def _rglru_mixer(pc, pl, conv_w, conv_b, w_a, b_a, w_x, b_x, lam, need_ctx):
    h0 = jnp.zeros((pc[0].shape[0], LRU_WIDTH), jnp.float32)
    prm = (conv_w, conv_b, w_a, b_a, w_x, b_x, lam)
    hc, hl = _bidirectional(_linear_scan, _rglru_prep(pc, *prm), _rglru_prep(pl, *prm), h0)

    def out(h, gb):
        return (h * jax.nn.gelu(gb.astype(jnp.float32))).astype(gb.dtype)

    y_ctx = out(hc, pc[1]) if need_ctx else None
    return y_ctx, out(hl, pl[1])


def _rope_2d_tables(rows):
    row_id = jnp.repeat(jnp.arange(rows), GRID_W).astype(jnp.float32)
    col_id = jnp.tile(jnp.arange(GRID_W), rows).astype(jnp.float32)
    axis_dim = ATT_HEAD_DIM // 2
    inv = ROPE_THETA ** (-jnp.arange(0, axis_dim, 2, dtype=jnp.float32) / axis_dim)
    ang = jnp.stack([row_id[:, None] * inv, col_id[:, None] * inv], axis=1)
    return jnp.cos(ang), jnp.sin(ang)


def _rope_2d(x, cos, sin):
    bsz, t, h, dh = x.shape
    xr = x.astype(jnp.float32).reshape(bsz, t, h, 2, 2, dh // 4)
    x1, x2 = xr[..., 0, :], xr[..., 1, :]
    c, s = cos[None, :, None], sin[None, :, None]
    out = jnp.stack([x1 * c - x2 * s, x2 * c + x1 * s], axis=-2)
    return out.reshape(bsz, t, h, dh).astype(x.dtype)


def _block_attention(q, keys, vals):
    bsz, t, hq, dh = q.shape
    grp = hq // ATT_KV_HEADS
    qb = jnp.moveaxis(q.reshape(bsz, t // ATT_BLOCK, ATT_BLOCK, ATT_KV_HEADS, grp, dh), 1, 0)
    scale = dh ** -0.5

    def one(qblk):
        s = jnp.einsum('bqkgd,bskd->bkgqs', qblk, keys).astype(jnp.float32) * scale
        p = jax.nn.softmax(s, axis=-1).astype(vals.dtype)
        return jnp.einsum('bkgqs,bskd->bqkgd', p, vals)

    o = lax.map(one, qb)
    return jnp.moveaxis(o, 0, 1).reshape(bsz, t, hq * dh)


def _gqa_mixer(pc, pl, qn_w, kn_w, cos, sin, need_ctx):
    def heads(parts):
        q, k, v = parts
        bsz, t, _ = q.shape
        q = _rms(q.reshape(bsz, t, ATT_Q_HEADS, ATT_HEAD_DIM), qn_w)
        k = _rms(k.reshape(bsz, t, ATT_KV_HEADS, ATT_HEAD_DIM), kn_w)
        return q, k, v.reshape(bsz, t, ATT_KV_HEADS, ATT_HEAD_DIM)

    cq, ck, cv = heads(pc)
    lq, lk, lv = heads(pl)
    lq, lk = _rope_2d(lq, cos, sin), _rope_2d(lk, cos, sin)
    y_lat = _block_attention(lq, jnp.concatenate([ck, lk], axis=1), jnp.concatenate([cv, lv], axis=1))
    y_ctx = _block_attention(cq, ck, cv) if need_ctx else None
    return y_ctx, y_lat


def _residual_update(h, ys, gate_raw, mods, norm2_w, w_branch, w_out, w1, w2):
    bsz, t, _ = gate_raw.shape
    g = gate_raw.reshape(bsz, t, N_BRANCH, D_MODEL)
    merged = sum(jax.nn.sigmoid(g[:, :, b]) * (ys[b] @ w_branch[b]) for b in range(N_BRANCH))
    h = h + mods[2] * (merged @ w_out)
    z = _modulate(_rms(h, norm2_w), mods[3], mods[4])
    return h + mods[5] * (jnp.square(jax.nn.relu(z @ w1)) @ w2)


def setup_inputs(seed: int = 0) -> dict:
    key = jax.random.key(seed)
    k = jax.random.split(key, 32)
    f32 = jnp.float32

    def nrm(i, shape, scale):
        return jax.random.normal(k[i], shape, f32) * scale

    hd = LRU_WIDTH // LRU_HEADS
    dt = jnp.exp(jax.random.uniform(k[10], (DEPTH, 2, DN_HEADS), f32, np.log(1e-3), np.log(1e-1)))
    a_c = jax.random.uniform(k[20], (DEPTH, 2, LRU_WIDTH), f32, 0.9, 0.999)
    s_l = a_c ** (1.0 / LRU_C)
    return {
        'x': nrm(0, (BATCH, SEQ, D_MODEL), 1.0),
        'c': nrm(1, (BATCH, D_MODEL), 1.0),
        'ctx': nrm(2, (BATCH, CTX_LEN, D_MODEL), 1.0),
        'c_ctx': nrm(3, (D_MODEL,), 1.0),
        'mod_w': nrm(4, (DEPTH, D_MODEL, 6 * D_MODEL), 0.5 * D_MODEL ** -0.5),
        'mod_b': nrm(5, (DEPTH, 6 * D_MODEL), 0.01),
        'norm1_w': 1.0 + nrm(6, (DEPTH, D_MODEL), 0.02),
        'norm2_w': 1.0 + nrm(7, (DEPTH, D_MODEL), 0.02),
        'w_in': nrm(8, (DEPTH, D_MODEL, IN_WIDTH), D_MODEL ** -0.5),
        'dn_conv_w': nrm(9, (DEPTH, DN_CONV, DN_HEADS * (2 * DN_DK + DN_DV)), DN_CONV ** -0.5),
        'dn_a_log': jnp.log(jax.random.uniform(k[11], (DEPTH, 2, DN_HEADS), f32, 1.0, 16.0)),
        'dn_dt_bias': dt + jnp.log(-jnp.expm1(-dt)),
        'dn_norm_w': 1.0 + nrm(12, (DEPTH, DN_DV), 0.02),
        'hg_lower_bounds': 1.0 + nrm(13, (2, DEPTH, HG_HEADS * HG_DK), 0.1),
        'hg_norm_w': 1.0 + nrm(14, (DEPTH, HG_DV), 0.02),
        'lru_conv_w': nrm(15, (DEPTH, LRU_CONV, LRU_WIDTH), LRU_CONV ** -0.5),
        'lru_conv_b': nrm(16, (DEPTH, LRU_WIDTH), 0.01),
        'lru_w_a': nrm(17, (DEPTH, 2, LRU_HEADS, hd, hd), hd ** -0.5),
        'lru_b_a': nrm(18, (DEPTH, 2, LRU_WIDTH), 0.01),
        'lru_w_x': nrm(19, (DEPTH, 2, LRU_HEADS, hd, hd), hd ** -0.5),
        'lru_b_x': nrm(21, (DEPTH, 2, LRU_WIDTH), 0.01),
        'lru_lambda': jnp.log(s_l) - jnp.log1p(-s_l),
        'att_q_norm_w': 1.0 + nrm(22, (DEPTH, ATT_HEAD_DIM), 0.02),
        'att_k_norm_w': 1.0 + nrm(23, (DEPTH, ATT_HEAD_DIM), 0.02),
        'w_branch': nrm(24, (DEPTH, N_BRANCH, BRANCH_WIDTH, D_MODEL), BRANCH_WIDTH ** -0.5),
        'w_out': nrm(25, (DEPTH, D_MODEL, D_MODEL), D_MODEL ** -0.5),
        'mlp_w1': nrm(26, (DEPTH, D_MODEL, D_FF), D_MODEL ** -0.5),
        'mlp_w2': nrm(27, (DEPTH, D_FF, D_MODEL), D_FF ** -0.5),
    }


def reference(x, c, ctx, c_ctx, mod_w, mod_b, norm1_w, norm2_w, w_in, dn_conv_w, dn_a_log, dn_dt_bias,
              dn_norm_w, hg_lower_bounds, hg_norm_w, lru_conv_w, lru_conv_b, lru_w_a, lru_b_a, lru_w_x,
              lru_b_x, lru_lambda, att_q_norm_w, att_k_norm_w, w_branch, w_out, mlp_w1, mlp_w2):
    rows = x.shape[1] // GRID_W
    cos, sin = _rope_2d_tables(rows)
    lb_all = _hgrn2_lower_bounds(hg_lower_bounds)
    silu_c = jax.nn.silu(c)
    silu_cc = jax.nn.silu(c_ctx)
    h_lat, h_ctx = x, ctx
    for layer in range(DEPTH):
        upd = layer < DEPTH - 1
        mod_lat = jnp.split((silu_c @ mod_w[layer] + mod_b[layer])[:, None, :], 6, axis=-1)
        mod_ctx = jnp.split(silu_cc @ mod_w[layer] + mod_b[layer], 6, axis=-1)
        pl = _split_cols(_modulate(_rms(h_lat, norm1_w[layer]), mod_lat[0], mod_lat[1]) @ w_in[layer])
        pc = _split_cols(_modulate(_rms(h_ctx, norm1_w[layer]), mod_ctx[0], mod_ctx[1]) @ w_in[layer])
        ya = _deltanet_mixer(pc[0:6], pl[0:6], dn_conv_w[layer], dn_a_log[layer], dn_dt_bias[layer],
                             dn_norm_w[layer], upd)
        yb = _hgrn2_mixer(pc[6:11], pl[6:11], lb_all[:, layer], hg_norm_w[layer], upd)
        yc = _rglru_mixer(pc[11:13], pl[11:13], lru_conv_w[layer], lru_conv_b[layer], lru_w_a[layer],
                          lru_b_a[layer], lru_w_x[layer], lru_b_x[layer], lru_lambda[layer], upd)
        yd = _gqa_mixer(pc[13:16], pl[13:16], att_q_norm_w[layer], att_k_norm_w[layer], cos, sin, upd)
        lw = (norm2_w[layer], w_branch[layer], w_out[layer], mlp_w1[layer], mlp_w2[layer])
        new_lat = _residual_update(h_lat, [ya[1], yb[1], yc[1], yd[1]], pl[16], mod_lat, *lw)
        if upd:
            h_ctx = _residual_update(h_ctx, [ya[0], yb[0], yc[0], yd[0]], pc[16], mod_ctx, *lw)
        h_lat = new_lat
    return h_lat
```

```python
import functools

import jax
import jax.numpy as jnp
import numpy as np
from jax import lax
from jax.experimental import pallas as pl
from jax.experimental.pallas import tpu as pltpu

F32 = jnp.float32
BF16 = jnp.bfloat16

EPS = 1e-6
D_MODEL = 1024
GRID_W = 64
N_HEADS = 4
HEAD_W = 128
MIX_W = N_HEADS * HEAD_W
CHUNK = 64
SUB = 16
LRU_C = 8.0
LRU_BLOCK = 256
ATT_Q_HEADS = 8
ATT_KV_HEADS = 2
ATT_HEAD_DIM = 64
ATT_GROUP_W = (ATT_Q_HEADS // ATT_KV_HEADS) * ATT_HEAD_DIM
ATT_TQ = 256
ROPE_THETA = 10000.0
N_BRANCH = 4
D_FF = 4 * D_MODEL
FF_BLOCK = 1024

C_GATE = 0
C_HG_Q, C_HG_FF, C_HG_FB, C_HG_I, C_HG_G = 4096, 4608, 5120, 5632, 6144
C_DN_Q, C_DN_K, C_DN_V = 6656, 7168, 7680
C_LRU_X, C_LRU_G = 8192, 8704
C_DN_G = 9216
C_ATT_Q, C_ATT_K, C_ATT_V = 9728, 10240, 10368
C_DN_BA = 10496
N_U = 10752
IN_TN = 512

VMEM_LIMIT_V7X = 48 * 1024 * 1024


def _params(sem, vmem=VMEM_LIMIT_V7X):
    return pltpu.CompilerParams(dimension_semantics=sem, vmem_limit_bytes=vmem)


def _sigmoid(x):
    return 1.0 / (1.0 + jnp.exp(-x))


def _softplus(x):
    return jnp.maximum(x, 0.0) + jnp.log1p(jnp.exp(-jnp.abs(x)))


def _dot(a, b):
    return jnp.dot(a.astype(BF16), b.astype(BF16), preferred_element_type=F32)


def _dot_nt(a, b):
    return lax.dot_general(a.astype(BF16), b.astype(BF16), (((1,), (1,)), ((), ())),
                           preferred_element_type=F32)


def _dot_tn(a, b):
    return lax.dot_general(a.astype(BF16), b.astype(BF16), (((0,), (0,)), ((), ())),
                           preferred_element_type=F32)


def _dot_x3(x, m):
    m = m.astype(BF16)
    hi = x.astype(BF16)
    r1 = x - hi.astype(F32)
    mid = r1.astype(BF16)
    lo = (r1 - mid.astype(F32)).astype(BF16)
    out = jnp.dot(hi, m, preferred_element_type=F32)
    out = out + jnp.dot(mid, m, preferred_element_type=F32)
    return out + jnp.dot(lo, m, preferred_element_type=F32)


def _dot_3x(m, x):
    m = m.astype(BF16)
    hi = x.astype(BF16)
    r1 = x - hi.astype(F32)
    mid = r1.astype(BF16)
    lo = (r1 - mid.astype(F32)).astype(BF16)
    out = jnp.dot(m, hi, preferred_element_type=F32)
    out = out + jnp.dot(m, mid, preferred_element_type=F32)
    return out + jnp.dot(m, lo, preferred_element_type=F32)


def _rms_rows(x, w):
    return x * lax.rsqrt(jnp.mean(x * x, axis=-1, keepdims=True) + EPS) * w


def _mod_kernel(c_ref, w_ref, b_ref, o_ref):
    c = c_ref[...]
    o_ref[...] = _dot(c * _sigmoid(c), w_ref[...]) + b_ref[...]


def _modulations(cc, mod_w, mod_b):
    depth = mod_w.shape[0]
    rows = cc.shape[0]
    n_out = mod_w.shape[2]
    return pl.pallas_call(
        _mod_kernel,
        out_shape=jax.ShapeDtypeStruct((depth, rows, n_out), F32),
        grid=(depth, n_out // D_MODEL),
        in_specs=[
            pl.BlockSpec((rows, D_MODEL), lambda l, j: (0, 0)),
            pl.BlockSpec((None, D_MODEL, D_MODEL), lambda l, j: (l, 0, j)),
            pl.BlockSpec((None, 1, D_MODEL), lambda l, j: (l, 0, j)),
        ],
        out_specs=pl.BlockSpec((None, rows, D_MODEL), lambda l, j: (l, 0, j)),
        compiler_params=_params(("parallel", "parallel")),
        name="modulations",
    )(cc, mod_w, mod_b.reshape(depth, 1, n_out))


def _pick_mod(modc_ref, modl_ref, idx, is_ctx):
    return jnp.where(is_ctx, modc_ref[idx:idx + 1, :], modl_ref[idx:idx + 1, :])


def _inproj_kernel(h_ref, modc_ref, modl_ref, nw_ref, w_ref, o_ref, xn_ref, *, tm, tiles_per_batch, ctx_len):
    @pl.when(pl.program_id(1) == 0)
    def _():
        y = _rms_rows(h_ref[...], nw_ref[...])
        row = (pl.program_id(0) % tiles_per_batch) * tm + lax.broadcasted_iota(jnp.int32, (tm, 1), 0)
        is_ctx = row < ctx_len
        shift = _pick_mod(modc_ref, modl_ref, 0, is_ctx)
        scale = _pick_mod(modc_ref, modl_ref, 1, is_ctx)
        xn_ref[...] = (y * (1.0 + scale) + shift).astype(BF16)

    o_ref[...] = jnp.dot(xn_ref[...], w_ref[...], preferred_element_type=F32)


def _inproj(h, modc, modl, nw, w, *, seq, ctx_len):
    rows = h.shape[0]
    tiles_per_batch = 4
    tm = seq // tiles_per_batch
    kern = functools.partial(_inproj_kernel, tm=tm, tiles_per_batch=tiles_per_batch, ctx_len=ctx_len)
    return pl.pallas_call(
        kern,
        out_shape=jax.ShapeDtypeStruct((rows, N_U), F32),
        grid=(rows // tm, N_U // IN_TN),
        in_specs=[
            pl.BlockSpec((tm, D_MODEL), lambda i, j: (i, 0)),
            pl.BlockSpec((6, D_MODEL), lambda i, j: (0, 0)),
            pl.BlockSpec((None, 6, D_MODEL), lambda i, j: (i // tiles_per_batch, 0, 0)),
            pl.BlockSpec((1, D_MODEL), lambda i, j: (0, 0)),
            pl.BlockSpec((D_MODEL, IN_TN), lambda i, j: (0, j)),
        ],
        out_specs=pl.BlockSpec((tm, IN_TN), lambda i, j: (i, j)),
        scratch_shapes=[pltpu.VMEM((tm, D_MODEL), BF16)],
        compiler_params=_params(("parallel", "arbitrary")),
        name="inproj",
    )(h, modc, modl, nw, w)


def _seg_conv(x, w_ref, ctx_len):
    n = x.shape[0]
    row = lax.broadcasted_iota(jnp.int32, (n, 1), 0)
    lo = jnp.where(row >= ctx_len, ctx_len, 0)
    hi = jnp.where(row >= ctx_len, n, ctx_len)

    def tap(k):
        tk = row + k
        valid = jnp.logical_and(tk >= lo, tk < hi)
        return jnp.where(valid, pltpu.roll(x, (-k) % n, 0), 0.0)

    return (tap(-1) * w_ref[0:1, :] + x * w_ref[1:2, :] + tap(1) * w_ref[2:3, :] + tap(2) * w_ref[3:4, :])


def _dn_prep_kernel(u_ref, w_ref, o_ref, *, ctx_len):
    j = pl.program_id(1)
    y = _seg_conv(u_ref[...], w_ref, ctx_len)
    y = y * _sigmoid(y)
    n = lax.rsqrt(jnp.sum(y * y, axis=-1, keepdims=True) + EPS)
    fac = jnp.where(j < N_HEADS, n * (HEAD_W ** -0.5), jnp.where(j < 2 * N_HEADS, n, 1.0))
    o_ref[...] = y * fac


def _dn_prep(u, conv_w, *, batch, seq, ctx_len):
    nblk = 3 * N_HEADS
    return pl.pallas_call(
        functools.partial(_dn_prep_kernel, ctx_len=ctx_len),
        out_shape=jax.ShapeDtypeStruct((batch * seq, 3 * MIX_W), F32),
        grid=(batch, nblk),
        in_specs=[
            pl.BlockSpec((seq, HEAD_W), lambda b, j: (b, C_DN_Q // HEAD_W + j)),
            pl.BlockSpec((4, HEAD_W), lambda b, j: (0, j)),
        ],
        out_specs=pl.BlockSpec((seq, HEAD_W), lambda b, j: (b, j)),
        compiler_params=_params(("parallel", "parallel")),
        name="dn_prep",
    )(u, conv_w)


def _chunk_index(d, s, n_ctx, n_all):
    return jnp.where(d == 0, s, jnp.where(s < n_ctx, n_ctx - 1 - s, n_all + n_ctx - 1 - s))


def _unit_tri_inverse(a, eye, bd):
    ad = jnp.where(bd, a, 0.0)
    ao = a - ad
    p = -ad
    dinv = eye + p
    for _ in range(3):
        p = _dot(p, p)
        dinv = dinv + _dot(dinv, p)
    m = -_dot(dinv, ao)
    t = eye + m
    t = t + _dot(t, _dot(m, m))
    return _dot(t, dinv)


def _gdn_kernel(q_ref, k_ref, v_ref, ba_ref, alog_ref, dtb_ref, tri_ref, o_ref, s_ref):
    d = pl.program_id(1)

    @pl.when(pl.program_id(2) == 0)
    def _():
        s_ref[...] = jnp.zeros_like(s_ref)

    tri = tri_ref[d]
    tri_t = tri_ref[1 - d]
    ii = lax.broadcasted_iota(jnp.int32, (CHUNK, CHUNK), 0)
    jj = lax.broadcasted_iota(jnp.int32, (CHUNK, CHUNK), 1)
    is_eye = ii == jj
    eye = jnp.where(is_eye, 1.0, 0.0)
    bd = (ii // SUB) == (jj // SUB)
    incl = tri > 0.5

    ba = ba_ref[...]
    beta_r = _sigmoid(ba[0:N_HEADS])
    g_r = -jnp.exp(alog_ref[...]) * _softplus(ba[N_HEADS:2 * N_HEADS] + dtb_ref[...])

    for h in range(N_HEADS):
        sl = slice(h * HEAD_W, (h + 1) * HEAD_W)
        q, k, v = q_ref[:, sl], k_ref[:, sl], v_ref[:, sl]
        gr = g_r[h:h + 1]
        beta_c = jnp.sum(eye * beta_r[h:h + 1], axis=1, keepdims=True)
        g_c = jnp.sum(eye * gr, axis=1, keepdims=True)
        gam_c = jnp.sum(tri * gr, axis=1, keepdims=True)
        gam_r = jnp.sum(tri_t * g_c, axis=0, keepdims=True)
        tot = jnp.sum(gr, axis=1, keepdims=True)
        dec_i = jnp.where(incl, jnp.exp(gam_c - gam_r), 0.0)
        dec_s = jnp.where(is_eye, 0.0, dec_i)
        kb = k * beta_c
        a_low = _dot_nt(kb, k) * dec_s
        a_qk = _dot_nt(q, k) * dec_i
        t_inv = _unit_tri_inverse(a_low, eye, bd)
        egc = jnp.exp(gam_c)
        sol = _dot(t_inv, jnp.concatenate([v * beta_c, kb * egc], axis=1))
        u_i, w_i = sol[:, :HEAD_W], sol[:, HEAD_W:]
        s = s_ref[h]
        v_new = u_i - _dot(w_i, s)
        o_ref[:, sl] = _dot(q * egc, s) + _dot(a_qk, v_new)
        s_ref[h] = s * jnp.exp(tot) + _dot_tn(k * jnp.exp(tot - gam_c), v_new)


def _gdn(qkv, ba_t, a_log, dt_bias, tri, *, batch, seq, ctx_len):
    n_all = seq // CHUNK
    n_ctx = ctx_len // CHUNK

    def rows(b, d, s):
        return b * n_all + _chunk_index(d, s, n_ctx, n_all)

    return pl.pallas_call(
        _gdn_kernel,
        out_shape=jax.ShapeDtypeStruct((2, batch * seq, MIX_W), F32),
        grid=(batch, 2, n_all),
        in_specs=[
            pl.BlockSpec((CHUNK, MIX_W), lambda b, d, s: (rows(b, d, s), 0)),
            pl.BlockSpec((CHUNK, MIX_W), lambda b, d, s: (rows(b, d, s), 1)),
            pl.BlockSpec((CHUNK, MIX_W), lambda b, d, s: (rows(b, d, s), 2)),
            pl.BlockSpec((None, None, 2 * N_HEADS, CHUNK), lambda b, d, s: (d, rows(b, d, s), 0, 0)),
            pl.BlockSpec((None, N_HEADS, 1), lambda b, d, s: (d, 0, 0)),
            pl.BlockSpec((None, N_HEADS, 1), lambda b, d, s: (d, 0, 0)),
            pl.BlockSpec((2, CHUNK, CHUNK), lambda b, d, s: (0, 0, 0)),
        ],
        out_specs=pl.BlockSpec((None, CHUNK, MIX_W), lambda b, d, s: (d, rows(b, d, s), 0)),
        scratch_shapes=[pltpu.VMEM((N_HEADS, HEAD_W, HEAD_W), F32)],
        compiler_params=_params(("parallel", "parallel", "arbitrary")),
        name="gdn",
    )(qkv, qkv, qkv, ba_t, a_log, dt_bias, tri)


def _hgrn2_kernel(q_ref, f_ref, i_ref, lb_ref, tri_ref, o_ref, s_ref):
    d = pl.program_id(1)

    @pl.when(pl.program_id(2) == 0)
    def _():
        s_ref[...] = jnp.zeros_like(s_ref)

    tri = tri_ref[d]
    incl = tri > 0.5
    row = lax.broadcasted_iota(jnp.int32, (CHUNK, 1), 0)
    n_sub = CHUNK // SUB

    for h in range(N_HEADS):
        sl = slice(h * HEAD_W, (h + 1) * HEAD_W)
        qr = q_ref[:, sl]
        q = qr * _sigmoid(qr)
        lb = lb_ref[:, sl]
        f = lb + (1.0 - lb) * _sigmoid(f_ref[:, sl])
        k = 1.0 - f
        lf = jnp.log(f)
        v = i_ref[:, sl]
        gc = _dot_3x(tri, lf)
        tot = jnp.sum(lf, axis=0, keepdims=True)
        blocks = []
        for i in range(n_sub):
            mid = i * SUB + SUB // 2
            gref = gc[mid:mid + 1, :]
            qs = q[i * SUB:(i + 1) * SUB, :] * jnp.exp(gc[i * SUB:(i + 1) * SUB, :] - gref)
            lo = jnp.where(d == 0, 0, i * SUB)
            hi = jnp.where(d == 0, (i + 1) * SUB, CHUNK)
            reach = jnp.logical_and(row >= lo, row < hi)
            ks = jnp.where(reach, k * jnp.exp(gref - gc), 0.0)
            blocks.append(_dot_nt(qs, ks))
        scores = jnp.where(incl, jnp.concatenate(blocks, axis=0), 0.0)
        st = s_ref[h]
        o_ref[:, sl] = _dot_nt(q * jnp.exp(gc), st) + _dot(scores, v)
        s_ref[h] = st * jnp.exp(tot) + _dot_tn(v, k * jnp.exp(tot - gc))


def _hgrn2(u, lb, tri, *, batch, seq, ctx_len):
    n_all = seq // CHUNK
    n_ctx = ctx_len // CHUNK

    def rows(b, d, s):
        return b * n_all + _chunk_index(d, s, n_ctx, n_all)

    return pl.pallas_call(
        _hgrn2_kernel,
        out_shape=jax.ShapeDtypeStruct((2, batch * seq, MIX_W), F32),
        grid=(batch, 2, n_all),
        in_specs=[
            pl.BlockSpec((CHUNK, MIX_W), lambda b, d, s: (rows(b, d, s), C_HG_Q // MIX_W)),
            pl.BlockSpec((CHUNK, MIX_W), lambda b, d, s: (rows(b, d, s), C_HG_FF // MIX_W + d)),
            pl.BlockSpec((CHUNK, MIX_W), lambda b, d, s: (rows(b, d, s), C_HG_I // MIX_W)),
            pl.BlockSpec((None, 1, MIX_W), lambda b, d, s: (d, 0, 0)),
            pl.BlockSpec((2, CHUNK, CHUNK), lambda b, d, s: (0, 0, 0)),
        ],
        out_specs=pl.BlockSpec((None, CHUNK, MIX_W), lambda b, d, s: (d, rows(b, d, s), 0)),
        scratch_shapes=[pltpu.VMEM((N_HEADS, HEAD_W, HEAD_W), F32)],
        compiler_params=_params(("parallel", "parallel", "arbitrary")),
        name="hgrn2",
    )(u, u, u, lb, tri)


def _block_scan(a, b, rev):
    n = a.shape[0]
    row = lax.broadcasted_iota(jnp.int32, (n, 1), 0)
    s = 1
    while s < n:
        if rev:
            a_s, b_s, valid = pltpu.roll(a, n - s, 0), pltpu.roll(b, n - s, 0), row < n - s
        else:
            a_s, b_s, valid = pltpu.roll(a, s, 0), pltpu.roll(b, s, 0), row >= s
        b = jnp.where(valid, a * b_s + b, b)
        a = jnp.where(valid, a * a_s, a)
        s *= 2
    return a, b


def _gelu_tanh(x):
    return 0.5 * x * (1.0 + jnp.tanh(0.7978845608028654 * (x + 0.044715 * (x * x * x))))


def _lru_kernel(xb_ref, gb_ref, cw_ref, cb_ref, wa_ref, ba_ref, wx_ref, bx_ref, lam_ref, o_ref, xc_ref,
                *, ctx_len):
    n = xb_ref.shape[0]
    nblk = n // LRU_BLOCK
    nctx = ctx_len // LRU_BLOCK
    xc_ref[...] = _seg_conv(xb_ref[...], cw_ref, ctx_len) + cb_ref[...]

    def gates(xc, d):
        r = _sigmoid(_dot(xc, wa_ref[d]) + ba_ref[d])
        ig = _sigmoid(_dot(xc, wx_ref[d]) + bx_ref[d])
        log_a = -LRU_C * r * _softplus(-lam_ref[d])
        a = jnp.exp(log_a)
        return a, jnp.sqrt(1.0 - jnp.exp(2.0 * log_a)) * ig * xc

    def fwd(blk, carry):
        rows = pl.ds(pl.multiple_of(blk * LRU_BLOCK, LRU_BLOCK), LRU_BLOCK)
        a, b = _block_scan(*gates(xc_ref[rows, :], 0), rev=False)
        hb = a * carry + b
        o_ref[rows, :] = hb
        return hb[LRU_BLOCK - 1:LRU_BLOCK, :]

    def bwd(blk, carry):
        rows = pl.ds(pl.multiple_of(blk * LRU_BLOCK, LRU_BLOCK), LRU_BLOCK)
        a, b = _block_scan(*gates(xc_ref[rows, :], 1), rev=True)
        hb = a * carry + b
        o_ref[rows, :] = (o_ref[rows, :] + hb) * _gelu_tanh(gb_ref[rows, :])
        return hb[0:1, :]

    zero = jnp.zeros((1, HEAD_W), F32)
    lax.fori_loop(0, nblk, fwd, zero)
    carry = lax.fori_loop(0, nctx, lambda i, c: bwd(nctx - 1 - i, c), zero)
    lax.fori_loop(0, nblk - nctx, lambda i, c: bwd(nblk - 1 - i, c), carry)


def _lru(u, conv_w, conv_b, w_a, b_a, w_x, b_x, lam, *, batch, seq, ctx_len):
    vec = pl.BlockSpec((2, 1, HEAD_W), lambda b, h: (0, 0, h))
    mat = pl.BlockSpec((2, None, HEAD_W, HEAD_W), lambda b, h: (0, h, 0, 0))
    return pl.pallas_call(
        functools.partial(_lru_kernel, ctx_len=ctx_len),
        out_shape=jax.ShapeDtypeStruct((batch * seq, MIX_W), F32),
        grid=(batch, N_HEADS),
        in_specs=[
            pl.BlockSpec((seq, HEAD_W), lambda b, h: (b, C_LRU_X // HEAD_W + h)),
            pl.BlockSpec((seq, HEAD_W), lambda b, h: (b, C_LRU_G // HEAD_W + h)),
            pl.BlockSpec((4, HEAD_W), lambda b, h: (0, h)),
            pl.BlockSpec((1, HEAD_W), lambda b, h: (0, h)),
            mat, vec, mat, vec, vec,
        ],
        out_specs=pl.BlockSpec((seq, HEAD_W), lambda b, h: (b, h)),
        scratch_shapes=[pltpu.VMEM((seq, HEAD_W), F32)],
        compiler_params=_params(("parallel", "parallel")),
        name="rglru",
    )(u, u, conv_w, conv_b, w_a, b_a, w_x, b_x, lam)


def _att_prep_kernel(q_ref, k_ref, cos_ref, sin_ref, qw_ref, kw_ref, grp_ref, qo_ref, ko_ref):
    grp = grp_ref[...]
    cos, sin = cos_ref[...], sin_ref[...]
    lane = lax.broadcasted_iota(jnp.int32, (1, 128), 1)
    first = (lane % (ATT_HEAD_DIM // 2)) < (ATT_HEAD_DIM // 4)

    def norm_rope(x, w, scale):
        ss = _dot_x3(x * x, grp)
        y = x * lax.rsqrt(ss * (1.0 / ATT_HEAD_DIM) + EPS) * w
        rot = jnp.where(first, pltpu.roll(y, 128 - ATT_HEAD_DIM // 4, 1), pltpu.roll(y, ATT_HEAD_DIM // 4, 1))
        return (y * cos + rot * sin) * scale

    for s in range(q_ref.shape[1] // 128):
        sl = slice(s * 128, (s + 1) * 128)
        qo_ref[:, sl] = norm_rope(q_ref[:, sl], qw_ref[...], ATT_HEAD_DIM ** -0.5)
    ko_ref[...] = norm_rope(k_ref[...], kw_ref[...], 1.0)


def _att_prep(u, cos, sin, qw, kw, grp, *, seq):
    rows = u.shape[0]
    tm = ATT_TQ
    tiles_per_batch = seq // tm
    qw_cols = ATT_Q_HEADS * ATT_HEAD_DIM
    kw_cols = ATT_KV_HEADS * ATT_HEAD_DIM
    return pl.pallas_call(
        _att_prep_kernel,
        out_shape=(jax.ShapeDtypeStruct((rows, qw_cols), F32), jax.ShapeDtypeStruct((rows, kw_cols), F32)),
        grid=(rows // tm,),
        in_specs=[
            pl.BlockSpec((tm, qw_cols), lambda i: (i, C_ATT_Q // qw_cols)),
            pl.BlockSpec((tm, kw_cols), lambda i: (i, C_ATT_K // kw_cols)),
            pl.BlockSpec((tm, 128), lambda i: (i % tiles_per_batch, 0)),
            pl.BlockSpec((tm, 128), lambda i: (i % tiles_per_batch, 0)),
            pl.BlockSpec((1, 128), lambda i: (0, 0)),
            pl.BlockSpec((1, 128), lambda i: (0, 0)),
            pl.BlockSpec((128, 128), lambda i: (0, 0)),
        ],
        out_specs=(pl.BlockSpec((tm, qw_cols), lambda i: (i, 0)), pl.BlockSpec((tm, kw_cols), lambda i: (i, 0))),
        compiler_params=_params(("parallel",)),
        name="att_prep",
    )(u, u, cos, sin, qw, kw, grp)


def _att_kernel(q_ref, kt_ref, v_ref, o_ref, *, ctx_len):
    def attend(n_keys):
        kt = kt_ref[:, :n_keys].astype(BF16)
        v = v_ref[:n_keys, :].astype(BF16)
        for h in range(ATT_Q_HEADS // ATT_KV_HEADS):
            sl = slice(h * ATT_HEAD_DIM, (h + 1) * ATT_HEAD_DIM)
            s = jnp.dot(q_ref[:, sl].astype(BF16), kt, preferred_element_type=F32)
            p = jnp.exp(s - jnp.max(s, axis=-1, keepdims=True))
            den = jnp.sum(p, axis=-1, keepdims=True)
            o_ref[:, sl] = jnp.dot(p.astype(BF16), v, preferred_element_type=F32) / den

    is_ctx = pl.program_id(2) == 0

    @pl.when(is_ctx)
    def _():
        attend(ctx_len)

    @pl.when(jnp.logical_not(is_ctx))
    def _():
        attend(kt_ref.shape[1])


def _attention(qn, kt, vv, *, batch, seq, ctx_len):
    tiles = seq // ATT_TQ
    return pl.pallas_call(
        functools.partial(_att_kernel, ctx_len=ctx_len),
        out_shape=jax.ShapeDtypeStruct((batch * seq, ATT_Q_HEADS * ATT_HEAD_DIM), F32),
        grid=(batch, ATT_KV_HEADS, tiles),
        in_specs=[
            pl.BlockSpec((ATT_TQ, ATT_GROUP_W), lambda b, g, i: (b * tiles + i, g)),
            pl.BlockSpec((None, None, ATT_HEAD_DIM, seq), lambda b, g, i: (b, g, 0, 0)),
            pl.BlockSpec((None, None, seq, ATT_HEAD_DIM), lambda b, g, i: (b, g, 0, 0)),
        ],
        out_specs=pl.BlockSpec((ATT_TQ, ATT_GROUP_W), lambda b, g, i: (b * tiles + i, g)),
        compiler_params=_params(("parallel", "parallel", "arbitrary")),
        name="attention",
    )(qn, kt, vv)


def _merge_kernel(oa_ref, ob_ref, ga_ref, gb_ref, yc_ref, yd_ref, gate_ref, h_ref, modc_ref, modl_ref,
                  dnw_ref, hgw_ref, wb_ref, wo_ref, out_ref, *, tiles_per_batch):
    def gated(o_ref, g_ref, nw_ref):
        o = o_ref[0] + o_ref[1]
        ys = [_rms_rows(o[:, h * HEAD_W:(h + 1) * HEAD_W], nw_ref[...]) for h in range(N_HEADS)]
        g = g_ref[...]
        return jnp.concatenate(ys, axis=1) * (g * _sigmoid(g))

    ys = (gated(oa_ref, ga_ref, dnw_ref), gated(ob_ref, gb_ref, hgw_ref), yc_ref[...], yd_ref[...])
    acc = None
    for b in range(N_BRANCH):
        term = _sigmoid(gate_ref[:, b * D_MODEL:(b + 1) * D_MODEL]) * _dot(ys[b], wb_ref[b])
        acc = term if acc is None else acc + term
    is_ctx = (pl.program_id(0) % tiles_per_batch) == 0
    out_ref[...] = h_ref[...] + _pick_mod(modc_ref, modl_ref, 2, is_ctx) * _dot(acc, wo_ref[...])


def _merge(oa, ob, u, yc, yd, h, modc, modl, dnw, hgw, wb, wo, *, seq, ctx_len):
    rows = h.shape[0]
    tm = ctx_len
    tiles_per_batch = seq // tm
    mix = lambda c: pl.BlockSpec((tm, MIX_W), lambda i: (i, c // MIX_W))
    two = pl.BlockSpec((2, tm, MIX_W), lambda i: (0, i, 0))
    return pl.pallas_call(
        functools.partial(_merge_kernel, tiles_per_batch=tiles_per_batch),
        out_shape=jax.ShapeDtypeStruct((rows, D_MODEL), F32),
        grid=(rows // tm,),
        in_specs=[
            two, two, mix(C_DN_G), mix(C_HG_G), mix(0), mix(0),
            pl.BlockSpec((tm, N_BRANCH * D_MODEL), lambda i: (i, 0)),
            pl.BlockSpec((tm, D_MODEL), lambda i: (i, 0)),
            pl.BlockSpec((6, D_MODEL), lambda i: (0, 0)),
            pl.BlockSpec((None, 6, D_MODEL), lambda i: (i // tiles_per_batch, 0, 0)),
            pl.BlockSpec((1, HEAD_W), lambda i: (0, 0)),
            pl.BlockSpec((1, HEAD_W), lambda i: (0, 0)),
            pl.BlockSpec((N_BRANCH, MIX_W, D_MODEL), lambda i: (0, 0, 0)),
            pl.BlockSpec((D_MODEL, D_MODEL), lambda i: (0, 0)),
        ],
        out_specs=pl.BlockSpec((tm, D_MODEL), lambda i: (i, 0)),
        compiler_params=_params(("parallel",)),
        name="merge",
    )(oa, ob, u, u, yc, yd, u, h, modc, modl, dnw, hgw, wb, wo)


def _mlp_kernel(h_ref, modc_ref, modl_ref, nw_ref, w1_ref, w2_ref, o_ref, z_ref, acc_ref,
                *, tm, tiles_per_batch, ctx_len):
    j = pl.program_id(1)
    row = (pl.program_id(0) % tiles_per_batch) * tm + lax.broadcasted_iota(jnp.int32, (tm, 1), 0)
    is_ctx = row < ctx_len

    @pl.when(j == 0)
    def _():
        y = _rms_rows(h_ref[...], nw_ref[...])
        shift = _pick_mod(modc_ref, modl_ref, 3, is_ctx)
        scale = _pick_mod(modc_ref, modl_ref, 4, is_ctx)
        z_ref[...] = (y * (1.0 + scale) + shift).astype(BF16)
        acc_ref[...] = jnp.zeros_like(acc_ref)

    a = jnp.maximum(jnp.dot(z_ref[...], w1_ref[...], preferred_element_type=F32), 0.0)
    acc_ref[...] += _dot(a * a, w2_ref[...])

    @pl.when(j == pl.num_programs(1) - 1)
    def _():
        o_ref[...] = h_ref[...] + _pick_mod(modc_ref, modl_ref, 5, is_ctx) * acc_ref[...]


def _mlp(h, modc, modl, nw, w1, w2, *, seq, ctx_len):
    rows = h.shape[0]
    tiles_per_batch = 4
    tm = seq // tiles_per_batch
    kern = functools.partial(_mlp_kernel, tm=tm, tiles_per_batch=tiles_per_batch, ctx_len=ctx_len)
    return pl.pallas_call(
        kern,
        out_shape=jax.ShapeDtypeStruct((rows, D_MODEL), F32),
        grid=(rows // tm, D_FF // FF_BLOCK),
        in_specs=[
            pl.BlockSpec((tm, D_MODEL), lambda i, j: (i, 0)),
            pl.BlockSpec((6, D_MODEL), lambda i, j: (0, 0)),
            pl.BlockSpec((None, 6, D_MODEL), lambda i, j: (i // tiles_per_batch, 0, 0)),
            pl.BlockSpec((1, D_MODEL), lambda i, j: (0, 0)),
            pl.BlockSpec((D_MODEL, FF_BLOCK), lambda i, j: (0, j)),
            pl.BlockSpec((FF_BLOCK, D_MODEL), lambda i, j: (j, 0)),
        ],
        out_specs=pl.BlockSpec((tm, D_MODEL), lambda i, j: (i, 0)),
        scratch_shapes=[pltpu.VMEM((tm, D_MODEL), BF16), pltpu.VMEM((tm, D_MODEL), F32)],
        compiler_params=_params(("parallel", "arbitrary")),
        name="mlp",
    )(h, modc, modl, nw, w1, w2)


def _reorder_w_in(w_in):
    o = np.cumsum([0, 512, 512, 512, 512, 8, 8, 512, 512, 512, 512, 512, 512, 512, 512, 128, 128, 4096])
    dn_q, dn_g, dn_ba, hg, lru, att, gate, end = o[0], o[3], o[4], o[6], o[11], o[13], o[16], o[17]
    parts = [w_in[..., gate:end], w_in[..., hg:lru], w_in[..., dn_q:dn_g], w_in[..., lru:att],
             w_in[..., dn_g:dn_ba], w_in[..., att:gate], w_in[..., dn_ba:hg]]
    w = jnp.concatenate(parts, axis=-1)
    return jnp.pad(w, ((0, 0), (0, 0), (0, N_U - w.shape[-1]))).astype(BF16)


def _rope_tables(t_len, ctx_len):
    rows = t_len // GRID_W
    row_id = jnp.repeat(jnp.arange(rows), GRID_W).astype(F32)
    col_id = jnp.tile(jnp.arange(GRID_W), rows).astype(F32)
    axis_dim = ATT_HEAD_DIM // 2
    inv = ROPE_THETA ** (-jnp.arange(0, axis_dim, 2, dtype=F32) / axis_dim)
    ang = jnp.stack([row_id[:, None] * inv, col_id[:, None] * inv], axis=1)
    cos, sin = jnp.cos(ang), jnp.sin(ang)
    cos_h = jnp.concatenate([cos, cos], axis=-1).reshape(t_len, ATT_HEAD_DIM)
    sin_h = jnp.concatenate([-sin, sin], axis=-1).reshape(t_len, ATT_HEAD_DIM)
    cos_t = jnp.concatenate([jnp.ones((ctx_len, ATT_HEAD_DIM), F32), cos_h], axis=0)
    sin_t = jnp.concatenate([jnp.zeros((ctx_len, ATT_HEAD_DIM), F32), sin_h], axis=0)
    return jnp.tile(cos_t, (1, 2)), jnp.tile(sin_t, (1, 2))


def _hgrn2_lower_bounds(p):
    sm = jax.nn.softmax(p.astype(F32), axis=1)
    cs = jnp.cumsum(sm, axis=1)
    return cs - cs[:, :1]


def kernel(x, c, ctx, c_ctx, mod_w, mod_b, norm1_w, norm2_w, w_in, dn_conv_w, dn_a_log, dn_dt_bias, dn_norm_w,
           hg_lower_bounds, hg_norm_w, lru_conv_w, lru_conv_b, lru_w_a, lru_b_a, lru_w_x, lru_b_x, lru_lambda,
           att_q_norm_w, att_k_norm_w, w_branch, w_out, mlp_w1, mlp_w2):
    batch, t_len, _ = x.shape
    ctx_len = ctx.shape[1]
    depth = mod_w.shape[0]
    seq = ctx_len + t_len
    n_chunks = seq // CHUNK
    dims = dict(batch=batch, seq=seq, ctx_len=ctx_len)

    w_in_r = _reorder_w_in(w_in)
    w_branch_b, w_out_b = w_branch.astype(BF16), w_out.astype(BF16)
    w1_b, w2_b = mlp_w1.astype(BF16), mlp_w2.astype(BF16)
    lru_wa_b, lru_wx_b = lru_w_a.astype(BF16), lru_w_x.astype(BF16)
    lb_all = _hgrn2_lower_bounds(hg_lower_bounds)
    cos_t, sin_t = _rope_tables(t_len, ctx_len)
    pos = np.arange(CHUNK)
    tri = jnp.asarray(np.stack([pos[:, None] >= pos[None, :], pos[:, None] <= pos[None, :]]), F32)
    grp = jnp.asarray((np.arange(128)[:, None] // ATT_HEAD_DIM) == (np.arange(128)[None, :] // ATT_HEAD_DIM), F32)

    n_mod_rows = batch + 8
    cc = jnp.zeros((n_mod_rows, D_MODEL), F32).at[:batch].set(c).at[batch].set(c_ctx)
    mods = _modulations(cc, mod_w.astype(BF16), mod_b)

    h = jnp.concatenate([ctx, x], axis=1).reshape(batch * seq, D_MODEL)
    for l in range(depth):
        modl = mods[l, :batch].reshape(batch, 6, D_MODEL)
        modc = mods[l, batch].reshape(6, D_MODEL)
        u = _inproj(h, modc, modl, norm1_w[l].reshape(1, D_MODEL), w_in_r[l], seq=seq, ctx_len=ctx_len)

        qkv = _dn_prep(u, dn_conv_w[l], **dims)
        ba = u[:, C_DN_BA:C_DN_BA + 4 * N_HEADS].reshape(batch * n_chunks, CHUNK, 2, 2, N_HEADS)
        ba_t = jnp.transpose(ba, (3, 0, 2, 4, 1)).reshape(2, batch * n_chunks, 2 * N_HEADS, CHUNK)
        oa = _gdn(qkv, ba_t, dn_a_log[l].reshape(2, N_HEADS, 1), dn_dt_bias[l].reshape(2, N_HEADS, 1), tri, **dims)

        ob = _hgrn2(u, lb_all[:, l].reshape(2, 1, MIX_W), tri, **dims)

        yc = _lru(u, lru_conv_w[l], lru_conv_b[l].reshape(1, MIX_W), lru_wa_b[l], lru_b_a[l].reshape(2, 1, MIX_W),
                  lru_wx_b[l], lru_b_x[l].reshape(2, 1, MIX_W), lru_lambda[l].reshape(2, 1, MIX_W), **dims)

        qn, kn = _att_prep(u, cos_t, sin_t, jnp.tile(att_q_norm_w[l], 2).reshape(1, 128),
                           jnp.tile(att_k_norm_w[l], 2).reshape(1, 128), grp, seq=seq)
        kt = jnp.transpose(kn.reshape(batch, seq, ATT_KV_HEADS, ATT_HEAD_DIM), (0, 2, 3, 1))
        vv = jnp.transpose(u[:, C_ATT_V:C_ATT_V + ATT_KV_HEADS * ATT_HEAD_DIM]
                           .reshape(batch, seq, ATT_KV_HEADS, ATT_HEAD_DIM), (0, 2, 1, 3))
        yd = _attention(qn, kt, vv, **dims)

        h = _merge(oa, ob, u, yc, yd, h, modc, modl, dn_norm_w[l].reshape(1, HEAD_W),
                   hg_norm_w[l].reshape(1, HEAD_W), w_branch_b[l], w_out_b[l], seq=seq, ctx_len=ctx_len)
        h = _mlp(h, modc, modl, norm2_w[l].reshape(1, D_MODEL), w1_b[l], w2_b[l], seq=seq, ctx_len=ctx_len)

    return h.reshape(batch, seq, D_MODEL)[:, ctx_len:]
```

```python
import functools

import jax
import jax.numpy as jnp
import numpy as np
from jax import lax
from jax.experimental import pallas as pl
from jax.experimental.pallas import tpu as pltpu

F32 = jnp.float32
BF16 = jnp.bfloat16

EPS = 1e-6
D_MODEL = 1024
GRID_W = 64
N_HEADS = 4
HEAD_W = 128
MIX_W = N_HEADS * HEAD_W
CHUNK = 64
SUB = 16
LRU_C = 8.0
LRU_BLOCK = 256
ATT_Q_HEADS = 8
ATT_KV_HEADS = 2
ATT_HEAD_DIM = 64
ATT_GROUP_W = (ATT_Q_HEADS // ATT_KV_HEADS) * ATT_HEAD_DIM
ATT_TQ = 256
ROPE_THETA = 10000.0
N_BRANCH = 4
D_FF = 4 * D_MODEL
FF_BLOCK = 1024

C_GATE = 0
C_HG_Q, C_HG_FF, C_HG_FB, C_HG_I, C_HG_G = 4096, 4608, 5120, 5632, 6144
C_DN_Q, C_DN_K, C_DN_V = 6656, 7168, 7680
C_LRU_X, C_LRU_G = 8192, 8704
C_DN_G = 9216
C_ATT_Q, C_ATT_K, C_ATT_V = 9728, 10240, 10368
C_DN_BA = 10496
N_U = 10752
IN_TN = 512

VMEM_LIMIT_V7X = 48 * 1024 * 1024


def _params(sem, vmem=VMEM_LIMIT_V7X):
    return pltpu.CompilerParams(dimension_semantics=sem, vmem_limit_bytes=vmem)


def _sigmoid(x):
    return 1.0 / (1.0 + jnp.exp(-x))


def _softplus(x):
    return jnp.maximum(x, 0.0) + jnp.log1p(jnp.exp(-jnp.abs(x)))


def _dot(a, b):
    return jnp.dot(a.astype(BF16), b.astype(BF16), preferred_element_type=F32)


def _dot_nt(a, b):
    return lax.dot_general(a.astype(BF16), b.astype(BF16), (((1,), (1,)), ((), ())),
                           preferred_element_type=F32)


def _dot_tn(a, b):
    return lax.dot_general(a.astype(BF16), b.astype(BF16), (((0,), (0,)), ((), ())),
                           preferred_element_type=F32)


def _dot_x3(x, m):
    m = m.astype(BF16)
    hi = x.astype(BF16)
    r1 = x - hi.astype(F32)
    mid = r1.astype(BF16)
    lo = (r1 - mid.astype(F32)).astype(BF16)
    out = jnp.dot(hi, m, preferred_element_type=F32)
    out = out + jnp.dot(mid, m, preferred_element_type=F32)
    return out + jnp.dot(lo, m, preferred_element_type=F32)


def _dot_3x_many(m, xs):
    w = xs[0].shape[1]
    his = [x.astype(BF16) for x in xs]
    r1s = [x - hi.astype(F32) for x, hi in zip(xs, his)]
    mids = [r.astype(BF16) for r in r1s]
    los = [(r - mid.astype(F32)).astype(BF16) for r, mid in zip(r1s, mids)]
    wide = jnp.dot(m.astype(BF16), jnp.concatenate(his + mids + los, axis=1), preferred_element_type=F32)
    n = len(xs)
    part = lambda p, i: wide[:, (p * n + i) * w:(p * n + i + 1) * w]
    return [part(0, i) + part(1, i) + part(2, i) for i in range(n)]


def _rms_rows(x, w):
    return x * lax.rsqrt(jnp.mean(x * x, axis=-1, keepdims=True) + EPS) * w


def _mod_kernel(c_ref, w_ref, b_ref, o_ref):
    c = c_ref[...]
    o_ref[...] = _dot(c * _sigmoid(c), w_ref[...]) + b_ref[...]


def _modulations(cc, mod_w, mod_b):
    depth = mod_w.shape[0]
    rows = cc.shape[0]
    n_out = mod_w.shape[2]
    return pl.pallas_call(
        _mod_kernel,
        out_shape=jax.ShapeDtypeStruct((depth, rows, n_out), F32),
        grid=(depth, n_out // D_MODEL),
        in_specs=[
            pl.BlockSpec((rows, D_MODEL), lambda l, j: (0, 0)),
            pl.BlockSpec((None, D_MODEL, D_MODEL), lambda l, j: (l, 0, j)),
            pl.BlockSpec((None, 1, D_MODEL), lambda l, j: (l, 0, j)),
        ],
        out_specs=pl.BlockSpec((None, rows, D_MODEL), lambda l, j: (l, 0, j)),
        compiler_params=_params(("parallel", "parallel")),
        name="modulations",
    )(cc, mod_w, mod_b.reshape(depth, 1, n_out))


def _pick_mod(modc_ref, modl_ref, idx, is_ctx):
    return jnp.where(is_ctx, modc_ref[idx:idx + 1, :], modl_ref[idx:idx + 1, :])


def _inproj_kernel(h_ref, modc_ref, modl_ref, nw_ref, w_ref, o_ref, xn_ref, *, tm, tiles_per_batch, ctx_len):
    @pl.when(pl.program_id(1) == 0)
    def _():
        y = _rms_rows(h_ref[...], nw_ref[...])
        row = (pl.program_id(0) % tiles_per_batch) * tm + lax.broadcasted_iota(jnp.int32, (tm, 1), 0)
        is_ctx = row < ctx_len
        shift = _pick_mod(modc_ref, modl_ref, 0, is_ctx)
        scale = _pick_mod(modc_ref, modl_ref, 1, is_ctx)
        xn_ref[...] = (y * (1.0 + scale) + shift).astype(BF16)

    o_ref[...] = jnp.dot(xn_ref[...], w_ref[...], preferred_element_type=F32)


def _inproj(h, modc, modl, nw, w, *, seq, ctx_len):
    rows = h.shape[0]
    tiles_per_batch = 4
    tm = seq // tiles_per_batch
    kern = functools.partial(_inproj_kernel, tm=tm, tiles_per_batch=tiles_per_batch, ctx_len=ctx_len)
    return pl.pallas_call(
        kern,
        out_shape=jax.ShapeDtypeStruct((rows, N_U), F32),
        grid=(rows // tm, N_U // IN_TN),
        in_specs=[
            pl.BlockSpec((tm, D_MODEL), lambda i, j: (i, 0)),
            pl.BlockSpec((6, D_MODEL), lambda i, j: (0, 0)),
            pl.BlockSpec((None, 6, D_MODEL), lambda i, j: (i // tiles_per_batch, 0, 0)),
            pl.BlockSpec((1, D_MODEL), lambda i, j: (0, 0)),
            pl.BlockSpec((D_MODEL, IN_TN), lambda i, j: (0, j)),
        ],
        out_specs=pl.BlockSpec((tm, IN_TN), lambda i, j: (i, j)),
        scratch_shapes=[pltpu.VMEM((tm, D_MODEL), BF16)],
        compiler_params=_params(("parallel", "arbitrary")),
        name="inproj",
    )(h, modc, modl, nw, w)


def _seg_conv(x, w_ref, ctx_len):
    n = x.shape[0]
    row = lax.broadcasted_iota(jnp.int32, (n, 1), 0)
    lo = jnp.where(row >= ctx_len, ctx_len, 0)
    hi = jnp.where(row >= ctx_len, n, ctx_len)

    def tap(k):
        tk = row + k
        valid = jnp.logical_and(tk >= lo, tk < hi)
        return jnp.where(valid, pltpu.roll(x, (-k) % n, 0), 0.0)

    return (tap(-1) * w_ref[0:1, :] + x * w_ref[1:2, :] + tap(1) * w_ref[2:3, :] + tap(2) * w_ref[3:4, :])


def _dn_prep_kernel(u_ref, w_ref, o_ref, *, ctx_len):
    j = pl.program_id(1)
    y = _seg_conv(u_ref[...], w_ref, ctx_len)
    y = y * _sigmoid(y)
    n = lax.rsqrt(jnp.sum(y * y, axis=-1, keepdims=True) + EPS)
    fac = jnp.where(j < N_HEADS, n * (HEAD_W ** -0.5), jnp.where(j < 2 * N_HEADS, n, 1.0))
    o_ref[...] = y * fac


def _dn_prep(u, conv_w, *, batch, seq, ctx_len):
    nblk = 3 * N_HEADS
    return pl.pallas_call(
        functools.partial(_dn_prep_kernel, ctx_len=ctx_len),
        out_shape=jax.ShapeDtypeStruct((batch * seq, 3 * MIX_W), F32),
        grid=(batch, nblk),
        in_specs=[
            pl.BlockSpec((seq, HEAD_W), lambda b, j: (b, C_DN_Q // HEAD_W + j)),
            pl.BlockSpec((4, HEAD_W), lambda b, j: (0, j)),
        ],
        out_specs=pl.BlockSpec((seq, HEAD_W), lambda b, j: (b, j)),
        compiler_params=_params(("parallel", "parallel")),
        name="dn_prep",
    )(u, conv_w)


def _block_order(s, n_ctx, n_all, rev):
    if not rev:
        return s
    return jnp.where(s < n_ctx, n_ctx - 1 - s, n_all + n_ctx - 1 - s)


def _chunk_masks(rev):
    ii = lax.broadcasted_iota(jnp.int32, (CHUNK, CHUNK), 0)
    jj = lax.broadcasted_iota(jnp.int32, (CHUNK, CHUNK), 1)
    incl = (ii <= jj) if rev else (ii >= jj)
    incl_t = (ii >= jj) if rev else (ii <= jj)
    return ii, jj, incl, incl_t


def _unit_tri_inverse(a_all, eye, bd):
    ad = [jnp.where(bd, a, 0.0) for a in a_all]
    ao = [a - d for a, d in zip(a_all, ad)]
    p = [-d for d in ad]
    dinv = [eye + x for x in p]
    for _ in range(3):
        p = [_dot(x, x) for x in p]
        dinv = [d + _dot(d, x) for d, x in zip(dinv, p)]
    m = [-_dot(d, o) for d, o in zip(dinv, ao)]
    mm = [_dot(x, x) for x in m]
    t = [eye + x for x in m]
    t = [x + _dot(x, y) for x, y in zip(t, mm)]
    return [_dot(x, d) for x, d in zip(t, dinv)]


def _gdn_kernel(q_ref, k_ref, v_ref, ba_ref, alog_ref, dtb_ref, o_ref, s_ref, *, rev):
    @pl.when(pl.program_id(1) == 0)
    def _():
        s_ref[...] = jnp.zeros_like(s_ref)

    ii, jj, incl, incl_t = _chunk_masks(rev)
    is_eye = ii == jj
    eye = jnp.where(is_eye, 1.0, 0.0)
    tri = jnp.where(incl, 1.0, 0.0)
    tri_t = jnp.where(incl_t, 1.0, 0.0)
    bd = (ii // SUB) == (jj // SUB)
    n_chunks = q_ref.shape[0] // CHUNK

    alog_neg = -jnp.exp(alog_ref[...])
    dtb = dtb_ref[...]
    chains = [(j, h) for j in range(n_chunks) for h in range(N_HEADS)]
    vec = {}
    for j in range(n_chunks):
        ba = ba_ref[j]
        beta_r = _sigmoid(ba[0:N_HEADS])
        g_r = alog_neg * _softplus(ba[N_HEADS:2 * N_HEADS] + dtb)
        for h in range(N_HEADS):
            gr = g_r[h:h + 1]
            beta_c = jnp.sum(eye * beta_r[h:h + 1], axis=1, keepdims=True)
            g_c = jnp.sum(eye * gr, axis=1, keepdims=True)
            gam_c = jnp.sum(tri * gr, axis=1, keepdims=True)
            gam_r = jnp.sum(tri_t * g_c, axis=0, keepdims=True)
            tot = jnp.sum(gr, axis=1, keepdims=True)
            dec_i = jnp.where(incl, jnp.exp(gam_c - gam_r), 0.0)
            vec[j, h] = (beta_c, gam_c, tot, dec_i)

    def tile(ref, j, h):
        return ref[j * CHUNK:(j + 1) * CHUNK, h * HEAD_W:(h + 1) * HEAD_W]

    ks = [tile(k_ref, *c) for c in chains]
    kbs = [k * vec[c][0] for k, c in zip(ks, chains)]
    a_low = [jnp.where(is_eye, 0.0, _dot_nt(kb, k) * vec[c][3]) for kb, k, c in zip(kbs, ks, chains)]
    a_qk = [_dot_nt(tile(q_ref, *c), k) * vec[c][3] for k, c in zip(ks, chains)]
    t_inv = _unit_tri_inverse(a_low, eye, bd)
    egc = [jnp.exp(vec[c][1]) for c in chains]
    sols = [_dot(t, jnp.concatenate([tile(v_ref, *c) * vec[c][0], kb * e], axis=1))
            for t, kb, e, c in zip(t_inv, kbs, egc, chains)]
    q_dec = [tile(q_ref, *c) * e for e, c in zip(egc, chains)]
    k_dec = [k * jnp.exp(vec[c][2] - vec[c][1]) for k, c in zip(ks, chains)]
    pre = {c: (sol[:, :HEAD_W], sol[:, HEAD_W:], aq, qd, kd, jnp.exp(vec[c][2]))
           for c, sol, aq, qd, kd in zip(chains, sols, a_qk, q_dec, k_dec)}

    states = [s_ref[h] for h in range(N_HEADS)]
    heads = range(N_HEADS)
    for j in (range(n_chunks - 1, -1, -1) if rev else range(n_chunks)):
        ws = [_dot(pre[j, h][1], states[h]) for h in heads]
        qs = [_dot(pre[j, h][3], states[h]) for h in heads]
        v_new = [pre[j, h][0] - ws[h] for h in heads]
        kv = [_dot_tn(pre[j, h][4], v_new[h]) for h in heads]
        av = [_dot(pre[j, h][2], v_new[h]) for h in heads]
        for h in heads:
            o_ref[j * CHUNK:(j + 1) * CHUNK, h * HEAD_W:(h + 1) * HEAD_W] = qs[h] + av[h]
        states = [states[h] * pre[j, h][5] + kv[h] for h in heads]
    for h in heads:
        s_ref[h] = states[h]


SCAN_ROWS = 256


def _gdn(qkv, ba_t, a_log, dt_bias, *, batch, seq, ctx_len, rev):
    n_all = seq // SCAN_ROWS
    n_ctx = ctx_len // SCAN_ROWS
    cpb = SCAN_ROWS // CHUNK

    def rows(b, s):
        return b * n_all + _block_order(s, n_ctx, n_all, rev)

    return pl.pallas_call(
        functools.partial(_gdn_kernel, rev=rev),
        out_shape=jax.ShapeDtypeStruct((batch * seq, MIX_W), F32),
        grid=(batch, n_all),
        in_specs=[
            pl.BlockSpec((SCAN_ROWS, MIX_W), lambda b, s: (rows(b, s), 0)),
            pl.BlockSpec((SCAN_ROWS, MIX_W), lambda b, s: (rows(b, s), 1)),
            pl.BlockSpec((SCAN_ROWS, MIX_W), lambda b, s: (rows(b, s), 2)),
            pl.BlockSpec((cpb, 2 * N_HEADS, CHUNK), lambda b, s: (rows(b, s), 0, 0)),
            pl.BlockSpec((N_HEADS, 1), lambda b, s: (0, 0)),
            pl.BlockSpec((N_HEADS, 1), lambda b, s: (0, 0)),
        ],
        out_specs=pl.BlockSpec((SCAN_ROWS, MIX_W), lambda b, s: (rows(b, s), 0)),
        scratch_shapes=[pltpu.VMEM((N_HEADS, HEAD_W, HEAD_W), F32)],
        compiler_params=_params(("parallel", "arbitrary")),
        name="gdn_rev" if rev else "gdn_fwd",
    )(qkv, qkv, qkv, ba_t, a_log, dt_bias)


def _hgrn2_kernel(q_ref, f_ref, i_ref, lb_ref, o_ref, s_ref, *, rev):
    @pl.when(pl.program_id(1) == 0)
    def _():
        s_ref[...] = jnp.zeros_like(s_ref)

    _, _, incl, _ = _chunk_masks(rev)
    tri = jnp.where(incl, 1.0, 0.0)
    row = lax.broadcasted_iota(jnp.int32, (CHUNK, 1), 0)
    n_sub = CHUNK // SUB
    n_chunks = q_ref.shape[0] // CHUNK

    def tile(ref, j, h):
        return ref[j * CHUNK:(j + 1) * CHUNK, h * HEAD_W:(h + 1) * HEAD_W]

    chains = [(j, h) for j in range(n_chunks) for h in range(N_HEADS)]
    qs_, ks_, lfs = [], [], []
    for j, h in chains:
        qr = tile(q_ref, j, h)
        lb = lb_ref[:, h * HEAD_W:(h + 1) * HEAD_W]
        f = lb + (1.0 - lb) * _sigmoid(tile(f_ref, j, h))
        qs_.append(qr * _sigmoid(qr))
        ks_.append(1.0 - f)
        lfs.append(jnp.log(f))
    gcs = _dot_3x_many(tri, lfs)
    tots = [jnp.sum(lf, axis=0, keepdims=True) for lf in lfs]
    blocks = []
    for q, k, gc in zip(qs_, ks_, gcs):
        row_blocks = []
        for i in range(n_sub):
            mid = i * SUB + SUB // 2
            gref = gc[mid:mid + 1, :]
            qsc = q[i * SUB:(i + 1) * SUB, :] * jnp.exp(gc[i * SUB:(i + 1) * SUB, :] - gref)
            reach = (row >= i * SUB) if rev else (row < (i + 1) * SUB)
            ksc = jnp.where(reach, k * jnp.exp(gref - gc), 0.0)
            row_blocks.append((qsc, ksc))
        blocks.append(row_blocks)
    scores = [jnp.where(incl, jnp.concatenate([_dot_nt(a, b) for a, b in rb], axis=0), 0.0) for rb in blocks]
    pre = {c: (sc, q * jnp.exp(gc), k * jnp.exp(tot - gc), jnp.exp(tot))
           for c, sc, q, k, gc, tot in zip(chains, scores, qs_, ks_, gcs, tots)}

    states = [s_ref[h] for h in range(N_HEADS)]
    heads = range(N_HEADS)
    for j in (range(n_chunks - 1, -1, -1) if rev else range(n_chunks)):
        vs = [tile(i_ref, j, h) for h in heads]
        qs = [_dot_nt(pre[j, h][1], states[h]) for h in heads]
        sv = [_dot(pre[j, h][0], vs[h]) for h in heads]
        kv = [_dot_tn(vs[h], pre[j, h][2]) for h in heads]
        for h in heads:
            o_ref[j * CHUNK:(j + 1) * CHUNK, h * HEAD_W:(h + 1) * HEAD_W] = qs[h] + sv[h]
        states = [states[h] * pre[j, h][3] + kv[h] for h in heads]
    for h in heads:
        s_ref[h] = states[h]


def _hgrn2(u, lb, *, batch, seq, ctx_len, rev):
    n_all = seq // SCAN_ROWS
    n_ctx = ctx_len // SCAN_ROWS

    def rows(b, s):
        return b * n_all + _block_order(s, n_ctx, n_all, rev)

    f_col = (C_HG_FB if rev else C_HG_FF) // MIX_W
    return pl.pallas_call(
        functools.partial(_hgrn2_kernel, rev=rev),
        out_shape=jax.ShapeDtypeStruct((batch * seq, MIX_W), F32),
        grid=(batch, n_all),
        in_specs=[
            pl.BlockSpec((SCAN_ROWS, MIX_W), lambda b, s: (rows(b, s), C_HG_Q // MIX_W)),
            pl.BlockSpec((SCAN_ROWS, MIX_W), lambda b, s: (rows(b, s), f_col)),
            pl.BlockSpec((SCAN_ROWS, MIX_W), lambda b, s: (rows(b, s), C_HG_I // MIX_W)),
            pl.BlockSpec((1, MIX_W), lambda b, s: (0, 0)),
        ],
        out_specs=pl.BlockSpec((SCAN_ROWS, MIX_W), lambda b, s: (rows(b, s), 0)),
        scratch_shapes=[pltpu.VMEM((N_HEADS, HEAD_W, HEAD_W), F32)],
        compiler_params=_params(("parallel", "arbitrary")),
        name="hgrn2_rev" if rev else "hgrn2_fwd",
    )(u, u, u, lb)


def _block_scan(a, b, rev):
    n = a.shape[0]
    row = lax.broadcasted_iota(jnp.int32, (n, 1), 0)
    s = 1
    while s < n:
        if rev:
            a_s, b_s, valid = pltpu.roll(a, n - s, 0), pltpu.roll(b, n - s, 0), row < n - s
        else:
            a_s, b_s, valid = pltpu.roll(a, s, 0), pltpu.roll(b, s, 0), row >= s
        b = jnp.where(valid, a * b_s + b, b)
        a = jnp.where(valid, a * a_s, a)
        s *= 2
    return a, b


def _gelu_tanh(x):
    return 0.5 * x * (1.0 + jnp.tanh(0.7978845608028654 * (x + 0.044715 * (x * x * x))))


def _lru_kernel(xb_ref, gb_ref, cw_ref, cb_ref, wa_ref, ba_ref, wx_ref, bx_ref, lam_ref, o_ref, xc_ref,
                *, ctx_len):
    n = xb_ref.shape[0]
    nblk = n // LRU_BLOCK
    nctx = ctx_len // LRU_BLOCK
    xc_ref[...] = _seg_conv(xb_ref[...], cw_ref, ctx_len) + cb_ref[...]

    def gates(xc, d):
        r = _sigmoid(_dot(xc, wa_ref[d]) + ba_ref[d])
        ig = _sigmoid(_dot(xc, wx_ref[d]) + bx_ref[d])
        log_a = -LRU_C * r * _softplus(-lam_ref[d])
        a = jnp.exp(log_a)
        return a, jnp.sqrt(1.0 - jnp.exp(2.0 * log_a)) * ig * xc

    def fwd(blk, carry):
        rows = pl.ds(pl.multiple_of(blk * LRU_BLOCK, LRU_BLOCK), LRU_BLOCK)
        a, b = _block_scan(*gates(xc_ref[rows, :], 0), rev=False)
        hb = a * carry + b
        o_ref[rows, :] = hb
        return hb[LRU_BLOCK - 1:LRU_BLOCK, :]

    def bwd(blk, carry):
        rows = pl.ds(pl.multiple_of(blk * LRU_BLOCK, LRU_BLOCK), LRU_BLOCK)
        a, b = _block_scan(*gates(xc_ref[rows, :], 1), rev=True)
        hb = a * carry + b
        o_ref[rows, :] = (o_ref[rows, :] + hb) * _gelu_tanh(gb_ref[rows, :])
        return hb[0:1, :]

    zero = jnp.zeros((1, HEAD_W), F32)
    lax.fori_loop(0, nblk, fwd, zero)
    carry = lax.fori_loop(0, nctx, lambda i, c: bwd(nctx - 1 - i, c), zero)
    lax.fori_loop(0, nblk - nctx, lambda i, c: bwd(nblk - 1 - i, c), carry)


def _lru(u, conv_w, conv_b, w_a, b_a, w_x, b_x, lam, *, batch, seq, ctx_len):
    vec = pl.BlockSpec((2, 1, HEAD_W), lambda b, h: (0, 0, h))
    mat = pl.BlockSpec((2, None, HEAD_W, HEAD_W), lambda b, h: (0, h, 0, 0))
    return pl.pallas_call(
        functools.partial(_lru_kernel, ctx_len=ctx_len),
        out_shape=jax.ShapeDtypeStruct((batch * seq, MIX_W), F32),
        grid=(batch, N_HEADS),
        in_specs=[
            pl.BlockSpec((seq, HEAD_W), lambda b, h: (b, C_LRU_X // HEAD_W + h)),
            pl.BlockSpec((seq, HEAD_W), lambda b, h: (b, C_LRU_G // HEAD_W + h)),
            pl.BlockSpec((4, HEAD_W), lambda b, h: (0, h)),
            pl.BlockSpec((1, HEAD_W), lambda b, h: (0, h)),
            mat, vec, mat, vec, vec,
        ],
        out_specs=pl.BlockSpec((seq, HEAD_W), lambda b, h: (b, h)),
        scratch_shapes=[pltpu.VMEM((seq, HEAD_W), F32)],
        compiler_params=_params(("parallel", "parallel")),
        name="rglru",
    )(u, u, conv_w, conv_b, w_a, b_a, w_x, b_x, lam)


def _att_prep_kernel(q_ref, k_ref, cos_ref, sin_ref, qw_ref, kw_ref, grp_ref, qo_ref, ko_ref):
    grp = grp_ref[...]
    cos, sin = cos_ref[...], sin_ref[...]
    lane = lax.broadcasted_iota(jnp.int32, (1, 128), 1)
    first = (lane % (ATT_HEAD_DIM // 2)) < (ATT_HEAD_DIM // 4)

    def norm_rope(x, w, scale):
        ss = _dot_x3(x * x, grp)
        y = x * lax.rsqrt(ss * (1.0 / ATT_HEAD_DIM) + EPS) * w
        rot = jnp.where(first, pltpu.roll(y, 128 - ATT_HEAD_DIM // 4, 1), pltpu.roll(y, ATT_HEAD_DIM // 4, 1))
        return (y * cos + rot * sin) * scale

    for s in range(q_ref.shape[1] // 128):
        sl = slice(s * 128, (s + 1) * 128)
        qo_ref[:, sl] = norm_rope(q_ref[:, sl], qw_ref[...], ATT_HEAD_DIM ** -0.5)
    ko_ref[...] = norm_rope(k_ref[...], kw_ref[...], 1.0)


def _att_prep(u, cos, sin, qw, kw, grp, *, seq):
    rows = u.shape[0]
    tm = ATT_TQ
    tiles_per_batch = seq // tm
    qw_cols = ATT_Q_HEADS * ATT_HEAD_DIM
    kw_cols = ATT_KV_HEADS * ATT_HEAD_DIM
    return pl.pallas_call(
        _att_prep_kernel,
        out_shape=(jax.ShapeDtypeStruct((rows, qw_cols), F32), jax.ShapeDtypeStruct((rows, kw_cols), F32)),
        grid=(rows // tm,),
        in_specs=[
            pl.BlockSpec((tm, qw_cols), lambda i: (i, C_ATT_Q // qw_cols)),
            pl.BlockSpec((tm, kw_cols), lambda i: (i, C_ATT_K // kw_cols)),
            pl.BlockSpec((tm, 128), lambda i: (i % tiles_per_batch, 0)),
            pl.BlockSpec((tm, 128), lambda i: (i % tiles_per_batch, 0)),
            pl.BlockSpec((1, 128), lambda i: (0, 0)),
            pl.BlockSpec((1, 128), lambda i: (0, 0)),
            pl.BlockSpec((128, 128), lambda i: (0, 0)),
        ],
        out_specs=(pl.BlockSpec((tm, qw_cols), lambda i: (i, 0)), pl.BlockSpec((tm, kw_cols), lambda i: (i, 0))),
        compiler_params=_params(("parallel",)),
        name="att_prep",
    )(u, u, cos, sin, qw, kw, grp)


def _att_kernel(q_ref, kt_ref, v_ref, o_ref, *, ctx_len):
    def attend(n_keys):
        kt = kt_ref[:, :n_keys].astype(BF16)
        v = v_ref[:n_keys, :].astype(BF16)
        for h in range(ATT_Q_HEADS // ATT_KV_HEADS):
            sl = slice(h * ATT_HEAD_DIM, (h + 1) * ATT_HEAD_DIM)
            s = jnp.dot(q_ref[:, sl].astype(BF16), kt, preferred_element_type=F32)
            p = jnp.exp(s - jnp.max(s, axis=-1, keepdims=True))
            den = jnp.sum(p, axis=-1, keepdims=True)
            o_ref[:, sl] = jnp.dot(p.astype(BF16), v, preferred_element_type=F32) / den

    is_ctx = pl.program_id(2) == 0

    @pl.when(is_ctx)
    def _():
        attend(ctx_len)

    @pl.when(jnp.logical_not(is_ctx))
    def _():
        attend(kt_ref.shape[1])


def _attention(qn, kt, vv, *, batch, seq, ctx_len):
    tiles = seq // ATT_TQ
    return pl.pallas_call(
        functools.partial(_att_kernel, ctx_len=ctx_len),
        out_shape=jax.ShapeDtypeStruct((batch * seq, ATT_Q_HEADS * ATT_HEAD_DIM), F32),
        grid=(batch, ATT_KV_HEADS, tiles),
        in_specs=[
            pl.BlockSpec((ATT_TQ, ATT_GROUP_W), lambda b, g, i: (b * tiles + i, g)),
            pl.BlockSpec((None, None, ATT_HEAD_DIM, seq), lambda b, g, i: (b, g, 0, 0)),
            pl.BlockSpec((None, None, seq, ATT_HEAD_DIM), lambda b, g, i: (b, g, 0, 0)),
        ],
        out_specs=pl.BlockSpec((ATT_TQ, ATT_GROUP_W), lambda b, g, i: (b * tiles + i, g)),
        compiler_params=_params(("parallel", "parallel", "arbitrary")),
        name="attention",
    )(qn, kt, vv)


def _merge_kernel(oaf_ref, oab_ref, obf_ref, obb_ref, ga_ref, gb_ref, yc_ref, yd_ref, gate_ref, h_ref, modc_ref,
                  modl_ref, dnw_ref, hgw_ref, wb_ref, wo_ref, out_ref, *, tiles_per_batch):
    def gated(of_ref, ob_ref, g_ref, nw_ref):
        o = of_ref[...] + ob_ref[...]
        ys = [_rms_rows(o[:, h * HEAD_W:(h + 1) * HEAD_W], nw_ref[...]) for h in range(N_HEADS)]
        g = g_ref[...]
        return jnp.concatenate(ys, axis=1) * (g * _sigmoid(g))

    ys = (gated(oaf_ref, oab_ref, ga_ref, dnw_ref), gated(obf_ref, obb_ref, gb_ref, hgw_ref), yc_ref[...], yd_ref[...])
    acc = None
    for b in range(N_BRANCH):
        term = _sigmoid(gate_ref[:, b * D_MODEL:(b + 1) * D_MODEL]) * _dot(ys[b], wb_ref[b])
        acc = term if acc is None else acc + term
    is_ctx = (pl.program_id(0) % tiles_per_batch) == 0
    out_ref[...] = h_ref[...] + _pick_mod(modc_ref, modl_ref, 2, is_ctx) * _dot(acc, wo_ref[...])


def _merge(oaf, oab, obf, obb, u, yc, yd, h, modc, modl, dnw, hgw, wb, wo, *, seq, ctx_len):
    rows = h.shape[0]
    tm = ctx_len
    tiles_per_batch = seq // tm
    mix = lambda c: pl.BlockSpec((tm, MIX_W), lambda i: (i, c // MIX_W))
    return pl.pallas_call(
        functools.partial(_merge_kernel, tiles_per_batch=tiles_per_batch),
        out_shape=jax.ShapeDtypeStruct((rows, D_MODEL), F32),
        grid=(rows // tm,),
        in_specs=[
            mix(0), mix(0), mix(0), mix(0), mix(C_DN_G), mix(C_HG_G), mix(0), mix(0),
            pl.BlockSpec((tm, N_BRANCH * D_MODEL), lambda i: (i, 0)),
            pl.BlockSpec((tm, D_MODEL), lambda i: (i, 0)),
            pl.BlockSpec((6, D_MODEL), lambda i: (0, 0)),
            pl.BlockSpec((None, 6, D_MODEL), lambda i: (i // tiles_per_batch, 0, 0)),
            pl.BlockSpec((1, HEAD_W), lambda i: (0, 0)),
            pl.BlockSpec((1, HEAD_W), lambda i: (0, 0)),
            pl.BlockSpec((N_BRANCH, MIX_W, D_MODEL), lambda i: (0, 0, 0)),
            pl.BlockSpec((D_MODEL, D_MODEL), lambda i: (0, 0)),
        ],
        out_specs=pl.BlockSpec((tm, D_MODEL), lambda i: (i, 0)),
        compiler_params=_params(("parallel",)),
        name="merge",
    )(oaf, oab, obf, obb, u, u, yc, yd, u, h, modc, modl, dnw, hgw, wb, wo)


def _mlp_kernel(h_ref, modc_ref, modl_ref, nw_ref, w1_ref, w2_ref, o_ref, z_ref, acc_ref,
                *, tm, tiles_per_batch, ctx_len):
    j = pl.program_id(1)
    row = (pl.program_id(0) % tiles_per_batch) * tm + lax.broadcasted_iota(jnp.int32, (tm, 1), 0)
    is_ctx = row < ctx_len

    @pl.when(j == 0)
    def _():
        y = _rms_rows(h_ref[...], nw_ref[...])
        shift = _pick_mod(modc_ref, modl_ref, 3, is_ctx)
        scale = _pick_mod(modc_ref, modl_ref, 4, is_ctx)
        z_ref[...] = (y * (1.0 + scale) + shift).astype(BF16)
        acc_ref[...] = jnp.zeros_like(acc_ref)

    a = jnp.maximum(jnp.dot(z_ref[...], w1_ref[...], preferred_element_type=F32), 0.0)
    acc_ref[...] += _dot(a * a, w2_ref[...])

    @pl.when(j == pl.num_programs(1) - 1)
    def _():
        o_ref[...] = h_ref[...] + _pick_mod(modc_ref, modl_ref, 5, is_ctx) * acc_ref[...]


def _mlp(h, modc, modl, nw, w1, w2, *, seq, ctx_len):
    rows = h.shape[0]
    tiles_per_batch = 4
    tm = seq // tiles_per_batch
    kern = functools.partial(_mlp_kernel, tm=tm, tiles_per_batch=tiles_per_batch, ctx_len=ctx_len)
    return pl.pallas_call(
        kern,
        out_shape=jax.ShapeDtypeStruct((rows, D_MODEL), F32),
        grid=(rows // tm, D_FF // FF_BLOCK),
        in_specs=[
            pl.BlockSpec((tm, D_MODEL), lambda i, j: (i, 0)),
            pl.BlockSpec((6, D_MODEL), lambda i, j: (0, 0)),
            pl.BlockSpec((None, 6, D_MODEL), lambda i, j: (i // tiles_per_batch, 0, 0)),
            pl.BlockSpec((1, D_MODEL), lambda i, j: (0, 0)),
            pl.BlockSpec((D_MODEL, FF_BLOCK), lambda i, j: (0, j)),
            pl.BlockSpec((FF_BLOCK, D_MODEL), lambda i, j: (j, 0)),
        ],
        out_specs=pl.BlockSpec((tm, D_MODEL), lambda i, j: (i, 0)),
        scratch_shapes=[pltpu.VMEM((tm, D_MODEL), BF16), pltpu.VMEM((tm, D_MODEL), F32)],
        compiler_params=_params(("parallel", "arbitrary")),
        name="mlp",
    )(h, modc, modl, nw, w1, w2)


def _reorder_w_in(w_in):
    o = np.cumsum([0, 512, 512, 512, 512, 8, 8, 512, 512, 512, 512, 512, 512, 512, 512, 128, 128, 4096])
    dn_q, dn_g, dn_ba, hg, lru, att, gate, end = o[0], o[3], o[4], o[6], o[11], o[13], o[16], o[17]
    parts = [w_in[..., gate:end], w_in[..., hg:lru], w_in[..., dn_q:dn_g], w_in[..., lru:att],
             w_in[..., dn_g:dn_ba], w_in[..., att:gate], w_in[..., dn_ba:hg]]
    w = jnp.concatenate(parts, axis=-1)
    return jnp.pad(w, ((0, 0), (0, 0), (0, N_U - w.shape[-1]))).astype(BF16)


def _rope_tables(t_len, ctx_len):
    rows = t_len // GRID_W
    row_id = jnp.repeat(jnp.arange(rows), GRID_W).astype(F32)
    col_id = jnp.tile(jnp.arange(GRID_W), rows).astype(F32)
    axis_dim = ATT_HEAD_DIM // 2
    inv = ROPE_THETA ** (-jnp.arange(0, axis_dim, 2, dtype=F32) / axis_dim)
    ang = jnp.stack([row_id[:, None] * inv, col_id[:, None] * inv], axis=1)
    cos, sin = jnp.cos(ang), jnp.sin(ang)
    cos_h = jnp.concatenate([cos, cos], axis=-1).reshape(t_len, ATT_HEAD_DIM)
    sin_h = jnp.concatenate([-sin, sin], axis=-1).reshape(t_len, ATT_HEAD_DIM)
    cos_t = jnp.concatenate([jnp.ones((ctx_len, ATT_HEAD_DIM), F32), cos_h], axis=0)
    sin_t = jnp.concatenate([jnp.zeros((ctx_len, ATT_HEAD_DIM), F32), sin_h], axis=0)
    return jnp.tile(cos_t, (1, 2)), jnp.tile(sin_t, (1, 2))


def _hgrn2_lower_bounds(p):
    sm = jax.nn.softmax(p.astype(F32), axis=1)
    cs = jnp.cumsum(sm, axis=1)
    return cs - cs[:, :1]


def kernel(x, c, ctx, c_ctx, mod_w, mod_b, norm1_w, norm2_w, w_in, dn_conv_w, dn_a_log, dn_dt_bias, dn_norm_w,
           hg_lower_bounds, hg_norm_w, lru_conv_w, lru_conv_b, lru_w_a, lru_b_a, lru_w_x, lru_b_x, lru_lambda,
           att_q_norm_w, att_k_norm_w, w_branch, w_out, mlp_w1, mlp_w2):
    batch, t_len, _ = x.shape
    ctx_len = ctx.shape[1]
    depth = mod_w.shape[0]
    seq = ctx_len + t_len
    n_chunks = seq // CHUNK
    dims = dict(batch=batch, seq=seq, ctx_len=ctx_len)

    w_in_r = _reorder_w_in(w_in)
    w_branch_b, w_out_b = w_branch.astype(BF16), w_out.astype(BF16)
    w1_b, w2_b = mlp_w1.astype(BF16), mlp_w2.astype(BF16)
    lru_wa_b, lru_wx_b = lru_w_a.astype(BF16), lru_w_x.astype(BF16)
    lb_all = _hgrn2_lower_bounds(hg_lower_bounds)
    cos_t, sin_t = _rope_tables(t_len, ctx_len)
    grp = jnp.asarray((np.arange(128)[:, None] // ATT_HEAD_DIM) == (np.arange(128)[None, :] // ATT_HEAD_DIM), F32)

    n_mod_rows = batch + 8
    cc = jnp.zeros((n_mod_rows, D_MODEL), F32).at[:batch].set(c).at[batch].set(c_ctx)
    mods = _modulations(cc, mod_w.astype(BF16), mod_b)

    h = jnp.concatenate([ctx, x], axis=1).reshape(batch * seq, D_MODEL)
    for l in range(depth):
        modl = mods[l, :batch].reshape(batch, 6, D_MODEL)
        modc = mods[l, batch].reshape(6, D_MODEL)
        u = _inproj(h, modc, modl, norm1_w[l].reshape(1, D_MODEL), w_in_r[l], seq=seq, ctx_len=ctx_len)

        qkv = _dn_prep(u, dn_conv_w[l], **dims)
        ba = u[:, C_DN_BA:C_DN_BA + 4 * N_HEADS].reshape(batch * n_chunks, CHUNK, 2, 2, N_HEADS)
        ba_t = jnp.transpose(ba, (3, 0, 2, 4, 1)).reshape(2, batch * n_chunks, 2 * N_HEADS, CHUNK)
        oa = [_gdn(qkv, ba_t[d], dn_a_log[l, d].reshape(N_HEADS, 1), dn_dt_bias[l, d].reshape(N_HEADS, 1),
                   rev=bool(d), **dims) for d in range(2)]

        ob = [_hgrn2(u, lb_all[d, l].reshape(1, MIX_W), rev=bool(d), **dims) for d in range(2)]

        yc = _lru(u, lru_conv_w[l], lru_conv_b[l].reshape(1, MIX_W), lru_wa_b[l], lru_b_a[l].reshape(2, 1, MIX_W),
                  lru_wx_b[l], lru_b_x[l].reshape(2, 1, MIX_W), lru_lambda[l].reshape(2, 1, MIX_W), **dims)

        qn, kn = _att_prep(u, cos_t, sin_t, jnp.tile(att_q_norm_w[l], 2).reshape(1, 128),
                           jnp.tile(att_k_norm_w[l], 2).reshape(1, 128), grp, seq=seq)
        kt = jnp.transpose(kn.reshape(batch, seq, ATT_KV_HEADS, ATT_HEAD_DIM), (0, 2, 3, 1))
        vv = jnp.transpose(u[:, C_ATT_V:C_ATT_V + ATT_KV_HEADS * ATT_HEAD_DIM]
                           .reshape(batch, seq, ATT_KV_HEADS, ATT_HEAD_DIM), (0, 2, 1, 3))
        yd = _attention(qn, kt, vv, **dims)

        h = _merge(oa[0], oa[1], ob[0], ob[1], u, yc, yd, h, modc, modl, dn_norm_w[l].reshape(1, HEAD_W),
                   hg_norm_w[l].reshape(1, HEAD_W), w_branch_b[l], w_out_b[l], seq=seq, ctx_len=ctx_len)
        h = _mlp(h, modc, modl, norm2_w[l].reshape(1, D_MODEL), w1_b[l], w2_b[l], seq=seq, ctx_len=ctx_len)

    return h.reshape(batch, seq, D_MODEL)[:, ctx_len:]
```

```python
import functools

import jax
import jax.numpy as jnp
import numpy as np
from jax import lax
from jax.experimental import pallas as pl
from jax.experimental.pallas import tpu as pltpu

F32 = jnp.float32
BF16 = jnp.bfloat16

EPS = 1e-6
D_MODEL = 1024
GRID_W = 64
N_HEADS = 4
HEAD_W = 128
MIX_W = N_HEADS * HEAD_W
CHUNK = 64
SUB = 16
SCAN_ROWS = 256
LRU_C = 8.0
LRU_BLOCK = 256
ATT_Q_HEADS = 8
ATT_KV_HEADS = 2
ATT_HEAD_DIM = 64
ATT_GROUP = ATT_Q_HEADS // ATT_KV_HEADS
ATT_QW = ATT_Q_HEADS * ATT_HEAD_DIM
ATT_KW = ATT_KV_HEADS * ATT_HEAD_DIM
ATT_TQ = 256
ROPE_THETA = 10000.0
LOG2_E = 1.4426950408889634
N_BRANCH = 4
D_FF = 4 * D_MODEL
FF_BLOCK = 1024

UB_GATE = 0
UB_HG_Q, UB_HG_I, UB_HG_G = 4096, 4608, 5120
UB_DN_Q = 5632
UB_LRU_X, UB_LRU_G = 7168, 7680
UB_DN_G = 8192
UB_ATT_Q, UB_ATT_K, UB_ATT_V = 8704, 9216, 9344
N_UB_USED = 9472
TN_UB = 1664
N_UB = 6 * TN_UB
UF_HG_FF, UF_HG_FB, UF_DN_BA = 0, 512, 1024
N_UF = 1280

VMEM_LIMIT_V7X = 48 * 1024 * 1024


def _params(sem, vmem=VMEM_LIMIT_V7X):
    return pltpu.CompilerParams(dimension_semantics=sem, vmem_limit_bytes=vmem)


def _sigmoid(x):
    return 1.0 / (1.0 + jnp.exp(-x))


def _softplus(x):
    return jnp.maximum(x, 0.0) + jnp.log1p(jnp.exp(-jnp.abs(x)))


def _dot(a, b):
    return jnp.dot(a.astype(BF16), b.astype(BF16), preferred_element_type=F32)


def _dot_nt(a, b):
    return lax.dot_general(a.astype(BF16), b.astype(BF16), (((1,), (1,)), ((), ())),
                           preferred_element_type=F32)


def _dot_tn(a, b):
    return lax.dot_general(a.astype(BF16), b.astype(BF16), (((0,), (0,)), ((), ())),
                           preferred_element_type=F32)


def _split3(x):
    hi = x.astype(BF16)
    r1 = x - hi.astype(F32)
    mid = r1.astype(BF16)
    lo = (r1 - mid.astype(F32)).astype(BF16)
    return hi, mid, lo


def _dot_x3(x, m):
    m = m.astype(BF16)
    hi, mid, lo = _split3(x)
    out = jnp.dot(hi, m, preferred_element_type=F32)
    out = out + jnp.dot(mid, m, preferred_element_type=F32)
    return out + jnp.dot(lo, m, preferred_element_type=F32)


def _dot_3x_many(m, xs):
    w = xs[0].shape[1]
    pieces = [_split3(x) for x in xs]
    cols = [p[i] for i in range(3) for p in pieces]
    wide = jnp.dot(m.astype(BF16), jnp.concatenate(cols, axis=1), preferred_element_type=F32)
    n = len(xs)
    part = lambda p, i: wide[:, (p * n + i) * w:(p * n + i + 1) * w]
    return [part(0, i) + part(1, i) + part(2, i) for i in range(n)]


def _rms_rows(x, w):
    return x * lax.rsqrt(jnp.mean(x * x, axis=-1, keepdims=True) + EPS) * w


def _mod_kernel(c_ref, w_ref, b_ref, o_ref):
    c = c_ref[...]
    o_ref[...] = _dot(c * _sigmoid(c), w_ref[...]) + b_ref[...]


def _modulations(cc, mod_w, mod_b):
    depth = mod_w.shape[0]
    rows = cc.shape[0]
    n_out = mod_w.shape[2]
    return pl.pallas_call(
        _mod_kernel,
        out_shape=jax.ShapeDtypeStruct((depth, rows, n_out), F32),
        grid=(depth, n_out // D_MODEL),
        in_specs=[
            pl.BlockSpec((rows, D_MODEL), lambda l, j: (0, 0)),
            pl.BlockSpec((None, D_MODEL, D_MODEL), lambda l, j: (l, 0, j)),
            pl.BlockSpec((None, 1, D_MODEL), lambda l, j: (l, 0, j)),
        ],
        out_specs=pl.BlockSpec((None, rows, D_MODEL), lambda l, j: (l, 0, j)),
        compiler_params=_params(("parallel", "parallel")),
        name="modulations",
    )(cc, mod_w, mod_b.reshape(depth, 1, n_out))


def _pick_mod(modc_ref, modl_ref, idx, is_ctx):
    return jnp.where(is_ctx, modc_ref[idx:idx + 1, :], modl_ref[idx:idx + 1, :])


def _inproj_kernel(h_ref, modc_ref, modl_ref, nw_ref, w_ref, o_ref, xn_ref, *, tm, tiles_per_batch, ctx_len):
    @pl.when(pl.program_id(1) == 0)
    def _():
        y = _rms_rows(h_ref[...], nw_ref[...])
        row = (pl.program_id(0) % tiles_per_batch) * tm + lax.broadcasted_iota(jnp.int32, (tm, 1), 0)
        is_ctx = row < ctx_len
        shift = _pick_mod(modc_ref, modl_ref, 0, is_ctx)
        scale = _pick_mod(modc_ref, modl_ref, 1, is_ctx)
        xn_ref[...] = (y * (1.0 + scale) + shift).astype(BF16)

    o_ref[...] = jnp.dot(xn_ref[...], w_ref[...], preferred_element_type=F32).astype(o_ref.dtype)


def _inproj(h, modc, modl, nw, w, *, seq, ctx_len, tn, out_dtype, name):
    rows = h.shape[0]
    n_out = w.shape[1]
    tiles_per_batch = 4
    tm = seq // tiles_per_batch
    kern = functools.partial(_inproj_kernel, tm=tm, tiles_per_batch=tiles_per_batch, ctx_len=ctx_len)
    return pl.pallas_call(
        kern,
        out_shape=jax.ShapeDtypeStruct((rows, n_out), out_dtype),
        grid=(rows // tm, n_out // tn),
        in_specs=[
            pl.BlockSpec((tm, D_MODEL), lambda i, j: (i, 0)),
            pl.BlockSpec((6, D_MODEL), lambda i, j: (0, 0)),
            pl.BlockSpec((None, 6, D_MODEL), lambda i, j: (i // tiles_per_batch, 0, 0)),
            pl.BlockSpec((1, D_MODEL), lambda i, j: (0, 0)),
            pl.BlockSpec((D_MODEL, tn), lambda i, j: (0, j)),
        ],
        out_specs=pl.BlockSpec((tm, tn), lambda i, j: (i, j)),
        scratch_shapes=[pltpu.VMEM((tm, D_MODEL), BF16)],
        compiler_params=_params(("parallel", "arbitrary")),
        name=name,
    )(h, modc, modl, nw, w)


def _seg_conv(x, w_ref, ctx_len):
    n = x.shape[0]
    row = lax.broadcasted_iota(jnp.int32, (n, 1), 0)
    lo = jnp.where(row >= ctx_len, ctx_len, 0)
    hi = jnp.where(row >= ctx_len, n, ctx_len)

    def tap(k):
        tk = row + k
        valid = jnp.logical_and(tk >= lo, tk < hi)
        return jnp.where(valid, pltpu.roll(x, (-k) % n, 0), 0.0)

    return (tap(-1) * w_ref[0:1, :] + x * w_ref[1:2, :] + tap(1) * w_ref[2:3, :] + tap(2) * w_ref[3:4, :])


def _dn_prep_kernel(u_ref, w_ref, o_ref, *, ctx_len):
    j = pl.program_id(1)
    y = _seg_conv(u_ref[...].astype(F32), w_ref, ctx_len)
    y = y * _sigmoid(y)
    n = lax.rsqrt(jnp.sum(y * y, axis=-1, keepdims=True) + EPS)
    fac = jnp.where(j < N_HEADS, n * (HEAD_W ** -0.5), jnp.where(j < 2 * N_HEADS, n, 1.0))
    o_ref[...] = (y * fac).astype(o_ref.dtype)


def _dn_prep(ub, conv_w, *, batch, seq, ctx_len):
    nblk = 3 * N_HEADS
    return pl.pallas_call(
        functools.partial(_dn_prep_kernel, ctx_len=ctx_len),
        out_shape=jax.ShapeDtypeStruct((batch * seq, 3 * MIX_W), BF16),
        grid=(batch, nblk),
        in_specs=[
            pl.BlockSpec((seq, HEAD_W), lambda b, j: (b, UB_DN_Q // HEAD_W + j)),
            pl.BlockSpec((4, HEAD_W), lambda b, j: (0, j)),
        ],
        out_specs=pl.BlockSpec((seq, HEAD_W), lambda b, j: (b, j)),
        compiler_params=_params(("parallel", "parallel")),
        name="dn_prep",
    )(ub, conv_w)


def _block_order(s, n_ctx, n_all, rev):
    if not rev:
        return s
    return jnp.where(s < n_ctx, n_ctx - 1 - s, n_all + n_ctx - 1 - s)


def _chunk_masks(rev):
    ii = lax.broadcasted_iota(jnp.int32, (CHUNK, CHUNK), 0)
    jj = lax.broadcasted_iota(jnp.int32, (CHUNK, CHUNK), 1)
    incl = (ii <= jj) if rev else (ii >= jj)
    incl_t = (ii >= jj) if rev else (ii <= jj)
    return ii, jj, incl, incl_t


def _unit_tri_inverse(a_all, eye, bd):
    ad = [jnp.where(bd, a, 0.0) for a in a_all]
    ao = [a - d for a, d in zip(a_all, ad)]
    p = [-d for d in ad]
    dinv = [eye + x for x in p]
    for _ in range(3):
        p = [_dot(x, x) for x in p]
        dinv = [d + _dot(d, x) for d, x in zip(dinv, p)]
    m = [-_dot(d, o) for d, o in zip(dinv, ao)]
    mm = [_dot(x, x) for x in m]
    t = [eye + x for x in m]
    t = [x + _dot(x, y) for x, y in zip(t, mm)]
    return [_dot(x, d) for x, d in zip(t, dinv)]


def _gdn_kernel(q_ref, k_ref, v_ref, ba_ref, alog_ref, dtb_ref, o_ref, s_ref, *, rev):
    @pl.when(pl.program_id(1) == 0)
    def _():
        s_ref[...] = jnp.zeros_like(s_ref)

    ii, jj, incl, incl_t = _chunk_masks(rev)
    is_eye = ii == jj
    eye = jnp.where(is_eye, 1.0, 0.0)
    tri = jnp.where(incl, 1.0, 0.0)
    tri_t = jnp.where(incl_t, 1.0, 0.0)
    bd = (ii // SUB) == (jj // SUB)
    n_chunks = q_ref.shape[0] // CHUNK

    alog_neg = -jnp.exp(alog_ref[...])
    dtb = dtb_ref[...]
    chains = [(j, h) for j in range(n_chunks) for h in range(N_HEADS)]
    vec = {}
    for j in range(n_chunks):
        ba = ba_ref[j]
        beta_r = _sigmoid(ba[0:N_HEADS])
        g_r = alog_neg * _softplus(ba[N_HEADS:2 * N_HEADS] + dtb)
        for h in range(N_HEADS):
            gr = g_r[h:h + 1]
            beta_c = jnp.sum(eye * beta_r[h:h + 1], axis=1, keepdims=True)
            g_c = jnp.sum(eye * gr, axis=1, keepdims=True)
            gam_c = jnp.sum(tri * gr, axis=1, keepdims=True)
            gam_r = jnp.sum(tri_t * g_c, axis=0, keepdims=True)
            tot = jnp.sum(gr, axis=1, keepdims=True)
            dec_i = jnp.where(incl, jnp.exp(gam_c - gam_r), 0.0)
            vec[j, h] = (beta_c, gam_c, tot, dec_i)

    def tile(ref, j, h):
        return ref[j * CHUNK:(j + 1) * CHUNK, h * HEAD_W:(h + 1) * HEAD_W]

    kb16 = [tile(k_ref, *c) for c in chains]
    ks = [k.astype(F32) for k in kb16]
    kbs = [k * vec[c][0] for k, c in zip(ks, chains)]
    a_low = [jnp.where(is_eye, 0.0, _dot_nt(kb, k) * vec[c][3]) for kb, k, c in zip(kbs, kb16, chains)]
    a_qk = [_dot_nt(tile(q_ref, *c), k) * vec[c][3] for k, c in zip(kb16, chains)]
    t_inv = _unit_tri_inverse(a_low, eye, bd)
    egc = [jnp.exp(vec[c][1]) for c in chains]
    sols = [_dot(t, jnp.concatenate([tile(v_ref, *c).astype(F32) * vec[c][0], kb * e], axis=1))
            for t, kb, e, c in zip(t_inv, kbs, egc, chains)]
    q_dec = [tile(q_ref, *c).astype(F32) * e for e, c in zip(egc, chains)]
    k_dec = [k * jnp.exp(vec[c][2] - vec[c][1]) for k, c in zip(ks, chains)]
    pre = {c: (sol[:, :HEAD_W], sol[:, HEAD_W:], aq, qd, kd, jnp.exp(vec[c][2]))
           for c, sol, aq, qd, kd in zip(chains, sols, a_qk, q_dec, k_dec)}

    states = [s_ref[h] for h in range(N_HEADS)]
    heads = range(N_HEADS)
    for j in (range(n_chunks - 1, -1, -1) if rev else range(n_chunks)):
        ws = [_dot(pre[j, h][1], states[h]) for h in heads]
        qs = [_dot(pre[j, h][3], states[h]) for h in heads]
        v_new = [pre[j, h][0] - ws[h] for h in heads]
        kv = [_dot_tn(pre[j, h][4], v_new[h]) for h in heads]
        av = [_dot(pre[j, h][2], v_new[h]) for h in heads]
        for h in heads:
            o_ref[j * CHUNK:(j + 1) * CHUNK, h * HEAD_W:(h + 1) * HEAD_W] = qs[h] + av[h]
        states = [states[h] * pre[j, h][5] + kv[h] for h in heads]
    for h in heads:
        s_ref[h] = states[h]


def _gdn(qkv, ba_t, a_log, dt_bias, *, batch, seq, ctx_len, rev):
    n_all = seq // SCAN_ROWS
    n_ctx = ctx_len // SCAN_ROWS
    cpb = SCAN_ROWS // CHUNK

    def rows(b, s):
        return b * n_all + _block_order(s, n_ctx, n_all, rev)

    return pl.pallas_call(
        functools.partial(_gdn_kernel, rev=rev),
        out_shape=jax.ShapeDtypeStruct((batch * seq, MIX_W), F32),
        grid=(batch, n_all),
        in_specs=[
            pl.BlockSpec((SCAN_ROWS, MIX_W), lambda b, s: (rows(b, s), 0)),
            pl.BlockSpec((SCAN_ROWS, MIX_W), lambda b, s: (rows(b, s), 1)),
            pl.BlockSpec((SCAN_ROWS, MIX_W), lambda b, s: (rows(b, s), 2)),
            pl.BlockSpec((cpb, 2 * N_HEADS, CHUNK), lambda b, s: (rows(b, s), 0, 0)),
            pl.BlockSpec((N_HEADS, 1), lambda b, s: (0, 0)),
            pl.BlockSpec((N_HEADS, 1), lambda b, s: (0, 0)),
        ],
        out_specs=pl.BlockSpec((SCAN_ROWS, MIX_W), lambda b, s: (rows(b, s), 0)),
        scratch_shapes=[pltpu.VMEM((N_HEADS, HEAD_W, HEAD_W), F32)],
        compiler_params=_params(("parallel", "arbitrary")),
        name="gdn_rev" if rev else "gdn_fwd",
    )(qkv, qkv, qkv, ba_t, a_log, dt_bias)


def _hgrn2_kernel(q_ref, f_ref, i_ref, lb_ref, o_ref, s_ref, *, rev):
    @pl.when(pl.program_id(1) == 0)
    def _():
        s_ref[...] = jnp.zeros_like(s_ref)

    _, _, incl, _ = _chunk_masks(rev)
    tri = jnp.where(incl, 1.0, 0.0)
    row = lax.broadcasted_iota(jnp.int32, (CHUNK, 1), 0)
    n_sub = CHUNK // SUB
    n_chunks = q_ref.shape[0] // CHUNK

    def tile(ref, j, h):
        return ref[j * CHUNK:(j + 1) * CHUNK, h * HEAD_W:(h + 1) * HEAD_W]

    chains = [(j, h) for j in range(n_chunks) for h in range(N_HEADS)]
    qs_, ks_, lfs = [], [], []
    for j, h in chains:
        qr = tile(q_ref, j, h).astype(F32)
        lb = lb_ref[:, h * HEAD_W:(h + 1) * HEAD_W]
        f = lb + (1.0 - lb) * _sigmoid(tile(f_ref, j, h))
        qs_.append(qr * _sigmoid(qr))
        ks_.append(1.0 - f)
        lfs.append(jnp.log(f))
    gcs = _dot_3x_many(tri, lfs)
    tots = [jnp.sum(lf, axis=0, keepdims=True) for lf in lfs]
    blocks = []
    for q, k, gc in zip(qs_, ks_, gcs):
        row_blocks = []
        for i in range(n_sub):
            mid = i * SUB + SUB // 2
            gref = gc[mid:mid + 1, :]
            qsc = q[i * SUB:(i + 1) * SUB, :] * jnp.exp(gc[i * SUB:(i + 1) * SUB, :] - gref)
            reach = (row >= i * SUB) if rev else (row < (i + 1) * SUB)
            ksc = jnp.where(reach, k * jnp.exp(gref - gc), 0.0)
            row_blocks.append((qsc, ksc))
        blocks.append(row_blocks)
    scores = [jnp.where(incl, jnp.concatenate([_dot_nt(a, b) for a, b in rb], axis=0), 0.0) for rb in blocks]
    pre = {c: (sc, q * jnp.exp(gc), k * jnp.exp(tot - gc), jnp.exp(tot))
           for c, sc, q, k, gc, tot in zip(chains, scores, qs_, ks_, gcs, tots)}

    states = [s_ref[h] for h in range(N_HEADS)]
    heads = range(N_HEADS)
    for j in (range(n_chunks - 1, -1, -1) if rev else range(n_chunks)):
        vs = [tile(i_ref, j, h) for h in heads]
        qs = [_dot_nt(pre[j, h][1], states[h]) for h in heads]
        sv = [_dot(pre[j, h][0], vs[h]) for h in heads]
        kv = [_dot_tn(vs[h], pre[j, h][2]) for h in heads]
        for h in heads:
            o_ref[j * CHUNK:(j + 1) * CHUNK, h * HEAD_W:(h + 1) * HEAD_W] = qs[h] + sv[h]
        states = [states[h] * pre[j, h][3] + kv[h] for h in heads]
    for h in heads:
        s_ref[h] = states[h]


def _hgrn2(ub, uf, lb, *, batch, seq, ctx_len, rev):
    n_all = seq // SCAN_ROWS
    n_ctx = ctx_len // SCAN_ROWS

    def rows(b, s):
        return b * n_all + _block_order(s, n_ctx, n_all, rev)

    f_col = (UF_HG_FB if rev else UF_HG_FF) // MIX_W
    return pl.pallas_call(
        functools.partial(_hgrn2_kernel, rev=rev),
        out_shape=jax.ShapeDtypeStruct((batch * seq, MIX_W), F32),
        grid=(batch, n_all),
        in_specs=[
            pl.BlockSpec((SCAN_ROWS, MIX_W), lambda b, s: (rows(b, s), UB_HG_Q // MIX_W)),
            pl.BlockSpec((SCAN_ROWS, MIX_W), lambda b, s: (rows(b, s), f_col)),
            pl.BlockSpec((SCAN_ROWS, MIX_W), lambda b, s: (rows(b, s), UB_HG_I // MIX_W)),
            pl.BlockSpec((1, MIX_W), lambda b, s: (0, 0)),
        ],
        out_specs=pl.BlockSpec((SCAN_ROWS, MIX_W), lambda b, s: (rows(b, s), 0)),
        scratch_shapes=[pltpu.VMEM((N_HEADS, HEAD_W, HEAD_W), F32)],
        compiler_params=_params(("parallel", "arbitrary")),
        name="hgrn2_rev" if rev else "hgrn2_fwd",
    )(ub, uf, ub, lb)


def _block_scan(a, b, rev):
    n = a.shape[0]
    row = lax.broadcasted_iota(jnp.int32, (n, 1), 0)
    s = 1
    while s < n:
        if rev:
            a_s, b_s, valid = pltpu.roll(a, n - s, 0), pltpu.roll(b, n - s, 0), row < n - s
        else:
            a_s, b_s, valid = pltpu.roll(a, s, 0), pltpu.roll(b, s, 0), row >= s
        b = jnp.where(valid, a * b_s + b, b)
        a = jnp.where(valid, a * a_s, a)
        s *= 2
    return a, b


def _gelu_tanh(x):
    return 0.5 * x * (1.0 + jnp.tanh(0.7978845608028654 * (x + 0.044715 * (x * x * x))))


def _lru_kernel(xb_ref, gb_ref, cw_ref, cb_ref, wa_ref, ba_ref, wx_ref, bx_ref, lam_ref, o_ref, xc_ref, acc_ref,
                *, ctx_len):
    n = xb_ref.shape[0]
    nblk = n // LRU_BLOCK
    nctx = ctx_len // LRU_BLOCK
    xc_ref[...] = _seg_conv(xb_ref[...].astype(F32), cw_ref, ctx_len) + cb_ref[...]

    def gates(xc, d):
        r = _sigmoid(_dot(xc, wa_ref[d]) + ba_ref[d])
        ig = _sigmoid(_dot(xc, wx_ref[d]) + bx_ref[d])
        log_a = -LRU_C * r * _softplus(-lam_ref[d])
        a = jnp.exp(log_a)
        return a, jnp.sqrt(1.0 - jnp.exp(2.0 * log_a)) * ig * xc

    def fwd(blk, carry):
        rows = pl.ds(pl.multiple_of(blk * LRU_BLOCK, LRU_BLOCK), LRU_BLOCK)
        a, b = _block_scan(*gates(xc_ref[rows, :], 0), rev=False)
        hb = a * carry + b
        acc_ref[rows, :] = hb
        return hb[LRU_BLOCK - 1:LRU_BLOCK, :]

    def bwd(blk, carry):
        rows = pl.ds(pl.multiple_of(blk * LRU_BLOCK, LRU_BLOCK), LRU_BLOCK)
        a, b = _block_scan(*gates(xc_ref[rows, :], 1), rev=True)
        hb = a * carry + b
        o_ref[rows, :] = ((acc_ref[rows, :] + hb) * _gelu_tanh(gb_ref[rows, :].astype(F32))).astype(o_ref.dtype)
        return hb[0:1, :]

    zero = jnp.zeros((1, HEAD_W), F32)
    lax.fori_loop(0, nblk, fwd, zero)
    carry = lax.fori_loop(0, nctx, lambda i, c: bwd(nctx - 1 - i, c), zero)
    lax.fori_loop(0, nblk - nctx, lambda i, c: bwd(nblk - 1 - i, c), carry)


def _lru(ub, conv_w, conv_b, w_a, b_a, w_x, b_x, lam, *, batch, seq, ctx_len):
    vec = pl.BlockSpec((2, 1, HEAD_W), lambda b, h: (0, 0, h))
    mat = pl.BlockSpec((2, None, HEAD_W, HEAD_W), lambda b, h: (0, h, 0, 0))
    return pl.pallas_call(
        functools.partial(_lru_kernel, ctx_len=ctx_len),
        out_shape=jax.ShapeDtypeStruct((batch * seq, MIX_W), BF16),
        grid=(batch, N_HEADS),
        in_specs=[
            pl.BlockSpec((seq, HEAD_W), lambda b, h: (b, UB_LRU_X // HEAD_W + h)),
            pl.BlockSpec((seq, HEAD_W), lambda b, h: (b, UB_LRU_G // HEAD_W + h)),
            pl.BlockSpec((4, HEAD_W), lambda b, h: (0, h)),
            pl.BlockSpec((1, HEAD_W), lambda b, h: (0, h)),
            mat, vec, mat, vec, vec,
        ],
        out_specs=pl.BlockSpec((seq, HEAD_W), lambda b, h: (b, h)),
        scratch_shapes=[pltpu.VMEM((seq, HEAD_W), F32), pltpu.VMEM((seq, HEAD_W), F32)],
        compiler_params=_params(("parallel", "parallel")),
        name="rglru",
    )(ub, ub, conv_w, conv_b, w_a, b_a, w_x, b_x, lam)


def _att_prep_kernel(q_ref, k_ref, cos_ref, sin_ref, qw_ref, kw_ref, grp_ref, qo_ref, kto_ref):
    grp = grp_ref[...]
    cos, sin = cos_ref[...], sin_ref[...]
    lane = lax.broadcasted_iota(jnp.int32, (1, 128), 1)
    first = (lane % (ATT_HEAD_DIM // 2)) < (ATT_HEAD_DIM // 4)

    def norm_rope(x, w, scale):
        ss = _dot_x3(x * x, grp)
        y = x * lax.rsqrt(ss * (1.0 / ATT_HEAD_DIM) + EPS) * w
        rot = jnp.where(first, pltpu.roll(y, 128 - ATT_HEAD_DIM // 4, 1), pltpu.roll(y, ATT_HEAD_DIM // 4, 1))
        return (y * cos + rot * sin) * scale

    q_scale = (ATT_HEAD_DIM ** -0.5) * LOG2_E
    for s in range(q_ref.shape[1] // 128):
        sl = slice(s * 128, (s + 1) * 128)
        qo_ref[:, sl] = norm_rope(q_ref[:, sl].astype(F32), qw_ref[...], q_scale).astype(qo_ref.dtype)
    kto_ref[...] = norm_rope(k_ref[...].astype(F32), kw_ref[...], 1.0).T.astype(kto_ref.dtype)


def _att_prep(ub, cos, sin, qw, kw, grp, *, batch, seq):
    rows = ub.shape[0]
    tm = ATT_TQ
    tpb = seq // tm
    return pl.pallas_call(
        _att_prep_kernel,
        out_shape=(jax.ShapeDtypeStruct((rows, ATT_QW), BF16),
                   jax.ShapeDtypeStruct((batch, ATT_KW, seq), BF16)),
        grid=(rows // tm,),
        in_specs=[
            pl.BlockSpec((tm, ATT_QW), lambda i: (i, UB_ATT_Q // ATT_QW)),
            pl.BlockSpec((tm, ATT_KW), lambda i: (i, UB_ATT_K // ATT_KW)),
            pl.BlockSpec((tm, 128), lambda i: (i % tpb, 0)),
            pl.BlockSpec((tm, 128), lambda i: (i % tpb, 0)),
            pl.BlockSpec((1, 128), lambda i: (0, 0)),
            pl.BlockSpec((1, 128), lambda i: (0, 0)),
            pl.BlockSpec((128, 128), lambda i: (0, 0)),
        ],
        out_specs=(pl.BlockSpec((tm, ATT_QW), lambda i: (i, 0)),
                   pl.BlockSpec((None, ATT_KW, tm), lambda i: (i // tpb, 0, i % tpb))),
        compiler_params=_params(("parallel",)),
        name="att_prep",
    )(ub, ub, cos, sin, qw, kw, grp)


def _att_kernel(q_ref, kt_ref, v_ref, o_ref, *, ctx_len):
    def attend(n_keys):
        v = v_ref[:n_keys, :]
        for g in range(ATT_KV_HEADS):
            kt = kt_ref[g * ATT_HEAD_DIM:(g + 1) * ATT_HEAD_DIM, :n_keys]
            for h in range(ATT_GROUP):
                head = g * ATT_GROUP + h
                sl = slice(head * ATT_HEAD_DIM, (head + 1) * ATT_HEAD_DIM)
                s = jnp.dot(q_ref[:, sl], kt, preferred_element_type=F32)
                p = jnp.exp2(s - jnp.max(s, axis=-1, keepdims=True))
                den = jnp.sum(p, axis=-1, keepdims=True)
                pv = jnp.dot(p.astype(BF16), v, preferred_element_type=F32)
                o_ref[:, sl] = (pv[:, g * ATT_HEAD_DIM:(g + 1) * ATT_HEAD_DIM] / den).astype(o_ref.dtype)

    is_ctx = pl.program_id(1) == 0

    @pl.when(is_ctx)
    def _():
        attend(ctx_len)

    @pl.when(jnp.logical_not(is_ctx))
    def _():
        attend(kt_ref.shape[1])


def _attention(qn, kt, ub, *, batch, seq, ctx_len):
    tiles = seq // ATT_TQ
    return pl.pallas_call(
        functools.partial(_att_kernel, ctx_len=ctx_len),
        out_shape=jax.ShapeDtypeStruct((batch * seq, ATT_QW), BF16),
        grid=(batch, tiles),
        in_specs=[
            pl.BlockSpec((ATT_TQ, ATT_QW), lambda b, i: (b * tiles + i, 0)),
            pl.BlockSpec((None, ATT_KW, seq), lambda b, i: (b, 0, 0)),
            pl.BlockSpec((seq, ATT_KW), lambda b, i: (b, UB_ATT_V // ATT_KW)),
        ],
        out_specs=pl.BlockSpec((ATT_TQ, ATT_QW), lambda b, i: (b * tiles + i, 0)),
        compiler_params=_params(("parallel", "arbitrary")),
        name="attention",
    )(qn, kt, ub)


def _merge_kernel(oaf_ref, oab_ref, obf_ref, obb_ref, ga_ref, gb_ref, yc_ref, yd_ref, gate_ref, h_ref, modc_ref,
                  modl_ref, dnw_ref, hgw_ref, wb_ref, wo_ref, out_ref, *, tiles_per_batch):
    def gated(of_ref, ob_ref, g_ref, nw_ref):
        o = of_ref[...] + ob_ref[...]
        ys = [_rms_rows(o[:, h * HEAD_W:(h + 1) * HEAD_W], nw_ref[...]) for h in range(N_HEADS)]
        g = g_ref[...].astype(F32)
        return jnp.concatenate(ys, axis=1) * (g * _sigmoid(g))

    ys = (gated(oaf_ref, oab_ref, ga_ref, dnw_ref), gated(obf_ref, obb_ref, gb_ref, hgw_ref), yc_ref[...], yd_ref[...])
    acc = None
    for b in range(N_BRANCH):
        gate = gate_ref[:, b * D_MODEL:(b + 1) * D_MODEL].astype(F32)
        term = _sigmoid(gate) * _dot(ys[b], wb_ref[b])
        acc = term if acc is None else acc + term
    is_ctx = (pl.program_id(0) % tiles_per_batch) == 0
    out_ref[...] = h_ref[...] + _pick_mod(modc_ref, modl_ref, 2, is_ctx) * _dot(acc, wo_ref[...])


def _merge(oaf, oab, obf, obb, ub, yc, yd, h, modc, modl, dnw, hgw, wb, wo, *, seq, ctx_len):
    rows = h.shape[0]
    tm = ctx_len
    tiles_per_batch = seq // tm
    mix = lambda c: pl.BlockSpec((tm, MIX_W), lambda i: (i, c // MIX_W))
    return pl.pallas_call(
        functools.partial(_merge_kernel, tiles_per_batch=tiles_per_batch),
        out_shape=jax.ShapeDtypeStruct((rows, D_MODEL), F32),
        grid=(rows // tm,),
        in_specs=[
            mix(0), mix(0), mix(0), mix(0), mix(UB_DN_G), mix(UB_HG_G), mix(0), mix(0),
            pl.BlockSpec((tm, N_BRANCH * D_MODEL), lambda i: (i, 0)),
            pl.BlockSpec((tm, D_MODEL), lambda i: (i, 0)),
            pl.BlockSpec((6, D_MODEL), lambda i: (0, 0)),
            pl.BlockSpec((None, 6, D_MODEL), lambda i: (i // tiles_per_batch, 0, 0)),
            pl.BlockSpec((1, HEAD_W), lambda i: (0, 0)),
            pl.BlockSpec((1, HEAD_W), lambda i: (0, 0)),
            pl.BlockSpec((N_BRANCH, MIX_W, D_MODEL), lambda i: (0, 0, 0)),
            pl.BlockSpec((D_MODEL, D_MODEL), lambda i: (0, 0)),
        ],
        out_specs=pl.BlockSpec((tm, D_MODEL), lambda i: (i, 0)),
        compiler_params=_params(("parallel",)),
        name="merge",
    )(oaf, oab, obf, obb, ub, ub, yc, yd, ub, h, modc, modl, dnw, hgw, wb, wo)


def _mlp_kernel(h_ref, modc_ref, modl_ref, nw_ref, w1_ref, w2_ref, o_ref, z_ref, acc_ref,
                *, tm, tiles_per_batch, ctx_len):
    j = pl.program_id(1)
    row = (pl.program_id(0) % tiles_per_batch) * tm + lax.broadcasted_iota(jnp.int32, (tm, 1), 0)
    is_ctx = row < ctx_len

    @pl.when(j == 0)
    def _():
        y = _rms_rows(h_ref[...], nw_ref[...])
        shift = _pick_mod(modc_ref, modl_ref, 3, is_ctx)
        scale = _pick_mod(modc_ref, modl_ref, 4, is_ctx)
        z_ref[...] = (y * (1.0 + scale) + shift).astype(BF16)
        acc_ref[...] = jnp.zeros_like(acc_ref)

    a = jnp.maximum(jnp.dot(z_ref[...], w1_ref[...], preferred_element_type=F32), 0.0)
    acc_ref[...] += _dot(a * a, w2_ref[...])

    @pl.when(j == pl.num_programs(1) - 1)
    def _():
        o_ref[...] = h_ref[...] + _pick_mod(modc_ref, modl_ref, 5, is_ctx) * acc_ref[...]


def _mlp(h, modc, modl, nw, w1, w2, *, seq, ctx_len):
    rows = h.shape[0]
    tiles_per_batch = 4
    tm = seq // tiles_per_batch
    kern = functools.partial(_mlp_kernel, tm=tm, tiles_per_batch=tiles_per_batch, ctx_len=ctx_len)
    return pl.pallas_call(
        kern,
        out_shape=jax.ShapeDtypeStruct((rows, D_MODEL), F32),
        grid=(rows // tm, D_FF // FF_BLOCK),
        in_specs=[
            pl.BlockSpec((tm, D_MODEL), lambda i, j: (i, 0)),
            pl.BlockSpec((6, D_MODEL), lambda i, j: (0, 0)),
            pl.BlockSpec((None, 6, D_MODEL), lambda i, j: (i // tiles_per_batch, 0, 0)),
            pl.BlockSpec((1, D_MODEL), lambda i, j: (0, 0)),
            pl.BlockSpec((D_MODEL, FF_BLOCK), lambda i, j: (0, j)),
            pl.BlockSpec((FF_BLOCK, D_MODEL), lambda i, j: (j, 0)),
        ],
        out_specs=pl.BlockSpec((tm, D_MODEL), lambda i, j: (i, 0)),
        scratch_shapes=[pltpu.VMEM((tm, D_MODEL), BF16), pltpu.VMEM((tm, D_MODEL), F32)],
        compiler_params=_params(("parallel", "arbitrary")),
        name="mlp",
    )(h, modc, modl, nw, w1, w2)


def _regroup_w_in(w_in):
    o = np.cumsum([0, 512, 512, 512, 512, 8, 8, 512, 512, 512, 512, 512, 512, 512, 512, 128, 128, 4096])
    cols = lambda a, b: w_in[..., o[a]:o[b]]
    wb = jnp.concatenate([cols(16, 17), cols(6, 7), cols(9, 11), cols(0, 3), cols(11, 13), cols(3, 4), cols(13, 16)],
                         axis=-1)
    wf = jnp.concatenate([cols(7, 9), cols(4, 6)], axis=-1)
    pad = lambda w, n: jnp.pad(w, ((0, 0), (0, 0), (0, n - w.shape[-1]))).astype(BF16)
    return pad(wb, N_UB), pad(wf, N_UF)


def _rope_tables(t_len, ctx_len):
    rows = t_len // GRID_W
    row_id = jnp.repeat(jnp.arange(rows), GRID_W).astype(F32)
    col_id = jnp.tile(jnp.arange(GRID_W), rows).astype(F32)
    axis_dim = ATT_HEAD_DIM // 2
    inv = ROPE_THETA ** (-jnp.arange(0, axis_dim, 2, dtype=F32) / axis_dim)
    ang = jnp.stack([row_id[:, None] * inv, col_id[:, None] * inv], axis=1)
    cos, sin = jnp.cos(ang), jnp.sin(ang)
    cos_h = jnp.concatenate([cos, cos], axis=-1).reshape(t_len, ATT_HEAD_DIM)
    sin_h = jnp.concatenate([-sin, sin], axis=-1).reshape(t_len, ATT_HEAD_DIM)
    cos_t = jnp.concatenate([jnp.ones((ctx_len, ATT_HEAD_DIM), F32), cos_h], axis=0)
    sin_t = jnp.concatenate([jnp.zeros((ctx_len, ATT_HEAD_DIM), F32), sin_h], axis=0)
    return jnp.tile(cos_t, (1, 2)), jnp.tile(sin_t, (1, 2))


def _hgrn2_lower_bounds(p):
    sm = jax.nn.softmax(p.astype(F32), axis=1)
    cs = jnp.cumsum(sm, axis=1)
    return cs - cs[:, :1]


def kernel(x, c, ctx, c_ctx, mod_w, mod_b, norm1_w, norm2_w, w_in, dn_conv_w, dn_a_log, dn_dt_bias, dn_norm_w,
           hg_lower_bounds, hg_norm_w, lru_conv_w, lru_conv_b, lru_w_a, lru_b_a, lru_w_x, lru_b_x, lru_lambda,
           att_q_norm_w, att_k_norm_w, w_branch, w_out, mlp_w1, mlp_w2):
    batch, t_len, _ = x.shape
    ctx_len = ctx.shape[1]
    depth = mod_w.shape[0]
    seq = ctx_len + t_len
    n_chunks = seq // CHUNK
    assert ctx_len % SCAN_ROWS == 0 and t_len % SCAN_ROWS == 0 and ctx_len == ATT_TQ
    dims = dict(batch=batch, seq=seq, ctx_len=ctx_len)

    w_ub, w_uf = _regroup_w_in(w_in)
    w_branch_b, w_out_b = w_branch.astype(BF16), w_out.astype(BF16)
    w1_b, w2_b = mlp_w1.astype(BF16), mlp_w2.astype(BF16)
    lru_wa_b, lru_wx_b = lru_w_a.astype(BF16), lru_w_x.astype(BF16)
    lb_all = _hgrn2_lower_bounds(hg_lower_bounds)
    cos_t, sin_t = _rope_tables(t_len, ctx_len)
    grp = jnp.asarray((np.arange(128)[:, None] // ATT_HEAD_DIM) == (np.arange(128)[None, :] // ATT_HEAD_DIM), F32)

    n_mod_rows = batch + 8
    cc = jnp.zeros((n_mod_rows, D_MODEL), F32).at[:batch].set(c).at[batch].set(c_ctx)
    mods = _modulations(cc, mod_w.astype(BF16), mod_b)

    h = jnp.concatenate([ctx, x], axis=1).reshape(batch * seq, D_MODEL)
    for l in range(depth):
        modl = mods[l, :batch].reshape(batch, 6, D_MODEL)
        modc = mods[l, batch].reshape(6, D_MODEL)
        nw1 = norm1_w[l].reshape(1, D_MODEL)
        ub = _inproj(h, modc, modl, nw1, w_ub[l], seq=seq, ctx_len=ctx_len, tn=TN_UB, out_dtype=BF16, name="inproj_b")
        uf = _inproj(h, modc, modl, nw1, w_uf[l], seq=seq, ctx_len=ctx_len, tn=N_UF, out_dtype=F32, name="inproj_f")

        qkv = _dn_prep(ub, dn_conv_w[l], **dims)
        ba = uf[:, UF_DN_BA:UF_DN_BA + 4 * N_HEADS].reshape(batch * n_chunks, CHUNK, 2, 2, N_HEADS)
        ba_t = jnp.transpose(ba, (3, 0, 2, 4, 1)).reshape(2, batch * n_chunks, 2 * N_HEADS, CHUNK)
        oa = [_gdn(qkv, ba_t[d], dn_a_log[l, d].reshape(N_HEADS, 1), dn_dt_bias[l, d].reshape(N_HEADS, 1),
                   rev=bool(d), **dims) for d in range(2)]

        ob = [_hgrn2(ub, uf, lb_all[d, l].reshape(1, MIX_W), rev=bool(d), **dims) for d in range(2)]

        yc = _lru(ub, lru_conv_w[l], lru_conv_b[l].reshape(1, MIX_W), lru_wa_b[l], lru_b_a[l].reshape(2, 1, MIX_W),
                  lru_wx_b[l], lru_b_x[l].reshape(2, 1, MIX_W), lru_lambda[l].reshape(2, 1, MIX_W), **dims)

        qn, kt = _att_prep(ub, cos_t, sin_t, jnp.tile(att_q_norm_w[l], 2).reshape(1, 128),
                           jnp.tile(att_k_norm_w[l], 2).reshape(1, 128), grp, batch=batch, seq=seq)
        yd = _attention(qn, kt, ub, **dims)

        h = _merge(oa[0], oa[1], ob[0], ob[1], ub, yc, yd, h, modc, modl, dn_norm_w[l].reshape(1, HEAD_W),
                   hg_norm_w[l].reshape(1, HEAD_W), w_branch_b[l], w_out_b[l], seq=seq, ctx_len=ctx_len)
        h = _mlp(h, modc, modl, norm2_w[l].reshape(1, D_MODEL), w1_b[l], w2_b[l], seq=seq, ctx_len=ctx_len)

    return h.reshape(batch, seq, D_MODEL)[:, ctx_len:]
```

```python
import functools

import jax
import jax.numpy as jnp
import numpy as np
from jax import lax
from jax.experimental import pallas as pl
from jax.experimental.pallas import tpu as pltpu

F32 = jnp.float32
BF16 = jnp.bfloat16

EPS = 1e-6
D_MODEL = 1024
GRID_W = 64
N_HEADS = 4
HEAD_W = 128
MIX_W = N_HEADS * HEAD_W
CHUNK = 64
SUB = 16
SCAN_ROWS = 256
LRU_C = 8.0
LRU_BLOCK = 256
ATT_Q_HEADS = 8
ATT_KV_HEADS = 2
ATT_HEAD_DIM = 64
ATT_GROUP = ATT_Q_HEADS // ATT_KV_HEADS
ATT_QW = ATT_Q_HEADS * ATT_HEAD_DIM
ATT_KW = ATT_KV_HEADS * ATT_HEAD_DIM
ATT_TQ = 256
ROPE_THETA = 10000.0
LOG2_E = 1.4426950408889634
N_BRANCH = 4
D_FF = 4 * D_MODEL
FF_BLOCK = 1024

UB_GATE = 0
UB_HG_Q, UB_HG_I, UB_HG_G = 4096, 4608, 5120
UB_DN_Q = 5632
UB_LRU_X, UB_LRU_G = 7168, 7680
UB_DN_G = 8192
UB_ATT_Q, UB_ATT_K, UB_ATT_V = 8704, 9216, 9344
N_UB_USED = 9472
TN_UB = 1664
N_UB = 6 * TN_UB
UF_HG_FF, UF_HG_FB, UF_DN_BA = 0, 512, 1024
N_UF = 1280

VMEM_LIMIT_V7X = 48 * 1024 * 1024


def _params(sem, vmem=VMEM_LIMIT_V7X):
    return pltpu.CompilerParams(dimension_semantics=sem, vmem_limit_bytes=vmem)


def _sigmoid(x):
    return 1.0 / (1.0 + jnp.exp(-x))


def _softplus(x):
    return jnp.maximum(x, 0.0) + jnp.log1p(jnp.exp(-jnp.abs(x)))


def _dot(a, b):
    return jnp.dot(a.astype(BF16), b.astype(BF16), preferred_element_type=F32)


def _dot_nt(a, b):
    return lax.dot_general(a.astype(BF16), b.astype(BF16), (((1,), (1,)), ((), ())),
                           preferred_element_type=F32)


def _dot_tn(a, b):
    return lax.dot_general(a.astype(BF16), b.astype(BF16), (((0,), (0,)), ((), ())),
                           preferred_element_type=F32)


def _split3(x):
    hi = x.astype(BF16)
    r1 = x - hi.astype(F32)
    mid = r1.astype(BF16)
    lo = (r1 - mid.astype(F32)).astype(BF16)
    return hi, mid, lo


def _dot_x3(x, m):
    m = m.astype(BF16)
    hi, mid, lo = _split3(x)
    out = jnp.dot(hi, m, preferred_element_type=F32)
    out = out + jnp.dot(mid, m, preferred_element_type=F32)
    return out + jnp.dot(lo, m, preferred_element_type=F32)


def _dot_3x_many(m, xs):
    w = xs[0].shape[1]
    pieces = [_split3(x) for x in xs]
    cols = [p[i] for i in range(3) for p in pieces]
    wide = jnp.dot(m.astype(BF16), jnp.concatenate(cols, axis=1), preferred_element_type=F32)
    n = len(xs)
    part = lambda p, i: wide[:, (p * n + i) * w:(p * n + i + 1) * w]
    return [part(0, i) + part(1, i) + part(2, i) for i in range(n)]


def _rms_rows(x, w):
    return x * lax.rsqrt(jnp.mean(x * x, axis=-1, keepdims=True) + EPS) * w


def _mod_kernel(c_ref, w_ref, b_ref, o_ref):
    c = c_ref[...]
    o_ref[...] = _dot(c * _sigmoid(c), w_ref[...]) + b_ref[...]


def _modulations(cc, mod_w, mod_b):
    depth = mod_w.shape[0]
    rows = cc.shape[0]
    n_out = mod_w.shape[2]
    return pl.pallas_call(
        _mod_kernel,
        out_shape=jax.ShapeDtypeStruct((depth, rows, n_out), F32),
        grid=(depth, n_out // D_MODEL),
        in_specs=[
            pl.BlockSpec((rows, D_MODEL), lambda l, j: (0, 0)),
            pl.BlockSpec((None, D_MODEL, D_MODEL), lambda l, j: (l, 0, j)),
            pl.BlockSpec((None, 1, D_MODEL), lambda l, j: (l, 0, j)),
        ],
        out_specs=pl.BlockSpec((None, rows, D_MODEL), lambda l, j: (l, 0, j)),
        compiler_params=_params(("parallel", "parallel")),
        name="modulations",
    )(cc, mod_w, mod_b.reshape(depth, 1, n_out))


def _pick_mod(modc_ref, modl_ref, idx, is_ctx):
    return jnp.where(is_ctx, modc_ref[idx:idx + 1, :], modl_ref[idx:idx + 1, :])


def _inproj_kernel(h_ref, modc_ref, modl_ref, nw_ref, w_ref, o_ref, xn_ref, *, tm, tiles_per_batch, ctx_len):
    @pl.when(pl.program_id(1) == 0)
    def _():
        y = _rms_rows(h_ref[...], nw_ref[...])
        row = (pl.program_id(0) % tiles_per_batch) * tm + lax.broadcasted_iota(jnp.int32, (tm, 1), 0)
        is_ctx = row < ctx_len
        shift = _pick_mod(modc_ref, modl_ref, 0, is_ctx)
        scale = _pick_mod(modc_ref, modl_ref, 1, is_ctx)
        xn_ref[...] = (y * (1.0 + scale) + shift).astype(BF16)

    o_ref[...] = jnp.dot(xn_ref[...], w_ref[...], preferred_element_type=F32).astype(o_ref.dtype)


def _inproj(h, modc, modl, nw, w, *, seq, ctx_len, tn, out_dtype, name):
    rows = h.shape[0]
    n_out = w.shape[1]
    tiles_per_batch = 4
    tm = seq // tiles_per_batch
    kern = functools.partial(_inproj_kernel, tm=tm, tiles_per_batch=tiles_per_batch, ctx_len=ctx_len)
    return pl.pallas_call(
        kern,
        out_shape=jax.ShapeDtypeStruct((rows, n_out), out_dtype),
        grid=(rows // tm, n_out // tn),
        in_specs=[
            pl.BlockSpec((tm, D_MODEL), lambda i, j: (i, 0)),
            pl.BlockSpec((6, D_MODEL), lambda i, j: (0, 0)),
            pl.BlockSpec((None, 6, D_MODEL), lambda i, j: (i // tiles_per_batch, 0, 0)),
            pl.BlockSpec((1, D_MODEL), lambda i, j: (0, 0)),
            pl.BlockSpec((D_MODEL, tn), lambda i, j: (0, j)),
        ],
        out_specs=pl.BlockSpec((tm, tn), lambda i, j: (i, j)),
        scratch_shapes=[pltpu.VMEM((tm, D_MODEL), BF16)],
        compiler_params=_params(("parallel", "arbitrary")),
        name=name,
    )(h, modc, modl, nw, w)


def _seg_conv(x, w_ref, ctx_len):
    n = x.shape[0]
    row = lax.broadcasted_iota(jnp.int32, (n, 1), 0)
    lo = jnp.where(row >= ctx_len, ctx_len, 0)
    hi = jnp.where(row >= ctx_len, n, ctx_len)

    def tap(k):
        tk = row + k
        valid = jnp.logical_and(tk >= lo, tk < hi)
        return jnp.where(valid, pltpu.roll(x, (-k) % n, 0), 0.0)

    return (tap(-1) * w_ref[0:1, :] + x * w_ref[1:2, :] + tap(1) * w_ref[2:3, :] + tap(2) * w_ref[3:4, :])


def _dn_prep_kernel(u_ref, w_ref, o_ref, *, ctx_len):
    j = pl.program_id(1)
    y = _seg_conv(u_ref[...].astype(F32), w_ref, ctx_len)
    y = y * _sigmoid(y)
    n = lax.rsqrt(jnp.sum(y * y, axis=-1, keepdims=True) + EPS)
    fac = jnp.where(j < N_HEADS, n * (HEAD_W ** -0.5), jnp.where(j < 2 * N_HEADS, n, 1.0))
    o_ref[...] = (y * fac).astype(o_ref.dtype)


def _dn_prep(ub, conv_w, *, batch, seq, ctx_len):
    nblk = 3 * N_HEADS
    return pl.pallas_call(
        functools.partial(_dn_prep_kernel, ctx_len=ctx_len),
        out_shape=jax.ShapeDtypeStruct((batch * seq, 3 * MIX_W), BF16),
        grid=(batch, nblk),
        in_specs=[
            pl.BlockSpec((seq, HEAD_W), lambda b, j: (b, UB_DN_Q // HEAD_W + j)),
            pl.BlockSpec((4, HEAD_W), lambda b, j: (0, j)),
        ],
        out_specs=pl.BlockSpec((seq, HEAD_W), lambda b, j: (b, j)),
        compiler_params=_params(("parallel", "parallel")),
        name="dn_prep",
    )(ub, conv_w)


def _block_order(s, n_ctx, n_all, rev):
    if not rev:
        return s
    return jnp.where(s < n_ctx, n_ctx - 1 - s, n_all + n_ctx - 1 - s)


def _chunk_masks(rev):
    ii = lax.broadcasted_iota(jnp.int32, (CHUNK, CHUNK), 0)
    jj = lax.broadcasted_iota(jnp.int32, (CHUNK, CHUNK), 1)
    incl = (ii <= jj) if rev else (ii >= jj)
    incl_t = (ii >= jj) if rev else (ii <= jj)
    return ii, jj, incl, incl_t


def _unit_tri_inverse(a_all, eye, bd):
    ad = [jnp.where(bd, a, 0.0) for a in a_all]
    ao = [a - d for a, d in zip(a_all, ad)]
    p = [-d for d in ad]
    dinv = [eye + x for x in p]
    for _ in range(3):
        p = [_dot(x, x) for x in p]
        dinv = [d + _dot(d, x) for d, x in zip(dinv, p)]
    m = [-_dot(d, o) for d, o in zip(dinv, ao)]
    mm = [_dot(x, x) for x in m]
    t = [eye + x for x in m]
    t = [x + _dot(x, y) for x, y in zip(t, mm)]
    return [_dot(x, d) for x, d in zip(t, dinv)]


def _gdn_kernel(q_ref, k_ref, v_ref, ba_ref, alog_ref, dtb_ref, o_ref, s_ref, *, rev):
    @pl.when(pl.program_id(1) == 0)
    def _():
        s_ref[...] = jnp.zeros_like(s_ref)

    ii, jj, incl, incl_t = _chunk_masks(rev)
    is_eye = ii == jj
    eye = jnp.where(is_eye, 1.0, 0.0)
    tri = jnp.where(incl, 1.0, 0.0)
    tri_t = jnp.where(incl_t, 1.0, 0.0)
    bd = (ii // SUB) == (jj // SUB)
    n_chunks = q_ref.shape[0] // CHUNK

    alog_neg = -jnp.exp(alog_ref[...])
    dtb = dtb_ref[...]
    chains = [(j, h) for j in range(n_chunks) for h in range(N_HEADS)]
    vec = {}
    for j in range(n_chunks):
        ba = ba_ref[j]
        beta_r = _sigmoid(ba[0:N_HEADS])
        g_r = alog_neg * _softplus(ba[N_HEADS:2 * N_HEADS] + dtb)
        for h in range(N_HEADS):
            gr = g_r[h:h + 1]
            beta_c = jnp.sum(eye * beta_r[h:h + 1], axis=1, keepdims=True)
            g_c = jnp.sum(eye * gr, axis=1, keepdims=True)
            gam_c = jnp.sum(tri * gr, axis=1, keepdims=True)
            gam_r = jnp.sum(tri_t * g_c, axis=0, keepdims=True)
            tot = jnp.sum(gr, axis=1, keepdims=True)
            dec_i = jnp.where(incl, jnp.exp(gam_c - gam_r), 0.0)
            vec[j, h] = (beta_c, gam_c, tot, dec_i)

    def tile(ref, j, h):
        return ref[j * CHUNK:(j + 1) * CHUNK, h * HEAD_W:(h + 1) * HEAD_W]

    k16 = [tile(k_ref, *c) for c in chains]
    q16 = [tile(q_ref, *c) for c in chains]
    ks = [k.astype(F32) for k in k16]
    kbs = [k * vec[c][0] for k, c in zip(ks, chains)]
    kq = [_dot_nt(jnp.concatenate([kb.astype(BF16), q], axis=0), k) for kb, q, k in zip(kbs, q16, k16)]
    a_low = [jnp.where(is_eye, 0.0, x[:CHUNK] * vec[c][3]) for x, c in zip(kq, chains)]
    a_qk = [x[CHUNK:] * vec[c][3] for x, c in zip(kq, chains)]
    t_inv = _unit_tri_inverse(a_low, eye, bd)
    egc = [jnp.exp(vec[c][1]) for c in chains]
    sols = [_dot(t, jnp.concatenate([tile(v_ref, *c).astype(F32) * vec[c][0], kb * e], axis=1))
            for t, kb, e, c in zip(t_inv, kbs, egc, chains)]
    pre = {}
    for c, sol, aq, q, k, e in zip(chains, sols, a_qk, q16, ks, egc):
        q_dec = q.astype(F32) * e
        k_dec_t = (k * jnp.exp(vec[c][2] - vec[c][1])).T
        pre[c] = (sol[:, :HEAD_W], jnp.concatenate([sol[:, HEAD_W:], q_dec], axis=0).astype(BF16),
                  jnp.concatenate([aq, k_dec_t], axis=0).astype(BF16), jnp.exp(vec[c][2]))

    states = [s_ref[h] for h in range(N_HEADS)]
    heads = range(N_HEADS)
    for j in (range(n_chunks - 1, -1, -1) if rev else range(n_chunks)):
        wq = [_dot(pre[j, h][1], states[h]) for h in heads]
        v_new = [pre[j, h][0] - wq[h][:CHUNK] for h in heads]
        ak = [_dot(pre[j, h][2], v_new[h]) for h in heads]
        for h in heads:
            o_ref[j * CHUNK:(j + 1) * CHUNK, h * HEAD_W:(h + 1) * HEAD_W] = wq[h][CHUNK:] + ak[h][:CHUNK]
        states = [states[h] * pre[j, h][3] + ak[h][CHUNK:] for h in heads]
    for h in heads:
        s_ref[h] = states[h]


def _gdn(qkv, ba_t, a_log, dt_bias, *, batch, seq, ctx_len, rev):
    n_all = seq // SCAN_ROWS
    n_ctx = ctx_len // SCAN_ROWS
    cpb = SCAN_ROWS // CHUNK

    def rows(b, s):
        return b * n_all + _block_order(s, n_ctx, n_all, rev)

    return pl.pallas_call(
        functools.partial(_gdn_kernel, rev=rev),
        out_shape=jax.ShapeDtypeStruct((batch * seq, MIX_W), F32),
        grid=(batch, n_all),
        in_specs=[
            pl.BlockSpec((SCAN_ROWS, MIX_W), lambda b, s: (rows(b, s), 0)),
            pl.BlockSpec((SCAN_ROWS, MIX_W), lambda b, s: (rows(b, s), 1)),
            pl.BlockSpec((SCAN_ROWS, MIX_W), lambda b, s: (rows(b, s), 2)),
            pl.BlockSpec((cpb, 2 * N_HEADS, CHUNK), lambda b, s: (rows(b, s), 0, 0)),
            pl.BlockSpec((N_HEADS, 1), lambda b, s: (0, 0)),
            pl.BlockSpec((N_HEADS, 1), lambda b, s: (0, 0)),
        ],
        out_specs=pl.BlockSpec((SCAN_ROWS, MIX_W), lambda b, s: (rows(b, s), 0)),
        scratch_shapes=[pltpu.VMEM((N_HEADS, HEAD_W, HEAD_W), F32)],
        compiler_params=_params(("parallel", "arbitrary")),
        name="gdn_rev" if rev else "gdn_fwd",
    )(qkv, qkv, qkv, ba_t, a_log, dt_bias)


def _hgrn2_kernel(q_ref, f_ref, i_ref, lb_ref, o_ref, s_ref, *, rev):
    @pl.when(pl.program_id(1) == 0)
    def _():
        s_ref[...] = jnp.zeros_like(s_ref)

    _, _, incl, _ = _chunk_masks(rev)
    tri = jnp.where(incl, 1.0, 0.0)
    row = lax.broadcasted_iota(jnp.int32, (CHUNK, 1), 0)
    n_sub = CHUNK // SUB
    n_chunks = q_ref.shape[0] // CHUNK

    def tile(ref, j, h):
        return ref[j * CHUNK:(j + 1) * CHUNK, h * HEAD_W:(h + 1) * HEAD_W]

    chains = [(j, h) for j in range(n_chunks) for h in range(N_HEADS)]
    qs_, ks_, lfs = [], [], []
    for j, h in chains:
        qr = tile(q_ref, j, h).astype(F32)
        lb = lb_ref[:, h * HEAD_W:(h + 1) * HEAD_W]
        f = lb + (1.0 - lb) * _sigmoid(tile(f_ref, j, h))
        qs_.append(qr * _sigmoid(qr))
        ks_.append(1.0 - f)
        lfs.append(jnp.log(f))
    gcs = _dot_3x_many(tri, lfs)
    tots = [jnp.sum(lf, axis=0, keepdims=True) for lf in lfs]
    blocks = []
    for q, k, gc in zip(qs_, ks_, gcs):
        row_blocks = []
        for i in range(n_sub):
            mid = i * SUB + SUB // 2
            gref = gc[mid:mid + 1, :]
            qsc = q[i * SUB:(i + 1) * SUB, :] * jnp.exp(gc[i * SUB:(i + 1) * SUB, :] - gref)
            reach = (row >= i * SUB) if rev else (row < (i + 1) * SUB)
            ksc = jnp.where(reach, k * jnp.exp(gref - gc), 0.0)
            row_blocks.append((qsc, ksc))
        blocks.append(row_blocks)
    scores = [jnp.where(incl, jnp.concatenate([_dot_nt(a, b) for a, b in rb], axis=0), 0.0) for rb in blocks]
    pre = {c: (sc, q * jnp.exp(gc), k * jnp.exp(tot - gc), jnp.exp(tot))
           for c, sc, q, k, gc, tot in zip(chains, scores, qs_, ks_, gcs, tots)}

    states = [s_ref[h] for h in range(N_HEADS)]
    heads = range(N_HEADS)
    for j in (range(n_chunks - 1, -1, -1) if rev else range(n_chunks)):
        vs = [tile(i_ref, j, h) for h in heads]
        qs = [_dot_nt(pre[j, h][1], states[h]) for h in heads]
        sv = [_dot(pre[j, h][0], vs[h]) for h in heads]
        kv = [_dot_tn(vs[h], pre[j, h][2]) for h in heads]
        for h in heads:
            o_ref[j * CHUNK:(j + 1) * CHUNK, h * HEAD_W:(h + 1) * HEAD_W] = qs[h] + sv[h]
        states = [states[h] * pre[j, h][3] + kv[h] for h in heads]
    for h in heads:
        s_ref[h] = states[h]


def _hgrn2(ub, uf, lb, *, batch, seq, ctx_len, rev):
    n_all = seq // SCAN_ROWS
    n_ctx = ctx_len // SCAN_ROWS

    def rows(b, s):
        return b * n_all + _block_order(s, n_ctx, n_all, rev)

    f_col = (UF_HG_FB if rev else UF_HG_FF) // MIX_W
    return pl.pallas_call(
        functools.partial(_hgrn2_kernel, rev=rev),
        out_shape=jax.ShapeDtypeStruct((batch * seq, MIX_W), F32),
        grid=(batch, n_all),
        in_specs=[
            pl.BlockSpec((SCAN_ROWS, MIX_W), lambda b, s: (rows(b, s), UB_HG_Q // MIX_W)),
            pl.BlockSpec((SCAN_ROWS, MIX_W), lambda b, s: (rows(b, s), f_col)),
            pl.BlockSpec((SCAN_ROWS, MIX_W), lambda b, s: (rows(b, s), UB_HG_I // MIX_W)),
            pl.BlockSpec((1, MIX_W), lambda b, s: (0, 0)),
        ],
        out_specs=pl.BlockSpec((SCAN_ROWS, MIX_W), lambda b, s: (rows(b, s), 0)),
        scratch_shapes=[pltpu.VMEM((N_HEADS, HEAD_W, HEAD_W), F32)],
        compiler_params=_params(("parallel", "arbitrary")),
        name="hgrn2_rev" if rev else "hgrn2_fwd",
    )(ub, uf, ub, lb)


LRU_GROUP = 8


def _group_scan(a, b, rev):
    n = a.shape[0]
    sub = lax.broadcasted_iota(jnp.int32, (n, 1), 0) % LRU_GROUP
    s = 1
    while s < LRU_GROUP:
        if rev:
            a_s, b_s, valid = pltpu.roll(a, n - s, 0), pltpu.roll(b, n - s, 0), sub < LRU_GROUP - s
        else:
            a_s, b_s, valid = pltpu.roll(a, s, 0), pltpu.roll(b, s, 0), sub >= s
        b = jnp.where(valid, a * b_s + b, b)
        a = jnp.where(valid, a * a_s, a)
        s *= 2
    return a, b


def _chain_groups(a, b, carry, rev, store):
    n = a.shape[0] // LRU_GROUP
    for g in (range(n - 1, -1, -1) if rev else range(n)):
        hg = a[g * LRU_GROUP:(g + 1) * LRU_GROUP, :] * carry + b[g * LRU_GROUP:(g + 1) * LRU_GROUP, :]
        store(g, hg)
        carry = hg[0:1, :] if rev else hg[LRU_GROUP - 1:LRU_GROUP, :]
    return carry


def _gelu_tanh(x):
    return 0.5 * x * (1.0 + jnp.tanh(0.7978845608028654 * (x + 0.044715 * (x * x * x))))


def _lru_kernel(xb_ref, gb_ref, cw_ref, cb_ref, wa_ref, ba_ref, wx_ref, bx_ref, lam_ref, o_ref, xc_ref, hf_ref,
                hb_ref, *, ctx_len):
    n = xb_ref.shape[0]
    nblk = n // LRU_BLOCK
    nctx = ctx_len // LRU_BLOCK
    xc_ref[...] = _seg_conv(xb_ref[...].astype(F32), cw_ref, ctx_len) + cb_ref[...]

    def gates(xc, d):
        r = _sigmoid(_dot(xc, wa_ref[d]) + ba_ref[d])
        ig = _sigmoid(_dot(xc, wx_ref[d]) + bx_ref[d])
        log_a = -LRU_C * r * _softplus(-lam_ref[d])
        a = jnp.exp(log_a)
        return a, jnp.sqrt(1.0 - a * a) * ig * xc

    def scan_block(blk, carry, d, out_ref):
        base = pl.multiple_of(blk * LRU_BLOCK, LRU_BLOCK)
        a, b = _group_scan(*gates(xc_ref[pl.ds(base, LRU_BLOCK), :], d), rev=bool(d))

        def store(g, hg):
            out_ref[pl.ds(pl.multiple_of(base + g * LRU_GROUP, LRU_GROUP), LRU_GROUP), :] = hg

        return _chain_groups(a, b, carry, bool(d), store)

    def step(i, carry):
        cf = scan_block(i, carry[0], 0, hf_ref)
        cb = scan_block(_block_order(i, nctx, nblk, True), carry[1], 1, hb_ref)
        return cf, cb

    zero = jnp.zeros((1, HEAD_W), F32)
    lax.fori_loop(0, nblk, step, (zero, zero))
    o_ref[...] = ((hf_ref[...] + hb_ref[...]) * _gelu_tanh(gb_ref[...].astype(F32))).astype(o_ref.dtype)


def _lru(ub, conv_w, conv_b, w_a, b_a, w_x, b_x, lam, *, batch, seq, ctx_len):
    vec = pl.BlockSpec((2, 1, HEAD_W), lambda b, h: (0, 0, h))
    mat = pl.BlockSpec((2, None, HEAD_W, HEAD_W), lambda b, h: (0, h, 0, 0))
    return pl.pallas_call(
        functools.partial(_lru_kernel, ctx_len=ctx_len),
        out_shape=jax.ShapeDtypeStruct((batch * seq, MIX_W), BF16),
        grid=(batch, N_HEADS),
        in_specs=[
            pl.BlockSpec((seq, HEAD_W), lambda b, h: (b, UB_LRU_X // HEAD_W + h)),
            pl.BlockSpec((seq, HEAD_W), lambda b, h: (b, UB_LRU_G // HEAD_W + h)),
            pl.BlockSpec((4, HEAD_W), lambda b, h: (0, h)),
            pl.BlockSpec((1, HEAD_W), lambda b, h: (0, h)),
            mat, vec, mat, vec, vec,
        ],
        out_specs=pl.BlockSpec((seq, HEAD_W), lambda b, h: (b, h)),
        scratch_shapes=[pltpu.VMEM((seq, HEAD_W), F32)] * 3,
        compiler_params=_params(("parallel", "parallel")),
        name="rglru",
    )(ub, ub, conv_w, conv_b, w_a, b_a, w_x, b_x, lam)


def _att_prep_kernel(q_ref, k_ref, cos_ref, sin_ref, qw_ref, kw_ref, grp_ref, qo_ref, kto_ref):
    grp = grp_ref[...]
    cos, sin = cos_ref[...], sin_ref[...]
    lane = lax.broadcasted_iota(jnp.int32, (1, 128), 1)
    first = (lane % (ATT_HEAD_DIM // 2)) < (ATT_HEAD_DIM // 4)

    def norm_rope(x, w, scale):
        ss = _dot_x3(x * x, grp)
        y = x * lax.rsqrt(ss * (1.0 / ATT_HEAD_DIM) + EPS) * w
        rot = jnp.where(first, pltpu.roll(y, 128 - ATT_HEAD_DIM // 4, 1), pltpu.roll(y, ATT_HEAD_DIM // 4, 1))
        return (y * cos + rot * sin) * scale

    q_scale = (ATT_HEAD_DIM ** -0.5) * LOG2_E
    for s in range(q_ref.shape[1] // 128):
        sl = slice(s * 128, (s + 1) * 128)
        qo_ref[:, sl] = norm_rope(q_ref[:, sl].astype(F32), qw_ref[...], q_scale).astype(qo_ref.dtype)
    kto_ref[...] = norm_rope(k_ref[...].astype(F32), kw_ref[...], 1.0).T.astype(kto_ref.dtype)


def _att_prep(ub, cos, sin, qw, kw, grp, *, batch, seq):
    rows = ub.shape[0]
    tm = ATT_TQ
    tpb = seq // tm
    return pl.pallas_call(
        _att_prep_kernel,
        out_shape=(jax.ShapeDtypeStruct((rows, ATT_QW), BF16),
                   jax.ShapeDtypeStruct((batch, ATT_KW, seq), BF16)),
        grid=(rows // tm,),
        in_specs=[
            pl.BlockSpec((tm, ATT_QW), lambda i: (i, UB_ATT_Q // ATT_QW)),
            pl.BlockSpec((tm, ATT_KW), lambda i: (i, UB_ATT_K // ATT_KW)),
            pl.BlockSpec((tm, 128), lambda i: (i % tpb, 0)),
            pl.BlockSpec((tm, 128), lambda i: (i % tpb, 0)),
            pl.BlockSpec((1, 128), lambda i: (0, 0)),
            pl.BlockSpec((1, 128), lambda i: (0, 0)),
            pl.BlockSpec((128, 128), lambda i: (0, 0)),
        ],
        out_specs=(pl.BlockSpec((tm, ATT_QW), lambda i: (i, 0)),
                   pl.BlockSpec((None, ATT_KW, tm), lambda i: (i // tpb, 0, i % tpb))),
        compiler_params=_params(("parallel",)),
        name="att_prep",
    )(ub, ub, cos, sin, qw, kw, grp)


def _att_kernel(q_ref, kt_ref, v_ref, o_ref, *, ctx_len):
    def attend(n_keys):
        v = v_ref[:n_keys, :]

        def scores(head):
            g = head // ATT_GROUP
            kt = kt_ref[g * ATT_HEAD_DIM:(g + 1) * ATT_HEAD_DIM, :n_keys]
            return jnp.dot(q_ref[:, head * ATT_HEAD_DIM:(head + 1) * ATT_HEAD_DIM], kt, preferred_element_type=F32)

        s_next = scores(0)
        for head in range(ATT_Q_HEADS):
            s = s_next
            if head + 1 < ATT_Q_HEADS:
                s_next = scores(head + 1)
            g = head // ATT_GROUP
            p = jnp.exp2(s - jnp.max(s, axis=-1, keepdims=True))
            den = jnp.sum(p, axis=-1, keepdims=True)
            pv = jnp.dot(p.astype(BF16), v, preferred_element_type=F32)
            o_ref[:, head * ATT_HEAD_DIM:(head + 1) * ATT_HEAD_DIM] = (
                pv[:, g * ATT_HEAD_DIM:(g + 1) * ATT_HEAD_DIM] / den).astype(o_ref.dtype)

    is_ctx = pl.program_id(1) == 0

    @pl.when(is_ctx)
    def _():
        attend(ctx_len)

    @pl.when(jnp.logical_not(is_ctx))
    def _():
        attend(kt_ref.shape[1])


def _attention(qn, kt, ub, *, batch, seq, ctx_len):
    tiles = seq // ATT_TQ
    return pl.pallas_call(
        functools.partial(_att_kernel, ctx_len=ctx_len),
        out_shape=jax.ShapeDtypeStruct((batch * seq, ATT_QW), BF16),
        grid=(batch, tiles),
        in_specs=[
            pl.BlockSpec((ATT_TQ, ATT_QW), lambda b, i: (b * tiles + i, 0)),
            pl.BlockSpec((None, ATT_KW, seq), lambda b, i: (b, 0, 0)),
            pl.BlockSpec((seq, ATT_KW), lambda b, i: (b, UB_ATT_V // ATT_KW)),
        ],
        out_specs=pl.BlockSpec((ATT_TQ, ATT_QW), lambda b, i: (b * tiles + i, 0)),
        compiler_params=_params(("parallel", "arbitrary")),
        name="attention",
    )(qn, kt, ub)


def _merge_kernel(oaf_ref, oab_ref, obf_ref, obb_ref, ga_ref, gb_ref, yc_ref, yd_ref, gate_ref, h_ref, modc_ref,
                  modl_ref, dnw_ref, hgw_ref, wb_ref, wo_ref, out_ref, *, tiles_per_batch):
    def gated(of_ref, ob_ref, g_ref, nw_ref):
        o = of_ref[...] + ob_ref[...]
        ys = [_rms_rows(o[:, h * HEAD_W:(h + 1) * HEAD_W], nw_ref[...]) for h in range(N_HEADS)]
        g = g_ref[...].astype(F32)
        return jnp.concatenate(ys, axis=1) * (g * _sigmoid(g))

    ys = (gated(oaf_ref, oab_ref, ga_ref, dnw_ref), gated(obf_ref, obb_ref, gb_ref, hgw_ref), yc_ref[...], yd_ref[...])
    acc = None
    for b in range(N_BRANCH):
        gate = gate_ref[:, b * D_MODEL:(b + 1) * D_MODEL].astype(F32)
        term = _sigmoid(gate) * _dot(ys[b], wb_ref[b])
        acc = term if acc is None else acc + term
    is_ctx = (pl.program_id(0) % tiles_per_batch) == 0
    out_ref[...] = h_ref[...] + _pick_mod(modc_ref, modl_ref, 2, is_ctx) * _dot(acc, wo_ref[...])


def _merge(oaf, oab, obf, obb, ub, yc, yd, h, modc, modl, dnw, hgw, wb, wo, *, seq, ctx_len):
    rows = h.shape[0]
    tm = ctx_len
    tiles_per_batch = seq // tm
    mix = lambda c: pl.BlockSpec((tm, MIX_W), lambda i: (i, c // MIX_W))
    return pl.pallas_call(
        functools.partial(_merge_kernel, tiles_per_batch=tiles_per_batch),
        out_shape=jax.ShapeDtypeStruct((rows, D_MODEL), F32),
        grid=(rows // tm,),
        in_specs=[
            mix(0), mix(0), mix(0), mix(0), mix(UB_DN_G), mix(UB_HG_G), mix(0), mix(0),
            pl.BlockSpec((tm, N_BRANCH * D_MODEL), lambda i: (i, 0)),
            pl.BlockSpec((tm, D_MODEL), lambda i: (i, 0)),
            pl.BlockSpec((6, D_MODEL), lambda i: (0, 0)),
            pl.BlockSpec((None, 6, D_MODEL), lambda i: (i // tiles_per_batch, 0, 0)),
            pl.BlockSpec((1, HEAD_W), lambda i: (0, 0)),
            pl.BlockSpec((1, HEAD_W), lambda i: (0, 0)),
            pl.BlockSpec((N_BRANCH, MIX_W, D_MODEL), lambda i: (0, 0, 0)),
            pl.BlockSpec((D_MODEL, D_MODEL), lambda i: (0, 0)),
        ],
        out_specs=pl.BlockSpec((tm, D_MODEL), lambda i: (i, 0)),
        compiler_params=_params(("parallel",)),
        name="merge",
    )(oaf, oab, obf, obb, ub, ub, yc, yd, ub, h, modc, modl, dnw, hgw, wb, wo)


def _mlp_kernel(h_ref, modc_ref, modl_ref, nw_ref, w1_ref, w2_ref, o_ref, z_ref, acc_ref,
                *, tm, tiles_per_batch, ctx_len):
    j = pl.program_id(1)
    row = (pl.program_id(0) % tiles_per_batch) * tm + lax.broadcasted_iota(jnp.int32, (tm, 1), 0)
    is_ctx = row < ctx_len

    @pl.when(j == 0)
    def _():
        y = _rms_rows(h_ref[...], nw_ref[...])
        shift = _pick_mod(modc_ref, modl_ref, 3, is_ctx)
        scale = _pick_mod(modc_ref, modl_ref, 4, is_ctx)
        z_ref[...] = (y * (1.0 + scale) + shift).astype(BF16)
        acc_ref[...] = jnp.zeros_like(acc_ref)

    a = jnp.maximum(jnp.dot(z_ref[...], w1_ref[...], preferred_element_type=F32), 0.0)
    acc_ref[...] += _dot(a * a, w2_ref[...])

    @pl.when(j == pl.num_programs(1) - 1)
    def _():
        o_ref[...] = h_ref[...] + _pick_mod(modc_ref, modl_ref, 5, is_ctx) * acc_ref[...]


def _mlp(h, modc, modl, nw, w1, w2, *, seq, ctx_len):
    rows = h.shape[0]
    tiles_per_batch = 4
    tm = seq // tiles_per_batch
    kern = functools.partial(_mlp_kernel, tm=tm, tiles_per_batch=tiles_per_batch, ctx_len=ctx_len)
    return pl.pallas_call(
        kern,
        out_shape=jax.ShapeDtypeStruct((rows, D_MODEL), F32),
        grid=(rows // tm, D_FF // FF_BLOCK),
        in_specs=[
            pl.BlockSpec((tm, D_MODEL), lambda i, j: (i, 0)),
            pl.BlockSpec((6, D_MODEL), lambda i, j: (0, 0)),
            pl.BlockSpec((None, 6, D_MODEL), lambda i, j: (i // tiles_per_batch, 0, 0)),
            pl.BlockSpec((1, D_MODEL), lambda i, j: (0, 0)),
            pl.BlockSpec((D_MODEL, FF_BLOCK), lambda i, j: (0, j)),
            pl.BlockSpec((FF_BLOCK, D_MODEL), lambda i, j: (j, 0)),
        ],
        out_specs=pl.BlockSpec((tm, D_MODEL), lambda i, j: (i, 0)),
        scratch_shapes=[pltpu.VMEM((tm, D_MODEL), BF16), pltpu.VMEM((tm, D_MODEL), F32)],
        compiler_params=_params(("parallel", "arbitrary")),
        name="mlp",
    )(h, modc, modl, nw, w1, w2)


def _regroup_w_in(w_in):
    o = np.cumsum([0, 512, 512, 512, 512, 8, 8, 512, 512, 512, 512, 512, 512, 512, 512, 128, 128, 4096])
    cols = lambda a, b: w_in[..., o[a]:o[b]]
    wb = jnp.concatenate([cols(16, 17), cols(6, 7), cols(9, 11), cols(0, 3), cols(11, 13), cols(3, 4), cols(13, 16)],
                         axis=-1)
    wf = jnp.concatenate([cols(7, 9), cols(4, 6)], axis=-1)
    pad = lambda w, n: jnp.pad(w, ((0, 0), (0, 0), (0, n - w.shape[-1]))).astype(BF16)
    return pad(wb, N_UB), pad(wf, N_UF)


def _rope_tables(t_len, ctx_len):
    rows = t_len // GRID_W
    row_id = jnp.repeat(jnp.arange(rows), GRID_W).astype(F32)
    col_id = jnp.tile(jnp.arange(GRID_W), rows).astype(F32)
    axis_dim = ATT_HEAD_DIM // 2
    inv = ROPE_THETA ** (-jnp.arange(0, axis_dim, 2, dtype=F32) / axis_dim)
    ang = jnp.stack([row_id[:, None] * inv, col_id[:, None] * inv], axis=1)
    cos, sin = jnp.cos(ang), jnp.sin(ang)
    cos_h = jnp.concatenate([cos, cos], axis=-1).reshape(t_len, ATT_HEAD_DIM)
    sin_h = jnp.concatenate([-sin, sin], axis=-1).reshape(t_len, ATT_HEAD_DIM)
    cos_t = jnp.concatenate([jnp.ones((ctx_len, ATT_HEAD_DIM), F32), cos_h], axis=0)
    sin_t = jnp.concatenate([jnp.zeros((ctx_len, ATT_HEAD_DIM), F32), sin_h], axis=0)
    return jnp.tile(cos_t, (1, 2)), jnp.tile(sin_t, (1, 2))


def _hgrn2_lower_bounds(p):
    sm = jax.nn.softmax(p.astype(F32), axis=1)
    cs = jnp.cumsum(sm, axis=1)
    return cs - cs[:, :1]


def kernel(x, c, ctx, c_ctx, mod_w, mod_b, norm1_w, norm2_w, w_in, dn_conv_w, dn_a_log, dn_dt_bias, dn_norm_w,
           hg_lower_bounds, hg_norm_w, lru_conv_w, lru_conv_b, lru_w_a, lru_b_a, lru_w_x, lru_b_x, lru_lambda,
           att_q_norm_w, att_k_norm_w, w_branch, w_out, mlp_w1, mlp_w2):
    batch, t_len, _ = x.shape
    ctx_len = ctx.shape[1]
    depth = mod_w.shape[0]
    seq = ctx_len + t_len
    n_chunks = seq // CHUNK
    assert ctx_len % SCAN_ROWS == 0 and t_len % SCAN_ROWS == 0 and ctx_len == ATT_TQ
    dims = dict(batch=batch, seq=seq, ctx_len=ctx_len)

    w_ub, w_uf = _regroup_w_in(w_in)
    w_branch_b, w_out_b = w_branch.astype(BF16), w_out.astype(BF16)
    w1_b, w2_b = mlp_w1.astype(BF16), mlp_w2.astype(BF16)
    lru_wa_b, lru_wx_b = lru_w_a.astype(BF16), lru_w_x.astype(BF16)
    lb_all = _hgrn2_lower_bounds(hg_lower_bounds)
    cos_t, sin_t = _rope_tables(t_len, ctx_len)
    grp = jnp.asarray((np.arange(128)[:, None] // ATT_HEAD_DIM) == (np.arange(128)[None, :] // ATT_HEAD_DIM), F32)

    n_mod_rows = batch + 8
    cc = jnp.zeros((n_mod_rows, D_MODEL), F32).at[:batch].set(c).at[batch].set(c_ctx)
    mods = _modulations(cc, mod_w.astype(BF16), mod_b)

    h = jnp.concatenate([ctx, x], axis=1).reshape(batch * seq, D_MODEL)
    for l in range(depth):
        modl = mods[l, :batch].reshape(batch, 6, D_MODEL)
        modc = mods[l, batch].reshape(6, D_MODEL)
        nw1 = norm1_w[l].reshape(1, D_MODEL)
        ub = _inproj(h, modc, modl, nw1, w_ub[l], seq=seq, ctx_len=ctx_len, tn=TN_UB, out_dtype=BF16, name="inproj_b")
        uf = _inproj(h, modc, modl, nw1, w_uf[l], seq=seq, ctx_len=ctx_len, tn=N_UF, out_dtype=F32, name="inproj_f")

        qkv = _dn_prep(ub, dn_conv_w[l], **dims)
        ba = uf[:, UF_DN_BA:UF_DN_BA + 4 * N_HEADS].reshape(batch * n_chunks, CHUNK, 2, 2, N_HEADS)
        ba_t = jnp.transpose(ba, (3, 0, 2, 4, 1)).reshape(2, batch * n_chunks, 2 * N_HEADS, CHUNK)
        oa = [_gdn(qkv, ba_t[d], dn_a_log[l, d].reshape(N_HEADS, 1), dn_dt_bias[l, d].reshape(N_HEADS, 1),
                   rev=bool(d), **dims) for d in range(2)]

        ob = [_hgrn2(ub, uf, lb_all[d, l].reshape(1, MIX_W), rev=bool(d), **dims) for d in range(2)]

        yc = _lru(ub, lru_conv_w[l], lru_conv_b[l].reshape(1, MIX_W), lru_wa_b[l], lru_b_a[l].reshape(2, 1, MIX_W),
                  lru_wx_b[l], lru_b_x[l].reshape(2, 1, MIX_W), lru_lambda[l].reshape(2, 1, MIX_W), **dims)

        qn, kt = _att_prep(ub, cos_t, sin_t, jnp.tile(att_q_norm_w[l], 2).reshape(1, 128),
                           jnp.tile(att_k_norm_w[l], 2).reshape(1, 128), grp, batch=batch, seq=seq)
        yd = _attention(qn, kt, ub, **dims)

        h = _merge(oa[0], oa[1], ob[0], ob[1], ub, yc, yd, h, modc, modl, dn_norm_w[l].reshape(1, HEAD_W),
                   hg_norm_w[l].reshape(1, HEAD_W), w_branch_b[l], w_out_b[l], seq=seq, ctx_len=ctx_len)
        h = _mlp(h, modc, modl, norm2_w[l].reshape(1, D_MODEL), w1_b[l], w2_b[l], seq=seq, ctx_len=ctx_len)

    return h.reshape(batch, seq, D_MODEL)[:, ctx_len:]
```

```python
import functools

import jax
import jax.numpy as jnp
import numpy as np
from jax import lax
from jax.experimental import pallas as pl
from jax.experimental.pallas import tpu as pltpu

F32 = jnp.float32
BF16 = jnp.bfloat16

EPS = 1e-6
D_MODEL = 1024
GRID_W = 64
N_HEADS = 4
HEAD_W = 128
MIX_W = N_HEADS * HEAD_W
CHUNK = 64
SUB = 16
SCAN_ROWS = 256
LRU_C = 8.0
LRU_BLOCK = 256
ATT_Q_HEADS = 8
ATT_KV_HEADS = 2
ATT_HEAD_DIM = 64
ATT_GROUP = ATT_Q_HEADS // ATT_KV_HEADS
ATT_QW = ATT_Q_HEADS * ATT_HEAD_DIM
ATT_KW = ATT_KV_HEADS * ATT_HEAD_DIM
ATT_TQ = 256
ROPE_THETA = 10000.0
LOG2_E = 1.4426950408889634
N_BRANCH = 4
D_FF = 4 * D_MODEL
FF_BLOCK = 1024

UB_GATE = 0
UB_HG_Q, UB_HG_I, UB_HG_G = 4096, 4608, 5120
UB_DN_Q = 5632
UB_LRU_X, UB_LRU_G = 7168, 7680
UB_DN_G = 8192
UB_ATT_Q, UB_ATT_K, UB_ATT_V = 8704, 9216, 9344
N_UB_USED = 9472
TN_UB = 1664
N_UB = 6 * TN_UB
UF_HG_FF, UF_HG_FB, UF_DN_BA = 0, 512, 1024
N_UF = 1280

VMEM_LIMIT_V7X = 48 * 1024 * 1024


def _params(sem, vmem=VMEM_LIMIT_V7X):
    return pltpu.CompilerParams(dimension_semantics=sem, vmem_limit_bytes=vmem)


def _sigmoid(x):
    return 1.0 / (1.0 + jnp.exp(-x))


def _softplus(x):
    return jnp.maximum(x, 0.0) + jnp.log1p(jnp.exp(-jnp.abs(x)))


def _dot(a, b):
    return jnp.dot(a.astype(BF16), b.astype(BF16), preferred_element_type=F32)


def _dot_nt(a, b):
    return lax.dot_general(a.astype(BF16), b.astype(BF16), (((1,), (1,)), ((), ())),
                           preferred_element_type=F32)


def _dot_tn(a, b):
    return lax.dot_general(a.astype(BF16), b.astype(BF16), (((0,), (0,)), ((), ())),
                           preferred_element_type=F32)


def _split3(x):
    hi = x.astype(BF16)
    r1 = x - hi.astype(F32)
    mid = r1.astype(BF16)
    lo = (r1 - mid.astype(F32)).astype(BF16)
    return hi, mid, lo


def _dot_x3(x, m):
    m = m.astype(BF16)
    hi, mid, lo = _split3(x)
    out = jnp.dot(hi, m, preferred_element_type=F32)
    out = out + jnp.dot(mid, m, preferred_element_type=F32)
    return out + jnp.dot(lo, m, preferred_element_type=F32)


def _dot_3x_many(m, xs):
    w = xs[0].shape[1]
    pieces = [_split3(x) for x in xs]
    cols = [p[i] for i in range(3) for p in pieces]
    wide = jnp.dot(m.astype(BF16), jnp.concatenate(cols, axis=1), preferred_element_type=F32)
    n = len(xs)
    part = lambda p, i: wide[:, (p * n + i) * w:(p * n + i + 1) * w]
    return [part(0, i) + part(1, i) + part(2, i) for i in range(n)]


def _rms_rows(x, w):
    return x * lax.rsqrt(jnp.mean(x * x, axis=-1, keepdims=True) + EPS) * w


def _mod_kernel(c_ref, w_ref, b_ref, o_ref):
    c = c_ref[...]
    o_ref[...] = _dot(c * _sigmoid(c), w_ref[...]) + b_ref[...]


def _modulations(cc, mod_w, mod_b):
    depth = mod_w.shape[0]
    rows = cc.shape[0]
    n_out = mod_w.shape[2]
    return pl.pallas_call(
        _mod_kernel,
        out_shape=jax.ShapeDtypeStruct((depth, rows, n_out), F32),
        grid=(depth, n_out // D_MODEL),
        in_specs=[
            pl.BlockSpec((rows, D_MODEL), lambda l, j: (0, 0)),
            pl.BlockSpec((None, D_MODEL, D_MODEL), lambda l, j: (l, 0, j)),
            pl.BlockSpec((None, 1, D_MODEL), lambda l, j: (l, 0, j)),
        ],
        out_specs=pl.BlockSpec((None, rows, D_MODEL), lambda l, j: (l, 0, j)),
        compiler_params=_params(("parallel", "parallel")),
        name="modulations",
    )(cc, mod_w, mod_b.reshape(depth, 1, n_out))


def _pick_mod(modc_ref, modl_ref, idx, is_ctx):
    return jnp.where(is_ctx, modc_ref[idx:idx + 1, :], modl_ref[idx:idx + 1, :])


def _inproj_kernel(h_ref, modc_ref, modl_ref, nw_ref, w_ref, o_ref, xn_ref, *, tm, tiles_per_batch, ctx_len):
    @pl.when(pl.program_id(1) == 0)
    def _():
        y = _rms_rows(h_ref[...], nw_ref[...])
        row = (pl.program_id(0) % tiles_per_batch) * tm + lax.broadcasted_iota(jnp.int32, (tm, 1), 0)
        is_ctx = row < ctx_len
        shift = _pick_mod(modc_ref, modl_ref, 0, is_ctx)
        scale = _pick_mod(modc_ref, modl_ref, 1, is_ctx)
        xn_ref[...] = (y * (1.0 + scale) + shift).astype(BF16)

    o_ref[...] = jnp.dot(xn_ref[...], w_ref[...], preferred_element_type=F32).astype(o_ref.dtype)


def _inproj(h, modc, modl, nw, w, *, seq, ctx_len, tn, out_dtype, name):
    rows = h.shape[0]
    n_out = w.shape[1]
    tiles_per_batch = 4
    tm = seq // tiles_per_batch
    kern = functools.partial(_inproj_kernel, tm=tm, tiles_per_batch=tiles_per_batch, ctx_len=ctx_len)
    return pl.pallas_call(
        kern,
        out_shape=jax.ShapeDtypeStruct((rows, n_out), out_dtype),
        grid=(rows // tm, n_out // tn),
        in_specs=[
            pl.BlockSpec((tm, D_MODEL), lambda i, j: (i, 0)),
            pl.BlockSpec((6, D_MODEL), lambda i, j: (0, 0)),
            pl.BlockSpec((None, 6, D_MODEL), lambda i, j: (i // tiles_per_batch, 0, 0)),
            pl.BlockSpec((1, D_MODEL), lambda i, j: (0, 0)),
            pl.BlockSpec((D_MODEL, tn), lambda i, j: (0, j)),
        ],
        out_specs=pl.BlockSpec((tm, tn), lambda i, j: (i, j)),
        scratch_shapes=[pltpu.VMEM((tm, D_MODEL), BF16)],
        compiler_params=_params(("parallel", "arbitrary")),
        name=name,
    )(h, modc, modl, nw, w)


def _seg_conv(x, w_ref, ctx_len):
    n = x.shape[0]
    row = lax.broadcasted_iota(jnp.int32, (n, 1), 0)
    lo = jnp.where(row >= ctx_len, ctx_len, 0)
    hi = jnp.where(row >= ctx_len, n, ctx_len)

    def tap(k):
        tk = row + k
        valid = jnp.logical_and(tk >= lo, tk < hi)
        return jnp.where(valid, pltpu.roll(x, (-k) % n, 0), 0.0)

    return (tap(-1) * w_ref[0:1, :] + x * w_ref[1:2, :] + tap(1) * w_ref[2:3, :] + tap(2) * w_ref[3:4, :])


def _dn_prep_kernel(u_ref, w_ref, o_ref, *, ctx_len):
    j = pl.program_id(1)
    y = _seg_conv(u_ref[...].astype(F32), w_ref, ctx_len)
    y = y * _sigmoid(y)
    n = lax.rsqrt(jnp.sum(y * y, axis=-1, keepdims=True) + EPS)
    fac = jnp.where(j < N_HEADS, n * (HEAD_W ** -0.5), jnp.where(j < 2 * N_HEADS, n, 1.0))
    o_ref[...] = (y * fac).astype(o_ref.dtype)


def _dn_prep(ub, conv_w, *, batch, seq, ctx_len):
    nblk = 3 * N_HEADS
    return pl.pallas_call(
        functools.partial(_dn_prep_kernel, ctx_len=ctx_len),
        out_shape=jax.ShapeDtypeStruct((batch * seq, 3 * MIX_W), BF16),
        grid=(batch, nblk),
        in_specs=[
            pl.BlockSpec((seq, HEAD_W), lambda b, j: (b, UB_DN_Q // HEAD_W + j)),
            pl.BlockSpec((4, HEAD_W), lambda b, j: (0, j)),
        ],
        out_specs=pl.BlockSpec((seq, HEAD_W), lambda b, j: (b, j)),
        compiler_params=_params(("parallel", "parallel")),
        name="dn_prep",
    )(ub, conv_w)


def _block_order(s, n_ctx, n_all, rev):
    if not rev:
        return s
    return jnp.where(s < n_ctx, n_ctx - 1 - s, n_all + n_ctx - 1 - s)


def _chunk_masks(rev):
    ii = lax.broadcasted_iota(jnp.int32, (CHUNK, CHUNK), 0)
    jj = lax.broadcasted_iota(jnp.int32, (CHUNK, CHUNK), 1)
    incl = (ii <= jj) if rev else (ii >= jj)
    incl_t = (ii >= jj) if rev else (ii <= jj)
    return ii, jj, incl, incl_t


def _unit_tri_inverse(a_all, eye, bd):
    ad = [jnp.where(bd, a, 0.0) for a in a_all]
    ao = [a - d for a, d in zip(a_all, ad)]
    p = [-d for d in ad]
    dinv = [eye + x for x in p]
    for _ in range(3):
        p = [_dot(x, x) for x in p]
        dinv = [d + _dot(d, x) for d, x in zip(dinv, p)]
    m = [-_dot(d, o) for d, o in zip(dinv, ao)]
    mm = [_dot(x, x) for x in m]
    t = [eye + x for x in m]
    t = [x + _dot(x, y) for x, y in zip(t, mm)]
    return [_dot(x, d) for x, d in zip(t, dinv)]


def _gdn_prepare(q_ref, k_ref, v_ref, ba_ref, alog, dtb, rev):
    ii, jj, incl, incl_t = _chunk_masks(rev)
    is_eye = ii == jj
    eye = jnp.where(is_eye, 1.0, 0.0)
    tri = jnp.where(incl, 1.0, 0.0)
    tri_t = jnp.where(incl_t, 1.0, 0.0)
    bd = (ii // SUB) == (jj // SUB)
    n_chunks = q_ref.shape[0] // CHUNK

    alog_neg = -jnp.exp(alog)
    chains = [(j, h) for j in range(n_chunks) for h in range(N_HEADS)]
    vec = {}
    for j in range(n_chunks):
        ba = ba_ref[j]
        beta_r = _sigmoid(ba[0:N_HEADS])
        g_r = alog_neg * _softplus(ba[N_HEADS:2 * N_HEADS] + dtb)
        for h in range(N_HEADS):
            gr = g_r[h:h + 1]
            beta_c = jnp.sum(eye * beta_r[h:h + 1], axis=1, keepdims=True)
            g_c = jnp.sum(eye * gr, axis=1, keepdims=True)
            gam_c = jnp.sum(tri * gr, axis=1, keepdims=True)
            gam_r = jnp.sum(tri_t * g_c, axis=0, keepdims=True)
            tot = jnp.sum(gr, axis=1, keepdims=True)
            dec_i = jnp.where(incl, jnp.exp(gam_c - gam_r), 0.0)
            vec[j, h] = (beta_c, gam_c, tot, dec_i)

    def tile(ref, j, h):
        return ref[j * CHUNK:(j + 1) * CHUNK, h * HEAD_W:(h + 1) * HEAD_W]

    k16 = [tile(k_ref, *c) for c in chains]
    q16 = [tile(q_ref, *c) for c in chains]
    ks = [k.astype(F32) for k in k16]
    kbs = [k * vec[c][0] for k, c in zip(ks, chains)]
    kq = [_dot_nt(jnp.concatenate([kb.astype(BF16), q], axis=0), k) for kb, q, k in zip(kbs, q16, k16)]
    a_low = [jnp.where(is_eye, 0.0, x[:CHUNK] * vec[c][3]) for x, c in zip(kq, chains)]
    a_qk = [x[CHUNK:] * vec[c][3] for x, c in zip(kq, chains)]
    t_inv = _unit_tri_inverse(a_low, eye, bd)
    egc = [jnp.exp(vec[c][1]) for c in chains]
    sols = [_dot(t, jnp.concatenate([tile(v_ref, *c).astype(F32) * vec[c][0], kb * e], axis=1))
            for t, kb, e, c in zip(t_inv, kbs, egc, chains)]
    pre = {}
    for c, sol, aq, q, k, e in zip(chains, sols, a_qk, q16, ks, egc):
        q_dec = q.astype(F32) * e
        k_dec_t = (k * jnp.exp(vec[c][2] - vec[c][1])).T
        pre[c] = (sol[:, :HEAD_W], jnp.concatenate([sol[:, HEAD_W:], q_dec], axis=0).astype(BF16),
                  jnp.concatenate([aq, k_dec_t], axis=0).astype(BF16), jnp.exp(vec[c][2]))
    return pre


def _gdn_kernel(qf_ref, kf_ref, vf_ref, baf_ref, qr_ref, kr_ref, vr_ref, bar_ref, alog_ref, dtb_ref,
                of_ref, or_ref, s_ref):
    @pl.when(pl.program_id(1) == 0)
    def _():
        s_ref[...] = jnp.zeros_like(s_ref)

    n_chunks = qf_ref.shape[0] // CHUNK
    pre = (_gdn_prepare(qf_ref, kf_ref, vf_ref, baf_ref, alog_ref[0], dtb_ref[0], False),
           _gdn_prepare(qr_ref, kr_ref, vr_ref, bar_ref, alog_ref[1], dtb_ref[1], True))
    o_refs = (of_ref, or_ref)

    lanes = [(d, h) for d in range(2) for h in range(N_HEADS)]
    states = {dh: s_ref[dh[0], dh[1]] for dh in lanes}
    for t in range(n_chunks):
        js = (t, n_chunks - 1 - t)
        cur = {dh: pre[dh[0]][js[dh[0]], dh[1]] for dh in lanes}
        wq = {dh: _dot(cur[dh][1], states[dh]) for dh in lanes}
        v_new = {dh: cur[dh][0] - wq[dh][:CHUNK] for dh in lanes}
        ak = {dh: _dot(cur[dh][2], v_new[dh]) for dh in lanes}
        for d, h in lanes:
            j = js[d]
            o_refs[d][j * CHUNK:(j + 1) * CHUNK, h * HEAD_W:(h + 1) * HEAD_W] = wq[d, h][CHUNK:] + ak[d, h][:CHUNK]
        states = {dh: states[dh] * cur[dh][3] + ak[dh][CHUNK:] for dh in lanes}
    for d, h in lanes:
        s_ref[d, h] = states[d, h]


def _gdn(qkv, ba_t, a_log, dt_bias, *, batch, seq, ctx_len):
    n_all = seq // SCAN_ROWS
    n_ctx = ctx_len // SCAN_ROWS
    cpb = SCAN_ROWS // CHUNK

    def rows(b, s, rev):
        return b * n_all + _block_order(s, n_ctx, n_all, rev)

    def in_specs(rev):
        d = int(rev)
        return [
            pl.BlockSpec((SCAN_ROWS, MIX_W), lambda b, s: (rows(b, s, rev), 0)),
            pl.BlockSpec((SCAN_ROWS, MIX_W), lambda b, s: (rows(b, s, rev), 1)),
            pl.BlockSpec((SCAN_ROWS, MIX_W), lambda b, s: (rows(b, s, rev), 2)),
            pl.BlockSpec((None, cpb, 2 * N_HEADS, CHUNK), lambda b, s: (d, rows(b, s, rev), 0, 0)),
        ]

    vec = pl.BlockSpec((2, N_HEADS, 1), lambda b, s: (0, 0, 0))
    out = jax.ShapeDtypeStruct((batch * seq, MIX_W), F32)
    return pl.pallas_call(
        _gdn_kernel,
        out_shape=(out, out),
        grid=(batch, n_all),
        in_specs=in_specs(False) + in_specs(True) + [vec, vec],
        out_specs=(pl.BlockSpec((SCAN_ROWS, MIX_W), lambda b, s: (rows(b, s, False), 0)),
                   pl.BlockSpec((SCAN_ROWS, MIX_W), lambda b, s: (rows(b, s, True), 0))),
        scratch_shapes=[pltpu.VMEM((2, N_HEADS, HEAD_W, HEAD_W), F32)],
        compiler_params=_params(("parallel", "arbitrary")),
        name="gdn",
    )(qkv, qkv, qkv, ba_t, qkv, qkv, qkv, ba_t, a_log, dt_bias)


def _hgrn2_prepare(q_ref, f_ref, i_ref, lb_ref, rev):
    _, _, incl, _ = _chunk_masks(rev)
    tri = jnp.where(incl, 1.0, 0.0)
    row = lax.broadcasted_iota(jnp.int32, (CHUNK, 1), 0)
    n_sub = CHUNK // SUB
    n_chunks = q_ref.shape[0] // CHUNK

    def tile(ref, j, h):
        return ref[j * CHUNK:(j + 1) * CHUNK, h * HEAD_W:(h + 1) * HEAD_W]

    chains = [(j, h) for j in range(n_chunks) for h in range(N_HEADS)]
    qs_, ks_, lfs = [], [], []
    for j, h in chains:
        qr = tile(q_ref, j, h).astype(F32)
        lb = lb_ref[:, h * HEAD_W:(h + 1) * HEAD_W]
        f = lb + (1.0 - lb) * _sigmoid(tile(f_ref, j, h))
        qs_.append(qr * _sigmoid(qr))
        ks_.append(1.0 - f)
        lfs.append(jnp.log(f))
    gcs = _dot_3x_many(tri, lfs)
    tots = [jnp.sum(lf, axis=0, keepdims=True) for lf in lfs]
    blocks = []
    for q, k, gc in zip(qs_, ks_, gcs):
        row_blocks = []
        for i in range(n_sub):
            mid = i * SUB + SUB // 2
            gref = gc[mid:mid + 1, :]
            qsc = q[i * SUB:(i + 1) * SUB, :] * jnp.exp(gc[i * SUB:(i + 1) * SUB, :] - gref)
            reach = (row >= i * SUB) if rev else (row < (i + 1) * SUB)
            ksc = jnp.where(reach, k * jnp.exp(gref - gc), 0.0)
            row_blocks.append((qsc, ksc))
        blocks.append(row_blocks)
    scores = [jnp.where(incl, jnp.concatenate([_dot_nt(a, b) for a, b in rb], axis=0), 0.0) for rb in blocks]
    vs = [tile(i_ref, *c) for c in chains]
    local = [_dot(sc, v) for sc, v in zip(scores, vs)]
    incr = [_dot_tn(v, k * jnp.exp(tot - gc)) for v, k, gc, tot in zip(vs, ks_, gcs, tots)]
    return {c: (q * jnp.exp(gc), lo, inc, jnp.exp(tot))
            for c, q, gc, lo, inc, tot in zip(chains, qs_, gcs, local, incr, tots)}


def _hgrn2_kernel(qf_ref, ff_ref, if_ref, qr_ref, fr_ref, ir_ref, lb_ref, of_ref, or_ref, s_ref):
    @pl.when(pl.program_id(1) == 0)
    def _():
        s_ref[...] = jnp.zeros_like(s_ref)

    n_chunks = qf_ref.shape[0] // CHUNK
    pre = (_hgrn2_prepare(qf_ref, ff_ref, if_ref, lb_ref.at[0], False),
           _hgrn2_prepare(qr_ref, fr_ref, ir_ref, lb_ref.at[1], True))
    o_refs = (of_ref, or_ref)

    lanes = [(d, h) for d in range(2) for h in range(N_HEADS)]
    states = {dh: s_ref[dh[0], dh[1]] for dh in lanes}
    for t in range(n_chunks):
        js = (t, n_chunks - 1 - t)
        cur = {dh: pre[dh[0]][js[dh[0]], dh[1]] for dh in lanes}
        qs = {dh: _dot_nt(cur[dh][0], states[dh]) for dh in lanes}
        for d, h in lanes:
            j = js[d]
            o_refs[d][j * CHUNK:(j + 1) * CHUNK, h * HEAD_W:(h + 1) * HEAD_W] = qs[d, h] + cur[d, h][1]
        states = {dh: states[dh] * cur[dh][3] + cur[dh][2] for dh in lanes}
    for d, h in lanes:
        s_ref[d, h] = states[d, h]


def _hgrn2(ub, uf, lb, *, batch, seq, ctx_len):
    n_all = seq // SCAN_ROWS
    n_ctx = ctx_len // SCAN_ROWS

    def rows(b, s, rev):
        return b * n_all + _block_order(s, n_ctx, n_all, rev)

    def in_specs(rev):
        f_col = (UF_HG_FB if rev else UF_HG_FF) // MIX_W
        return [
            pl.BlockSpec((SCAN_ROWS, MIX_W), lambda b, s: (rows(b, s, rev), UB_HG_Q // MIX_W)),
            pl.BlockSpec((SCAN_ROWS, MIX_W), lambda b, s: (rows(b, s, rev), f_col)),
            pl.BlockSpec((SCAN_ROWS, MIX_W), lambda b, s: (rows(b, s, rev), UB_HG_I // MIX_W)),
        ]

    out = jax.ShapeDtypeStruct((batch * seq, MIX_W), F32)
    return pl.pallas_call(
        _hgrn2_kernel,
        out_shape=(out, out),
        grid=(batch, n_all),
        in_specs=in_specs(False) + in_specs(True) + [pl.BlockSpec((2, 1, MIX_W), lambda b, s: (0, 0, 0))],
        out_specs=(pl.BlockSpec((SCAN_ROWS, MIX_W), lambda b, s: (rows(b, s, False), 0)),
                   pl.BlockSpec((SCAN_ROWS, MIX_W), lambda b, s: (rows(b, s, True), 0))),
        scratch_shapes=[pltpu.VMEM((2, N_HEADS, HEAD_W, HEAD_W), F32)],
        compiler_params=_params(("parallel", "arbitrary")),
        name="hgrn2",
    )(ub, uf, ub, ub, uf, ub, lb)


LRU_GROUP = 8


def _group_scan(a, b, rev):
    n = a.shape[0]
    sub = lax.broadcasted_iota(jnp.int32, (n, 1), 0) % LRU_GROUP
    s = 1
    while s < LRU_GROUP:
        if rev:
            a_s, b_s, valid = pltpu.roll(a, n - s, 0), pltpu.roll(b, n - s, 0), sub < LRU_GROUP - s
        else:
            a_s, b_s, valid = pltpu.roll(a, s, 0), pltpu.roll(b, s, 0), sub >= s
        b = jnp.where(valid, a * b_s + b, b)
        a = jnp.where(valid, a * a_s, a)
        s *= 2
    return a, b


def _chain_groups(a, b, carry, rev, store):
    n = a.shape[0] // LRU_GROUP
    for g in (range(n - 1, -1, -1) if rev else range(n)):
        hg = a[g * LRU_GROUP:(g + 1) * LRU_GROUP, :] * carry + b[g * LRU_GROUP:(g + 1) * LRU_GROUP, :]
        store(g, hg)
        carry = hg[0:1, :] if rev else hg[LRU_GROUP - 1:LRU_GROUP, :]
    return carry


def _gelu_tanh(x):
    return 0.5 * x * (1.0 + jnp.tanh(0.7978845608028654 * (x + 0.044715 * (x * x * x))))


def _lru_kernel(xb_ref, gb_ref, cw_ref, cb_ref, wa_ref, ba_ref, wx_ref, bx_ref, lam_ref, o_ref, xc_ref, hf_ref,
                hb_ref, *, ctx_len):
    n = xb_ref.shape[0]
    nblk = n // LRU_BLOCK
    nctx = ctx_len // LRU_BLOCK
    xc_ref[...] = _seg_conv(xb_ref[...].astype(F32), cw_ref, ctx_len) + cb_ref[...]

    def gates(xc, d):
        r = _sigmoid(_dot(xc, wa_ref[d]) + ba_ref[d])
        ig = _sigmoid(_dot(xc, wx_ref[d]) + bx_ref[d])
        log_a = -LRU_C * r * _softplus(-lam_ref[d])
        a = jnp.exp(log_a)
        return a, jnp.sqrt(1.0 - a * a) * ig * xc

    def scan_block(blk, carry, d, out_ref):
        base = pl.multiple_of(blk * LRU_BLOCK, LRU_BLOCK)
        a, b = _group_scan(*gates(xc_ref[pl.ds(base, LRU_BLOCK), :], d), rev=bool(d))

        def store(g, hg):
            out_ref[pl.ds(pl.multiple_of(base + g * LRU_GROUP, LRU_GROUP), LRU_GROUP), :] = hg

        return _chain_groups(a, b, carry, bool(d), store)

    def step(i, carry):
        cf = scan_block(i, carry[0], 0, hf_ref)
        cb = scan_block(_block_order(i, nctx, nblk, True), carry[1], 1, hb_ref)
        return cf, cb

    zero = jnp.zeros((1, HEAD_W), F32)
    lax.fori_loop(0, nblk, step, (zero, zero))
    o_ref[...] = ((hf_ref[...] + hb_ref[...]) * _gelu_tanh(gb_ref[...].astype(F32))).astype(o_ref.dtype)


def _lru(ub, conv_w, conv_b, w_a, b_a, w_x, b_x, lam, *, batch, seq, ctx_len):
    vec = pl.BlockSpec((2, 1, HEAD_W), lambda b, h: (0, 0, h))
    mat = pl.BlockSpec((2, None, HEAD_W, HEAD_W), lambda b, h: (0, h, 0, 0))
    return pl.pallas_call(
        functools.partial(_lru_kernel, ctx_len=ctx_len),
        out_shape=jax.ShapeDtypeStruct((batch * seq, MIX_W), BF16),
        grid=(batch, N_HEADS),
        in_specs=[
            pl.BlockSpec((seq, HEAD_W), lambda b, h: (b, UB_LRU_X // HEAD_W + h)),
            pl.BlockSpec((seq, HEAD_W), lambda b, h: (b, UB_LRU_G // HEAD_W + h)),
            pl.BlockSpec((4, HEAD_W), lambda b, h: (0, h)),
            pl.BlockSpec((1, HEAD_W), lambda b, h: (0, h)),
            mat, vec, mat, vec, vec,
        ],
        out_specs=pl.BlockSpec((seq, HEAD_W), lambda b, h: (b, h)),
        scratch_shapes=[pltpu.VMEM((seq, HEAD_W), F32)] * 3,
        compiler_params=_params(("parallel", "parallel")),
        name="rglru",
    )(ub, ub, conv_w, conv_b, w_a, b_a, w_x, b_x, lam)


def _att_prep_kernel(q_ref, k_ref, cos_ref, sin_ref, qw_ref, kw_ref, grp_ref, qo_ref, kto_ref):
    grp = grp_ref[...]
    cos, sin = cos_ref[...], sin_ref[...]
    lane = lax.broadcasted_iota(jnp.int32, (1, 128), 1)
    first = (lane % (ATT_HEAD_DIM // 2)) < (ATT_HEAD_DIM // 4)

    def norm_rope(x, w, scale):
        ss = _dot_x3(x * x, grp)
        y = x * lax.rsqrt(ss * (1.0 / ATT_HEAD_DIM) + EPS) * w
        rot = jnp.where(first, pltpu.roll(y, 128 - ATT_HEAD_DIM // 4, 1), pltpu.roll(y, ATT_HEAD_DIM // 4, 1))
        return (y * cos + rot * sin) * scale

    q_scale = (ATT_HEAD_DIM ** -0.5) * LOG2_E
    for s in range(q_ref.shape[1] // 128):
        sl = slice(s * 128, (s + 1) * 128)
        qo_ref[:, sl] = norm_rope(q_ref[:, sl].astype(F32), qw_ref[...], q_scale).astype(qo_ref.dtype)
    kto_ref[...] = norm_rope(k_ref[...].astype(F32), kw_ref[...], 1.0).T.astype(kto_ref.dtype)


def _att_prep(ub, cos, sin, qw, kw, grp, *, batch, seq):
    rows = ub.shape[0]
    tm = ATT_TQ
    tpb = seq // tm
    return pl.pallas_call(
        _att_prep_kernel,
        out_shape=(jax.ShapeDtypeStruct((rows, ATT_QW), BF16),
                   jax.ShapeDtypeStruct((batch, ATT_KW, seq), BF16)),
        grid=(rows // tm,),
        in_specs=[
            pl.BlockSpec((tm, ATT_QW), lambda i: (i, UB_ATT_Q // ATT_QW)),
            pl.BlockSpec((tm, ATT_KW), lambda i: (i, UB_ATT_K // ATT_KW)),
            pl.BlockSpec((tm, 128), lambda i: (i % tpb, 0)),
            pl.BlockSpec((tm, 128), lambda i: (i % tpb, 0)),
            pl.BlockSpec((1, 128), lambda i: (0, 0)),
            pl.BlockSpec((1, 128), lambda i: (0, 0)),
            pl.BlockSpec((128, 128), lambda i: (0, 0)),
        ],
        out_specs=(pl.BlockSpec((tm, ATT_QW), lambda i: (i, 0)),
                   pl.BlockSpec((None, ATT_KW, tm), lambda i: (i // tpb, 0, i % tpb))),
        compiler_params=_params(("parallel",)),
        name="att_prep",
    )(ub, ub, cos, sin, qw, kw, grp)


def _att_kernel(q_ref, kt_ref, v_ref, o_ref, *, ctx_len):
    def attend(n_keys):
        v = v_ref[:n_keys, :]

        def scores(head):
            g = head // ATT_GROUP
            kt = kt_ref[g * ATT_HEAD_DIM:(g + 1) * ATT_HEAD_DIM, :n_keys]
            return jnp.dot(q_ref[:, head * ATT_HEAD_DIM:(head + 1) * ATT_HEAD_DIM], kt, preferred_element_type=F32)

        s_next = scores(0)
        for head in range(ATT_Q_HEADS):
            s = s_next
            if head + 1 < ATT_Q_HEADS:
                s_next = scores(head + 1)
            g = head // ATT_GROUP
            p = jnp.exp2(s - jnp.max(s, axis=-1, keepdims=True))
            den = jnp.sum(p, axis=-1, keepdims=True)
            pv = jnp.dot(p.astype(BF16), v, preferred_element_type=F32)
            o_ref[:, head * ATT_HEAD_DIM:(head + 1) * ATT_HEAD_DIM] = (
                pv[:, g * ATT_HEAD_DIM:(g + 1) * ATT_HEAD_DIM] / den).astype(o_ref.dtype)

    is_ctx = pl.program_id(1) == 0

    @pl.when(is_ctx)
    def _():
        attend(ctx_len)

    @pl.when(jnp.logical_not(is_ctx))
    def _():
        attend(kt_ref.shape[1])


def _attention(qn, kt, ub, *, batch, seq, ctx_len):
    tiles = seq // ATT_TQ
    return pl.pallas_call(
        functools.partial(_att_kernel, ctx_len=ctx_len),
        out_shape=jax.ShapeDtypeStruct((batch * seq, ATT_QW), BF16),
        grid=(batch, tiles),
        in_specs=[
            pl.BlockSpec((ATT_TQ, ATT_QW), lambda b, i: (b * tiles + i, 0)),
            pl.BlockSpec((None, ATT_KW, seq), lambda b, i: (b, 0, 0)),
            pl.BlockSpec((seq, ATT_KW), lambda b, i: (b, UB_ATT_V // ATT_KW)),
        ],
        out_specs=pl.BlockSpec((ATT_TQ, ATT_QW), lambda b, i: (b * tiles + i, 0)),
        compiler_params=_params(("parallel", "arbitrary")),
        name="attention",
    )(qn, kt, ub)


def _merge_kernel(oaf_ref, oab_ref, obf_ref, obb_ref, ga_ref, gb_ref, yc_ref, yd_ref, gate_ref, h_ref, modc_ref,
                  modl_ref, dnw_ref, hgw_ref, wb_ref, wo_ref, out_ref, *, tiles_per_batch):
    def gated(of_ref, ob_ref, g_ref, nw_ref):
        o = of_ref[...] + ob_ref[...]
        ys = [_rms_rows(o[:, h * HEAD_W:(h + 1) * HEAD_W], nw_ref[...]) for h in range(N_HEADS)]
        g = g_ref[...].astype(F32)
        return jnp.concatenate(ys, axis=1) * (g * _sigmoid(g))

    ys = (gated(oaf_ref, oab_ref, ga_ref, dnw_ref), gated(obf_ref, obb_ref, gb_ref, hgw_ref), yc_ref[...], yd_ref[...])
    acc = None
    for b in range(N_BRANCH):
        gate = gate_ref[:, b * D_MODEL:(b + 1) * D_MODEL].astype(F32)
        term = _sigmoid(gate) * _dot(ys[b], wb_ref[b])
        acc = term if acc is None else acc + term
    is_ctx = (pl.program_id(0) % tiles_per_batch) == 0
    out_ref[...] = h_ref[...] + _pick_mod(modc_ref, modl_ref, 2, is_ctx) * _dot(acc, wo_ref[...])


def _merge(oaf, oab, obf, obb, ub, yc, yd, h, modc, modl, dnw, hgw, wb, wo, *, seq, ctx_len):
    rows = h.shape[0]
    tm = ctx_len
    tiles_per_batch = seq // tm
    mix = lambda c: pl.BlockSpec((tm, MIX_W), lambda i: (i, c // MIX_W))
    return pl.pallas_call(
        functools.partial(_merge_kernel, tiles_per_batch=tiles_per_batch),
        out_shape=jax.ShapeDtypeStruct((rows, D_MODEL), F32),
        grid=(rows // tm,),
        in_specs=[
            mix(0), mix(0), mix(0), mix(0), mix(UB_DN_G), mix(UB_HG_G), mix(0), mix(0),
            pl.BlockSpec((tm, N_BRANCH * D_MODEL), lambda i: (i, 0)),
            pl.BlockSpec((tm, D_MODEL), lambda i: (i, 0)),
            pl.BlockSpec((6, D_MODEL), lambda i: (0, 0)),
            pl.BlockSpec((None, 6, D_MODEL), lambda i: (i // tiles_per_batch, 0, 0)),
            pl.BlockSpec((1, HEAD_W), lambda i: (0, 0)),
            pl.BlockSpec((1, HEAD_W), lambda i: (0, 0)),
            pl.BlockSpec((N_BRANCH, MIX_W, D_MODEL), lambda i: (0, 0, 0)),
            pl.BlockSpec((D_MODEL, D_MODEL), lambda i: (0, 0)),
        ],
        out_specs=pl.BlockSpec((tm, D_MODEL), lambda i: (i, 0)),
        compiler_params=_params(("parallel",)),
        name="merge",
    )(oaf, oab, obf, obb, ub, ub, yc, yd, ub, h, modc, modl, dnw, hgw, wb, wo)


def _mlp_kernel(h_ref, modc_ref, modl_ref, nw_ref, w1_ref, w2_ref, o_ref, z_ref, acc_ref,
                *, tm, tiles_per_batch, ctx_len):
    j = pl.program_id(1)
    row = (pl.program_id(0) % tiles_per_batch) * tm + lax.broadcasted_iota(jnp.int32, (tm, 1), 0)
    is_ctx = row < ctx_len

    @pl.when(j == 0)
    def _():
        y = _rms_rows(h_ref[...], nw_ref[...])
        shift = _pick_mod(modc_ref, modl_ref, 3, is_ctx)
        scale = _pick_mod(modc_ref, modl_ref, 4, is_ctx)
        z_ref[...] = (y * (1.0 + scale) + shift).astype(BF16)
        acc_ref[...] = jnp.zeros_like(acc_ref)

    a = jnp.maximum(jnp.dot(z_ref[...], w1_ref[...], preferred_element_type=F32), 0.0)
    acc_ref[...] += _dot(a * a, w2_ref[...])

    @pl.when(j == pl.num_programs(1) - 1)
    def _():
        o_ref[...] = h_ref[...] + _pick_mod(modc_ref, modl_ref, 5, is_ctx) * acc_ref[...]


def _mlp(h, modc, modl, nw, w1, w2, *, seq, ctx_len):
    rows = h.shape[0]
    tiles_per_batch = 4
    tm = seq // tiles_per_batch
    kern = functools.partial(_mlp_kernel, tm=tm, tiles_per_batch=tiles_per_batch, ctx_len=ctx_len)
    return pl.pallas_call(
        kern,
        out_shape=jax.ShapeDtypeStruct((rows, D_MODEL), F32),
        grid=(rows // tm, D_FF // FF_BLOCK),
        in_specs=[
            pl.BlockSpec((tm, D_MODEL), lambda i, j: (i, 0)),
            pl.BlockSpec((6, D_MODEL), lambda i, j: (0, 0)),
            pl.BlockSpec((None, 6, D_MODEL), lambda i, j: (i // tiles_per_batch, 0, 0)),
            pl.BlockSpec((1, D_MODEL), lambda i, j: (0, 0)),
            pl.BlockSpec((D_MODEL, FF_BLOCK), lambda i, j: (0, j)),
            pl.BlockSpec((FF_BLOCK, D_MODEL), lambda i, j: (j, 0)),
        ],
        out_specs=pl.BlockSpec((tm, D_MODEL), lambda i, j: (i, 0)),
        scratch_shapes=[pltpu.VMEM((tm, D_MODEL), BF16), pltpu.VMEM((tm, D_MODEL), F32)],
        compiler_params=_params(("parallel", "arbitrary")),
        name="mlp",
    )(h, modc, modl, nw, w1, w2)


def _regroup_w_in(w_in):
    o = np.cumsum([0, 512, 512, 512, 512, 8, 8, 512, 512, 512, 512, 512, 512, 512, 512, 128, 128, 4096])
    cols = lambda a, b: w_in[..., o[a]:o[b]]
    wb = jnp.concatenate([cols(16, 17), cols(6, 7), cols(9, 11), cols(0, 3), cols(11, 13), cols(3, 4), cols(13, 16)],
                         axis=-1)
    wf = jnp.concatenate([cols(7, 9), cols(4, 6)], axis=-1)
    pad = lambda w, n: jnp.pad(w, ((0, 0), (0, 0), (0, n - w.shape[-1]))).astype(BF16)
    return pad(wb, N_UB), pad(wf, N_UF)


def _rope_tables(t_len, ctx_len):
    rows = t_len // GRID_W
    row_id = jnp.repeat(jnp.arange(rows), GRID_W).astype(F32)
    col_id = jnp.tile(jnp.arange(GRID_W), rows).astype(F32)
    axis_dim = ATT_HEAD_DIM // 2
    inv = ROPE_THETA ** (-jnp.arange(0, axis_dim, 2, dtype=F32) / axis_dim)
    ang = jnp.stack([row_id[:, None] * inv, col_id[:, None] * inv], axis=1)
    cos, sin = jnp.cos(ang), jnp.sin(ang)
    cos_h = jnp.concatenate([cos, cos], axis=-1).reshape(t_len, ATT_HEAD_DIM)
    sin_h = jnp.concatenate([-sin, sin], axis=-1).reshape(t_len, ATT_HEAD_DIM)
    cos_t = jnp.concatenate([jnp.ones((ctx_len, ATT_HEAD_DIM), F32), cos_h], axis=0)
    sin_t = jnp.concatenate([jnp.zeros((ctx_len, ATT_HEAD_DIM), F32), sin_h], axis=0)
    return jnp.tile(cos_t, (1, 2)), jnp.tile(sin_t, (1, 2))


def _hgrn2_lower_bounds(p):
    sm = jax.nn.softmax(p.astype(F32), axis=1)
    cs = jnp.cumsum(sm, axis=1)
    return cs - cs[:, :1]


def kernel(x, c, ctx, c_ctx, mod_w, mod_b, norm1_w, norm2_w, w_in, dn_conv_w, dn_a_log, dn_dt_bias, dn_norm_w,
           hg_lower_bounds, hg_norm_w, lru_conv_w, lru_conv_b, lru_w_a, lru_b_a, lru_w_x, lru_b_x, lru_lambda,
           att_q_norm_w, att_k_norm_w, w_branch, w_out, mlp_w1, mlp_w2):
    batch, t_len, _ = x.shape
    ctx_len = ctx.shape[1]
    depth = mod_w.shape[0]
    seq = ctx_len + t_len
    n_chunks = seq // CHUNK
    assert ctx_len % SCAN_ROWS == 0 and t_len % SCAN_ROWS == 0 and ctx_len == ATT_TQ
    dims = dict(batch=batch, seq=seq, ctx_len=ctx_len)

    w_ub, w_uf = _regroup_w_in(w_in)
    w_branch_b, w_out_b = w_branch.astype(BF16), w_out.astype(BF16)
    w1_b, w2_b = mlp_w1.astype(BF16), mlp_w2.astype(BF16)
    lru_wa_b, lru_wx_b = lru_w_a.astype(BF16), lru_w_x.astype(BF16)
    lb_all = _hgrn2_lower_bounds(hg_lower_bounds)
    cos_t, sin_t = _rope_tables(t_len, ctx_len)
    grp = jnp.asarray((np.arange(128)[:, None] // ATT_HEAD_DIM) == (np.arange(128)[None, :] // ATT_HEAD_DIM), F32)

    n_mod_rows = batch + 8
    cc = jnp.zeros((n_mod_rows, D_MODEL), F32).at[:batch].set(c).at[batch].set(c_ctx)
    mods = _modulations(cc, mod_w.astype(BF16), mod_b)

    h = jnp.concatenate([ctx, x], axis=1).reshape(batch * seq, D_MODEL)
    for l in range(depth):
        modl = mods[l, :batch].reshape(batch, 6, D_MODEL)
        modc = mods[l, batch].reshape(6, D_MODEL)
        nw1 = norm1_w[l].reshape(1, D_MODEL)
        ub = _inproj(h, modc, modl, nw1, w_ub[l], seq=seq, ctx_len=ctx_len, tn=TN_UB, out_dtype=BF16, name="inproj_b")
        uf = _inproj(h, modc, modl, nw1, w_uf[l], seq=seq, ctx_len=ctx_len, tn=N_UF, out_dtype=F32, name="inproj_f")

        qkv = _dn_prep(ub, dn_conv_w[l], **dims)
        ba = uf[:, UF_DN_BA:UF_DN_BA + 4 * N_HEADS].reshape(batch * n_chunks, CHUNK, 2, 2, N_HEADS)
        ba_t = jnp.transpose(ba, (3, 0, 2, 4, 1)).reshape(2, batch * n_chunks, 2 * N_HEADS, CHUNK)
        oa = _gdn(qkv, ba_t, dn_a_log[l].reshape(2, N_HEADS, 1), dn_dt_bias[l].reshape(2, N_HEADS, 1), **dims)

        ob = _hgrn2(ub, uf, lb_all[:, l].reshape(2, 1, MIX_W), **dims)

        yc = _lru(ub, lru_conv_w[l], lru_conv_b[l].reshape(1, MIX_W), lru_wa_b[l], lru_b_a[l].reshape(2, 1, MIX_W),
                  lru_wx_b[l], lru_b_x[l].reshape(2, 1, MIX_W), lru_lambda[l].reshape(2, 1, MIX_W), **dims)

        qn, kt = _att_prep(ub, cos_t, sin_t, jnp.tile(att_q_norm_w[l], 2).reshape(1, 128),
                           jnp.tile(att_k_norm_w[l], 2).reshape(1, 128), grp, batch=batch, seq=seq)
        yd = _attention(qn, kt, ub, **dims)

        h = _merge(oa[0], oa[1], ob[0], ob[1], ub, yc, yd, h, modc, modl, dn_norm_w[l].reshape(1, HEAD_W),
                   hg_norm_w[l].reshape(1, HEAD_W), w_branch_b[l], w_out_b[l], seq=seq, ctx_len=ctx_len)
        h = _mlp(h, modc, modl, norm2_w[l].reshape(1, D_MODEL), w1_b[l], w2_b[l], seq=seq, ctx_len=ctx_len)

    return h.reshape(batch, seq, D_MODEL)[:, ctx_len:]
```

```python
import functools

import jax
import jax.numpy as jnp
import numpy as np
from jax import lax
from jax.experimental import pallas as pl
from jax.experimental.pallas import tpu as pltpu

F32 = jnp.float32
BF16 = jnp.bfloat16

EPS = 1e-6
D_MODEL = 1024
GRID_W = 64
N_HEADS = 4
HEAD_W = 128
MIX_W = N_HEADS * HEAD_W
CHUNK = 64
SUB = 16
SCAN_ROWS = 256
LRU_C = 8.0
LRU_BLOCK = 256
ATT_Q_HEADS = 8
ATT_KV_HEADS = 2
ATT_HEAD_DIM = 64
ATT_GROUP = ATT_Q_HEADS // ATT_KV_HEADS
ATT_QW = ATT_Q_HEADS * ATT_HEAD_DIM
ATT_KW = ATT_KV_HEADS * ATT_HEAD_DIM
ATT_TQ = 256
ROPE_THETA = 10000.0
LOG2_E = 1.4426950408889634
N_BRANCH = 4
D_FF = 4 * D_MODEL
FF_BLOCK = 1024

UB_GATE = 0
UB_HG_Q, UB_HG_I, UB_HG_G = 4096, 4608, 5120
UB_DN_Q = 5632
UB_LRU_X, UB_LRU_G = 7168, 7680
UB_DN_G = 8192
UB_ATT_Q, UB_ATT_K, UB_ATT_V = 8704, 9216, 9344
N_UB_USED = 9472
TN_UB = 1664
N_UB = 6 * TN_UB
UF_HG_FF, UF_HG_FB, UF_DN_BA = 0, 512, 1024
N_UF = 1280

VMEM_LIMIT_V7X = 48 * 1024 * 1024


def _params(sem, vmem=VMEM_LIMIT_V7X):
    return pltpu.CompilerParams(dimension_semantics=sem, vmem_limit_bytes=vmem)


def _sigmoid(x):
    return 1.0 / (1.0 + jnp.exp(-x))


def _softplus(x):
    return jnp.maximum(x, 0.0) + jnp.log1p(jnp.exp(-jnp.abs(x)))


def _dot(a, b):
    return jnp.dot(a.astype(BF16), b.astype(BF16), preferred_element_type=F32)


def _dot_nt(a, b):
    return lax.dot_general(a.astype(BF16), b.astype(BF16), (((1,), (1,)), ((), ())),
                           preferred_element_type=F32)


def _dot_tn(a, b):
    return lax.dot_general(a.astype(BF16), b.astype(BF16), (((0,), (0,)), ((), ())),
                           preferred_element_type=F32)


def _split3(x):
    hi = x.astype(BF16)
    r1 = x - hi.astype(F32)
    mid = r1.astype(BF16)
    lo = (r1 - mid.astype(F32)).astype(BF16)
    return hi, mid, lo


def _dot_x3(x, m):
    m = m.astype(BF16)
    hi, mid, lo = _split3(x)
    out = jnp.dot(hi, m, preferred_element_type=F32)
    out = out + jnp.dot(mid, m, preferred_element_type=F32)
    return out + jnp.dot(lo, m, preferred_element_type=F32)


def _dot_3x_many(m, xs):
    w = xs[0].shape[1]
    pieces = [_split3(x) for x in xs]
    cols = [p[i] for i in range(3) for p in pieces]
    wide = jnp.dot(m.astype(BF16), jnp.concatenate(cols, axis=1), preferred_element_type=F32)
    n = len(xs)
    part = lambda p, i: wide[:, (p * n + i) * w:(p * n + i + 1) * w]
    return [part(0, i) + part(1, i) + part(2, i) for i in range(n)]


def _rms_rows(x, w):
    return x * lax.rsqrt(jnp.mean(x * x, axis=-1, keepdims=True) + EPS) * w


def _mod_kernel(c_ref, w_ref, b_ref, o_ref):
    c = c_ref[...]
    o_ref[...] = _dot(c * _sigmoid(c), w_ref[...]) + b_ref[...]


def _modulations(cc, mod_w, mod_b):
    depth = mod_w.shape[0]
    rows = cc.shape[0]
    n_out = mod_w.shape[2]
    return pl.pallas_call(
        _mod_kernel,
        out_shape=jax.ShapeDtypeStruct((depth, rows, n_out), F32),
        grid=(depth, n_out // D_MODEL),
        in_specs=[
            pl.BlockSpec((rows, D_MODEL), lambda l, j: (0, 0)),
            pl.BlockSpec((None, D_MODEL, D_MODEL), lambda l, j: (l, 0, j)),
            pl.BlockSpec((None, 1, D_MODEL), lambda l, j: (l, 0, j)),
        ],
        out_specs=pl.BlockSpec((None, rows, D_MODEL), lambda l, j: (l, 0, j)),
        compiler_params=_params(("parallel", "parallel")),
        name="modulations",
    )(cc, mod_w, mod_b.reshape(depth, 1, n_out))


def _pick_mod(modc_ref, modl_ref, idx, is_ctx):
    return jnp.where(is_ctx, modc_ref[idx:idx + 1, :], modl_ref[idx:idx + 1, :])


def _inproj_kernel(h_ref, modc_ref, modl_ref, nw_ref, wb_ref, wf_ref, ob_ref, of_ref, xn_ref,
                   *, tm, tiles_per_batch, ctx_len):
    j = pl.program_id(1)

    @pl.when(j == 0)
    def _():
        y = _rms_rows(h_ref[...], nw_ref[...])
        row = (pl.program_id(0) % tiles_per_batch) * tm + lax.broadcasted_iota(jnp.int32, (tm, 1), 0)
        is_ctx = row < ctx_len
        shift = _pick_mod(modc_ref, modl_ref, 0, is_ctx)
        scale = _pick_mod(modc_ref, modl_ref, 1, is_ctx)
        xn_ref[...] = (y * (1.0 + scale) + shift).astype(BF16)

    ob_ref[...] = jnp.dot(xn_ref[...], wb_ref[...], preferred_element_type=F32).astype(ob_ref.dtype)

    @pl.when(j == pl.num_programs(1) - 1)
    def _():
        of_ref[...] = jnp.dot(xn_ref[...], wf_ref[...], preferred_element_type=F32)


def _inproj(h, modc, modl, nw, w_b, w_f, *, seq, ctx_len):
    rows = h.shape[0]
    tiles_per_batch = 4
    tm = seq // tiles_per_batch
    kern = functools.partial(_inproj_kernel, tm=tm, tiles_per_batch=tiles_per_batch, ctx_len=ctx_len)
    return pl.pallas_call(
        kern,
        out_shape=(jax.ShapeDtypeStruct((rows, N_UB), BF16), jax.ShapeDtypeStruct((rows, N_UF), F32)),
        grid=(rows // tm, N_UB // TN_UB),
        in_specs=[
            pl.BlockSpec((tm, D_MODEL), lambda i, j: (i, 0)),
            pl.BlockSpec((6, D_MODEL), lambda i, j: (0, 0)),
            pl.BlockSpec((None, 6, D_MODEL), lambda i, j: (i // tiles_per_batch, 0, 0)),
            pl.BlockSpec((1, D_MODEL), lambda i, j: (0, 0)),
            pl.BlockSpec((D_MODEL, TN_UB), lambda i, j: (0, j)),
            pl.BlockSpec((D_MODEL, N_UF), lambda i, j: (0, 0)),
        ],
        out_specs=(pl.BlockSpec((tm, TN_UB), lambda i, j: (i, j)),
                   pl.BlockSpec((tm, N_UF), lambda i, j: (i, 0))),
        scratch_shapes=[pltpu.VMEM((tm, D_MODEL), BF16)],
        compiler_params=_params(("parallel", "arbitrary")),
        name="inproj",
    )(h, modc, modl, nw, w_b, w_f)


def _seg_conv(x, w_ref, ctx_len):
    n = x.shape[0]
    row = lax.broadcasted_iota(jnp.int32, (n, 1), 0)
    lo = jnp.where(row >= ctx_len, ctx_len, 0)
    hi = jnp.where(row >= ctx_len, n, ctx_len)

    def tap(k):
        tk = row + k
        valid = jnp.logical_and(tk >= lo, tk < hi)
        return jnp.where(valid, pltpu.roll(x, (-k) % n, 0), 0.0)

    return (tap(-1) * w_ref[0:1, :] + x * w_ref[1:2, :] + tap(1) * w_ref[2:3, :] + tap(2) * w_ref[3:4, :])


def _dn_prep_kernel(u_ref, w_ref, o_ref, *, ctx_len):
    j = pl.program_id(1)
    y = _seg_conv(u_ref[...].astype(F32), w_ref, ctx_len)
    y = y * _sigmoid(y)
    n = lax.rsqrt(jnp.sum(y * y, axis=-1, keepdims=True) + EPS)
    fac = jnp.where(j < N_HEADS, n * (HEAD_W ** -0.5), jnp.where(j < 2 * N_HEADS, n, 1.0))
    o_ref[...] = (y * fac).astype(o_ref.dtype)


def _dn_prep(ub, conv_w, *, batch, seq, ctx_len):
    nblk = 3 * N_HEADS
    return pl.pallas_call(
        functools.partial(_dn_prep_kernel, ctx_len=ctx_len),
        out_shape=jax.ShapeDtypeStruct((batch * seq, 3 * MIX_W), BF16),
        grid=(batch, nblk),
        in_specs=[
            pl.BlockSpec((seq, HEAD_W), lambda b, j: (b, UB_DN_Q // HEAD_W + j)),
            pl.BlockSpec((4, HEAD_W), lambda b, j: (0, j)),
        ],
        out_specs=pl.BlockSpec((seq, HEAD_W), lambda b, j: (b, j)),
        compiler_params=_params(("parallel", "parallel")),
        name="dn_prep",
    )(ub, conv_w)


def _block_order(s, n_ctx, n_all, rev):
    if not rev:
        return s
    return jnp.where(s < n_ctx, n_ctx - 1 - s, n_all + n_ctx - 1 - s)


def _chunk_masks(rev):
    ii = lax.broadcasted_iota(jnp.int32, (CHUNK, CHUNK), 0)
    jj = lax.broadcasted_iota(jnp.int32, (CHUNK, CHUNK), 1)
    incl = (ii <= jj) if rev else (ii >= jj)
    incl_t = (ii >= jj) if rev else (ii <= jj)
    return ii, jj, incl, incl_t


def _unit_tri_inverse(a_all, eye, bd):
    ad = [jnp.where(bd, a, 0.0) for a in a_all]
    ao = [a - d for a, d in zip(a_all, ad)]
    p = [-d for d in ad]
    dinv = [eye + x for x in p]
    for _ in range(3):
        p = [_dot(x, x) for x in p]
        dinv = [d + _dot(d, x) for d, x in zip(dinv, p)]
    m = [-_dot(d, o) for d, o in zip(dinv, ao)]
    mm = [_dot(x, x) for x in m]
    t = [eye + x for x in m]
    t = [x + _dot(x, y) for x, y in zip(t, mm)]
    return [_dot(x, d) for x, d in zip(t, dinv)]


def _gdn_prepare(q_ref, k_ref, v_ref, ba_ref, alog, dtb, rev):
    ii, jj, incl, incl_t = _chunk_masks(rev)
    is_eye = ii == jj
    eye = jnp.where(is_eye, 1.0, 0.0)
    tri = jnp.where(incl, 1.0, 0.0)
    tri_t = jnp.where(incl_t, 1.0, 0.0)
    bd = (ii // SUB) == (jj // SUB)
    n_chunks = q_ref.shape[0] // CHUNK

    alog_neg = -jnp.exp(alog)
    chains = [(j, h) for j in range(n_chunks) for h in range(N_HEADS)]
    vec = {}
    for j in range(n_chunks):
        ba = ba_ref[j]
        beta_r = _sigmoid(ba[0:N_HEADS])
        g_r = alog_neg * _softplus(ba[N_HEADS:2 * N_HEADS] + dtb)
        for h in range(N_HEADS):
            gr = g_r[h:h + 1]
            beta_c = jnp.sum(eye * beta_r[h:h + 1], axis=1, keepdims=True)
            g_c = jnp.sum(eye * gr, axis=1, keepdims=True)
            gam_c = jnp.sum(tri * gr, axis=1, keepdims=True)
            gam_r = jnp.sum(tri_t * g_c, axis=0, keepdims=True)
            tot = jnp.sum(gr, axis=1, keepdims=True)
            dec_i = jnp.where(incl, jnp.exp(gam_c - gam_r), 0.0)
            vec[j, h] = (beta_c, gam_c, tot, dec_i)

    def tile(ref, j, h):
        return ref[j * CHUNK:(j + 1) * CHUNK, h * HEAD_W:(h + 1) * HEAD_W]

    k16 = [tile(k_ref, *c) for c in chains]
    q16 = [tile(q_ref, *c) for c in chains]
    ks = [k.astype(F32) for k in k16]
    kbs = [k * vec[c][0] for k, c in zip(ks, chains)]
    kq = [_dot_nt(jnp.concatenate([kb.astype(BF16), q], axis=0), k) for kb, q, k in zip(kbs, q16, k16)]
    a_low = [jnp.where(is_eye, 0.0, x[:CHUNK] * vec[c][3]) for x, c in zip(kq, chains)]
    a_qk = [x[CHUNK:] * vec[c][3] for x, c in zip(kq, chains)]
    t_inv = _unit_tri_inverse(a_low, eye, bd)
    egc = [jnp.exp(vec[c][1]) for c in chains]
    sols = [_dot(t, jnp.concatenate([tile(v_ref, *c).astype(F32) * vec[c][0], kb * e], axis=1))
            for t, kb, e, c in zip(t_inv, kbs, egc, chains)]
    pre = {}
    for c, sol, aq, q, k, e in zip(chains, sols, a_qk, q16, ks, egc):
        q_dec = q.astype(F32) * e
        k_dec_t = (k * jnp.exp(vec[c][2] - vec[c][1])).T
        pre[c] = (sol[:, :HEAD_W], jnp.concatenate([sol[:, HEAD_W:], q_dec], axis=0).astype(BF16),
                  jnp.concatenate([aq, k_dec_t], axis=0).astype(BF16), jnp.exp(vec[c][2]))
    return pre


def _gdn_kernel(qf_ref, kf_ref, vf_ref, baf_ref, qr_ref, kr_ref, vr_ref, bar_ref, alog_ref, dtb_ref,
                of_ref, or_ref, s_ref):
    @pl.when(pl.program_id(1) == 0)
    def _():
        s_ref[...] = jnp.zeros_like(s_ref)

    n_chunks = qf_ref.shape[0] // CHUNK
    pre = (_gdn_prepare(qf_ref, kf_ref, vf_ref, baf_ref, alog_ref[0], dtb_ref[0], False),
           _gdn_prepare(qr_ref, kr_ref, vr_ref, bar_ref, alog_ref[1], dtb_ref[1], True))
    o_refs = (of_ref, or_ref)

    lanes = [(d, h) for d in range(2) for h in range(N_HEADS)]
    states = {dh: s_ref[dh[0], dh[1]] for dh in lanes}
    for t in range(n_chunks):
        js = (t, n_chunks - 1 - t)
        cur = {dh: pre[dh[0]][js[dh[0]], dh[1]] for dh in lanes}
        wq = {dh: _dot(cur[dh][1], states[dh]) for dh in lanes}
        v_new = {dh: cur[dh][0] - wq[dh][:CHUNK] for dh in lanes}
        ak = {dh: _dot(cur[dh][2], v_new[dh]) for dh in lanes}
        for d, h in lanes:
            j = js[d]
            o_refs[d][j * CHUNK:(j + 1) * CHUNK, h * HEAD_W:(h + 1) * HEAD_W] = wq[d, h][CHUNK:] + ak[d, h][:CHUNK]
        states = {dh: states[dh] * cur[dh][3] + ak[dh][CHUNK:] for dh in lanes}
    for d, h in lanes:
        s_ref[d, h] = states[d, h]


def _gdn(qkv, ba_t, a_log, dt_bias, *, batch, seq, ctx_len):
    n_all = seq // SCAN_ROWS
    n_ctx = ctx_len // SCAN_ROWS
    cpb = SCAN_ROWS // CHUNK

    def rows(b, s, rev):
        return b * n_all + _block_order(s, n_ctx, n_all, rev)

    def in_specs(rev):
        d = int(rev)
        return [
            pl.BlockSpec((SCAN_ROWS, MIX_W), lambda b, s: (rows(b, s, rev), 0)),
            pl.BlockSpec((SCAN_ROWS, MIX_W), lambda b, s: (rows(b, s, rev), 1)),
            pl.BlockSpec((SCAN_ROWS, MIX_W), lambda b, s: (rows(b, s, rev), 2)),
            pl.BlockSpec((None, cpb, 2 * N_HEADS, CHUNK), lambda b, s: (d, rows(b, s, rev), 0, 0)),
        ]

    vec = pl.BlockSpec((2, N_HEADS, 1), lambda b, s: (0, 0, 0))
    out = jax.ShapeDtypeStruct((batch * seq, MIX_W), F32)
    return pl.pallas_call(
        _gdn_kernel,
        out_shape=(out, out),
        grid=(batch, n_all),
        in_specs=in_specs(False) + in_specs(True) + [vec, vec],
        out_specs=(pl.BlockSpec((SCAN_ROWS, MIX_W), lambda b, s: (rows(b, s, False), 0)),
                   pl.BlockSpec((SCAN_ROWS, MIX_W), lambda b, s: (rows(b, s, True), 0))),
        scratch_shapes=[pltpu.VMEM((2, N_HEADS, HEAD_W, HEAD_W), F32)],
        compiler_params=_params(("parallel", "arbitrary")),
        name="gdn",
    )(qkv, qkv, qkv, ba_t, qkv, qkv, qkv, ba_t, a_log, dt_bias)


def _hgrn2_prepare(q_ref, f_ref, i_ref, lb_ref, rev):
    _, _, incl, _ = _chunk_masks(rev)
    tri = jnp.where(incl, 1.0, 0.0)
    row = lax.broadcasted_iota(jnp.int32, (CHUNK, 1), 0)
    n_sub = CHUNK // SUB
    n_chunks = q_ref.shape[0] // CHUNK

    def tile(ref, j, h):
        return ref[j * CHUNK:(j + 1) * CHUNK, h * HEAD_W:(h + 1) * HEAD_W]

    chains = [(j, h) for j in range(n_chunks) for h in range(N_HEADS)]
    qs_, ks_, lfs = [], [], []
    for j, h in chains:
        qr = tile(q_ref, j, h).astype(F32)
        lb = lb_ref[:, h * HEAD_W:(h + 1) * HEAD_W]
        f = lb + (1.0 - lb) * _sigmoid(tile(f_ref, j, h))
        qs_.append(qr * _sigmoid(qr))
        ks_.append(1.0 - f)
        lfs.append(jnp.log(f))
    gcs = _dot_3x_many(tri, lfs)
    tots = [jnp.sum(lf, axis=0, keepdims=True) for lf in lfs]
    blocks = []
    for q, k, gc in zip(qs_, ks_, gcs):
        row_blocks = []
        for i in range(n_sub):
            mid = i * SUB + SUB // 2
            gref = gc[mid:mid + 1, :]
            qsc = q[i * SUB:(i + 1) * SUB, :] * jnp.exp(gc[i * SUB:(i + 1) * SUB, :] - gref)
            reach = (row >= i * SUB) if rev else (row < (i + 1) * SUB)
            ksc = jnp.where(reach, k * jnp.exp(gref - gc), 0.0)
            row_blocks.append((qsc, ksc))
        blocks.append(row_blocks)
    scores = [jnp.where(incl, jnp.concatenate([_dot_nt(a, b) for a, b in rb], axis=0), 0.0) for rb in blocks]
    vs = [tile(i_ref, *c) for c in chains]
    local = [_dot(sc, v) for sc, v in zip(scores, vs)]
    incr = [_dot_tn(v, k * jnp.exp(tot - gc)) for v, k, gc, tot in zip(vs, ks_, gcs, tots)]
    return {c: (q * jnp.exp(gc), lo, inc, jnp.exp(tot))
            for c, q, gc, lo, inc, tot in zip(chains, qs_, gcs, local, incr, tots)}


def _hgrn2_kernel(qf_ref, ff_ref, if_ref, qr_ref, fr_ref, ir_ref, lb_ref, of_ref, or_ref, s_ref):
    @pl.when(pl.program_id(1) == 0)
    def _():
        s_ref[...] = jnp.zeros_like(s_ref)

    n_chunks = qf_ref.shape[0] // CHUNK
    pre = (_hgrn2_prepare(qf_ref, ff_ref, if_ref, lb_ref.at[0], False),
           _hgrn2_prepare(qr_ref, fr_ref, ir_ref, lb_ref.at[1], True))
    o_refs = (of_ref, or_ref)

    lanes = [(d, h) for d in range(2) for h in range(N_HEADS)]
    states = {dh: s_ref[dh[0], dh[1]] for dh in lanes}
    for t in range(n_chunks):
        js = (t, n_chunks - 1 - t)
        cur = {dh: pre[dh[0]][js[dh[0]], dh[1]] for dh in lanes}
        qs = {dh: _dot_nt(cur[dh][0], states[dh]) for dh in lanes}
        for d, h in lanes:
            j = js[d]
            o_refs[d][j * CHUNK:(j + 1) * CHUNK, h * HEAD_W:(h + 1) * HEAD_W] = qs[d, h] + cur[d, h][1]
        states = {dh: states[dh] * cur[dh][3] + cur[dh][2] for dh in lanes}
    for d, h in lanes:
        s_ref[d, h] = states[d, h]


def _hgrn2(ub, uf, lb, *, batch, seq, ctx_len):
    n_all = seq // SCAN_ROWS
    n_ctx = ctx_len // SCAN_ROWS

    def rows(b, s, rev):
        return b * n_all + _block_order(s, n_ctx, n_all, rev)

    def in_specs(rev):
        f_col = (UF_HG_FB if rev else UF_HG_FF) // MIX_W
        return [
            pl.BlockSpec((SCAN_ROWS, MIX_W), lambda b, s: (rows(b, s, rev), UB_HG_Q // MIX_W)),
            pl.BlockSpec((SCAN_ROWS, MIX_W), lambda b, s: (rows(b, s, rev), f_col)),
            pl.BlockSpec((SCAN_ROWS, MIX_W), lambda b, s: (rows(b, s, rev), UB_HG_I // MIX_W)),
        ]

    out = jax.ShapeDtypeStruct((batch * seq, MIX_W), F32)
    return pl.pallas_call(
        _hgrn2_kernel,
        out_shape=(out, out),
        grid=(batch, n_all),
        in_specs=in_specs(False) + in_specs(True) + [pl.BlockSpec((2, 1, MIX_W), lambda b, s: (0, 0, 0))],
        out_specs=(pl.BlockSpec((SCAN_ROWS, MIX_W), lambda b, s: (rows(b, s, False), 0)),
                   pl.BlockSpec((SCAN_ROWS, MIX_W), lambda b, s: (rows(b, s, True), 0))),
        scratch_shapes=[pltpu.VMEM((2, N_HEADS, HEAD_W, HEAD_W), F32)],
        compiler_params=_params(("parallel", "arbitrary")),
        name="hgrn2",
    )(ub, uf, ub, ub, uf, ub, lb)


LRU_GROUP = 8


def _group_scan(a, b, rev):
    n = a.shape[0]
    sub = lax.broadcasted_iota(jnp.int32, (n, 1), 0) % LRU_GROUP
    s = 1
    while s < LRU_GROUP:
        if rev:
            a_s, b_s, valid = pltpu.roll(a, n - s, 0), pltpu.roll(b, n - s, 0), sub < LRU_GROUP - s
        else:
            a_s, b_s, valid = pltpu.roll(a, s, 0), pltpu.roll(b, s, 0), sub >= s
        b = jnp.where(valid, a * b_s + b, b)
        a = jnp.where(valid, a * a_s, a)
        s *= 2
    return a, b


def _chain_groups(a, b, carry, rev, store):
    n = a.shape[0] // LRU_GROUP
    for g in (range(n - 1, -1, -1) if rev else range(n)):
        hg = a[g * LRU_GROUP:(g + 1) * LRU_GROUP, :] * carry + b[g * LRU_GROUP:(g + 1) * LRU_GROUP, :]
        store(g, hg)
        carry = hg[0:1, :] if rev else hg[LRU_GROUP - 1:LRU_GROUP, :]
    return carry


def _gelu_tanh(x):
    return 0.5 * x * (1.0 + jnp.tanh(0.7978845608028654 * (x + 0.044715 * (x * x * x))))


def _lru_kernel(xb_ref, gb_ref, cw_ref, cb_ref, wa_ref, ba_ref, wx_ref, bx_ref, lam_ref, o_ref, xc_ref, hf_ref,
                hb_ref, *, ctx_len):
    n = xb_ref.shape[0]
    nblk = n // LRU_BLOCK
    nctx = ctx_len // LRU_BLOCK
    xc_ref[...] = _seg_conv(xb_ref[...].astype(F32), cw_ref, ctx_len) + cb_ref[...]

    def gates(xc, d):
        r = _sigmoid(_dot(xc, wa_ref[d]) + ba_ref[d])
        ig = _sigmoid(_dot(xc, wx_ref[d]) + bx_ref[d])
        log_a = -LRU_C * r * _softplus(-lam_ref[d])
        a = jnp.exp(log_a)
        return a, jnp.sqrt(1.0 - a * a) * ig * xc

    def scan_block(blk, carry, d, out_ref):
        base = pl.multiple_of(blk * LRU_BLOCK, LRU_BLOCK)
        a, b = _group_scan(*gates(xc_ref[pl.ds(base, LRU_BLOCK), :], d), rev=bool(d))

        def store(g, hg):
            out_ref[pl.ds(pl.multiple_of(base + g * LRU_GROUP, LRU_GROUP), LRU_GROUP), :] = hg

        return _chain_groups(a, b, carry, bool(d), store)

    def step(i, carry):
        cf = scan_block(i, carry[0], 0, hf_ref)
        cb = scan_block(_block_order(i, nctx, nblk, True), carry[1], 1, hb_ref)
        return cf, cb

    zero = jnp.zeros((1, HEAD_W), F32)
    lax.fori_loop(0, nblk, step, (zero, zero))
    o_ref[...] = ((hf_ref[...] + hb_ref[...]) * _gelu_tanh(gb_ref[...].astype(F32))).astype(o_ref.dtype)


def _lru(ub, conv_w, conv_b, w_a, b_a, w_x, b_x, lam, *, batch, seq, ctx_len):
    vec = pl.BlockSpec((2, 1, HEAD_W), lambda b, h: (0, 0, h))
    mat = pl.BlockSpec((2, None, HEAD_W, HEAD_W), lambda b, h: (0, h, 0, 0))
    return pl.pallas_call(
        functools.partial(_lru_kernel, ctx_len=ctx_len),
        out_shape=jax.ShapeDtypeStruct((batch * seq, MIX_W), BF16),
        grid=(batch, N_HEADS),
        in_specs=[
            pl.BlockSpec((seq, HEAD_W), lambda b, h: (b, UB_LRU_X // HEAD_W + h)),
            pl.BlockSpec((seq, HEAD_W), lambda b, h: (b, UB_LRU_G // HEAD_W + h)),
            pl.BlockSpec((4, HEAD_W), lambda b, h: (0, h)),
            pl.BlockSpec((1, HEAD_W), lambda b, h: (0, h)),
            mat, vec, mat, vec, vec,
        ],
        out_specs=pl.BlockSpec((seq, HEAD_W), lambda b, h: (b, h)),
        scratch_shapes=[pltpu.VMEM((seq, HEAD_W), F32)] * 3,
        compiler_params=_params(("parallel", "parallel")),
        name="rglru",
    )(ub, ub, conv_w, conv_b, w_a, b_a, w_x, b_x, lam)


def _att_prep_kernel(q_ref, k_ref, cos_ref, sin_ref, qw_ref, kw_ref, grp_ref, qo_ref, kto_ref):
    grp = grp_ref[...]
    cos, sin = cos_ref[...], sin_ref[...]
    lane = lax.broadcasted_iota(jnp.int32, (1, 128), 1)
    first = (lane % (ATT_HEAD_DIM // 2)) < (ATT_HEAD_DIM // 4)

    def norm_rope(x, w, scale):
        ss = _dot_x3(x * x, grp)
        y = x * lax.rsqrt(ss * (1.0 / ATT_HEAD_DIM) + EPS) * w
        rot = jnp.where(first, pltpu.roll(y, 128 - ATT_HEAD_DIM // 4, 1), pltpu.roll(y, ATT_HEAD_DIM // 4, 1))
        return (y * cos + rot * sin) * scale

    q_scale = (ATT_HEAD_DIM ** -0.5) * LOG2_E
    for s in range(q_ref.shape[1] // 128):
        sl = slice(s * 128, (s + 1) * 128)
        qo_ref[:, sl] = norm_rope(q_ref[:, sl].astype(F32), qw_ref[...], q_scale).astype(qo_ref.dtype)
    kto_ref[...] = norm_rope(k_ref[...].astype(F32), kw_ref[...], 1.0).T.astype(kto_ref.dtype)


def _att_prep(ub, cos, sin, qw, kw, grp, *, batch, seq):
    rows = ub.shape[0]
    tm = ATT_TQ
    tpb = seq // tm
    return pl.pallas_call(
        _att_prep_kernel,
        out_shape=(jax.ShapeDtypeStruct((rows, ATT_QW), BF16),
                   jax.ShapeDtypeStruct((batch, ATT_KW, seq), BF16)),
        grid=(rows // tm,),
        in_specs=[
            pl.BlockSpec((tm, ATT_QW), lambda i: (i, UB_ATT_Q // ATT_QW)),
            pl.BlockSpec((tm, ATT_KW), lambda i: (i, UB_ATT_K // ATT_KW)),
            pl.BlockSpec((tm, 128), lambda i: (i % tpb, 0)),
            pl.BlockSpec((tm, 128), lambda i: (i % tpb, 0)),
            pl.BlockSpec((1, 128), lambda i: (0, 0)),
            pl.BlockSpec((1, 128), lambda i: (0, 0)),
            pl.BlockSpec((128, 128), lambda i: (0, 0)),
        ],
        out_specs=(pl.BlockSpec((tm, ATT_QW), lambda i: (i, 0)),
                   pl.BlockSpec((None, ATT_KW, tm), lambda i: (i // tpb, 0, i % tpb))),
        compiler_params=_params(("parallel",)),
        name="att_prep",
    )(ub, ub, cos, sin, qw, kw, grp)


def _att_kernel(q_ref, kt_ref, v_ref, o_ref, *, ctx_len):
    def attend(n_keys):
        v = v_ref[:n_keys, :]

        def scores(head):
            g = head // ATT_GROUP
            kt = kt_ref[g * ATT_HEAD_DIM:(g + 1) * ATT_HEAD_DIM, :n_keys]
            return jnp.dot(q_ref[:, head * ATT_HEAD_DIM:(head + 1) * ATT_HEAD_DIM], kt, preferred_element_type=F32)

        s_next = scores(0)
        for head in range(ATT_Q_HEADS):
            s = s_next
            if head + 1 < ATT_Q_HEADS:
                s_next = scores(head + 1)
            g = head // ATT_GROUP
            p = jnp.exp2(s - jnp.max(s, axis=-1, keepdims=True))
            den = jnp.sum(p, axis=-1, keepdims=True)
            pv = jnp.dot(p.astype(BF16), v, preferred_element_type=F32)
            o_ref[:, head * ATT_HEAD_DIM:(head + 1) * ATT_HEAD_DIM] = (
                pv[:, g * ATT_HEAD_DIM:(g + 1) * ATT_HEAD_DIM] / den).astype(o_ref.dtype)

    is_ctx = pl.program_id(1) == 0

    @pl.when(is_ctx)
    def _():
        attend(ctx_len)

    @pl.when(jnp.logical_not(is_ctx))
    def _():
        attend(kt_ref.shape[1])


def _attention(qn, kt, ub, *, batch, seq, ctx_len):
    tiles = seq // ATT_TQ
    return pl.pallas_call(
        functools.partial(_att_kernel, ctx_len=ctx_len),
        out_shape=jax.ShapeDtypeStruct((batch * seq, ATT_QW), BF16),
        grid=(batch, tiles),
        in_specs=[
            pl.BlockSpec((ATT_TQ, ATT_QW), lambda b, i: (b * tiles + i, 0)),
            pl.BlockSpec((None, ATT_KW, seq), lambda b, i: (b, 0, 0)),
            pl.BlockSpec((seq, ATT_KW), lambda b, i: (b, UB_ATT_V // ATT_KW)),
        ],
        out_specs=pl.BlockSpec((ATT_TQ, ATT_QW), lambda b, i: (b * tiles + i, 0)),
        compiler_params=_params(("parallel", "arbitrary")),
        name="attention",
    )(qn, kt, ub)


def _merge_kernel(oaf_ref, oab_ref, obf_ref, obb_ref, ga_ref, gb_ref, yc_ref, yd_ref, gate_ref, h_ref, modc_ref,
                  modl_ref, dnw_ref, hgw_ref, wb_ref, wo_ref, out_ref, *, tiles_per_batch):
    def gated(of_ref, ob_ref, g_ref, nw_ref):
        o = of_ref[...] + ob_ref[...]
        ys = [_rms_rows(o[:, h * HEAD_W:(h + 1) * HEAD_W], nw_ref[...]) for h in range(N_HEADS)]
        g = g_ref[...].astype(F32)
        return jnp.concatenate(ys, axis=1) * (g * _sigmoid(g))

    ys = (gated(oaf_ref, oab_ref, ga_ref, dnw_ref), gated(obf_ref, obb_ref, gb_ref, hgw_ref), yc_ref[...], yd_ref[...])
    acc = None
    for b in range(N_BRANCH):
        gate = gate_ref[:, b * D_MODEL:(b + 1) * D_MODEL].astype(F32)
        term = _sigmoid(gate) * _dot(ys[b], wb_ref[b])
        acc = term if acc is None else acc + term
    is_ctx = (pl.program_id(0) % tiles_per_batch) == 0
    out_ref[...] = h_ref[...] + _pick_mod(modc_ref, modl_ref, 2, is_ctx) * _dot(acc, wo_ref[...])


def _merge(oaf, oab, obf, obb, ub, yc, yd, h, modc, modl, dnw, hgw, wb, wo, *, seq, ctx_len):
    rows = h.shape[0]
    tm = ctx_len
    tiles_per_batch = seq // tm
    mix = lambda c: pl.BlockSpec((tm, MIX_W), lambda i: (i, c // MIX_W))
    return pl.pallas_call(
        functools.partial(_merge_kernel, tiles_per_batch=tiles_per_batch),
        out_shape=jax.ShapeDtypeStruct((rows, D_MODEL), F32),
        grid=(rows // tm,),
        in_specs=[
            mix(0), mix(0), mix(0), mix(0), mix(UB_DN_G), mix(UB_HG_G), mix(0), mix(0),
            pl.BlockSpec((tm, N_BRANCH * D_MODEL), lambda i: (i, 0)),
            pl.BlockSpec((tm, D_MODEL), lambda i: (i, 0)),
            pl.BlockSpec((6, D_MODEL), lambda i: (0, 0)),
            pl.BlockSpec((None, 6, D_MODEL), lambda i: (i // tiles_per_batch, 0, 0)),
            pl.BlockSpec((1, HEAD_W), lambda i: (0, 0)),
            pl.BlockSpec((1, HEAD_W), lambda i: (0, 0)),
            pl.BlockSpec((N_BRANCH, MIX_W, D_MODEL), lambda i: (0, 0, 0)),
            pl.BlockSpec((D_MODEL, D_MODEL), lambda i: (0, 0)),
        ],
        out_specs=pl.BlockSpec((tm, D_MODEL), lambda i: (i, 0)),
        compiler_params=_params(("parallel",)),
        name="merge",
    )(oaf, oab, obf, obb, ub, ub, yc, yd, ub, h, modc, modl, dnw, hgw, wb, wo)


def _mlp_kernel(h_ref, modc_ref, modl_ref, nw_ref, w1_ref, w2_ref, o_ref, z_ref, acc_ref,
                *, tm, tiles_per_batch, ctx_len):
    j = pl.program_id(1)
    row = (pl.program_id(0) % tiles_per_batch) * tm + lax.broadcasted_iota(jnp.int32, (tm, 1), 0)
    is_ctx = row < ctx_len

    @pl.when(j == 0)
    def _():
        y = _rms_rows(h_ref[...], nw_ref[...])
        shift = _pick_mod(modc_ref, modl_ref, 3, is_ctx)
        scale = _pick_mod(modc_ref, modl_ref, 4, is_ctx)
        z_ref[...] = (y * (1.0 + scale) + shift).astype(BF16)
        acc_ref[...] = jnp.zeros_like(acc_ref)

    a = jnp.maximum(jnp.dot(z_ref[...], w1_ref[...], preferred_element_type=F32), 0.0)
    acc_ref[...] += _dot(a * a, w2_ref[...])

    @pl.when(j == pl.num_programs(1) - 1)
    def _():
        o_ref[...] = h_ref[...] + _pick_mod(modc_ref, modl_ref, 5, is_ctx) * acc_ref[...]


def _mlp(h, modc, modl, nw, w1, w2, *, seq, ctx_len):
    rows = h.shape[0]
    tiles_per_batch = 4
    tm = seq // tiles_per_batch
    kern = functools.partial(_mlp_kernel, tm=tm, tiles_per_batch=tiles_per_batch, ctx_len=ctx_len)
    return pl.pallas_call(
        kern,
        out_shape=jax.ShapeDtypeStruct((rows, D_MODEL), F32),
        grid=(rows // tm, D_FF // FF_BLOCK),
        in_specs=[
            pl.BlockSpec((tm, D_MODEL), lambda i, j: (i, 0)),
            pl.BlockSpec((6, D_MODEL), lambda i, j: (0, 0)),
            pl.BlockSpec((None, 6, D_MODEL), lambda i, j: (i // tiles_per_batch, 0, 0)),
            pl.BlockSpec((1, D_MODEL), lambda i, j: (0, 0)),
            pl.BlockSpec((D_MODEL, FF_BLOCK), lambda i, j: (0, j)),
            pl.BlockSpec((FF_BLOCK, D_MODEL), lambda i, j: (j, 0)),
        ],
        out_specs=pl.BlockSpec((tm, D_MODEL), lambda i, j: (i, 0)),
        scratch_shapes=[pltpu.VMEM((tm, D_MODEL), BF16), pltpu.VMEM((tm, D_MODEL), F32)],
        compiler_params=_params(("parallel", "arbitrary")),
        name="mlp",
    )(h, modc, modl, nw, w1, w2)


def _regroup_w_in(w_in):
    o = np.cumsum([0, 512, 512, 512, 512, 8, 8, 512, 512, 512, 512, 512, 512, 512, 512, 128, 128, 4096])
    cols = lambda a, b: w_in[..., o[a]:o[b]]
    wb = jnp.concatenate([cols(16, 17), cols(6, 7), cols(9, 11), cols(0, 3), cols(11, 13), cols(3, 4), cols(13, 16)],
                         axis=-1)
    wf = jnp.concatenate([cols(7, 9), cols(4, 6)], axis=-1)
    pad = lambda w, n: jnp.pad(w, ((0, 0), (0, 0), (0, n - w.shape[-1]))).astype(BF16)
    return pad(wb, N_UB), pad(wf, N_UF)


def _rope_tables(t_len, ctx_len):
    rows = t_len // GRID_W
    row_id = jnp.repeat(jnp.arange(rows), GRID_W).astype(F32)
    col_id = jnp.tile(jnp.arange(GRID_W), rows).astype(F32)
    axis_dim = ATT_HEAD_DIM // 2
    inv = ROPE_THETA ** (-jnp.arange(0, axis_dim, 2, dtype=F32) / axis_dim)
    ang = jnp.stack([row_id[:, None] * inv, col_id[:, None] * inv], axis=1)
    cos, sin = jnp.cos(ang), jnp.sin(ang)
    cos_h = jnp.concatenate([cos, cos], axis=-1).reshape(t_len, ATT_HEAD_DIM)
    sin_h = jnp.concatenate([-sin, sin], axis=-1).reshape(t_len, ATT_HEAD_DIM)
    cos_t = jnp.concatenate([jnp.ones((ctx_len, ATT_HEAD_DIM), F32), cos_h], axis=0)
    sin_t = jnp.concatenate([jnp.zeros((ctx_len, ATT_HEAD_DIM), F32), sin_h], axis=0)
    return jnp.tile(cos_t, (1, 2)), jnp.tile(sin_t, (1, 2))


def _hgrn2_lower_bounds(p):
    sm = jax.nn.softmax(p.astype(F32), axis=1)
    cs = jnp.cumsum(sm, axis=1)
    return cs - cs[:, :1]


def kernel(x, c, ctx, c_ctx, mod_w, mod_b, norm1_w, norm2_w, w_in, dn_conv_w, dn_a_log, dn_dt_bias, dn_norm_w,
           hg_lower_bounds, hg_norm_w, lru_conv_w, lru_conv_b, lru_w_a, lru_b_a, lru_w_x, lru_b_x, lru_lambda,
           att_q_norm_w, att_k_norm_w, w_branch, w_out, mlp_w1, mlp_w2):
    batch, t_len, _ = x.shape
    ctx_len = ctx.shape[1]
    depth = mod_w.shape[0]
    seq = ctx_len + t_len
    n_chunks = seq // CHUNK
    assert ctx_len % SCAN_ROWS == 0 and t_len % SCAN_ROWS == 0 and ctx_len == ATT_TQ
    dims = dict(batch=batch, seq=seq, ctx_len=ctx_len)

    w_ub, w_uf = _regroup_w_in(w_in)
    w_branch_b, w_out_b = w_branch.astype(BF16), w_out.astype(BF16)
    w1_b, w2_b = mlp_w1.astype(BF16), mlp_w2.astype(BF16)
    lru_wa_b, lru_wx_b = lru_w_a.astype(BF16), lru_w_x.astype(BF16)
    lb_all = _hgrn2_lower_bounds(hg_lower_bounds)
    cos_t, sin_t = _rope_tables(t_len, ctx_len)
    grp = jnp.asarray((np.arange(128)[:, None] // ATT_HEAD_DIM) == (np.arange(128)[None, :] // ATT_HEAD_DIM), F32)

    n_mod_rows = batch + 8
    cc = jnp.zeros((n_mod_rows, D_MODEL), F32).at[:batch].set(c).at[batch].set(c_ctx)
    mods = _modulations(cc, mod_w.astype(BF16), mod_b)

    h = jnp.concatenate([ctx, x], axis=1).reshape(batch * seq, D_MODEL)
    for l in range(depth):
        modl = mods[l, :batch].reshape(batch, 6, D_MODEL)
        modc = mods[l, batch].reshape(6, D_MODEL)
        nw1 = norm1_w[l].reshape(1, D_MODEL)
        ub, uf = _inproj(h, modc, modl, nw1, w_ub[l], w_uf[l], seq=seq, ctx_len=ctx_len)

        qkv = _dn_prep(ub, dn_conv_w[l], **dims)
        ba = uf[:, UF_DN_BA:UF_DN_BA + 4 * N_HEADS].reshape(batch * n_chunks, CHUNK, 2, 2, N_HEADS)
        ba_t = jnp.transpose(ba, (3, 0, 2, 4, 1)).reshape(2, batch * n_chunks, 2 * N_HEADS, CHUNK)
        oa = _gdn(qkv, ba_t, dn_a_log[l].reshape(2, N_HEADS, 1), dn_dt_bias[l].reshape(2, N_HEADS, 1), **dims)

        ob = _hgrn2(ub, uf, lb_all[:, l].reshape(2, 1, MIX_W), **dims)

        yc = _lru(ub, lru_conv_w[l], lru_conv_b[l].reshape(1, MIX_W), lru_wa_b[l], lru_b_a[l].reshape(2, 1, MIX_W),
                  lru_wx_b[l], lru_b_x[l].reshape(2, 1, MIX_W), lru_lambda[l].reshape(2, 1, MIX_W), **dims)

        qn, kt = _att_prep(ub, cos_t, sin_t, jnp.tile(att_q_norm_w[l], 2).reshape(1, 128),
                           jnp.tile(att_k_norm_w[l], 2).reshape(1, 128), grp, batch=batch, seq=seq)
        yd = _attention(qn, kt, ub, **dims)

        h = _merge(oa[0], oa[1], ob[0], ob[1], ub, yc, yd, h, modc, modl, dn_norm_w[l].reshape(1, HEAD_W),
                   hg_norm_w[l].reshape(1, HEAD_W), w_branch_b[l], w_out_b[l], seq=seq, ctx_len=ctx_len)
        h = _mlp(h, modc, modl, norm2_w[l].reshape(1, D_MODEL), w1_b[l], w2_b[l], seq=seq, ctx_len=ctx_len)

    return h.reshape(batch, seq, D_MODEL)[:, ctx_len:]
```

```python
import functools

import jax
import jax.numpy as jnp
import numpy as np
from jax import lax
from jax.experimental import pallas as pl
from jax.experimental.pallas import tpu as pltpu

F32 = jnp.float32
BF16 = jnp.bfloat16

EPS = 1e-6
D_MODEL = 1024
GRID_W = 64
N_HEADS = 4
HEAD_W = 128
MIX_W = N_HEADS * HEAD_W
CHUNK = 64
SUB = 16
SCAN_ROWS = 256
LRU_C = 8.0
LRU_BLOCK = 256
ATT_Q_HEADS = 8
ATT_KV_HEADS = 2
ATT_HEAD_DIM = 64
ATT_GROUP = ATT_Q_HEADS // ATT_KV_HEADS
ATT_QW = ATT_Q_HEADS * ATT_HEAD_DIM
ATT_KW = ATT_KV_HEADS * ATT_HEAD_DIM
ATT_TQ = 256
ROPE_THETA = 10000.0
LOG2_E = 1.4426950408889634
N_BRANCH = 4
D_FF = 4 * D_MODEL
FF_BLOCK = 2048

UB_GATE = 0
UB_HG_Q, UB_HG_I, UB_HG_G = 4096, 4608, 5120
UB_DN_Q = 5632
UB_LRU_X, UB_LRU_G = 7168, 7680
UB_DN_G = 8192
UB_ATT_Q, UB_ATT_K, UB_ATT_V = 8704, 9216, 9344
N_UB_USED = 9472
TN_UB = 1664
N_UB = 6 * TN_UB
UF_HG_FF, UF_HG_FB, UF_DN_BA = 0, 512, 1024
N_UF = 1280

VMEM_LIMIT_V7X = 48 * 1024 * 1024


def _params(sem, vmem=VMEM_LIMIT_V7X):
    return pltpu.CompilerParams(dimension_semantics=sem, vmem_limit_bytes=vmem)


def _sigmoid(x):
    return 1.0 / (1.0 + jnp.exp(-x))


def _softplus(x):
    return jnp.maximum(x, 0.0) + jnp.log1p(jnp.exp(-jnp.abs(x)))


def _dot(a, b):
    return jnp.dot(a.astype(BF16), b.astype(BF16), preferred_element_type=F32)


def _dot_nt(a, b):
    return lax.dot_general(a.astype(BF16), b.astype(BF16), (((1,), (1,)), ((), ())),
                           preferred_element_type=F32)


def _dot_tn(a, b):
    return lax.dot_general(a.astype(BF16), b.astype(BF16), (((0,), (0,)), ((), ())),
                           preferred_element_type=F32)


def _split3(x):
    hi = x.astype(BF16)
    r1 = x - hi.astype(F32)
    mid = r1.astype(BF16)
    lo = (r1 - mid.astype(F32)).astype(BF16)
    return hi, mid, lo


def _dot_x3(x, m):
    m = m.astype(BF16)
    hi, mid, lo = _split3(x)
    out = jnp.dot(hi, m, preferred_element_type=F32)
    out = out + jnp.dot(mid, m, preferred_element_type=F32)
    return out + jnp.dot(lo, m, preferred_element_type=F32)


def _dot_3x_many(m, xs):
    w = xs[0].shape[1]
    pieces = [_split3(x) for x in xs]
    cols = [p[i] for i in range(3) for p in pieces]
    wide = jnp.dot(m.astype(BF16), jnp.concatenate(cols, axis=1), preferred_element_type=F32)
    n = len(xs)
    part = lambda p, i: wide[:, (p * n + i) * w:(p * n + i + 1) * w]
    return [part(0, i) + part(1, i) + part(2, i) for i in range(n)]


def _rms_rows(x, w):
    return x * lax.rsqrt(jnp.mean(x * x, axis=-1, keepdims=True) + EPS) * w


def _mod_kernel(c_ref, w_ref, b_ref, o_ref):
    c = c_ref[...]
    o_ref[...] = _dot(c * _sigmoid(c), w_ref[...]) + b_ref[...]


def _modulations(cc, mod_w, mod_b):
    depth = mod_w.shape[0]
    rows = cc.shape[0]
    n_out = mod_w.shape[2]
    return pl.pallas_call(
        _mod_kernel,
        out_shape=jax.ShapeDtypeStruct((depth, rows, n_out), F32),
        grid=(depth, n_out // D_MODEL),
        in_specs=[
            pl.BlockSpec((rows, D_MODEL), lambda l, j: (0, 0)),
            pl.BlockSpec((None, D_MODEL, D_MODEL), lambda l, j: (l, 0, j)),
            pl.BlockSpec((None, 1, D_MODEL), lambda l, j: (l, 0, j)),
        ],
        out_specs=pl.BlockSpec((None, rows, D_MODEL), lambda l, j: (l, 0, j)),
        compiler_params=_params(("parallel", "parallel")),
        name="modulations",
    )(cc, mod_w, mod_b.reshape(depth, 1, n_out))


def _pick_mod(modc_ref, modl_ref, idx, is_ctx):
    return jnp.where(is_ctx, modc_ref[idx:idx + 1, :], modl_ref[idx:idx + 1, :])


def _inproj_kernel(h_ref, modc_ref, modl_ref, nw_ref, wb_ref, wf_ref, ob_ref, of_ref, xn_ref,
                   *, tm, tiles_per_batch, ctx_len):
    j = pl.program_id(1)

    @pl.when(j == 0)
    def _():
        y = _rms_rows(h_ref[...], nw_ref[...])
        row = (pl.program_id(0) % tiles_per_batch) * tm + lax.broadcasted_iota(jnp.int32, (tm, 1), 0)
        is_ctx = row < ctx_len
        shift = _pick_mod(modc_ref, modl_ref, 0, is_ctx)
        scale = _pick_mod(modc_ref, modl_ref, 1, is_ctx)
        xn_ref[...] = (y * (1.0 + scale) + shift).astype(BF16)

    ob_ref[...] = jnp.dot(xn_ref[...], wb_ref[...], preferred_element_type=F32).astype(ob_ref.dtype)

    @pl.when(j == pl.num_programs(1) - 1)
    def _():
        of_ref[...] = jnp.dot(xn_ref[...], wf_ref[...], preferred_element_type=F32)


def _inproj(h, modc, modl, nw, w_b, w_f, *, seq, ctx_len):
    rows = h.shape[0]
    tiles_per_batch = 4
    tm = seq // tiles_per_batch
    kern = functools.partial(_inproj_kernel, tm=tm, tiles_per_batch=tiles_per_batch, ctx_len=ctx_len)
    return pl.pallas_call(
        kern,
        out_shape=(jax.ShapeDtypeStruct((rows, N_UB), BF16), jax.ShapeDtypeStruct((rows, N_UF), F32)),
        grid=(rows // tm, N_UB // TN_UB),
        in_specs=[
            pl.BlockSpec((tm, D_MODEL), lambda i, j: (i, 0)),
            pl.BlockSpec((6, D_MODEL), lambda i, j: (0, 0)),
            pl.BlockSpec((None, 6, D_MODEL), lambda i, j: (i // tiles_per_batch, 0, 0)),
            pl.BlockSpec((1, D_MODEL), lambda i, j: (0, 0)),
            pl.BlockSpec((D_MODEL, TN_UB), lambda i, j: (0, j)),
            pl.BlockSpec((D_MODEL, N_UF), lambda i, j: (0, 0)),
        ],
        out_specs=(pl.BlockSpec((tm, TN_UB), lambda i, j: (i, j)),
                   pl.BlockSpec((tm, N_UF), lambda i, j: (i, 0))),
        scratch_shapes=[pltpu.VMEM((tm, D_MODEL), BF16)],
        compiler_params=_params(("parallel", "arbitrary")),
        name="inproj",
    )(h, modc, modl, nw, w_b, w_f)


def _seg_conv(x, w_ref, ctx_len):
    n = x.shape[0]
    row = lax.broadcasted_iota(jnp.int32, (n, 1), 0)
    lo = jnp.where(row >= ctx_len, ctx_len, 0)
    hi = jnp.where(row >= ctx_len, n, ctx_len)

    def tap(k):
        tk = row + k
        valid = jnp.logical_and(tk >= lo, tk < hi)
        return jnp.where(valid, pltpu.roll(x, (-k) % n, 0), 0.0)

    return (tap(-1) * w_ref[0:1, :] + x * w_ref[1:2, :] + tap(1) * w_ref[2:3, :] + tap(2) * w_ref[3:4, :])


def _dn_prep_kernel(u_ref, w_ref, o_ref, *, ctx_len):
    j = pl.program_id(1)
    y = _seg_conv(u_ref[...].astype(F32), w_ref, ctx_len)
    y = y * _sigmoid(y)
    n = lax.rsqrt(jnp.sum(y * y, axis=-1, keepdims=True) + EPS)
    fac = jnp.where(j < N_HEADS, n * (HEAD_W ** -0.5), jnp.where(j < 2 * N_HEADS, n, 1.0))
    o_ref[...] = (y * fac).astype(o_ref.dtype)


def _dn_prep(ub, conv_w, *, batch, seq, ctx_len):
    nblk = 3 * N_HEADS
    return pl.pallas_call(
        functools.partial(_dn_prep_kernel, ctx_len=ctx_len),
        out_shape=jax.ShapeDtypeStruct((batch * seq, 3 * MIX_W), BF16),
        grid=(batch, nblk),
        in_specs=[
            pl.BlockSpec((seq, HEAD_W), lambda b, j: (b, UB_DN_Q // HEAD_W + j)),
            pl.BlockSpec((4, HEAD_W), lambda b, j: (0, j)),
        ],
        out_specs=pl.BlockSpec((seq, HEAD_W), lambda b, j: (b, j)),
        compiler_params=_params(("parallel", "parallel")),
        name="dn_prep",
    )(ub, conv_w)


def _block_order(s, n_ctx, n_all, rev):
    if not rev:
        return s
    return jnp.where(s < n_ctx, n_ctx - 1 - s, n_all + n_ctx - 1 - s)


def _chunk_masks(rev):
    ii = lax.broadcasted_iota(jnp.int32, (CHUNK, CHUNK), 0)
    jj = lax.broadcasted_iota(jnp.int32, (CHUNK, CHUNK), 1)
    incl = (ii <= jj) if rev else (ii >= jj)
    incl_t = (ii >= jj) if rev else (ii <= jj)
    return ii, jj, incl, incl_t


def _unit_tri_inverse(a_all, eye, bd):
    ad = [jnp.where(bd, a, 0.0) for a in a_all]
    ao = [a - d for a, d in zip(a_all, ad)]
    p = [-d for d in ad]
    dinv = [eye + x for x in p]
    for _ in range(3):
        p = [_dot(x, x) for x in p]
        dinv = [d + _dot(d, x) for d, x in zip(dinv, p)]
    m = [-_dot(d, o) for d, o in zip(dinv, ao)]
    mm = [_dot(x, x) for x in m]
    t = [eye + x for x in m]
    t = [x + _dot(x, y) for x, y in zip(t, mm)]
    return [_dot(x, d) for x, d in zip(t, dinv)]


def _gdn_prepare(q_ref, k_ref, v_ref, ba_ref, alog, dtb, rev):
    ii, jj, incl, incl_t = _chunk_masks(rev)
    is_eye = ii == jj
    eye = jnp.where(is_eye, 1.0, 0.0)
    tri = jnp.where(incl, 1.0, 0.0)
    tri_t = jnp.where(incl_t, 1.0, 0.0)
    bd = (ii // SUB) == (jj // SUB)
    n_chunks = q_ref.shape[0] // CHUNK

    alog_neg = -jnp.exp(alog)
    chains = [(j, h) for j in range(n_chunks) for h in range(N_HEADS)]
    vec = {}
    for j in range(n_chunks):
        ba = ba_ref[j]
        beta_r = _sigmoid(ba[0:N_HEADS])
        g_r = alog_neg * _softplus(ba[N_HEADS:2 * N_HEADS] + dtb)
        for h in range(N_HEADS):
            gr = g_r[h:h + 1]
            beta_c = jnp.sum(eye * beta_r[h:h + 1], axis=1, keepdims=True)
            g_c = jnp.sum(eye * gr, axis=1, keepdims=True)
            gam_c = jnp.sum(tri * gr, axis=1, keepdims=True)
            gam_r = jnp.sum(tri_t * g_c, axis=0, keepdims=True)
            tot = jnp.sum(gr, axis=1, keepdims=True)
            dec_i = jnp.where(incl, jnp.exp(gam_c - gam_r), 0.0)
            vec[j, h] = (beta_c, gam_c, tot, dec_i)

    def tile(ref, j, h):
        return ref[j * CHUNK:(j + 1) * CHUNK, h * HEAD_W:(h + 1) * HEAD_W]

    k16 = [tile(k_ref, *c) for c in chains]
    q16 = [tile(q_ref, *c) for c in chains]
    ks = [k.astype(F32) for k in k16]
    kbs = [k * vec[c][0] for k, c in zip(ks, chains)]
    kq = [_dot_nt(jnp.concatenate([kb.astype(BF16), q], axis=0), k) for kb, q, k in zip(kbs, q16, k16)]
    a_low = [jnp.where(is_eye, 0.0, x[:CHUNK] * vec[c][3]) for x, c in zip(kq, chains)]
    a_qk = [x[CHUNK:] * vec[c][3] for x, c in zip(kq, chains)]
    t_inv = _unit_tri_inverse(a_low, eye, bd)
    egc = [jnp.exp(vec[c][1]) for c in chains]
    sols = [_dot(t, jnp.concatenate([tile(v_ref, *c).astype(F32) * vec[c][0], kb * e], axis=1))
            for t, kb, e, c in zip(t_inv, kbs, egc, chains)]
    pre = {}
    for c, sol, aq, q, k, e in zip(chains, sols, a_qk, q16, ks, egc):
        q_dec = q.astype(F32) * e
        k_dec_t = (k * jnp.exp(vec[c][2] - vec[c][1])).T
        pre[c] = (sol[:, :HEAD_W], jnp.concatenate([sol[:, HEAD_W:], q_dec], axis=0).astype(BF16),
                  jnp.concatenate([aq, k_dec_t], axis=0).astype(BF16), jnp.exp(vec[c][2]))
    return pre


def _gdn_kernel(qf_ref, kf_ref, vf_ref, baf_ref, qr_ref, kr_ref, vr_ref, bar_ref, alog_ref, dtb_ref,
                of_ref, or_ref, s_ref):
    @pl.when(pl.program_id(1) == 0)
    def _():
        s_ref[...] = jnp.zeros_like(s_ref)

    n_chunks = qf_ref.shape[0] // CHUNK
    pre = (_gdn_prepare(qf_ref, kf_ref, vf_ref, baf_ref, alog_ref[0], dtb_ref[0], False),
           _gdn_prepare(qr_ref, kr_ref, vr_ref, bar_ref, alog_ref[1], dtb_ref[1], True))
    o_refs = (of_ref, or_ref)

    lanes = [(d, h) for d in range(2) for h in range(N_HEADS)]
    states = {dh: s_ref[dh[0], dh[1]] for dh in lanes}
    for t in range(n_chunks):
        js = (t, n_chunks - 1 - t)
        cur = {dh: pre[dh[0]][js[dh[0]], dh[1]] for dh in lanes}
        wq = {dh: _dot(cur[dh][1], states[dh]) for dh in lanes}
        v_new = {dh: cur[dh][0] - wq[dh][:CHUNK] for dh in lanes}
        ak = {dh: _dot(cur[dh][2], v_new[dh]) for dh in lanes}
        for d, h in lanes:
            j = js[d]
            o_refs[d][j * CHUNK:(j + 1) * CHUNK, h * HEAD_W:(h + 1) * HEAD_W] = wq[d, h][CHUNK:] + ak[d, h][:CHUNK]
        states = {dh: states[dh] * cur[dh][3] + ak[dh][CHUNK:] for dh in lanes}
    for d, h in lanes:
        s_ref[d, h] = states[d, h]


def _gdn(qkv, ba_t, a_log, dt_bias, *, batch, seq, ctx_len):
    n_all = seq // SCAN_ROWS
    n_ctx = ctx_len // SCAN_ROWS
    cpb = SCAN_ROWS // CHUNK

    def rows(b, s, rev):
        return b * n_all + _block_order(s, n_ctx, n_all, rev)

    def in_specs(rev):
        d = int(rev)
        return [
            pl.BlockSpec((SCAN_ROWS, MIX_W), lambda b, s: (rows(b, s, rev), 0)),
            pl.BlockSpec((SCAN_ROWS, MIX_W), lambda b, s: (rows(b, s, rev), 1)),
            pl.BlockSpec((SCAN_ROWS, MIX_W), lambda b, s: (rows(b, s, rev), 2)),
            pl.BlockSpec((None, cpb, 2 * N_HEADS, CHUNK), lambda b, s: (d, rows(b, s, rev), 0, 0)),
        ]

    vec = pl.BlockSpec((2, N_HEADS, 1), lambda b, s: (0, 0, 0))
    out = jax.ShapeDtypeStruct((batch * seq, MIX_W), F32)
    return pl.pallas_call(
        _gdn_kernel,
        out_shape=(out, out),
        grid=(batch, n_all),
        in_specs=in_specs(False) + in_specs(True) + [vec, vec],
        out_specs=(pl.BlockSpec((SCAN_ROWS, MIX_W), lambda b, s: (rows(b, s, False), 0)),
                   pl.BlockSpec((SCAN_ROWS, MIX_W), lambda b, s: (rows(b, s, True), 0))),
        scratch_shapes=[pltpu.VMEM((2, N_HEADS, HEAD_W, HEAD_W), F32)],
        compiler_params=_params(("parallel", "arbitrary")),
        name="gdn",
    )(qkv, qkv, qkv, ba_t, qkv, qkv, qkv, ba_t, a_log, dt_bias)


def _hgrn2_prepare(q_ref, f_ref, i_ref, lb_ref, rev):
    _, _, incl, _ = _chunk_masks(rev)
    tri = jnp.where(incl, 1.0, 0.0)
    row = lax.broadcasted_iota(jnp.int32, (CHUNK, 1), 0)
    n_sub = CHUNK // SUB
    n_chunks = q_ref.shape[0] // CHUNK

    def tile(ref, j, h):
        return ref[j * CHUNK:(j + 1) * CHUNK, h * HEAD_W:(h + 1) * HEAD_W]

    chains = [(j, h) for j in range(n_chunks) for h in range(N_HEADS)]
    qs_, ks_, lfs = [], [], []
    for j, h in chains:
        qr = tile(q_ref, j, h).astype(F32)
        lb = lb_ref[:, h * HEAD_W:(h + 1) * HEAD_W]
        f = lb + (1.0 - lb) * _sigmoid(tile(f_ref, j, h))
        qs_.append(qr * _sigmoid(qr))
        ks_.append(1.0 - f)
        lfs.append(jnp.log(f))
    gcs = _dot_3x_many(tri, lfs)
    tots = [jnp.sum(lf, axis=0, keepdims=True) for lf in lfs]
    blocks = []
    for q, k, gc in zip(qs_, ks_, gcs):
        row_blocks = []
        for i in range(n_sub):
            mid = i * SUB + SUB // 2
            gref = gc[mid:mid + 1, :]
            qsc = q[i * SUB:(i + 1) * SUB, :] * jnp.exp(gc[i * SUB:(i + 1) * SUB, :] - gref)
            reach = (row >= i * SUB) if rev else (row < (i + 1) * SUB)
            ksc = jnp.where(reach, k * jnp.exp(gref - gc), 0.0)
            row_blocks.append((qsc, ksc))
        blocks.append(row_blocks)
    scores = [jnp.where(incl, jnp.concatenate([_dot_nt(a, b) for a, b in rb], axis=0), 0.0) for rb in blocks]
    vs = [tile(i_ref, *c) for c in chains]
    local = [_dot(sc, v) for sc, v in zip(scores, vs)]
    incr = [_dot_tn(v, k * jnp.exp(tot - gc)) for v, k, gc, tot in zip(vs, ks_, gcs, tots)]
    return {c: (q * jnp.exp(gc), lo, inc, jnp.exp(tot))
            for c, q, gc, lo, inc, tot in zip(chains, qs_, gcs, local, incr, tots)}


def _hgrn2_kernel(qf_ref, ff_ref, if_ref, qr_ref, fr_ref, ir_ref, lb_ref, of_ref, or_ref, s_ref):
    @pl.when(pl.program_id(1) == 0)
    def _():
        s_ref[...] = jnp.zeros_like(s_ref)

    n_chunks = qf_ref.shape[0] // CHUNK
    pre = (_hgrn2_prepare(qf_ref, ff_ref, if_ref, lb_ref.at[0], False),
           _hgrn2_prepare(qr_ref, fr_ref, ir_ref, lb_ref.at[1], True))
    o_refs = (of_ref, or_ref)

    lanes = [(d, h) for d in range(2) for h in range(N_HEADS)]
    states = {dh: s_ref[dh[0], dh[1]] for dh in lanes}
    for t in range(n_chunks):
        js = (t, n_chunks - 1 - t)
        cur = {dh: pre[dh[0]][js[dh[0]], dh[1]] for dh in lanes}
        qs = {dh: _dot_nt(cur[dh][0], states[dh]) for dh in lanes}
        for d, h in lanes:
            j = js[d]
            o_refs[d][j * CHUNK:(j + 1) * CHUNK, h * HEAD_W:(h + 1) * HEAD_W] = qs[d, h] + cur[d, h][1]
        states = {dh: states[dh] * cur[dh][3] + cur[dh][2] for dh in lanes}
    for d, h in lanes:
        s_ref[d, h] = states[d, h]


def _hgrn2(ub, uf, lb, *, batch, seq, ctx_len):
    n_all = seq // SCAN_ROWS
    n_ctx = ctx_len // SCAN_ROWS

    def rows(b, s, rev):
        return b * n_all + _block_order(s, n_ctx, n_all, rev)

    def in_specs(rev):
        f_col = (UF_HG_FB if rev else UF_HG_FF) // MIX_W
        return [
            pl.BlockSpec((SCAN_ROWS, MIX_W), lambda b, s: (rows(b, s, rev), UB_HG_Q // MIX_W)),
            pl.BlockSpec((SCAN_ROWS, MIX_W), lambda b, s: (rows(b, s, rev), f_col)),
            pl.BlockSpec((SCAN_ROWS, MIX_W), lambda b, s: (rows(b, s, rev), UB_HG_I // MIX_W)),
        ]

    out = jax.ShapeDtypeStruct((batch * seq, MIX_W), F32)
    return pl.pallas_call(
        _hgrn2_kernel,
        out_shape=(out, out),
        grid=(batch, n_all),
        in_specs=in_specs(False) + in_specs(True) + [pl.BlockSpec((2, 1, MIX_W), lambda b, s: (0, 0, 0))],
        out_specs=(pl.BlockSpec((SCAN_ROWS, MIX_W), lambda b, s: (rows(b, s, False), 0)),
                   pl.BlockSpec((SCAN_ROWS, MIX_W), lambda b, s: (rows(b, s, True), 0))),
        scratch_shapes=[pltpu.VMEM((2, N_HEADS, HEAD_W, HEAD_W), F32)],
        compiler_params=_params(("parallel", "arbitrary")),
        name="hgrn2",
    )(ub, uf, ub, ub, uf, ub, lb)


LRU_GROUP = 8


def _group_scan(a, b, rev):
    n = a.shape[0]
    sub = lax.broadcasted_iota(jnp.int32, (n, 1), 0) % LRU_GROUP
    s = 1
    while s < LRU_GROUP:
        if rev:
            a_s, b_s, valid = pltpu.roll(a, n - s, 0), pltpu.roll(b, n - s, 0), sub < LRU_GROUP - s
        else:
            a_s, b_s, valid = pltpu.roll(a, s, 0), pltpu.roll(b, s, 0), sub >= s
        b = jnp.where(valid, a * b_s + b, b)
        a = jnp.where(valid, a * a_s, a)
        s *= 2
    return a, b


def _chain_groups(a, b, carry, rev, store):
    n = a.shape[0] // LRU_GROUP
    for g in (range(n - 1, -1, -1) if rev else range(n)):
        hg = a[g * LRU_GROUP:(g + 1) * LRU_GROUP, :] * carry + b[g * LRU_GROUP:(g + 1) * LRU_GROUP, :]
        store(g, hg)
        carry = hg[0:1, :] if rev else hg[LRU_GROUP - 1:LRU_GROUP, :]
    return carry


def _gelu_tanh(x):
    return 0.5 * x * (1.0 + jnp.tanh(0.7978845608028654 * (x + 0.044715 * (x * x * x))))


def _lru_kernel(xb_ref, gb_ref, cw_ref, cb_ref, wa_ref, ba_ref, wx_ref, bx_ref, lam_ref, o_ref, xc_ref, hf_ref,
                hb_ref, *, ctx_len):
    n = xb_ref.shape[0]
    nblk = n // LRU_BLOCK
    nctx = ctx_len // LRU_BLOCK
    xc_ref[...] = _seg_conv(xb_ref[...].astype(F32), cw_ref, ctx_len) + cb_ref[...]

    def gates(xc, d):
        r = _sigmoid(_dot(xc, wa_ref[d]) + ba_ref[d])
        ig = _sigmoid(_dot(xc, wx_ref[d]) + bx_ref[d])
        log_a = -LRU_C * r * _softplus(-lam_ref[d])
        a = jnp.exp(log_a)
        return a, jnp.sqrt(1.0 - a * a) * ig * xc

    def scan_block(blk, carry, d, out_ref):
        base = pl.multiple_of(blk * LRU_BLOCK, LRU_BLOCK)
        a, b = _group_scan(*gates(xc_ref[pl.ds(base, LRU_BLOCK), :], d), rev=bool(d))

        def store(g, hg):
            out_ref[pl.ds(pl.multiple_of(base + g * LRU_GROUP, LRU_GROUP), LRU_GROUP), :] = hg

        return _chain_groups(a, b, carry, bool(d), store)

    def step(i, carry):
        cf = scan_block(i, carry[0], 0, hf_ref)
        cb = scan_block(_block_order(i, nctx, nblk, True), carry[1], 1, hb_ref)
        return cf, cb

    zero = jnp.zeros((1, HEAD_W), F32)
    lax.fori_loop(0, nblk, step, (zero, zero))
    o_ref[...] = ((hf_ref[...] + hb_ref[...]) * _gelu_tanh(gb_ref[...].astype(F32))).astype(o_ref.dtype)


def _lru(ub, conv_w, conv_b, w_a, b_a, w_x, b_x, lam, *, batch, seq, ctx_len):
    vec = pl.BlockSpec((2, 1, HEAD_W), lambda b, h: (0, 0, h))
    mat = pl.BlockSpec((2, None, HEAD_W, HEAD_W), lambda b, h: (0, h, 0, 0))
    return pl.pallas_call(
        functools.partial(_lru_kernel, ctx_len=ctx_len),
        out_shape=jax.ShapeDtypeStruct((batch * seq, MIX_W), BF16),
        grid=(batch, N_HEADS),
        in_specs=[
            pl.BlockSpec((seq, HEAD_W), lambda b, h: (b, UB_LRU_X // HEAD_W + h)),
            pl.BlockSpec((seq, HEAD_W), lambda b, h: (b, UB_LRU_G // HEAD_W + h)),
            pl.BlockSpec((4, HEAD_W), lambda b, h: (0, h)),
            pl.BlockSpec((1, HEAD_W), lambda b, h: (0, h)),
            mat, vec, mat, vec, vec,
        ],
        out_specs=pl.BlockSpec((seq, HEAD_W), lambda b, h: (b, h)),
        scratch_shapes=[pltpu.VMEM((seq, HEAD_W), F32)] * 3,
        compiler_params=_params(("parallel", "parallel")),
        name="rglru",
    )(ub, ub, conv_w, conv_b, w_a, b_a, w_x, b_x, lam)


def _norm_rope(x, w, cos, sin, grp, scale):
    lane = lax.broadcasted_iota(jnp.int32, (1, 128), 1)
    first = (lane % (ATT_HEAD_DIM // 2)) < (ATT_HEAD_DIM // 4)
    ss = _dot_x3(x * x, grp)
    y = x * lax.rsqrt(ss * (1.0 / ATT_HEAD_DIM) + EPS) * w
    rot = jnp.where(first, pltpu.roll(y, 128 - ATT_HEAD_DIM // 4, 1), pltpu.roll(y, ATT_HEAD_DIM // 4, 1))
    return (y * cos + rot * sin) * scale


def _att_kernel(q_ref, k_ref, v_ref, cos_ref, sin_ref, qw_ref, kw_ref, grp_ref, o_ref, kt_ref, *, ctx_len):
    i = pl.program_id(1)
    grp = grp_ref[...]
    tq = q_ref.shape[0]

    @pl.when(i == 0)
    def _():
        for c in range(k_ref.shape[0] // tq):
            rows = slice(c * tq, (c + 1) * tq)
            kn = _norm_rope(k_ref[rows, :].astype(F32), kw_ref[...], cos_ref[rows, :], sin_ref[rows, :], grp, 1.0)
            kt_ref[:, rows] = kn.T.astype(kt_ref.dtype)

    q_scale = (ATT_HEAD_DIM ** -0.5) * LOG2_E
    q_rows = pl.ds(pl.multiple_of(i * tq, tq), tq)
    cos, sin = cos_ref[q_rows, :], sin_ref[q_rows, :]
    qn = [_norm_rope(q_ref[:, s * 128:(s + 1) * 128].astype(F32), qw_ref[...], cos, sin, grp, q_scale).astype(BF16)
          for s in range(q_ref.shape[1] // 128)]

    def attend(n_keys):
        v = v_ref[:n_keys, :]

        def scores(head):
            g = head // ATT_GROUP
            kt = kt_ref[g * ATT_HEAD_DIM:(g + 1) * ATT_HEAD_DIM, :n_keys]
            q = qn[head // 2][:, (head % 2) * ATT_HEAD_DIM:(head % 2 + 1) * ATT_HEAD_DIM]
            return jnp.dot(q, kt, preferred_element_type=F32)

        s_next = scores(0)
        for head in range(ATT_Q_HEADS):
            s = s_next
            if head + 1 < ATT_Q_HEADS:
                s_next = scores(head + 1)
            g = head // ATT_GROUP
            p = jnp.exp2(s - jnp.max(s, axis=-1, keepdims=True))
            den = jnp.sum(p, axis=-1, keepdims=True)
            pv = jnp.dot(p.astype(BF16), v, preferred_element_type=F32)
            o_ref[:, head * ATT_HEAD_DIM:(head + 1) * ATT_HEAD_DIM] = (
                pv[:, g * ATT_HEAD_DIM:(g + 1) * ATT_HEAD_DIM] / den).astype(o_ref.dtype)

    @pl.when(i == 0)
    def _():
        attend(ctx_len)

    @pl.when(i != 0)
    def _():
        attend(kt_ref.shape[1])


def _attention(ub, cos, sin, qw, kw, grp, *, batch, seq, ctx_len):
    tiles = seq // ATT_TQ
    whole = lambda shape: pl.BlockSpec(shape, lambda b, i: (0, 0))
    return pl.pallas_call(
        functools.partial(_att_kernel, ctx_len=ctx_len),
        out_shape=jax.ShapeDtypeStruct((batch * seq, ATT_QW), BF16),
        grid=(batch, tiles),
        in_specs=[
            pl.BlockSpec((ATT_TQ, ATT_QW), lambda b, i: (b * tiles + i, UB_ATT_Q // ATT_QW)),
            pl.BlockSpec((seq, ATT_KW), lambda b, i: (b, UB_ATT_K // ATT_KW)),
            pl.BlockSpec((seq, ATT_KW), lambda b, i: (b, UB_ATT_V // ATT_KW)),
            whole((seq, 128)), whole((seq, 128)), whole((1, 128)), whole((1, 128)), whole((128, 128)),
        ],
        out_specs=pl.BlockSpec((ATT_TQ, ATT_QW), lambda b, i: (b * tiles + i, 0)),
        scratch_shapes=[pltpu.VMEM((ATT_KW, seq), BF16)],
        compiler_params=_params(("parallel", "arbitrary")),
        name="attention",
    )(ub, ub, ub, cos, sin, qw, kw, grp)


def _merge_kernel(oaf_ref, oab_ref, obf_ref, obb_ref, ga_ref, gb_ref, yc_ref, yd_ref, gate_ref, h_ref, modc_ref,
                  modl_ref, dnw_ref, hgw_ref, wb_ref, wo_ref, out_ref, *, tiles_per_batch):
    def gated(of_ref, ob_ref, g_ref, nw_ref):
        o = of_ref[...] + ob_ref[...]
        ys = [_rms_rows(o[:, h * HEAD_W:(h + 1) * HEAD_W], nw_ref[...]) for h in range(N_HEADS)]
        g = g_ref[...].astype(F32)
        return jnp.concatenate(ys, axis=1) * (g * _sigmoid(g))

    ys = (gated(oaf_ref, oab_ref, ga_ref, dnw_ref), gated(obf_ref, obb_ref, gb_ref, hgw_ref), yc_ref[...], yd_ref[...])
    acc = None
    for b in range(N_BRANCH):
        gate = gate_ref[:, b * D_MODEL:(b + 1) * D_MODEL].astype(F32)
        term = _sigmoid(gate) * _dot(ys[b], wb_ref[b])
        acc = term if acc is None else acc + term
    is_ctx = (pl.program_id(0) % tiles_per_batch) == 0
    out_ref[...] = h_ref[...] + _pick_mod(modc_ref, modl_ref, 2, is_ctx) * _dot(acc, wo_ref[...])


def _merge(oaf, oab, obf, obb, ub, yc, yd, h, modc, modl, dnw, hgw, wb, wo, *, seq, ctx_len):
    rows = h.shape[0]
    tm = ctx_len
    tiles_per_batch = seq // tm
    mix = lambda c: pl.BlockSpec((tm, MIX_W), lambda i: (i, c // MIX_W))
    return pl.pallas_call(
        functools.partial(_merge_kernel, tiles_per_batch=tiles_per_batch),
        out_shape=jax.ShapeDtypeStruct((rows, D_MODEL), F32),
        grid=(rows // tm,),
        in_specs=[
            mix(0), mix(0), mix(0), mix(0), mix(UB_DN_G), mix(UB_HG_G), mix(0), mix(0),
            pl.BlockSpec((tm, N_BRANCH * D_MODEL), lambda i: (i, 0)),
            pl.BlockSpec((tm, D_MODEL), lambda i: (i, 0)),
            pl.BlockSpec((6, D_MODEL), lambda i: (0, 0)),
            pl.BlockSpec((None, 6, D_MODEL), lambda i: (i // tiles_per_batch, 0, 0)),
            pl.BlockSpec((1, HEAD_W), lambda i: (0, 0)),
            pl.BlockSpec((1, HEAD_W), lambda i: (0, 0)),
            pl.BlockSpec((N_BRANCH, MIX_W, D_MODEL), lambda i: (0, 0, 0)),
            pl.BlockSpec((D_MODEL, D_MODEL), lambda i: (0, 0)),
        ],
        out_specs=pl.BlockSpec((tm, D_MODEL), lambda i: (i, 0)),
        compiler_params=_params(("parallel",)),
        name="merge",
    )(oaf, oab, obf, obb, ub, ub, yc, yd, ub, h, modc, modl, dnw, hgw, wb, wo)


def _mlp_kernel(h_ref, modc_ref, modl_ref, nw_ref, w1_ref, w2_ref, o_ref, z_ref, acc_ref,
                *, tm, tiles_per_batch, ctx_len):
    j = pl.program_id(1)
    row = (pl.program_id(0) % tiles_per_batch) * tm + lax.broadcasted_iota(jnp.int32, (tm, 1), 0)
    is_ctx = row < ctx_len

    @pl.when(j == 0)
    def _():
        y = _rms_rows(h_ref[...], nw_ref[...])
        shift = _pick_mod(modc_ref, modl_ref, 3, is_ctx)
        scale = _pick_mod(modc_ref, modl_ref, 4, is_ctx)
        z_ref[...] = (y * (1.0 + scale) + shift).astype(BF16)
        acc_ref[...] = jnp.zeros_like(acc_ref)

    a = jnp.maximum(jnp.dot(z_ref[...], w1_ref[...], preferred_element_type=F32), 0.0)
    acc_ref[...] += _dot(a * a, w2_ref[...])

    @pl.when(j == pl.num_programs(1) - 1)
    def _():
        o_ref[...] = h_ref[...] + _pick_mod(modc_ref, modl_ref, 5, is_ctx) * acc_ref[...]


def _mlp(h, modc, modl, nw, w1, w2, *, seq, ctx_len):
    rows = h.shape[0]
    tiles_per_batch = 4
    tm = seq // tiles_per_batch
    kern = functools.partial(_mlp_kernel, tm=tm, tiles_per_batch=tiles_per_batch, ctx_len=ctx_len)
    return pl.pallas_call(
        kern,
        out_shape=jax.ShapeDtypeStruct((rows, D_MODEL), F32),
        grid=(rows // tm, D_FF // FF_BLOCK),
        in_specs=[
            pl.BlockSpec((tm, D_MODEL), lambda i, j: (i, 0)),
            pl.BlockSpec((6, D_MODEL), lambda i, j: (0, 0)),
            pl.BlockSpec((None, 6, D_MODEL), lambda i, j: (i // tiles_per_batch, 0, 0)),
            pl.BlockSpec((1, D_MODEL), lambda i, j: (0, 0)),
            pl.BlockSpec((D_MODEL, FF_BLOCK), lambda i, j: (0, j)),
            pl.BlockSpec((FF_BLOCK, D_MODEL), lambda i, j: (j, 0)),
        ],
        out_specs=pl.BlockSpec((tm, D_MODEL), lambda i, j: (i, 0)),
        scratch_shapes=[pltpu.VMEM((tm, D_MODEL), BF16), pltpu.VMEM((tm, D_MODEL), F32)],
        compiler_params=_params(("parallel", "arbitrary"), vmem=VMEM_LIMIT_V7X + 8 * 1024 * 1024),
        name="mlp",
    )(h, modc, modl, nw, w1, w2)


def _regroup_w_in(w_in):
    o = np.cumsum([0, 512, 512, 512, 512, 8, 8, 512, 512, 512, 512, 512, 512, 512, 512, 128, 128, 4096])
    cols = lambda a, b: w_in[..., o[a]:o[b]]
    wb = jnp.concatenate([cols(16, 17), cols(6, 7), cols(9, 11), cols(0, 3), cols(11, 13), cols(3, 4), cols(13, 16)],
                         axis=-1)
    wf = jnp.concatenate([cols(7, 9), cols(4, 6)], axis=-1)
    pad = lambda w, n: jnp.pad(w, ((0, 0), (0, 0), (0, n - w.shape[-1]))).astype(BF16)
    return pad(wb, N_UB), pad(wf, N_UF)


def _rope_tables(t_len, ctx_len):
    rows = t_len // GRID_W
    row_id = jnp.repeat(jnp.arange(rows), GRID_W).astype(F32)
    col_id = jnp.tile(jnp.arange(GRID_W), rows).astype(F32)
    axis_dim = ATT_HEAD_DIM // 2
    inv = ROPE_THETA ** (-jnp.arange(0, axis_dim, 2, dtype=F32) / axis_dim)
    ang = jnp.stack([row_id[:, None] * inv, col_id[:, None] * inv], axis=1)
    cos, sin = jnp.cos(ang), jnp.sin(ang)
    cos_h = jnp.concatenate([cos, cos], axis=-1).reshape(t_len, ATT_HEAD_DIM)
    sin_h = jnp.concatenate([-sin, sin], axis=-1).reshape(t_len, ATT_HEAD_DIM)
    cos_t = jnp.concatenate([jnp.ones((ctx_len, ATT_HEAD_DIM), F32), cos_h], axis=0)
    sin_t = jnp.concatenate([jnp.zeros((ctx_len, ATT_HEAD_DIM), F32), sin_h], axis=0)
    return jnp.tile(cos_t, (1, 2)), jnp.tile(sin_t, (1, 2))


def _hgrn2_lower_bounds(p):
    sm = jax.nn.softmax(p.astype(F32), axis=1)
    cs = jnp.cumsum(sm, axis=1)
    return cs - cs[:, :1]


def kernel(x, c, ctx, c_ctx, mod_w, mod_b, norm1_w, norm2_w, w_in, dn_conv_w, dn_a_log, dn_dt_bias, dn_norm_w,
           hg_lower_bounds, hg_norm_w, lru_conv_w, lru_conv_b, lru_w_a, lru_b_a, lru_w_x, lru_b_x, lru_lambda,
           att_q_norm_w, att_k_norm_w, w_branch, w_out, mlp_w1, mlp_w2):
    batch, t_len, _ = x.shape
    ctx_len = ctx.shape[1]
    depth = mod_w.shape[0]
    seq = ctx_len + t_len
    n_chunks = seq // CHUNK
    assert ctx_len % SCAN_ROWS == 0 and t_len % SCAN_ROWS == 0 and ctx_len == ATT_TQ
    dims = dict(batch=batch, seq=seq, ctx_len=ctx_len)

    w_ub, w_uf = _regroup_w_in(w_in)
    w_branch_b, w_out_b = w_branch.astype(BF16), w_out.astype(BF16)
    w1_b, w2_b = mlp_w1.astype(BF16), mlp_w2.astype(BF16)
    lru_wa_b, lru_wx_b = lru_w_a.astype(BF16), lru_w_x.astype(BF16)
    lb_all = _hgrn2_lower_bounds(hg_lower_bounds)
    cos_t, sin_t = _rope_tables(t_len, ctx_len)
    grp = jnp.asarray((np.arange(128)[:, None] // ATT_HEAD_DIM) == (np.arange(128)[None, :] // ATT_HEAD_DIM), F32)

    n_mod_rows = batch + 8
    cc = jnp.zeros((n_mod_rows, D_MODEL), F32).at[:batch].set(c).at[batch].set(c_ctx)
    mods = _modulations(cc, mod_w.astype(BF16), mod_b)

    h = jnp.concatenate([ctx, x], axis=1).reshape(batch * seq, D_MODEL)
    for l in range(depth):
        modl = mods[l, :batch].reshape(batch, 6, D_MODEL)
        modc = mods[l, batch].reshape(6, D_MODEL)
        nw1 = norm1_w[l].reshape(1, D_MODEL)
        ub, uf = _inproj(h, modc, modl, nw1, w_ub[l], w_uf[l], seq=seq, ctx_len=ctx_len)

        qkv = _dn_prep(ub, dn_conv_w[l], **dims)
        ba = uf[:, UF_DN_BA:UF_DN_BA + 4 * N_HEADS].reshape(batch * n_chunks, CHUNK, 2, 2, N_HEADS)
        ba_t = jnp.transpose(ba, (3, 0, 2, 4, 1)).reshape(2, batch * n_chunks, 2 * N_HEADS, CHUNK)
        oa = _gdn(qkv, ba_t, dn_a_log[l].reshape(2, N_HEADS, 1), dn_dt_bias[l].reshape(2, N_HEADS, 1), **dims)

        ob = _hgrn2(ub, uf, lb_all[:, l].reshape(2, 1, MIX_W), **dims)

        yc = _lru(ub, lru_conv_w[l], lru_conv_b[l].reshape(1, MIX_W), lru_wa_b[l], lru_b_a[l].reshape(2, 1, MIX_W),
                  lru_wx_b[l], lru_b_x[l].reshape(2, 1, MIX_W), lru_lambda[l].reshape(2, 1, MIX_W), **dims)

        yd = _attention(ub, cos_t, sin_t, jnp.tile(att_q_norm_w[l], 2).reshape(1, 128),
                        jnp.tile(att_k_norm_w[l], 2).reshape(1, 128), grp, **dims)

        h = _merge(oa[0], oa[1], ob[0], ob[1], ub, yc, yd, h, modc, modl, dn_norm_w[l].reshape(1, HEAD_W),
                   hg_norm_w[l].reshape(1, HEAD_W), w_branch_b[l], w_out_b[l], seq=seq, ctx_len=ctx_len)
        h = _mlp(h, modc, modl, norm2_w[l].reshape(1, D_MODEL), w1_b[l], w2_b[l], seq=seq, ctx_len=ctx_len)

    return h.reshape(batch, seq, D_MODEL)[:, ctx_len:]
```

```python
import functools

import jax
import jax.numpy as jnp
import numpy as np
from jax import lax
from jax.experimental import pallas as pl
from jax.experimental.pallas import tpu as pltpu

F32 = jnp.float32
BF16 = jnp.bfloat16

EPS = 1e-6
D_MODEL = 1024
GRID_W = 64
N_HEADS = 4
HEAD_W = 128
MIX_W = N_HEADS * HEAD_W
CHUNK = 64
SUB = 16
SCAN_ROWS = 256
LRU_C = 8.0
LRU_BLOCK = 256
ATT_Q_HEADS = 8
ATT_KV_HEADS = 2
ATT_HEAD_DIM = 64
ATT_GROUP = ATT_Q_HEADS // ATT_KV_HEADS
ATT_QW = ATT_Q_HEADS * ATT_HEAD_DIM
ATT_KW = ATT_KV_HEADS * ATT_HEAD_DIM
ATT_TQ = 256
ROPE_THETA = 10000.0
LOG2_E = 1.4426950408889634
N_BRANCH = 4
D_FF = 4 * D_MODEL
FF_BLOCK = 2048

UB_GATE = 0
UB_HG_Q, UB_HG_I, UB_HG_G = 4096, 4608, 5120
UB_DN_Q = 5632
UB_LRU_X, UB_LRU_G = 7168, 7680
UB_DN_G = 8192
UB_ATT_Q, UB_ATT_K, UB_ATT_V = 8704, 9216, 9344
N_UB_USED = 9472
TN_UB = 1664
N_UB = 6 * TN_UB
UF_HG_FF, UF_HG_FB, UF_DN_BA = 0, 512, 1024
N_UF = 1280

VMEM_LIMIT_V7X = 48 * 1024 * 1024


def _params(sem, vmem=VMEM_LIMIT_V7X):
    return pltpu.CompilerParams(dimension_semantics=sem, vmem_limit_bytes=vmem)


def _sigmoid(x):
    return 0.5 * jnp.tanh(0.5 * x) + 0.5


def _softplus(x):
    return jnp.maximum(x, 0.0) + jnp.log1p(jnp.exp(-jnp.abs(x)))


def _dot(a, b):
    return jnp.dot(a.astype(BF16), b.astype(BF16), preferred_element_type=F32)


def _dot_nt(a, b):
    return lax.dot_general(a.astype(BF16), b.astype(BF16), (((1,), (1,)), ((), ())),
                           preferred_element_type=F32)


def _dot_tn(a, b):
    return lax.dot_general(a.astype(BF16), b.astype(BF16), (((0,), (0,)), ((), ())),
                           preferred_element_type=F32)


def _split3(x):
    hi = x.astype(BF16)
    r1 = x - hi.astype(F32)
    mid = r1.astype(BF16)
    lo = (r1 - mid.astype(F32)).astype(BF16)
    return hi, mid, lo


def _dot_x3(x, m):
    m = m.astype(BF16)
    hi, mid, lo = _split3(x)
    out = jnp.dot(hi, m, preferred_element_type=F32)
    out = out + jnp.dot(mid, m, preferred_element_type=F32)
    return out + jnp.dot(lo, m, preferred_element_type=F32)


def _dot_3x_many(m, xs):
    w = xs[0].shape[1]
    pieces = [_split3(x) for x in xs]
    cols = [p[i] for i in range(3) for p in pieces]
    wide = jnp.dot(m.astype(BF16), jnp.concatenate(cols, axis=1), preferred_element_type=F32)
    n = len(xs)
    part = lambda p, i: wide[:, (p * n + i) * w:(p * n + i + 1) * w]
    return [part(0, i) + part(1, i) + part(2, i) for i in range(n)]


def _rms_rows(x, w):
    return x * lax.rsqrt(jnp.mean(x * x, axis=-1, keepdims=True) + EPS) * w


def _mod_kernel(c_ref, w_ref, b_ref, o_ref):
    c = c_ref[...]
    o_ref[...] = _dot(c * _sigmoid(c), w_ref[...]) + b_ref[...]


def _modulations(cc, mod_w, mod_b):
    depth = mod_w.shape[0]
    rows = cc.shape[0]
    n_out = mod_w.shape[2]
    return pl.pallas_call(
        _mod_kernel,
        out_shape=jax.ShapeDtypeStruct((depth, rows, n_out), F32),
        grid=(depth, n_out // D_MODEL),
        in_specs=[
            pl.BlockSpec((rows, D_MODEL), lambda l, j: (0, 0)),
            pl.BlockSpec((None, D_MODEL, D_MODEL), lambda l, j: (l, 0, j)),
            pl.BlockSpec((None, 1, D_MODEL), lambda l, j: (l, 0, j)),
        ],
        out_specs=pl.BlockSpec((None, rows, D_MODEL), lambda l, j: (l, 0, j)),
        compiler_params=_params(("parallel", "parallel")),
        name="modulations",
    )(cc, mod_w, mod_b.reshape(depth, 1, n_out))


def _pick_mod(modc_ref, modl_ref, idx, is_ctx):
    return jnp.where(is_ctx, modc_ref[idx:idx + 1, :], modl_ref[idx:idx + 1, :])


def _inproj_kernel(h_ref, modc_ref, modl_ref, nw_ref, wb_ref, wf_ref, ob_ref, of_ref, xn_ref,
                   *, tm, tiles_per_batch, ctx_len):
    j = pl.program_id(1)

    @pl.when(j == 0)
    def _():
        y = _rms_rows(h_ref[...], nw_ref[...])
        row = (pl.program_id(0) % tiles_per_batch) * tm + lax.broadcasted_iota(jnp.int32, (tm, 1), 0)
        is_ctx = row < ctx_len
        shift = _pick_mod(modc_ref, modl_ref, 0, is_ctx)
        scale = _pick_mod(modc_ref, modl_ref, 1, is_ctx)
        xn_ref[...] = (y * (1.0 + scale) + shift).astype(BF16)

    ob_ref[...] = jnp.dot(xn_ref[...], wb_ref[...], preferred_element_type=F32).astype(ob_ref.dtype)

    @pl.when(j == pl.num_programs(1) - 1)
    def _():
        of_ref[...] = jnp.dot(xn_ref[...], wf_ref[...], preferred_element_type=F32)


def _inproj(h, modc, modl, nw, w_b, w_f, *, seq, ctx_len):
    rows = h.shape[0]
    tiles_per_batch = 4
    tm = seq // tiles_per_batch
    kern = functools.partial(_inproj_kernel, tm=tm, tiles_per_batch=tiles_per_batch, ctx_len=ctx_len)
    return pl.pallas_call(
        kern,
        out_shape=(jax.ShapeDtypeStruct((rows, N_UB), BF16), jax.ShapeDtypeStruct((rows, N_UF), F32)),
        grid=(rows // tm, N_UB // TN_UB),
        in_specs=[
            pl.BlockSpec((tm, D_MODEL), lambda i, j: (i, 0)),
            pl.BlockSpec((6, D_MODEL), lambda i, j: (0, 0)),
            pl.BlockSpec((None, 6, D_MODEL), lambda i, j: (i // tiles_per_batch, 0, 0)),
            pl.BlockSpec((1, D_MODEL), lambda i, j: (0, 0)),
            pl.BlockSpec((D_MODEL, TN_UB), lambda i, j: (0, j)),
            pl.BlockSpec((D_MODEL, N_UF), lambda i, j: (0, 0)),
        ],
        out_specs=(pl.BlockSpec((tm, TN_UB), lambda i, j: (i, j)),
                   pl.BlockSpec((tm, N_UF), lambda i, j: (i, 0))),
        scratch_shapes=[pltpu.VMEM((tm, D_MODEL), BF16)],
        compiler_params=_params(("parallel", "arbitrary")),
        name="inproj",
    )(h, modc, modl, nw, w_b, w_f)


def _seg_conv(x, w_ref, ctx_len):
    n = x.shape[0]
    row = lax.broadcasted_iota(jnp.int32, (n, 1), 0)
    lo = jnp.where(row >= ctx_len, ctx_len, 0)
    hi = jnp.where(row >= ctx_len, n, ctx_len)

    def tap(k):
        tk = row + k
        valid = jnp.logical_and(tk >= lo, tk < hi)
        return jnp.where(valid, pltpu.roll(x, (-k) % n, 0), 0.0)

    return (tap(-1) * w_ref[0:1, :] + x * w_ref[1:2, :] + tap(1) * w_ref[2:3, :] + tap(2) * w_ref[3:4, :])


def _dn_prep_kernel(u_ref, w_ref, o_ref, *, ctx_len):
    j = pl.program_id(1)
    y = _seg_conv(u_ref[...].astype(F32), w_ref, ctx_len)
    y = y * _sigmoid(y)
    n = lax.rsqrt(jnp.sum(y * y, axis=-1, keepdims=True) + EPS)
    fac = jnp.where(j < N_HEADS, n * (HEAD_W ** -0.5), jnp.where(j < 2 * N_HEADS, n, 1.0))
    o_ref[...] = (y * fac).astype(o_ref.dtype)


def _dn_prep(ub, conv_w, *, batch, seq, ctx_len):
    nblk = 3 * N_HEADS
    return pl.pallas_call(
        functools.partial(_dn_prep_kernel, ctx_len=ctx_len),
        out_shape=jax.ShapeDtypeStruct((batch * seq, 3 * MIX_W), BF16),
        grid=(batch, nblk),
        in_specs=[
            pl.BlockSpec((seq, HEAD_W), lambda b, j: (b, UB_DN_Q // HEAD_W + j)),
            pl.BlockSpec((4, HEAD_W), lambda b, j: (0, j)),
        ],
        out_specs=pl.BlockSpec((seq, HEAD_W), lambda b, j: (b, j)),
        compiler_params=_params(("parallel", "parallel")),
        name="dn_prep",
    )(ub, conv_w)


def _block_order(s, n_ctx, n_all, rev):
    if not rev:
        return s
    return jnp.where(s < n_ctx, n_ctx - 1 - s, n_all + n_ctx - 1 - s)


def _chunk_masks(rev):
    ii = lax.broadcasted_iota(jnp.int32, (CHUNK, CHUNK), 0)
    jj = lax.broadcasted_iota(jnp.int32, (CHUNK, CHUNK), 1)
    incl = (ii <= jj) if rev else (ii >= jj)
    incl_t = (ii >= jj) if rev else (ii <= jj)
    return ii, jj, incl, incl_t


def _unit_tri_inverse(a_all, eye, bd):
    ad = [jnp.where(bd, a, 0.0) for a in a_all]
    ao = [a - d for a, d in zip(a_all, ad)]
    p = [-d for d in ad]
    dinv = [eye + x for x in p]
    for _ in range(3):
        p = [_dot(x, x) for x in p]
        dinv = [d + _dot(d, x) for d, x in zip(dinv, p)]
    m = [-_dot(d, o) for d, o in zip(dinv, ao)]
    mm = [_dot(x, x) for x in m]
    t = [eye + x for x in m]
    t = [x + _dot(x, y) for x, y in zip(t, mm)]
    return [_dot(x, d) for x, d in zip(t, dinv)]


def _gdn_prepare(q_ref, k_ref, v_ref, ba_ref, alog, dtb, rev):
    ii, jj, incl, incl_t = _chunk_masks(rev)
    is_eye = ii == jj
    eye = jnp.where(is_eye, 1.0, 0.0)
    tri = jnp.where(incl, 1.0, 0.0)
    tri_t = jnp.where(incl_t, 1.0, 0.0)
    bd = (ii // SUB) == (jj // SUB)
    n_chunks = q_ref.shape[0] // CHUNK

    alog_neg = -jnp.exp(alog)
    chains = [(j, h) for j in range(n_chunks) for h in range(N_HEADS)]
    vec = {}
    for j in range(n_chunks):
        ba = ba_ref[j]
        beta_r = _sigmoid(ba[0:N_HEADS])
        g_r = alog_neg * _softplus(ba[N_HEADS:2 * N_HEADS] + dtb)
        for h in range(N_HEADS):
            gr = g_r[h:h + 1]
            beta_c = jnp.sum(eye * beta_r[h:h + 1], axis=1, keepdims=True)
            g_c = jnp.sum(eye * gr, axis=1, keepdims=True)
            gam_c = jnp.sum(tri * gr, axis=1, keepdims=True)
            gam_r = jnp.sum(tri_t * g_c, axis=0, keepdims=True)
            tot = jnp.sum(gr, axis=1, keepdims=True)
            dec_i = jnp.where(incl, jnp.exp(gam_c - gam_r), 0.0)
            vec[j, h] = (beta_c, gam_c, tot, dec_i)

    def tile(ref, j, h):
        return ref[j * CHUNK:(j + 1) * CHUNK, h * HEAD_W:(h + 1) * HEAD_W]

    k16 = [tile(k_ref, *c) for c in chains]
    q16 = [tile(q_ref, *c) for c in chains]
    ks = [k.astype(F32) for k in k16]
    kbs = [k * vec[c][0] for k, c in zip(ks, chains)]
    kq = [_dot_nt(jnp.concatenate([kb.astype(BF16), q], axis=0), k) for kb, q, k in zip(kbs, q16, k16)]
    a_low = [jnp.where(is_eye, 0.0, x[:CHUNK] * vec[c][3]) for x, c in zip(kq, chains)]
    a_qk = [x[CHUNK:] * vec[c][3] for x, c in zip(kq, chains)]
    t_inv = _unit_tri_inverse(a_low, eye, bd)
    egc = [jnp.exp(vec[c][1]) for c in chains]
    sols = [_dot(t, jnp.concatenate([tile(v_ref, *c).astype(F32) * vec[c][0], kb * e], axis=1))
            for t, kb, e, c in zip(t_inv, kbs, egc, chains)]
    pre = {}
    for c, sol, aq, q, k, e in zip(chains, sols, a_qk, q16, ks, egc):
        q_dec = q.astype(F32) * e
        k_dec_t = (k * jnp.exp(vec[c][2] - vec[c][1])).T
        pre[c] = (sol[:, :HEAD_W], jnp.concatenate([sol[:, HEAD_W:], q_dec], axis=0).astype(BF16),
                  jnp.concatenate([aq, k_dec_t], axis=0).astype(BF16), jnp.exp(vec[c][2]))
    return pre


def _gdn_kernel(qf_ref, kf_ref, vf_ref, baf_ref, qr_ref, kr_ref, vr_ref, bar_ref, alog_ref, dtb_ref,
                of_ref, or_ref, s_ref):
    @pl.when(pl.program_id(1) == 0)
    def _():
        s_ref[...] = jnp.zeros_like(s_ref)

    n_chunks = qf_ref.shape[0] // CHUNK
    pre = (_gdn_prepare(qf_ref, kf_ref, vf_ref, baf_ref, alog_ref[0], dtb_ref[0], False),
           _gdn_prepare(qr_ref, kr_ref, vr_ref, bar_ref, alog_ref[1], dtb_ref[1], True))
    o_refs = (of_ref, or_ref)

    lanes = [(d, h) for d in range(2) for h in range(N_HEADS)]
    states = {dh: s_ref[dh[0], dh[1]] for dh in lanes}
    for t in range(n_chunks):
        js = (t, n_chunks - 1 - t)
        cur = {dh: pre[dh[0]][js[dh[0]], dh[1]] for dh in lanes}
        wq = {dh: _dot(cur[dh][1], states[dh]) for dh in lanes}
        v_new = {dh: cur[dh][0] - wq[dh][:CHUNK] for dh in lanes}
        ak = {dh: _dot(cur[dh][2], v_new[dh]) for dh in lanes}
        for d, h in lanes:
            j = js[d]
            o_refs[d][j * CHUNK:(j + 1) * CHUNK, h * HEAD_W:(h + 1) * HEAD_W] = wq[d, h][CHUNK:] + ak[d, h][:CHUNK]
        states = {dh: states[dh] * cur[dh][3] + ak[dh][CHUNK:] for dh in lanes}
    for d, h in lanes:
        s_ref[d, h] = states[d, h]


def _gdn(qkv, ba_t, a_log, dt_bias, *, batch, seq, ctx_len):
    n_all = seq // SCAN_ROWS
    n_ctx = ctx_len // SCAN_ROWS
    cpb = SCAN_ROWS // CHUNK

    def rows(b, s, rev):
        return b * n_all + _block_order(s, n_ctx, n_all, rev)

    def in_specs(rev):
        d = int(rev)
        return [
            pl.BlockSpec((SCAN_ROWS, MIX_W), lambda b, s: (rows(b, s, rev), 0)),
            pl.BlockSpec((SCAN_ROWS, MIX_W), lambda b, s: (rows(b, s, rev), 1)),
            pl.BlockSpec((SCAN_ROWS, MIX_W), lambda b, s: (rows(b, s, rev), 2)),
            pl.BlockSpec((None, cpb, 2 * N_HEADS, CHUNK), lambda b, s: (d, rows(b, s, rev), 0, 0)),
        ]

    vec = pl.BlockSpec((2, N_HEADS, 1), lambda b, s: (0, 0, 0))
    out = jax.ShapeDtypeStruct((batch * seq, MIX_W), F32)
    return pl.pallas_call(
        _gdn_kernel,
        out_shape=(out, out),
        grid=(batch, n_all),
        in_specs=in_specs(False) + in_specs(True) + [vec, vec],
        out_specs=(pl.BlockSpec((SCAN_ROWS, MIX_W), lambda b, s: (rows(b, s, False), 0)),
                   pl.BlockSpec((SCAN_ROWS, MIX_W), lambda b, s: (rows(b, s, True), 0))),
        scratch_shapes=[pltpu.VMEM((2, N_HEADS, HEAD_W, HEAD_W), F32)],
        compiler_params=_params(("parallel", "arbitrary")),
        name="gdn",
    )(qkv, qkv, qkv, ba_t, qkv, qkv, qkv, ba_t, a_log, dt_bias)


def _hgrn2_prepare(q_ref, f_ref, i_ref, lb_ref, rev):
    _, _, incl, _ = _chunk_masks(rev)
    tri = jnp.where(incl, 1.0, 0.0)
    row = lax.broadcasted_iota(jnp.int32, (CHUNK, 1), 0)
    n_sub = CHUNK // SUB
    n_chunks = q_ref.shape[0] // CHUNK

    def tile(ref, j, h):
        return ref[j * CHUNK:(j + 1) * CHUNK, h * HEAD_W:(h + 1) * HEAD_W]

    chains = [(j, h) for j in range(n_chunks) for h in range(N_HEADS)]
    qs_, ks_, lfs = [], [], []
    for j, h in chains:
        qr = tile(q_ref, j, h).astype(F32)
        lb = lb_ref[:, h * HEAD_W:(h + 1) * HEAD_W]
        f = lb + (1.0 - lb) * _sigmoid(tile(f_ref, j, h))
        qs_.append(qr * _sigmoid(qr))
        ks_.append(1.0 - f)
        lfs.append(jnp.log(f))
    gcs = _dot_3x_many(tri, lfs)
    tots = [jnp.sum(lf, axis=0, keepdims=True) for lf in lfs]
    blocks = []
    for q, k, gc in zip(qs_, ks_, gcs):
        row_blocks = []
        for i in range(n_sub):
            mid = i * SUB + SUB // 2
            gref = gc[mid:mid + 1, :]
            qsc = q[i * SUB:(i + 1) * SUB, :] * jnp.exp(gc[i * SUB:(i + 1) * SUB, :] - gref)
            lo, hi = (i * SUB, CHUNK) if rev else (0, (i + 1) * SUB)
            parts = [jnp.zeros((lo, HEAD_W), F32), k[lo:hi, :] * jnp.exp(gref - gc[lo:hi, :]),
                     jnp.zeros((CHUNK - hi, HEAD_W), F32)]
            ksc = jnp.concatenate([x for x in parts if x.shape[0]], axis=0)
            row_blocks.append((qsc, ksc))
        blocks.append(row_blocks)
    scores = [jnp.where(incl, jnp.concatenate([_dot_nt(a, b) for a, b in rb], axis=0), 0.0) for rb in blocks]
    vs = [tile(i_ref, *c) for c in chains]
    local = [_dot(sc, v) for sc, v in zip(scores, vs)]
    incr = [_dot_tn(v, k * jnp.exp(tot - gc)) for v, k, gc, tot in zip(vs, ks_, gcs, tots)]
    return {c: (q * jnp.exp(gc), lo, inc, jnp.exp(tot))
            for c, q, gc, lo, inc, tot in zip(chains, qs_, gcs, local, incr, tots)}


def _hgrn2_kernel(qf_ref, ff_ref, if_ref, qr_ref, fr_ref, ir_ref, lb_ref, of_ref, or_ref, s_ref):
    @pl.when(pl.program_id(1) == 0)
    def _():
        s_ref[...] = jnp.zeros_like(s_ref)

    n_chunks = qf_ref.shape[0] // CHUNK
    pre = (_hgrn2_prepare(qf_ref, ff_ref, if_ref, lb_ref.at[0], False),
           _hgrn2_prepare(qr_ref, fr_ref, ir_ref, lb_ref.at[1], True))
    o_refs = (of_ref, or_ref)

    lanes = [(d, h) for d in range(2) for h in range(N_HEADS)]
    states = {dh: s_ref[dh[0], dh[1]] for dh in lanes}
    for t in range(n_chunks):
        js = (t, n_chunks - 1 - t)
        cur = {dh: pre[dh[0]][js[dh[0]], dh[1]] for dh in lanes}
        qs = {dh: _dot_nt(cur[dh][0], states[dh]) for dh in lanes}
        for d, h in lanes:
            j = js[d]
            o_refs[d][j * CHUNK:(j + 1) * CHUNK, h * HEAD_W:(h + 1) * HEAD_W] = qs[d, h] + cur[d, h][1]
        states = {dh: states[dh] * cur[dh][3] + cur[dh][2] for dh in lanes}
    for d, h in lanes:
        s_ref[d, h] = states[d, h]


def _hgrn2(ub, uf, lb, *, batch, seq, ctx_len):
    n_all = seq // SCAN_ROWS
    n_ctx = ctx_len // SCAN_ROWS

    def rows(b, s, rev):
        return b * n_all + _block_order(s, n_ctx, n_all, rev)

    def in_specs(rev):
        f_col = (UF_HG_FB if rev else UF_HG_FF) // MIX_W
        return [
            pl.BlockSpec((SCAN_ROWS, MIX_W), lambda b, s: (rows(b, s, rev), UB_HG_Q // MIX_W)),
            pl.BlockSpec((SCAN_ROWS, MIX_W), lambda b, s: (rows(b, s, rev), f_col)),
            pl.BlockSpec((SCAN_ROWS, MIX_W), lambda b, s: (rows(b, s, rev), UB_HG_I // MIX_W)),
        ]

    out = jax.ShapeDtypeStruct((batch * seq, MIX_W), F32)
    return pl.pallas_call(
        _hgrn2_kernel,
        out_shape=(out, out),
        grid=(batch, n_all),
        in_specs=in_specs(False) + in_specs(True) + [pl.BlockSpec((2, 1, MIX_W), lambda b, s: (0, 0, 0))],
        out_specs=(pl.BlockSpec((SCAN_ROWS, MIX_W), lambda b, s: (rows(b, s, False), 0)),
                   pl.BlockSpec((SCAN_ROWS, MIX_W), lambda b, s: (rows(b, s, True), 0))),
        scratch_shapes=[pltpu.VMEM((2, N_HEADS, HEAD_W, HEAD_W), F32)],
        compiler_params=_params(("parallel", "arbitrary")),
        name="hgrn2",
    )(ub, uf, ub, ub, uf, ub, lb)


LRU_GROUP = 8


def _group_scan(a, b, rev):
    n = a.shape[0]
    sub = lax.broadcasted_iota(jnp.int32, (n, 1), 0) % LRU_GROUP
    s = 1
    while s < LRU_GROUP:
        if rev:
            a_s, b_s, valid = pltpu.roll(a, n - s, 0), pltpu.roll(b, n - s, 0), sub < LRU_GROUP - s
        else:
            a_s, b_s, valid = pltpu.roll(a, s, 0), pltpu.roll(b, s, 0), sub >= s
        b = jnp.where(valid, a * b_s + b, b)
        a = jnp.where(valid, a * a_s, a)
        s *= 2
    return a, b


def _chain_groups(a, b, carry, rev, store):
    n = a.shape[0] // LRU_GROUP
    for g in (range(n - 1, -1, -1) if rev else range(n)):
        hg = a[g * LRU_GROUP:(g + 1) * LRU_GROUP, :] * carry + b[g * LRU_GROUP:(g + 1) * LRU_GROUP, :]
        store(g, hg)
        carry = hg[0:1, :] if rev else hg[LRU_GROUP - 1:LRU_GROUP, :]
    return carry


def _gelu_tanh(x):
    return 0.5 * x * (1.0 + jnp.tanh(0.7978845608028654 * (x + 0.044715 * (x * x * x))))


def _lru_kernel(xb_ref, gb_ref, cw_ref, cb_ref, wa_ref, ba_ref, wx_ref, bx_ref, lam_ref, o_ref, xc_ref, hf_ref,
                hb_ref, *, ctx_len):
    n = xb_ref.shape[0]
    nblk = n // LRU_BLOCK
    nctx = ctx_len // LRU_BLOCK
    xc_ref[...] = _seg_conv(xb_ref[...].astype(F32), cw_ref, ctx_len) + cb_ref[...]

    def gates(xc, d):
        r = _sigmoid(_dot(xc, wa_ref[d]) + ba_ref[d])
        ig = _sigmoid(_dot(xc, wx_ref[d]) + bx_ref[d])
        log_a = -LRU_C * r * _softplus(-lam_ref[d])
        a = jnp.exp(log_a)
        return a, jnp.sqrt(1.0 - a * a) * ig * xc

    def scan_block(blk, carry, d, out_ref):
        base = pl.multiple_of(blk * LRU_BLOCK, LRU_BLOCK)
        a, b = _group_scan(*gates(xc_ref[pl.ds(base, LRU_BLOCK), :], d), rev=bool(d))

        def store(g, hg):
            out_ref[pl.ds(pl.multiple_of(base + g * LRU_GROUP, LRU_GROUP), LRU_GROUP), :] = hg

        return _chain_groups(a, b, carry, bool(d), store)

    def step(i, carry):
        cf = scan_block(i, carry[0], 0, hf_ref)
        cb = scan_block(_block_order(i, nctx, nblk, True), carry[1], 1, hb_ref)
        return cf, cb

    zero = jnp.zeros((1, HEAD_W), F32)
    lax.fori_loop(0, nblk, step, (zero, zero))
    o_ref[...] = ((hf_ref[...] + hb_ref[...]) * _gelu_tanh(gb_ref[...].astype(F32))).astype(o_ref.dtype)


def _lru(ub, conv_w, conv_b, w_a, b_a, w_x, b_x, lam, *, batch, seq, ctx_len):
    vec = pl.BlockSpec((2, 1, HEAD_W), lambda b, h: (0, 0, h))
    mat = pl.BlockSpec((2, None, HEAD_W, HEAD_W), lambda b, h: (0, h, 0, 0))
    return pl.pallas_call(
        functools.partial(_lru_kernel, ctx_len=ctx_len),
        out_shape=jax.ShapeDtypeStruct((batch * seq, MIX_W), BF16),
        grid=(batch, N_HEADS),
        in_specs=[
            pl.BlockSpec((seq, HEAD_W), lambda b, h: (b, UB_LRU_X // HEAD_W + h)),
            pl.BlockSpec((seq, HEAD_W), lambda b, h: (b, UB_LRU_G // HEAD_W + h)),
            pl.BlockSpec((4, HEAD_W), lambda b, h: (0, h)),
            pl.BlockSpec((1, HEAD_W), lambda b, h: (0, h)),
            mat, vec, mat, vec, vec,
        ],
        out_specs=pl.BlockSpec((seq, HEAD_W), lambda b, h: (b, h)),
        scratch_shapes=[pltpu.VMEM((seq, HEAD_W), F32)] * 3,
        compiler_params=_params(("parallel", "parallel")),
        name="rglru",
    )(ub, ub, conv_w, conv_b, w_a, b_a, w_x, b_x, lam)


def _norm_rope(x, w, cos, sin, grp, scale):
    lane = lax.broadcasted_iota(jnp.int32, (1, 128), 1)
    first = (lane % (ATT_HEAD_DIM // 2)) < (ATT_HEAD_DIM // 4)
    ss = _dot_x3(x * x, grp)
    y = x * lax.rsqrt(ss * (1.0 / ATT_HEAD_DIM) + EPS) * w
    rot = jnp.where(first, pltpu.roll(y, 128 - ATT_HEAD_DIM // 4, 1), pltpu.roll(y, ATT_HEAD_DIM // 4, 1))
    return (y * cos + rot * sin) * scale


def _att_kernel(q_ref, k_ref, v_ref, cos_ref, sin_ref, qw_ref, kw_ref, grp_ref, o_ref, kt_ref, *, ctx_len):
    i = pl.program_id(1)
    grp = grp_ref[...]
    tq = q_ref.shape[0]

    @pl.when(i == 0)
    def _():
        for c in range(k_ref.shape[0] // tq):
            rows = slice(c * tq, (c + 1) * tq)
            kn = _norm_rope(k_ref[rows, :].astype(F32), kw_ref[...], cos_ref[rows, :], sin_ref[rows, :], grp, 1.0)
            kt_ref[:, rows] = kn.T.astype(kt_ref.dtype)

    q_scale = (ATT_HEAD_DIM ** -0.5) * LOG2_E
    q_rows = pl.ds(pl.multiple_of(i * tq, tq), tq)
    cos, sin = cos_ref[q_rows, :], sin_ref[q_rows, :]
    qn = [_norm_rope(q_ref[:, s * 128:(s + 1) * 128].astype(F32), qw_ref[...], cos, sin, grp, q_scale).astype(BF16)
          for s in range(q_ref.shape[1] // 128)]

    def attend(n_keys):
        v = v_ref[:n_keys, :]

        def scores(head):
            g = head // ATT_GROUP
            kt = kt_ref[g * ATT_HEAD_DIM:(g + 1) * ATT_HEAD_DIM, :n_keys]
            q = qn[head // 2][:, (head % 2) * ATT_HEAD_DIM:(head % 2 + 1) * ATT_HEAD_DIM]
            return jnp.dot(q, kt, preferred_element_type=F32)

        s_next = scores(0)
        for head in range(ATT_Q_HEADS):
            s = s_next
            if head + 1 < ATT_Q_HEADS:
                s_next = scores(head + 1)
            g = head // ATT_GROUP
            p = jnp.exp2(s - jnp.max(s, axis=-1, keepdims=True))
            den = jnp.sum(p, axis=-1, keepdims=True)
            pv = jnp.dot(p.astype(BF16), v, preferred_element_type=F32)
            o_ref[:, head * ATT_HEAD_DIM:(head + 1) * ATT_HEAD_DIM] = (
                pv[:, g * ATT_HEAD_DIM:(g + 1) * ATT_HEAD_DIM] / den).astype(o_ref.dtype)

    @pl.when(i == 0)
    def _():
        attend(ctx_len)

    @pl.when(i != 0)
    def _():
        attend(kt_ref.shape[1])


def _attention(ub, cos, sin, qw, kw, grp, *, batch, seq, ctx_len):
    tiles = seq // ATT_TQ
    whole = lambda shape: pl.BlockSpec(shape, lambda b, i: (0, 0))
    return pl.pallas_call(
        functools.partial(_att_kernel, ctx_len=ctx_len),
        out_shape=jax.ShapeDtypeStruct((batch * seq, ATT_QW), BF16),
        grid=(batch, tiles),
        in_specs=[
            pl.BlockSpec((ATT_TQ, ATT_QW), lambda b, i: (b * tiles + i, UB_ATT_Q // ATT_QW)),
            pl.BlockSpec((seq, ATT_KW), lambda b, i: (b, UB_ATT_K // ATT_KW)),
            pl.BlockSpec((seq, ATT_KW), lambda b, i: (b, UB_ATT_V // ATT_KW)),
            whole((seq, 128)), whole((seq, 128)), whole((1, 128)), whole((1, 128)), whole((128, 128)),
        ],
        out_specs=pl.BlockSpec((ATT_TQ, ATT_QW), lambda b, i: (b * tiles + i, 0)),
        scratch_shapes=[pltpu.VMEM((ATT_KW, seq), BF16)],
        compiler_params=_params(("parallel", "arbitrary")),
        name="attention",
    )(ub, ub, ub, cos, sin, qw, kw, grp)


def _merge_kernel(oaf_ref, oab_ref, obf_ref, obb_ref, ga_ref, gb_ref, yc_ref, yd_ref, gate_ref, h_ref, modc_ref,
                  modl_ref, dnw_ref, hgw_ref, wb_ref, wo_ref, out_ref, *, tiles_per_batch):
    def gated(of_ref, ob_ref, g_ref, nw_ref):
        o = of_ref[...] + ob_ref[...]
        ys = [_rms_rows(o[:, h * HEAD_W:(h + 1) * HEAD_W], nw_ref[...]) for h in range(N_HEADS)]
        g = g_ref[...].astype(F32)
        return jnp.concatenate(ys, axis=1) * (g * _sigmoid(g))

    ys = (gated(oaf_ref, oab_ref, ga_ref, dnw_ref), gated(obf_ref, obb_ref, gb_ref, hgw_ref), yc_ref[...], yd_ref[...])
    acc = None
    for b in range(N_BRANCH):
        gate = gate_ref[:, b * D_MODEL:(b + 1) * D_MODEL].astype(F32)
        term = _sigmoid(gate) * _dot(ys[b], wb_ref[b])
        acc = term if acc is None else acc + term
    is_ctx = (pl.program_id(0) % tiles_per_batch) == 0
    out_ref[...] = h_ref[...] + _pick_mod(modc_ref, modl_ref, 2, is_ctx) * _dot(acc, wo_ref[...])


def _merge(oaf, oab, obf, obb, ub, yc, yd, h, modc, modl, dnw, hgw, wb, wo, *, seq, ctx_len):
    rows = h.shape[0]
    tm = ctx_len
    tiles_per_batch = seq // tm
    mix = lambda c: pl.BlockSpec((tm, MIX_W), lambda i: (i, c // MIX_W))
    return pl.pallas_call(
        functools.partial(_merge_kernel, tiles_per_batch=tiles_per_batch),
        out_shape=jax.ShapeDtypeStruct((rows, D_MODEL), F32),
        grid=(rows // tm,),
        in_specs=[
            mix(0), mix(0), mix(0), mix(0), mix(UB_DN_G), mix(UB_HG_G), mix(0), mix(0),
            pl.BlockSpec((tm, N_BRANCH * D_MODEL), lambda i: (i, 0)),
            pl.BlockSpec((tm, D_MODEL), lambda i: (i, 0)),
            pl.BlockSpec((6, D_MODEL), lambda i: (0, 0)),
            pl.BlockSpec((None, 6, D_MODEL), lambda i: (i // tiles_per_batch, 0, 0)),
            pl.BlockSpec((1, HEAD_W), lambda i: (0, 0)),
            pl.BlockSpec((1, HEAD_W), lambda i: (0, 0)),
            pl.BlockSpec((N_BRANCH, MIX_W, D_MODEL), lambda i: (0, 0, 0)),
            pl.BlockSpec((D_MODEL, D_MODEL), lambda i: (0, 0)),
        ],
        out_specs=pl.BlockSpec((tm, D_MODEL), lambda i: (i, 0)),
        compiler_params=_params(("parallel",)),
        name="merge",
    )(oaf, oab, obf, obb, ub, ub, yc, yd, ub, h, modc, modl, dnw, hgw, wb, wo)


def _mlp_kernel(h_ref, modc_ref, modl_ref, nw_ref, w1_ref, w2_ref, o_ref, z_ref, acc_ref,
                *, tm, tiles_per_batch, ctx_len):
    j = pl.program_id(1)
    row = (pl.program_id(0) % tiles_per_batch) * tm + lax.broadcasted_iota(jnp.int32, (tm, 1), 0)
    is_ctx = row < ctx_len

    @pl.when(j == 0)
    def _():
        y = _rms_rows(h_ref[...], nw_ref[...])
        shift = _pick_mod(modc_ref, modl_ref, 3, is_ctx)
        scale = _pick_mod(modc_ref, modl_ref, 4, is_ctx)
        z_ref[...] = (y * (1.0 + scale) + shift).astype(BF16)
        acc_ref[...] = jnp.zeros_like(acc_ref)

    a = jnp.maximum(jnp.dot(z_ref[...], w1_ref[...], preferred_element_type=F32), 0.0)
    acc_ref[...] += _dot(a * a, w2_ref[...])

    @pl.when(j == pl.num_programs(1) - 1)
    def _():
        o_ref[...] = h_ref[...] + _pick_mod(modc_ref, modl_ref, 5, is_ctx) * acc_ref[...]


def _mlp(h, modc, modl, nw, w1, w2, *, seq, ctx_len):
    rows = h.shape[0]
    tiles_per_batch = 4
    tm = seq // tiles_per_batch
    kern = functools.partial(_mlp_kernel, tm=tm, tiles_per_batch=tiles_per_batch, ctx_len=ctx_len)
    return pl.pallas_call(
        kern,
        out_shape=jax.ShapeDtypeStruct((rows, D_MODEL), F32),
        grid=(rows // tm, D_FF // FF_BLOCK),
        in_specs=[
            pl.BlockSpec((tm, D_MODEL), lambda i, j: (i, 0)),
            pl.BlockSpec((6, D_MODEL), lambda i, j: (0, 0)),
            pl.BlockSpec((None, 6, D_MODEL), lambda i, j: (i // tiles_per_batch, 0, 0)),
            pl.BlockSpec((1, D_MODEL), lambda i, j: (0, 0)),
            pl.BlockSpec((D_MODEL, FF_BLOCK), lambda i, j: (0, j)),
            pl.BlockSpec((FF_BLOCK, D_MODEL), lambda i, j: (j, 0)),
        ],
        out_specs=pl.BlockSpec((tm, D_MODEL), lambda i, j: (i, 0)),
        scratch_shapes=[pltpu.VMEM((tm, D_MODEL), BF16), pltpu.VMEM((tm, D_MODEL), F32)],
        compiler_params=_params(("parallel", "arbitrary"), vmem=VMEM_LIMIT_V7X + 8 * 1024 * 1024),
        name="mlp",
    )(h, modc, modl, nw, w1, w2)


def _regroup_w_in(w_in):
    o = np.cumsum([0, 512, 512, 512, 512, 8, 8, 512, 512, 512, 512, 512, 512, 512, 512, 128, 128, 4096])
    cols = lambda a, b: w_in[..., o[a]:o[b]]
    wb = jnp.concatenate([cols(16, 17), cols(6, 7), cols(9, 11), cols(0, 3), cols(11, 13), cols(3, 4), cols(13, 16)],
                         axis=-1)
    wf = jnp.concatenate([cols(7, 9), cols(4, 6)], axis=-1)
    pad = lambda w, n: jnp.pad(w, ((0, 0), (0, 0), (0, n - w.shape[-1]))).astype(BF16)
    return pad(wb, N_UB), pad(wf, N_UF)


def _rope_tables(t_len, ctx_len):
    rows = t_len // GRID_W
    row_id = jnp.repeat(jnp.arange(rows), GRID_W).astype(F32)
    col_id = jnp.tile(jnp.arange(GRID_W), rows).astype(F32)
    axis_dim = ATT_HEAD_DIM // 2
    inv = ROPE_THETA ** (-jnp.arange(0, axis_dim, 2, dtype=F32) / axis_dim)
    ang = jnp.stack([row_id[:, None] * inv, col_id[:, None] * inv], axis=1)
    cos, sin = jnp.cos(ang), jnp.sin(ang)
    cos_h = jnp.concatenate([cos, cos], axis=-1).reshape(t_len, ATT_HEAD_DIM)
    sin_h = jnp.concatenate([-sin, sin], axis=-1).reshape(t_len, ATT_HEAD_DIM)
    cos_t = jnp.concatenate([jnp.ones((ctx_len, ATT_HEAD_DIM), F32), cos_h], axis=0)
    sin_t = jnp.concatenate([jnp.zeros((ctx_len, ATT_HEAD_DIM), F32), sin_h], axis=0)
    return jnp.tile(cos_t, (1, 2)), jnp.tile(sin_t, (1, 2))


def _hgrn2_lower_bounds(p):
    sm = jax.nn.softmax(p.astype(F32), axis=1)
    cs = jnp.cumsum(sm, axis=1)
    return cs - cs[:, :1]


def kernel(x, c, ctx, c_ctx, mod_w, mod_b, norm1_w, norm2_w, w_in, dn_conv_w, dn_a_log, dn_dt_bias, dn_norm_w,
           hg_lower_bounds, hg_norm_w, lru_conv_w, lru_conv_b, lru_w_a, lru_b_a, lru_w_x, lru_b_x, lru_lambda,
           att_q_norm_w, att_k_norm_w, w_branch, w_out, mlp_w1, mlp_w2):
    batch, t_len, _ = x.shape
    ctx_len = ctx.shape[1]
    depth = mod_w.shape[0]
    seq = ctx_len + t_len
    n_chunks = seq // CHUNK
    assert ctx_len % SCAN_ROWS == 0 and t_len % SCAN_ROWS == 0 and ctx_len == ATT_TQ
    dims = dict(batch=batch, seq=seq, ctx_len=ctx_len)

    w_ub, w_uf = _regroup_w_in(w_in)
    w_branch_b, w_out_b = w_branch.astype(BF16), w_out.astype(BF16)
    w1_b, w2_b = mlp_w1.astype(BF16), mlp_w2.astype(BF16)
    lru_wa_b, lru_wx_b = lru_w_a.astype(BF16), lru_w_x.astype(BF16)
    lb_all = _hgrn2_lower_bounds(hg_lower_bounds)
    cos_t, sin_t = _rope_tables(t_len, ctx_len)
    grp = jnp.asarray((np.arange(128)[:, None] // ATT_HEAD_DIM) == (np.arange(128)[None, :] // ATT_HEAD_DIM), F32)

    n_mod_rows = batch + 8
    cc = jnp.zeros((n_mod_rows, D_MODEL), F32).at[:batch].set(c).at[batch].set(c_ctx)
    mods = _modulations(cc, mod_w.astype(BF16), mod_b)

    h = jnp.concatenate([ctx, x], axis=1).reshape(batch * seq, D_MODEL)
    for l in range(depth):
        modl = mods[l, :batch].reshape(batch, 6, D_MODEL)
        modc = mods[l, batch].reshape(6, D_MODEL)
        nw1 = norm1_w[l].reshape(1, D_MODEL)
        ub, uf = _inproj(h, modc, modl, nw1, w_ub[l], w_uf[l], seq=seq, ctx_len=ctx_len)

        qkv = _dn_prep(ub, dn_conv_w[l], **dims)
        ba = uf[:, UF_DN_BA:UF_DN_BA + 4 * N_HEADS].reshape(batch * n_chunks, CHUNK, 2, 2, N_HEADS)
        ba_t = jnp.transpose(ba, (3, 0, 2, 4, 1)).reshape(2, batch * n_chunks, 2 * N_HEADS, CHUNK)
        oa = _gdn(qkv, ba_t, dn_a_log[l].reshape(2, N_HEADS, 1), dn_dt_bias[l].reshape(2, N_HEADS, 1), **dims)

        ob = _hgrn2(ub, uf, lb_all[:, l].reshape(2, 1, MIX_W), **dims)

        yc = _lru(ub, lru_conv_w[l], lru_conv_b[l].reshape(1, MIX_W), lru_wa_b[l], lru_b_a[l].reshape(2, 1, MIX_W),
                  lru_wx_b[l], lru_b_x[l].reshape(2, 1, MIX_W), lru_lambda[l].reshape(2, 1, MIX_W), **dims)

        yd = _attention(ub, cos_t, sin_t, jnp.tile(att_q_norm_w[l], 2).reshape(1, 128),
                        jnp.tile(att_k_norm_w[l], 2).reshape(1, 128), grp, **dims)

        h = _merge(oa[0], oa[1], ob[0], ob[1], ub, yc, yd, h, modc, modl, dn_norm_w[l].reshape(1, HEAD_W),
                   hg_norm_w[l].reshape(1, HEAD_W), w_branch_b[l], w_out_b[l], seq=seq, ctx_len=ctx_len)
        h = _mlp(h, modc, modl, norm2_w[l].reshape(1, D_MODEL), w1_b[l], w2_b[l], seq=seq, ctx_len=ctx_len)

    return h.reshape(batch, seq, D_MODEL)[:, ctx_len:]
```

```python
import functools

import jax
import jax.numpy as jnp
import numpy as np
from jax import lax
from jax.experimental import pallas as pl
from jax.experimental.pallas import tpu as pltpu

F32 = jnp.float32
BF16 = jnp.bfloat16

EPS = 1e-6
D_MODEL = 1024
GRID_W = 64
N_HEADS = 4
HEAD_W = 128
MIX_W = N_HEADS * HEAD_W
CHUNK = 64
SUB = 16
SCAN_ROWS = 256
LRU_C = 8.0
LRU_BLOCK = 256
ATT_Q_HEADS = 8
ATT_KV_HEADS = 2
ATT_HEAD_DIM = 64
ATT_GROUP = ATT_Q_HEADS // ATT_KV_HEADS
ATT_QW = ATT_Q_HEADS * ATT_HEAD_DIM
ATT_KW = ATT_KV_HEADS * ATT_HEAD_DIM
ATT_TQ = 256
ROPE_THETA = 10000.0
LOG2_E = 1.4426950408889634
N_BRANCH = 4
D_FF = 4 * D_MODEL
FF_BLOCK = 2048

UB_GATE = 0
UB_HG_Q, UB_HG_I, UB_HG_G = 4096, 4608, 5120
UB_DN_Q = 5632
UB_LRU_X, UB_LRU_G = 7168, 7680
UB_DN_G = 8192
UB_ATT_Q, UB_ATT_K, UB_ATT_V = 8704, 9216, 9344
N_UB_USED = 9472
TN_UB = 1664
N_UB = 6 * TN_UB
UF_HG_FF, UF_HG_FB, UF_DN_BA = 0, 512, 1024
N_UF = 1280

VMEM_LIMIT_V7X = 48 * 1024 * 1024


def _params(sem, vmem=VMEM_LIMIT_V7X):
    return pltpu.CompilerParams(dimension_semantics=sem, vmem_limit_bytes=vmem)


def _sigmoid(x):
    return 0.5 * jnp.tanh(0.5 * x) + 0.5


def _softplus(x):
    return jnp.maximum(x, 0.0) + jnp.log1p(jnp.exp(-jnp.abs(x)))


def _dot(a, b):
    return jnp.dot(a.astype(BF16), b.astype(BF16), preferred_element_type=F32)


def _dot_nt(a, b):
    return lax.dot_general(a.astype(BF16), b.astype(BF16), (((1,), (1,)), ((), ())),
                           preferred_element_type=F32)


def _dot_tn(a, b):
    return lax.dot_general(a.astype(BF16), b.astype(BF16), (((0,), (0,)), ((), ())),
                           preferred_element_type=F32)


def _split3(x):
    hi = x.astype(BF16)
    r1 = x - hi.astype(F32)
    mid = r1.astype(BF16)
    lo = (r1 - mid.astype(F32)).astype(BF16)
    return hi, mid, lo


def _dot_x3(x, m):
    m = m.astype(BF16)
    hi, mid, lo = _split3(x)
    out = jnp.dot(hi, m, preferred_element_type=F32)
    out = out + jnp.dot(mid, m, preferred_element_type=F32)
    return out + jnp.dot(lo, m, preferred_element_type=F32)


def _dot_3x_many(m, xs):
    w = xs[0].shape[1]
    pieces = [_split3(x) for x in xs]
    cols = [p[i] for i in range(3) for p in pieces]
    wide = jnp.dot(m.astype(BF16), jnp.concatenate(cols, axis=1), preferred_element_type=F32)
    n = len(xs)
    part = lambda p, i: wide[:, (p * n + i) * w:(p * n + i + 1) * w]
    return [part(0, i) + part(1, i) + part(2, i) for i in range(n)]


def _rms_rows(x, w):
    return x * lax.rsqrt(jnp.mean(x * x, axis=-1, keepdims=True) + EPS) * w


def _mod_kernel(c_ref, w_ref, b_ref, o_ref):
    c = c_ref[...]
    o_ref[...] = _dot(c * _sigmoid(c), w_ref[...]) + b_ref[...]


def _modulations(cc, mod_w, mod_b):
    depth = mod_w.shape[0]
    rows = cc.shape[0]
    n_out = mod_w.shape[2]
    return pl.pallas_call(
        _mod_kernel,
        out_shape=jax.ShapeDtypeStruct((depth, rows, n_out), F32),
        grid=(depth, n_out // D_MODEL),
        in_specs=[
            pl.BlockSpec((rows, D_MODEL), lambda l, j: (0, 0)),
            pl.BlockSpec((None, D_MODEL, D_MODEL), lambda l, j: (l, 0, j)),
            pl.BlockSpec((None, 1, D_MODEL), lambda l, j: (l, 0, j)),
        ],
        out_specs=pl.BlockSpec((None, rows, D_MODEL), lambda l, j: (l, 0, j)),
        compiler_params=_params(("parallel", "parallel")),
        name="modulations",
    )(cc, mod_w, mod_b.reshape(depth, 1, n_out))


def _pick_mod(modc_ref, modl_ref, idx, is_ctx):
    return jnp.where(is_ctx, modc_ref[idx:idx + 1, :], modl_ref[idx:idx + 1, :])


def _inproj_kernel(h_ref, modc_ref, modl_ref, nw_ref, wb_ref, wf_ref, ob_ref, of_ref, xn_ref,
                   *, tm, tiles_per_batch, ctx_len):
    j = pl.program_id(1)

    @pl.when(j == 0)
    def _():
        y = _rms_rows(h_ref[...], nw_ref[...])
        row = (pl.program_id(0) % tiles_per_batch) * tm + lax.broadcasted_iota(jnp.int32, (tm, 1), 0)
        is_ctx = row < ctx_len
        shift = _pick_mod(modc_ref, modl_ref, 0, is_ctx)
        scale = _pick_mod(modc_ref, modl_ref, 1, is_ctx)
        xn_ref[...] = (y * (1.0 + scale) + shift).astype(BF16)

    ob_ref[...] = jnp.dot(xn_ref[...], wb_ref[...], preferred_element_type=F32).astype(ob_ref.dtype)

    @pl.when(j == pl.num_programs(1) - 1)
    def _():
        of_ref[...] = jnp.dot(xn_ref[...], wf_ref[...], preferred_element_type=F32)


def _inproj(h, modc, modl, nw, w_b, w_f, *, seq, ctx_len):
    rows = h.shape[0]
    tiles_per_batch = 4
    tm = seq // tiles_per_batch
    kern = functools.partial(_inproj_kernel, tm=tm, tiles_per_batch=tiles_per_batch, ctx_len=ctx_len)
    return pl.pallas_call(
        kern,
        out_shape=(jax.ShapeDtypeStruct((rows, N_UB), BF16), jax.ShapeDtypeStruct((rows, N_UF), F32)),
        grid=(rows // tm, N_UB // TN_UB),
        in_specs=[
            pl.BlockSpec((tm, D_MODEL), lambda i, j: (i, 0)),
            pl.BlockSpec((6, D_MODEL), lambda i, j: (0, 0)),
            pl.BlockSpec((None, 6, D_MODEL), lambda i, j: (i // tiles_per_batch, 0, 0)),
            pl.BlockSpec((1, D_MODEL), lambda i, j: (0, 0)),
            pl.BlockSpec((D_MODEL, TN_UB), lambda i, j: (0, j)),
            pl.BlockSpec((D_MODEL, N_UF), lambda i, j: (0, 0)),
        ],
        out_specs=(pl.BlockSpec((tm, TN_UB), lambda i, j: (i, j)),
                   pl.BlockSpec((tm, N_UF), lambda i, j: (i, 0))),
        scratch_shapes=[pltpu.VMEM((tm, D_MODEL), BF16)],
        compiler_params=_params(("parallel", "arbitrary")),
        name="inproj",
    )(h, modc, modl, nw, w_b, w_f)


def _seg_conv(x, w_ref, ctx_len):
    n = x.shape[0]
    row = lax.broadcasted_iota(jnp.int32, (n, 1), 0)
    lo = jnp.where(row >= ctx_len, ctx_len, 0)
    hi = jnp.where(row >= ctx_len, n, ctx_len)

    def tap(k):
        tk = row + k
        valid = jnp.logical_and(tk >= lo, tk < hi)
        return jnp.where(valid, pltpu.roll(x, (-k) % n, 0), 0.0)

    return (tap(-1) * w_ref[0:1, :] + x * w_ref[1:2, :] + tap(1) * w_ref[2:3, :] + tap(2) * w_ref[3:4, :])


def _dn_prep_kernel(u_ref, w_ref, o_ref, *, ctx_len):
    j = pl.program_id(1)
    y = _seg_conv(u_ref[...].astype(F32), w_ref, ctx_len)
    y = y * _sigmoid(y)
    n = lax.rsqrt(jnp.sum(y * y, axis=-1, keepdims=True) + EPS)
    fac = jnp.where(j < N_HEADS, n * (HEAD_W ** -0.5), jnp.where(j < 2 * N_HEADS, n, 1.0))
    o_ref[...] = (y * fac).astype(o_ref.dtype)


def _dn_prep(ub, conv_w, *, batch, seq, ctx_len):
    nblk = 3 * N_HEADS
    return pl.pallas_call(
        functools.partial(_dn_prep_kernel, ctx_len=ctx_len),
        out_shape=jax.ShapeDtypeStruct((batch * seq, 3 * MIX_W), BF16),
        grid=(batch, nblk),
        in_specs=[
            pl.BlockSpec((seq, HEAD_W), lambda b, j: (b, UB_DN_Q // HEAD_W + j)),
            pl.BlockSpec((4, HEAD_W), lambda b, j: (0, j)),
        ],
        out_specs=pl.BlockSpec((seq, HEAD_W), lambda b, j: (b, j)),
        compiler_params=_params(("parallel", "parallel")),
        name="dn_prep",
    )(ub, conv_w)


def _block_order(s, n_ctx, n_all, rev):
    if not rev:
        return s
    return jnp.where(s < n_ctx, n_ctx - 1 - s, n_all + n_ctx - 1 - s)


def _chunk_masks(rev):
    ii = lax.broadcasted_iota(jnp.int32, (CHUNK, CHUNK), 0)
    jj = lax.broadcasted_iota(jnp.int32, (CHUNK, CHUNK), 1)
    incl = (ii <= jj) if rev else (ii >= jj)
    incl_t = (ii >= jj) if rev else (ii <= jj)
    return ii, jj, incl, incl_t


def _unit_tri_inverse(a_all, eye, bd):
    ad = [jnp.where(bd, a, 0.0) for a in a_all]
    ao = [a - d for a, d in zip(a_all, ad)]
    p = [-d for d in ad]
    dinv = [eye + x for x in p]
    for _ in range(3):
        p = [_dot(x, x) for x in p]
        dinv = [d + _dot(d, x) for d, x in zip(dinv, p)]
    m = [-_dot(d, o) for d, o in zip(dinv, ao)]
    mm = [_dot(x, x) for x in m]
    t = [eye + x for x in m]
    t = [x + _dot(x, y) for x, y in zip(t, mm)]
    return [_dot(x, d) for x, d in zip(t, dinv)]


def _gdn_prepare(q_ref, k_ref, v_ref, ba_ref, alog, dtb, rev):
    ii, jj, incl, incl_t = _chunk_masks(rev)
    is_eye = ii == jj
    eye = jnp.where(is_eye, 1.0, 0.0)
    tri = jnp.where(incl, 1.0, 0.0)
    tri_t = jnp.where(incl_t, 1.0, 0.0)
    bd = (ii // SUB) == (jj // SUB)
    n_chunks = q_ref.shape[0] // CHUNK

    alog_neg = -jnp.exp(alog)
    chains = [(j, h) for j in range(n_chunks) for h in range(N_HEADS)]
    vec = {}
    for j in range(n_chunks):
        ba = ba_ref[j]
        beta_r = _sigmoid(ba[0:N_HEADS])
        g_r = alog_neg * _softplus(ba[N_HEADS:2 * N_HEADS] + dtb)
        for h in range(N_HEADS):
            gr = g_r[h:h + 1]
            beta_c = jnp.sum(eye * beta_r[h:h + 1], axis=1, keepdims=True)
            g_c = jnp.sum(eye * gr, axis=1, keepdims=True)
            gam_c = jnp.sum(tri * gr, axis=1, keepdims=True)
            gam_r = jnp.sum(tri_t * g_c, axis=0, keepdims=True)
            tot = jnp.sum(gr, axis=1, keepdims=True)
            dec_i = jnp.where(incl, jnp.exp(gam_c - gam_r), 0.0)
            vec[j, h] = (beta_c, gam_c, tot, dec_i)

    def tile(ref, j, h):
        return ref[j * CHUNK:(j + 1) * CHUNK, h * HEAD_W:(h + 1) * HEAD_W]

    k16 = [tile(k_ref, *c) for c in chains]
    q16 = [tile(q_ref, *c) for c in chains]
    ks = [k.astype(F32) for k in k16]
    kbs = [k * vec[c][0] for k, c in zip(ks, chains)]
    kq = [_dot_nt(jnp.concatenate([kb.astype(BF16), q], axis=0), k) for kb, q, k in zip(kbs, q16, k16)]
    a_low = [jnp.where(is_eye, 0.0, x[:CHUNK] * vec[c][3]) for x, c in zip(kq, chains)]
    a_qk = [x[CHUNK:] * vec[c][3] for x, c in zip(kq, chains)]
    t_inv = _unit_tri_inverse(a_low, eye, bd)
    egc = [jnp.exp(vec[c][1]) for c in chains]
    sols = [_dot(t, jnp.concatenate([tile(v_ref, *c).astype(F32) * vec[c][0], kb * e], axis=1))
            for t, kb, e, c in zip(t_inv, kbs, egc, chains)]
    pre = {}
    for c, sol, aq, q, k, e in zip(chains, sols, a_qk, q16, ks, egc):
        q_dec = q.astype(F32) * e
        k_dec_t = (k * jnp.exp(vec[c][2] - vec[c][1])).T
        pre[c] = (sol[:, :HEAD_W], jnp.concatenate([sol[:, HEAD_W:], q_dec], axis=0).astype(BF16),
                  jnp.concatenate([aq, k_dec_t], axis=0).astype(BF16), jnp.exp(vec[c][2]))
    return pre


def _gdn_kernel(qf_ref, kf_ref, vf_ref, baf_ref, qr_ref, kr_ref, vr_ref, bar_ref, alog_ref, dtb_ref,
                of_ref, or_ref, s_ref):
    @pl.when(pl.program_id(1) == 0)
    def _():
        s_ref[...] = jnp.zeros_like(s_ref)

    n_chunks = qf_ref.shape[0] // CHUNK
    pre = (_gdn_prepare(qf_ref, kf_ref, vf_ref, baf_ref, alog_ref[0], dtb_ref[0], False),
           _gdn_prepare(qr_ref, kr_ref, vr_ref, bar_ref, alog_ref[1], dtb_ref[1], True))
    o_refs = (of_ref, or_ref)

    lanes = [(d, h) for d in range(2) for h in range(N_HEADS)]
    states = {dh: s_ref[dh[0], dh[1]] for dh in lanes}
    for t in range(n_chunks):
        js = (t, n_chunks - 1 - t)
        cur = {dh: pre[dh[0]][js[dh[0]], dh[1]] for dh in lanes}
        wq = {dh: _dot(cur[dh][1], states[dh]) for dh in lanes}
        v_new = {dh: cur[dh][0] - wq[dh][:CHUNK] for dh in lanes}
        ak = {dh: _dot(cur[dh][2], v_new[dh]) for dh in lanes}
        for d, h in lanes:
            j = js[d]
            o_refs[d][j * CHUNK:(j + 1) * CHUNK, h * HEAD_W:(h + 1) * HEAD_W] = wq[d, h][CHUNK:] + ak[d, h][:CHUNK]
        states = {dh: states[dh] * cur[dh][3] + ak[dh][CHUNK:] for dh in lanes}
    for d, h in lanes:
        s_ref[d, h] = states[d, h]


def _gdn(qkv, ba_t, a_log, dt_bias, *, batch, seq, ctx_len):
    n_all = seq // SCAN_ROWS
    n_ctx = ctx_len // SCAN_ROWS
    cpb = SCAN_ROWS // CHUNK

    def rows(b, s, rev):
        return b * n_all + _block_order(s, n_ctx, n_all, rev)

    def in_specs(rev):
        d = int(rev)
        return [
            pl.BlockSpec((SCAN_ROWS, MIX_W), lambda b, s: (rows(b, s, rev), 0)),
            pl.BlockSpec((SCAN_ROWS, MIX_W), lambda b, s: (rows(b, s, rev), 1)),
            pl.BlockSpec((SCAN_ROWS, MIX_W), lambda b, s: (rows(b, s, rev), 2)),
            pl.BlockSpec((None, cpb, 2 * N_HEADS, CHUNK), lambda b, s: (d, rows(b, s, rev), 0, 0)),
        ]

    vec = pl.BlockSpec((2, N_HEADS, 1), lambda b, s: (0, 0, 0))
    out = jax.ShapeDtypeStruct((batch * seq, MIX_W), F32)
    return pl.pallas_call(
        _gdn_kernel,
        out_shape=(out, out),
        grid=(batch, n_all),
        in_specs=in_specs(False) + in_specs(True) + [vec, vec],
        out_specs=(pl.BlockSpec((SCAN_ROWS, MIX_W), lambda b, s: (rows(b, s, False), 0)),
                   pl.BlockSpec((SCAN_ROWS, MIX_W), lambda b, s: (rows(b, s, True), 0))),
        scratch_shapes=[pltpu.VMEM((2, N_HEADS, HEAD_W, HEAD_W), F32)],
        compiler_params=_params(("parallel", "arbitrary")),
        name="gdn",
    )(qkv, qkv, qkv, ba_t, qkv, qkv, qkv, ba_t, a_log, dt_bias)


def _hgrn2_prepare(q_ref, f_ref, i_ref, lb_ref, rev):
    _, _, incl, _ = _chunk_masks(rev)
    tri = jnp.where(incl, 1.0, 0.0)
    row = lax.broadcasted_iota(jnp.int32, (CHUNK, 1), 0)
    n_sub = CHUNK // SUB
    n_chunks = q_ref.shape[0] // CHUNK

    def tile(ref, j, h):
        return ref[j * CHUNK:(j + 1) * CHUNK, h * HEAD_W:(h + 1) * HEAD_W]

    chains = [(j, h) for j in range(n_chunks) for h in range(N_HEADS)]
    qs_, ks_, lfs = [], [], []
    for j, h in chains:
        qr = tile(q_ref, j, h).astype(F32)
        lb = lb_ref[:, h * HEAD_W:(h + 1) * HEAD_W]
        f = lb + (1.0 - lb) * _sigmoid(tile(f_ref, j, h))
        qs_.append(qr * _sigmoid(qr))
        ks_.append(1.0 - f)
        lfs.append(jnp.log(f))
    gcs = _dot_3x_many(tri, lfs)
    tots = [jnp.sum(lf, axis=0, keepdims=True) for lf in lfs]
    blocks = []
    for q, k, gc in zip(qs_, ks_, gcs):
        row_blocks = []
        for i in range(n_sub):
            mid = i * SUB + SUB // 2
            gref = gc[mid:mid + 1, :]
            qsc = q[i * SUB:(i + 1) * SUB, :] * jnp.exp(gc[i * SUB:(i + 1) * SUB, :] - gref)
            lo, hi = (i * SUB, CHUNK) if rev else (0, (i + 1) * SUB)
            parts = [jnp.zeros((lo, HEAD_W), F32), k[lo:hi, :] * jnp.exp(gref - gc[lo:hi, :]),
                     jnp.zeros((CHUNK - hi, HEAD_W), F32)]
            ksc = jnp.concatenate([x for x in parts if x.shape[0]], axis=0)
            row_blocks.append((qsc, ksc))
        blocks.append(row_blocks)
    scores = [jnp.where(incl, jnp.concatenate([_dot_nt(a, b) for a, b in rb], axis=0), 0.0) for rb in blocks]
    vs = [tile(i_ref, *c) for c in chains]
    local = [_dot(sc, v) for sc, v in zip(scores, vs)]
    incr = [_dot_tn(v, k * jnp.exp(tot - gc)) for v, k, gc, tot in zip(vs, ks_, gcs, tots)]
    return {c: (q * jnp.exp(gc), lo, inc, jnp.exp(tot))
            for c, q, gc, lo, inc, tot in zip(chains, qs_, gcs, local, incr, tots)}


def _hgrn2_kernel(qf_ref, ff_ref, if_ref, qr_ref, fr_ref, ir_ref, lb_ref, of_ref, or_ref, s_ref):
    @pl.when(pl.program_id(1) == 0)
    def _():
        s_ref[...] = jnp.zeros_like(s_ref)

    n_chunks = qf_ref.shape[0] // CHUNK
    pre = (_hgrn2_prepare(qf_ref, ff_ref, if_ref, lb_ref.at[0], False),
           _hgrn2_prepare(qr_ref, fr_ref, ir_ref, lb_ref.at[1], True))
    o_refs = (of_ref, or_ref)

    lanes = [(d, h) for d in range(2) for h in range(N_HEADS)]
    states = {dh: s_ref[dh[0], dh[1]] for dh in lanes}
    for t in range(n_chunks):
        js = (t, n_chunks - 1 - t)
        cur = {dh: pre[dh[0]][js[dh[0]], dh[1]] for dh in lanes}
        qs = {dh: _dot_nt(cur[dh][0], states[dh]) for dh in lanes}
        for d, h in lanes:
            j = js[d]
            o_refs[d][j * CHUNK:(j + 1) * CHUNK, h * HEAD_W:(h + 1) * HEAD_W] = qs[d, h] + cur[d, h][1]
        states = {dh: states[dh] * cur[dh][3] + cur[dh][2] for dh in lanes}
    for d, h in lanes:
        s_ref[d, h] = states[d, h]


def _hgrn2(ub, uf, lb, *, batch, seq, ctx_len):
    n_all = seq // SCAN_ROWS
    n_ctx = ctx_len // SCAN_ROWS

    def rows(b, s, rev):
        return b * n_all + _block_order(s, n_ctx, n_all, rev)

    def in_specs(rev):
        f_col = (UF_HG_FB if rev else UF_HG_FF) // MIX_W
        return [
            pl.BlockSpec((SCAN_ROWS, MIX_W), lambda b, s: (rows(b, s, rev), UB_HG_Q // MIX_W)),
            pl.BlockSpec((SCAN_ROWS, MIX_W), lambda b, s: (rows(b, s, rev), f_col)),
            pl.BlockSpec((SCAN_ROWS, MIX_W), lambda b, s: (rows(b, s, rev), UB_HG_I // MIX_W)),
        ]

    out = jax.ShapeDtypeStruct((batch * seq, MIX_W), F32)
    return pl.pallas_call(
        _hgrn2_kernel,
        out_shape=(out, out),
        grid=(batch, n_all),
        in_specs=in_specs(False) + in_specs(True) + [pl.BlockSpec((2, 1, MIX_W), lambda b, s: (0, 0, 0))],
        out_specs=(pl.BlockSpec((SCAN_ROWS, MIX_W), lambda b, s: (rows(b, s, False), 0)),
                   pl.BlockSpec((SCAN_ROWS, MIX_W), lambda b, s: (rows(b, s, True), 0))),
        scratch_shapes=[pltpu.VMEM((2, N_HEADS, HEAD_W, HEAD_W), F32)],
        compiler_params=_params(("parallel", "arbitrary")),
        name="hgrn2",
    )(ub, uf, ub, ub, uf, ub, lb)


LRU_GROUP = 8


def _group_scan(a, b, rev):
    n, w = a.shape
    a = a.reshape(n // LRU_GROUP, LRU_GROUP, w)
    b = b.reshape(n // LRU_GROUP, LRU_GROUP, w)
    sub = lax.broadcasted_iota(jnp.int32, (1, LRU_GROUP, 1), 1)
    s = 1
    while s < LRU_GROUP:
        shift, valid = (LRU_GROUP - s, sub < LRU_GROUP - s) if rev else (s, sub >= s)
        a_s, b_s = pltpu.roll(a, shift, 1), pltpu.roll(b, shift, 1)
        b = jnp.where(valid, a * b_s + b, b)
        a = jnp.where(valid, a * a_s, a)
        s *= 2
    return a, b


def _chain_groups(a, b, carry, rev, store):
    n = a.shape[0]
    for g in (range(n - 1, -1, -1) if rev else range(n)):
        hg = a[g] * carry + b[g]
        store(g, hg)
        carry = hg[0:1, :] if rev else hg[LRU_GROUP - 1:LRU_GROUP, :]
    return carry


def _gelu_tanh(x):
    return 0.5 * x * (1.0 + jnp.tanh(0.7978845608028654 * (x + 0.044715 * (x * x * x))))


def _lru_kernel(xb_ref, gb_ref, cw_ref, cb_ref, wa_ref, ba_ref, wx_ref, bx_ref, lam_ref, o_ref, xc_ref, hf_ref,
                hb_ref, *, ctx_len):
    n = xb_ref.shape[0]
    nblk = n // LRU_BLOCK
    nctx = ctx_len // LRU_BLOCK
    xc_ref[...] = _seg_conv(xb_ref[...].astype(F32), cw_ref, ctx_len) + cb_ref[...]

    rate = [(-LRU_C * LOG2_E) * _softplus(-lam_ref[d]) for d in range(2)]

    def gates(xc, d):
        r = _sigmoid(_dot(xc, wa_ref[d]) + ba_ref[d])
        ig = _sigmoid(_dot(xc, wx_ref[d]) + bx_ref[d])
        a = jnp.exp2(r * rate[d])
        return a, jnp.sqrt(1.0 - a * a) * ig * xc

    def scan_block(blk, carry, d, out_ref):
        base = pl.multiple_of(blk * LRU_BLOCK, LRU_BLOCK)
        a, b = _group_scan(*gates(xc_ref[pl.ds(base, LRU_BLOCK), :], d), rev=bool(d))

        def store(g, hg):
            out_ref[pl.ds(pl.multiple_of(base + g * LRU_GROUP, LRU_GROUP), LRU_GROUP), :] = hg

        return _chain_groups(a, b, carry, bool(d), store)

    def step(i, carry):
        cf = scan_block(i, carry[0], 0, hf_ref)
        cb = scan_block(_block_order(i, nctx, nblk, True), carry[1], 1, hb_ref)
        return cf, cb

    zero = jnp.zeros((1, HEAD_W), F32)
    lax.fori_loop(0, nblk, step, (zero, zero))
    o_ref[...] = ((hf_ref[...] + hb_ref[...]) * _gelu_tanh(gb_ref[...].astype(F32))).astype(o_ref.dtype)


def _lru(ub, conv_w, conv_b, w_a, b_a, w_x, b_x, lam, *, batch, seq, ctx_len):
    vec = pl.BlockSpec((2, 1, HEAD_W), lambda b, h: (0, 0, h))
    mat = pl.BlockSpec((2, None, HEAD_W, HEAD_W), lambda b, h: (0, h, 0, 0))
    return pl.pallas_call(
        functools.partial(_lru_kernel, ctx_len=ctx_len),
        out_shape=jax.ShapeDtypeStruct((batch * seq, MIX_W), BF16),
        grid=(batch, N_HEADS),
        in_specs=[
            pl.BlockSpec((seq, HEAD_W), lambda b, h: (b, UB_LRU_X // HEAD_W + h)),
            pl.BlockSpec((seq, HEAD_W), lambda b, h: (b, UB_LRU_G // HEAD_W + h)),
            pl.BlockSpec((4, HEAD_W), lambda b, h: (0, h)),
            pl.BlockSpec((1, HEAD_W), lambda b, h: (0, h)),
            mat, vec, mat, vec, vec,
        ],
        out_specs=pl.BlockSpec((seq, HEAD_W), lambda b, h: (b, h)),
        scratch_shapes=[pltpu.VMEM((seq, HEAD_W), F32)] * 3,
        compiler_params=_params(("parallel", "parallel")),
        name="rglru",
    )(ub, ub, conv_w, conv_b, w_a, b_a, w_x, b_x, lam)


def _norm_rope(x, w, cos, sin, grp, scale):
    lane = lax.broadcasted_iota(jnp.int32, (1, 128), 1)
    first = (lane % (ATT_HEAD_DIM // 2)) < (ATT_HEAD_DIM // 4)
    ss = _dot_x3(x * x, grp)
    y = x * lax.rsqrt(ss * (1.0 / ATT_HEAD_DIM) + EPS) * w
    rot = jnp.where(first, pltpu.roll(y, 128 - ATT_HEAD_DIM // 4, 1), pltpu.roll(y, ATT_HEAD_DIM // 4, 1))
    return (y * cos + rot * sin) * scale


def _att_kernel(q_ref, k_ref, v_ref, cos_ref, sin_ref, qw_ref, kw_ref, grp_ref, o_ref, kt_ref, *, ctx_len):
    i = pl.program_id(1)
    grp = grp_ref[...]
    tq = q_ref.shape[0]

    @pl.when(i == 0)
    def _():
        for c in range(k_ref.shape[0] // tq):
            rows = slice(c * tq, (c + 1) * tq)
            kn = _norm_rope(k_ref[rows, :].astype(F32), kw_ref[...], cos_ref[rows, :], sin_ref[rows, :], grp, 1.0)
            kt_ref[:, rows] = kn.T.astype(kt_ref.dtype)

    q_scale = (ATT_HEAD_DIM ** -0.5) * LOG2_E
    q_rows = pl.ds(pl.multiple_of(i * tq, tq), tq)
    cos, sin = cos_ref[q_rows, :], sin_ref[q_rows, :]
    qn = [_norm_rope(q_ref[:, s * 128:(s + 1) * 128].astype(F32), qw_ref[...], cos, sin, grp, q_scale).astype(BF16)
          for s in range(q_ref.shape[1] // 128)]

    def attend(n_keys):
        v = v_ref[:n_keys, :]

        def scores(head):
            g = head // ATT_GROUP
            kt = kt_ref[g * ATT_HEAD_DIM:(g + 1) * ATT_HEAD_DIM, :n_keys]
            q = qn[head // 2][:, (head % 2) * ATT_HEAD_DIM:(head % 2 + 1) * ATT_HEAD_DIM]
            return jnp.dot(q, kt, preferred_element_type=F32)

        s_next = scores(0)
        for head in range(ATT_Q_HEADS):
            s = s_next
            if head + 1 < ATT_Q_HEADS:
                s_next = scores(head + 1)
            g = head // ATT_GROUP
            p = jnp.exp2(s - jnp.max(s, axis=-1, keepdims=True))
            den = jnp.sum(p, axis=-1, keepdims=True)
            pv = jnp.dot(p.astype(BF16), v, preferred_element_type=F32)
            o_ref[:, head * ATT_HEAD_DIM:(head + 1) * ATT_HEAD_DIM] = (
                pv[:, g * ATT_HEAD_DIM:(g + 1) * ATT_HEAD_DIM] / den).astype(o_ref.dtype)

    @pl.when(i == 0)
    def _():
        attend(ctx_len)

    @pl.when(i != 0)
    def _():
        attend(kt_ref.shape[1])


def _attention(ub, cos, sin, qw, kw, grp, *, batch, seq, ctx_len):
    tiles = seq // ATT_TQ
    whole = lambda shape: pl.BlockSpec(shape, lambda b, i: (0, 0))
    return pl.pallas_call(
        functools.partial(_att_kernel, ctx_len=ctx_len),
        out_shape=jax.ShapeDtypeStruct((batch * seq, ATT_QW), BF16),
        grid=(batch, tiles),
        in_specs=[
            pl.BlockSpec((ATT_TQ, ATT_QW), lambda b, i: (b * tiles + i, UB_ATT_Q // ATT_QW)),
            pl.BlockSpec((seq, ATT_KW), lambda b, i: (b, UB_ATT_K // ATT_KW)),
            pl.BlockSpec((seq, ATT_KW), lambda b, i: (b, UB_ATT_V // ATT_KW)),
            whole((seq, 128)), whole((seq, 128)), whole((1, 128)), whole((1, 128)), whole((128, 128)),
        ],
        out_specs=pl.BlockSpec((ATT_TQ, ATT_QW), lambda b, i: (b * tiles + i, 0)),
        scratch_shapes=[pltpu.VMEM((ATT_KW, seq), BF16)],
        compiler_params=_params(("parallel", "arbitrary")),
        name="attention",
    )(ub, ub, ub, cos, sin, qw, kw, grp)


def _merge_kernel(oaf_ref, oab_ref, obf_ref, obb_ref, ga_ref, gb_ref, yc_ref, yd_ref, gate_ref, h_ref, modc_ref,
                  modl_ref, dnw_ref, hgw_ref, wb_ref, wo_ref, out_ref, *, tiles_per_batch):
    def gated(of_ref, ob_ref, g_ref, nw_ref):
        o = of_ref[...] + ob_ref[...]
        ys = [_rms_rows(o[:, h * HEAD_W:(h + 1) * HEAD_W], nw_ref[...]) for h in range(N_HEADS)]
        g = g_ref[...].astype(F32)
        return jnp.concatenate(ys, axis=1) * (g * _sigmoid(g))

    ys = (gated(oaf_ref, oab_ref, ga_ref, dnw_ref), gated(obf_ref, obb_ref, gb_ref, hgw_ref), yc_ref[...], yd_ref[...])
    acc = None
    for b in range(N_BRANCH):
        gate = gate_ref[:, b * D_MODEL:(b + 1) * D_MODEL].astype(F32)
        term = _sigmoid(gate) * _dot(ys[b], wb_ref[b])
        acc = term if acc is None else acc + term
    is_ctx = (pl.program_id(0) % tiles_per_batch) == 0
    out_ref[...] = h_ref[...] + _pick_mod(modc_ref, modl_ref, 2, is_ctx) * _dot(acc, wo_ref[...])


def _merge(oaf, oab, obf, obb, ub, yc, yd, h, modc, modl, dnw, hgw, wb, wo, *, seq, ctx_len):
    rows = h.shape[0]
    tm = ctx_len
    tiles_per_batch = seq // tm
    mix = lambda c: pl.BlockSpec((tm, MIX_W), lambda i: (i, c // MIX_W))
    return pl.pallas_call(
        functools.partial(_merge_kernel, tiles_per_batch=tiles_per_batch),
        out_shape=jax.ShapeDtypeStruct((rows, D_MODEL), F32),
        grid=(rows // tm,),
        in_specs=[
            mix(0), mix(0), mix(0), mix(0), mix(UB_DN_G), mix(UB_HG_G), mix(0), mix(0),
            pl.BlockSpec((tm, N_BRANCH * D_MODEL), lambda i: (i, 0)),
            pl.BlockSpec((tm, D_MODEL), lambda i: (i, 0)),
            pl.BlockSpec((6, D_MODEL), lambda i: (0, 0)),
            pl.BlockSpec((None, 6, D_MODEL), lambda i: (i // tiles_per_batch, 0, 0)),
            pl.BlockSpec((1, HEAD_W), lambda i: (0, 0)),
            pl.BlockSpec((1, HEAD_W), lambda i: (0, 0)),
            pl.BlockSpec((N_BRANCH, MIX_W, D_MODEL), lambda i: (0, 0, 0)),
            pl.BlockSpec((D_MODEL, D_MODEL), lambda i: (0, 0)),
        ],
        out_specs=pl.BlockSpec((tm, D_MODEL), lambda i: (i, 0)),
        compiler_params=_params(("parallel",)),
        name="merge",
    )(oaf, oab, obf, obb, ub, ub, yc, yd, ub, h, modc, modl, dnw, hgw, wb, wo)


def _mlp_kernel(h_ref, modc_ref, modl_ref, nw_ref, w1_ref, w2_ref, o_ref, z_ref, acc_ref,
                *, tm, tiles_per_batch, ctx_len):
    j = pl.program_id(1)
    row = (pl.program_id(0) % tiles_per_batch) * tm + lax.broadcasted_iota(jnp.int32, (tm, 1), 0)
    is_ctx = row < ctx_len

    @pl.when(j == 0)
    def _():
        y = _rms_rows(h_ref[...], nw_ref[...])
        shift = _pick_mod(modc_ref, modl_ref, 3, is_ctx)
        scale = _pick_mod(modc_ref, modl_ref, 4, is_ctx)
        z_ref[...] = (y * (1.0 + scale) + shift).astype(BF16)
        acc_ref[...] = jnp.zeros_like(acc_ref)

    a = jnp.maximum(jnp.dot(z_ref[...], w1_ref[...], preferred_element_type=F32), 0.0)
    acc_ref[...] += _dot(a * a, w2_ref[...])

    @pl.when(j == pl.num_programs(1) - 1)
    def _():
        o_ref[...] = h_ref[...] + _pick_mod(modc_ref, modl_ref, 5, is_ctx) * acc_ref[...]


def _mlp(h, modc, modl, nw, w1, w2, *, seq, ctx_len):
    rows = h.shape[0]
    tiles_per_batch = 4
    tm = seq // tiles_per_batch
    kern = functools.partial(_mlp_kernel, tm=tm, tiles_per_batch=tiles_per_batch, ctx_len=ctx_len)
    return pl.pallas_call(
        kern,
        out_shape=jax.ShapeDtypeStruct((rows, D_MODEL), F32),
        grid=(rows // tm, D_FF // FF_BLOCK),
        in_specs=[
            pl.BlockSpec((tm, D_MODEL), lambda i, j: (i, 0)),
            pl.BlockSpec((6, D_MODEL), lambda i, j: (0, 0)),
            pl.BlockSpec((None, 6, D_MODEL), lambda i, j: (i // tiles_per_batch, 0, 0)),
            pl.BlockSpec((1, D_MODEL), lambda i, j: (0, 0)),
            pl.BlockSpec((D_MODEL, FF_BLOCK), lambda i, j: (0, j)),
            pl.BlockSpec((FF_BLOCK, D_MODEL), lambda i, j: (j, 0)),
        ],
        out_specs=pl.BlockSpec((tm, D_MODEL), lambda i, j: (i, 0)),
        scratch_shapes=[pltpu.VMEM((tm, D_MODEL), BF16), pltpu.VMEM((tm, D_MODEL), F32)],
        compiler_params=_params(("parallel", "arbitrary"), vmem=VMEM_LIMIT_V7X + 8 * 1024 * 1024),
        name="mlp",
    )(h, modc, modl, nw, w1, w2)


def _regroup_w_in(w_in):
    o = np.cumsum([0, 512, 512, 512, 512, 8, 8, 512, 512, 512, 512, 512, 512, 512, 512, 128, 128, 4096])
    cols = lambda a, b: w_in[..., o[a]:o[b]]
    wb = jnp.concatenate([cols(16, 17), cols(6, 7), cols(9, 11), cols(0, 3), cols(11, 13), cols(3, 4), cols(13, 16)],
                         axis=-1)
    wf = jnp.concatenate([cols(7, 9), cols(4, 6)], axis=-1)
    pad = lambda w, n: jnp.pad(w, ((0, 0), (0, 0), (0, n - w.shape[-1]))).astype(BF16)
    return pad(wb, N_UB), pad(wf, N_UF)


def _rope_tables(t_len, ctx_len):
    rows = t_len // GRID_W
    row_id = jnp.repeat(jnp.arange(rows), GRID_W).astype(F32)
    col_id = jnp.tile(jnp.arange(GRID_W), rows).astype(F32)
    axis_dim = ATT_HEAD_DIM // 2
    inv = ROPE_THETA ** (-jnp.arange(0, axis_dim, 2, dtype=F32) / axis_dim)
    ang = jnp.stack([row_id[:, None] * inv, col_id[:, None] * inv], axis=1)
    cos, sin = jnp.cos(ang), jnp.sin(ang)
    cos_h = jnp.concatenate([cos, cos], axis=-1).reshape(t_len, ATT_HEAD_DIM)
    sin_h = jnp.concatenate([-sin, sin], axis=-1).reshape(t_len, ATT_HEAD_DIM)
    cos_t = jnp.concatenate([jnp.ones((ctx_len, ATT_HEAD_DIM), F32), cos_h], axis=0)
    sin_t = jnp.concatenate([jnp.zeros((ctx_len, ATT_HEAD_DIM), F32), sin_h], axis=0)
    return jnp.tile(cos_t, (1, 2)), jnp.tile(sin_t, (1, 2))


def _hgrn2_lower_bounds(p):
    sm = jax.nn.softmax(p.astype(F32), axis=1)
    cs = jnp.cumsum(sm, axis=1)
    return cs - cs[:, :1]


def kernel(x, c, ctx, c_ctx, mod_w, mod_b, norm1_w, norm2_w, w_in, dn_conv_w, dn_a_log, dn_dt_bias, dn_norm_w,
           hg_lower_bounds, hg_norm_w, lru_conv_w, lru_conv_b, lru_w_a, lru_b_a, lru_w_x, lru_b_x, lru_lambda,
           att_q_norm_w, att_k_norm_w, w_branch, w_out, mlp_w1, mlp_w2):
    batch, t_len, _ = x.shape
    ctx_len = ctx.shape[1]
    depth = mod_w.shape[0]
    seq = ctx_len + t_len
    n_chunks = seq // CHUNK
    assert ctx_len % SCAN_ROWS == 0 and t_len % SCAN_ROWS == 0 and ctx_len == ATT_TQ
    dims = dict(batch=batch, seq=seq, ctx_len=ctx_len)

    w_ub, w_uf = _regroup_w_in(w_in)
    w_branch_b, w_out_b = w_branch.astype(BF16), w_out.astype(BF16)
    w1_b, w2_b = mlp_w1.astype(BF16), mlp_w2.astype(BF16)
    lru_wa_b, lru_wx_b = lru_w_a.astype(BF16), lru_w_x.astype(BF16)
    lb_all = _hgrn2_lower_bounds(hg_lower_bounds)
    cos_t, sin_t = _rope_tables(t_len, ctx_len)
    grp = jnp.asarray((np.arange(128)[:, None] // ATT_HEAD_DIM) == (np.arange(128)[None, :] // ATT_HEAD_DIM), F32)

    n_mod_rows = batch + 8
    cc = jnp.zeros((n_mod_rows, D_MODEL), F32).at[:batch].set(c).at[batch].set(c_ctx)
    mods = _modulations(cc, mod_w.astype(BF16), mod_b)

    h = jnp.concatenate([ctx, x], axis=1).reshape(batch * seq, D_MODEL)
    for l in range(depth):
        modl = mods[l, :batch].reshape(batch, 6, D_MODEL)
        modc = mods[l, batch].reshape(6, D_MODEL)
        nw1 = norm1_w[l].reshape(1, D_MODEL)
        ub, uf = _inproj(h, modc, modl, nw1, w_ub[l], w_uf[l], seq=seq, ctx_len=ctx_len)

        qkv = _dn_prep(ub, dn_conv_w[l], **dims)
        ba = uf[:, UF_DN_BA:UF_DN_BA + 4 * N_HEADS].reshape(batch * n_chunks, CHUNK, 2, 2, N_HEADS)
        ba_t = jnp.transpose(ba, (3, 0, 2, 4, 1)).reshape(2, batch * n_chunks, 2 * N_HEADS, CHUNK)
        oa = _gdn(qkv, ba_t, dn_a_log[l].reshape(2, N_HEADS, 1), dn_dt_bias[l].reshape(2, N_HEADS, 1), **dims)

        ob = _hgrn2(ub, uf, lb_all[:, l].reshape(2, 1, MIX_W), **dims)

        yc = _lru(ub, lru_conv_w[l], lru_conv_b[l].reshape(1, MIX_W), lru_wa_b[l], lru_b_a[l].reshape(2, 1, MIX_W),
                  lru_wx_b[l], lru_b_x[l].reshape(2, 1, MIX_W), lru_lambda[l].reshape(2, 1, MIX_W), **dims)

        yd = _attention(ub, cos_t, sin_t, jnp.tile(att_q_norm_w[l], 2).reshape(1, 128),
                        jnp.tile(att_k_norm_w[l], 2).reshape(1, 128), grp, **dims)

        h = _merge(oa[0], oa[1], ob[0], ob[1], ub, yc, yd, h, modc, modl, dn_norm_w[l].reshape(1, HEAD_W),
                   hg_norm_w[l].reshape(1, HEAD_W), w_branch_b[l], w_out_b[l], seq=seq, ctx_len=ctx_len)
        h = _mlp(h, modc, modl, norm2_w[l].reshape(1, D_MODEL), w1_b[l], w2_b[l], seq=seq, ctx_len=ctx_len)

    return h.reshape(batch, seq, D_MODEL)[:, ctx_len:]
```

```python
import functools

import jax
import jax.numpy as jnp
import numpy as np
from jax import lax
from jax.experimental import pallas as pl
from jax.experimental.pallas import tpu as pltpu

F32 = jnp.float32
BF16 = jnp.bfloat16

EPS = 1e-6
D_MODEL = 1024
GRID_W = 64
N_HEADS = 4
HEAD_W = 128
MIX_W = N_HEADS * HEAD_W
CHUNK = 64
SUB = 16
SCAN_ROWS = 256
LRU_C = 8.0
LRU_BLOCK = 256
ATT_Q_HEADS = 8
ATT_KV_HEADS = 2
ATT_HEAD_DIM = 64
ATT_GROUP = ATT_Q_HEADS // ATT_KV_HEADS
ATT_QW = ATT_Q_HEADS * ATT_HEAD_DIM
ATT_KW = ATT_KV_HEADS * ATT_HEAD_DIM
ATT_TQ = 256
ROPE_THETA = 10000.0
LOG2_E = 1.4426950408889634
N_BRANCH = 4
D_FF = 4 * D_MODEL
FF_BLOCK = 2048

UB_GATE = 0
UB_HG_Q, UB_HG_I, UB_HG_G = 4096, 4608, 5120
UB_DN_Q = 5632
UB_LRU_X, UB_LRU_G = 7168, 7680
UB_DN_G = 8192
UB_ATT_Q, UB_ATT_K, UB_ATT_V = 8704, 9216, 9344
N_UB_USED = 9472
TN_UB = 2048
N_UB = 5 * TN_UB
UF_HG_FF, UF_HG_FB, UF_DN_BA = 0, 512, 1024
N_UF = 1280

VMEM_LIMIT_V7X = 48 * 1024 * 1024


def _params(sem, vmem=VMEM_LIMIT_V7X):
    return pltpu.CompilerParams(dimension_semantics=sem, vmem_limit_bytes=vmem)


def _sigmoid(x):
    return 0.5 * jnp.tanh(0.5 * x) + 0.5


def _silu(x):
    h = 0.5 * x
    return h * (1.0 + jnp.tanh(h))


def _softplus(x):
    return jnp.maximum(x, 0.0) + jnp.log1p(jnp.exp(-jnp.abs(x)))


def _dot(a, b):
    return jnp.dot(a.astype(BF16), b.astype(BF16), preferred_element_type=F32)


def _dot_nt(a, b):
    return lax.dot_general(a.astype(BF16), b.astype(BF16), (((1,), (1,)), ((), ())),
                           preferred_element_type=F32)


def _dot_tn(a, b):
    return lax.dot_general(a.astype(BF16), b.astype(BF16), (((0,), (0,)), ((), ())),
                           preferred_element_type=F32)


def _split3(x):
    hi = x.astype(BF16)
    r1 = x - hi.astype(F32)
    mid = r1.astype(BF16)
    lo = (r1 - mid.astype(F32)).astype(BF16)
    return hi, mid, lo


def _dot_x3(x, m):
    m = m.astype(BF16)
    hi, mid, lo = _split3(x)
    out = jnp.dot(hi, m, preferred_element_type=F32)
    out = out + jnp.dot(mid, m, preferred_element_type=F32)
    return out + jnp.dot(lo, m, preferred_element_type=F32)


def _dot_3x_many(m, xs):
    w = xs[0].shape[1]
    pieces = [_split3(x) for x in xs]
    cols = [p[i] for i in range(3) for p in pieces]
    wide = jnp.dot(m.astype(BF16), jnp.concatenate(cols, axis=1), preferred_element_type=F32)
    n = len(xs)
    part = lambda p, i: wide[:, (p * n + i) * w:(p * n + i + 1) * w]
    return [part(0, i) + part(1, i) + part(2, i) for i in range(n)]


def _rms_rows(x, w):
    return x * lax.rsqrt(jnp.mean(x * x, axis=-1, keepdims=True) + EPS) * w


def _mod_kernel(c_ref, w_ref, b_ref, o_ref):
    c = c_ref[...]
    o_ref[...] = _dot(_silu(c), w_ref[...]) + b_ref[...]


def _modulations(cc, mod_w, mod_b):
    depth = mod_w.shape[0]
    rows = cc.shape[0]
    n_out = mod_w.shape[2]
    return pl.pallas_call(
        _mod_kernel,
        out_shape=jax.ShapeDtypeStruct((depth, rows, n_out), F32),
        grid=(depth, n_out // D_MODEL),
        in_specs=[
            pl.BlockSpec((rows, D_MODEL), lambda l, j: (0, 0)),
            pl.BlockSpec((None, D_MODEL, D_MODEL), lambda l, j: (l, 0, j)),
            pl.BlockSpec((None, 1, D_MODEL), lambda l, j: (l, 0, j)),
        ],
        out_specs=pl.BlockSpec((None, rows, D_MODEL), lambda l, j: (l, 0, j)),
        compiler_params=_params(("parallel", "parallel")),
        name="modulations",
    )(cc, mod_w, mod_b.reshape(depth, 1, n_out))


def _pick_mod(modc_ref, modl_ref, idx, is_ctx):
    return jnp.where(is_ctx, modc_ref[idx:idx + 1, :], modl_ref[idx:idx + 1, :])


def _inproj_kernel(h_ref, modc_ref, modl_ref, nw_ref, wb_ref, wf_ref, ob_ref, of_ref, xn_ref,
                   *, tm, tiles_per_batch, ctx_len):
    j = pl.program_id(1)

    @pl.when(j == 0)
    def _():
        y = _rms_rows(h_ref[...], nw_ref[...])
        row = (pl.program_id(0) % tiles_per_batch) * tm + lax.broadcasted_iota(jnp.int32, (tm, 1), 0)
        is_ctx = row < ctx_len
        shift = _pick_mod(modc_ref, modl_ref, 0, is_ctx)
        scale = _pick_mod(modc_ref, modl_ref, 1, is_ctx)
        xn_ref[...] = (y * (1.0 + scale) + shift).astype(BF16)

    ob_ref[...] = jnp.dot(xn_ref[...], wb_ref[...], preferred_element_type=F32).astype(ob_ref.dtype)

    @pl.when(j == pl.num_programs(1) - 1)
    def _():
        of_ref[...] = jnp.dot(xn_ref[...], wf_ref[...], preferred_element_type=F32)


def _inproj(h, modc, modl, nw, w_b, w_f, *, seq, ctx_len):
    rows = h.shape[0]
    tiles_per_batch = 4
    tm = seq // tiles_per_batch
    kern = functools.partial(_inproj_kernel, tm=tm, tiles_per_batch=tiles_per_batch, ctx_len=ctx_len)
    return pl.pallas_call(
        kern,
        out_shape=(jax.ShapeDtypeStruct((rows, N_UB), BF16), jax.ShapeDtypeStruct((rows, N_UF), F32)),
        grid=(rows // tm, N_UB // TN_UB),
        in_specs=[
            pl.BlockSpec((tm, D_MODEL), lambda i, j: (i, 0)),
            pl.BlockSpec((6, D_MODEL), lambda i, j: (0, 0)),
            pl.BlockSpec((None, 6, D_MODEL), lambda i, j: (i // tiles_per_batch, 0, 0)),
            pl.BlockSpec((1, D_MODEL), lambda i, j: (0, 0)),
            pl.BlockSpec((D_MODEL, TN_UB), lambda i, j: (0, j)),
            pl.BlockSpec((D_MODEL, N_UF), lambda i, j: (0, 0)),
        ],
        out_specs=(pl.BlockSpec((tm, TN_UB), lambda i, j: (i, j)),
                   pl.BlockSpec((tm, N_UF), lambda i, j: (i, 0))),
        scratch_shapes=[pltpu.VMEM((tm, D_MODEL), BF16)],
        compiler_params=_params(("parallel", "arbitrary")),
        name="inproj",
    )(h, modc, modl, nw, w_b, w_f)


def _seg_conv(x, w_ref, ctx_len):
    n = x.shape[0]
    row = lax.broadcasted_iota(jnp.int32, (n, 1), 0)
    lo = jnp.where(row >= ctx_len, ctx_len, 0)
    hi = jnp.where(row >= ctx_len, n, ctx_len)

    def tap(k):
        tk = row + k
        valid = jnp.logical_and(tk >= lo, tk < hi)
        return jnp.where(valid, pltpu.roll(x, (-k) % n, 0), 0.0)

    return (tap(-1) * w_ref[0:1, :] + x * w_ref[1:2, :] + tap(1) * w_ref[2:3, :] + tap(2) * w_ref[3:4, :])


def _dn_prep_kernel(u_ref, w_ref, o_ref, *, ctx_len):
    j = pl.program_id(1)
    y = _seg_conv(u_ref[...].astype(F32), w_ref, ctx_len)
    y = _silu(y)
    n = lax.rsqrt(jnp.sum(y * y, axis=-1, keepdims=True) + EPS)
    fac = jnp.where(j < N_HEADS, n * (HEAD_W ** -0.5), jnp.where(j < 2 * N_HEADS, n, 1.0))
    o_ref[...] = (y * fac).astype(o_ref.dtype)


def _dn_prep(ub, conv_w, *, batch, seq, ctx_len):
    nblk = 3 * N_HEADS
    return pl.pallas_call(
        functools.partial(_dn_prep_kernel, ctx_len=ctx_len),
        out_shape=jax.ShapeDtypeStruct((batch * seq, 3 * MIX_W), BF16),
        grid=(batch, nblk),
        in_specs=[
            pl.BlockSpec((seq, HEAD_W), lambda b, j: (b, UB_DN_Q // HEAD_W + j)),
            pl.BlockSpec((4, HEAD_W), lambda b, j: (0, j)),
        ],
        out_specs=pl.BlockSpec((seq, HEAD_W), lambda b, j: (b, j)),
        compiler_params=_params(("parallel", "parallel")),
        name="dn_prep",
    )(ub, conv_w)


def _block_order(s, n_ctx, n_all, rev):
    if not rev:
        return s
    return jnp.where(s < n_ctx, n_ctx - 1 - s, n_all + n_ctx - 1 - s)


def _chunk_masks(rev):
    ii = lax.broadcasted_iota(jnp.int32, (CHUNK, CHUNK), 0)
    jj = lax.broadcasted_iota(jnp.int32, (CHUNK, CHUNK), 1)
    incl = (ii <= jj) if rev else (ii >= jj)
    incl_t = (ii >= jj) if rev else (ii <= jj)
    return ii, jj, incl, incl_t


def _unit_tri_inverse(a_all, eye, bd):
    ad = [jnp.where(bd, a, 0.0) for a in a_all]
    ao = [a - d for a, d in zip(a_all, ad)]
    p = [-d for d in ad]
    dinv = [eye + x for x in p]
    for _ in range(3):
        p = [_dot(x, x) for x in p]
        dinv = [d + _dot(d, x) for d, x in zip(dinv, p)]
    m = [-_dot(d, o) for d, o in zip(dinv, ao)]
    mm = [_dot(x, x) for x in m]
    t = [eye + x for x in m]
    t = [x + _dot(x, y) for x, y in zip(t, mm)]
    return [_dot(x, d) for x, d in zip(t, dinv)]


def _gdn_prepare(q_ref, k_ref, v_ref, ba_ref, alog, dtb, rev):
    ii, jj, incl, incl_t = _chunk_masks(rev)
    is_eye = ii == jj
    eye = jnp.where(is_eye, 1.0, 0.0)
    tri = jnp.where(incl, 1.0, 0.0)
    tri_t = jnp.where(incl_t, 1.0, 0.0)
    bd = (ii // SUB) == (jj // SUB)
    n_chunks = q_ref.shape[0] // CHUNK

    alog_neg = -jnp.exp(alog)
    chains = [(j, h) for j in range(n_chunks) for h in range(N_HEADS)]
    vec = {}
    for j in range(n_chunks):
        ba = ba_ref[j]
        beta_r = _sigmoid(ba[0:N_HEADS])
        g_r = alog_neg * _softplus(ba[N_HEADS:2 * N_HEADS] + dtb)
        for h in range(N_HEADS):
            gr = g_r[h:h + 1]
            beta_c = jnp.sum(eye * beta_r[h:h + 1], axis=1, keepdims=True)
            g_c = jnp.sum(eye * gr, axis=1, keepdims=True)
            gam_c = jnp.sum(tri * gr, axis=1, keepdims=True)
            gam_r = jnp.sum(tri_t * g_c, axis=0, keepdims=True)
            tot = jnp.sum(gr, axis=1, keepdims=True)
            dec_i = jnp.where(incl, jnp.exp(gam_c - gam_r), 0.0)
            vec[j, h] = (beta_c, gam_c, tot, dec_i)

    def tile(ref, j, h):
        return ref[j * CHUNK:(j + 1) * CHUNK, h * HEAD_W:(h + 1) * HEAD_W]

    k16 = [tile(k_ref, *c) for c in chains]
    q16 = [tile(q_ref, *c) for c in chains]
    ks = [k.astype(F32) for k in k16]
    kbs = [k * vec[c][0] for k, c in zip(ks, chains)]
    kq = [_dot_nt(jnp.concatenate([kb.astype(BF16), q], axis=0), k) for kb, q, k in zip(kbs, q16, k16)]
    a_low = [jnp.where(is_eye, 0.0, x[:CHUNK] * vec[c][3]) for x, c in zip(kq, chains)]
    a_qk = [x[CHUNK:] * vec[c][3] for x, c in zip(kq, chains)]
    t_inv = _unit_tri_inverse(a_low, eye, bd)
    egc = [jnp.exp(vec[c][1]) for c in chains]
    sols = [_dot(t, jnp.concatenate([tile(v_ref, *c).astype(F32) * vec[c][0], kb * e], axis=1))
            for t, kb, e, c in zip(t_inv, kbs, egc, chains)]
    pre = {}
    for c, sol, aq, q, k, e in zip(chains, sols, a_qk, q16, ks, egc):
        q_dec = q.astype(F32) * e
        k_dec_t = (k * jnp.exp(vec[c][2] - vec[c][1])).T
        pre[c] = (sol[:, :HEAD_W], jnp.concatenate([sol[:, HEAD_W:], q_dec], axis=0).astype(BF16),
                  jnp.concatenate([aq, k_dec_t], axis=0).astype(BF16), jnp.exp(vec[c][2]))
    return pre


def _gdn_kernel(qf_ref, kf_ref, vf_ref, baf_ref, qr_ref, kr_ref, vr_ref, bar_ref, alog_ref, dtb_ref,
                of_ref, or_ref, s_ref):
    @pl.when(pl.program_id(1) == 0)
    def _():
        s_ref[...] = jnp.zeros_like(s_ref)

    n_chunks = qf_ref.shape[0] // CHUNK
    pre = (_gdn_prepare(qf_ref, kf_ref, vf_ref, baf_ref, alog_ref[0], dtb_ref[0], False),
           _gdn_prepare(qr_ref, kr_ref, vr_ref, bar_ref, alog_ref[1], dtb_ref[1], True))
    o_refs = (of_ref, or_ref)

    lanes = [(d, h) for d in range(2) for h in range(N_HEADS)]
    states = {dh: s_ref[dh[0], dh[1]] for dh in lanes}
    for t in range(n_chunks):
        js = (t, n_chunks - 1 - t)
        cur = {dh: pre[dh[0]][js[dh[0]], dh[1]] for dh in lanes}
        wq = {dh: _dot(cur[dh][1], states[dh]) for dh in lanes}
        v_new = {dh: cur[dh][0] - wq[dh][:CHUNK] for dh in lanes}
        ak = {dh: _dot(cur[dh][2], v_new[dh]) for dh in lanes}
        for d, h in lanes:
            j = js[d]
            o_refs[d][j * CHUNK:(j + 1) * CHUNK, h * HEAD_W:(h + 1) * HEAD_W] = wq[d, h][CHUNK:] + ak[d, h][:CHUNK]
        states = {dh: states[dh] * cur[dh][3] + ak[dh][CHUNK:] for dh in lanes}
    for d, h in lanes:
        s_ref[d, h] = states[d, h]


def _gdn(qkv, ba_t, a_log, dt_bias, *, batch, seq, ctx_len):
    n_all = seq // SCAN_ROWS
    n_ctx = ctx_len // SCAN_ROWS
    cpb = SCAN_ROWS // CHUNK

    def rows(b, s, rev):
        return b * n_all + _block_order(s, n_ctx, n_all, rev)

    def in_specs(rev):
        d = int(rev)
        return [
            pl.BlockSpec((SCAN_ROWS, MIX_W), lambda b, s: (rows(b, s, rev), 0)),
            pl.BlockSpec((SCAN_ROWS, MIX_W), lambda b, s: (rows(b, s, rev), 1)),
            pl.BlockSpec((SCAN_ROWS, MIX_W), lambda b, s: (rows(b, s, rev), 2)),
            pl.BlockSpec((None, cpb, 2 * N_HEADS, CHUNK), lambda b, s: (d, rows(b, s, rev), 0, 0)),
        ]

    vec = pl.BlockSpec((2, N_HEADS, 1), lambda b, s: (0, 0, 0))
    out = jax.ShapeDtypeStruct((batch * seq, MIX_W), F32)
    return pl.pallas_call(
        _gdn_kernel,
        out_shape=(out, out),
        grid=(batch, n_all),
        in_specs=in_specs(False) + in_specs(True) + [vec, vec],
        out_specs=(pl.BlockSpec((SCAN_ROWS, MIX_W), lambda b, s: (rows(b, s, False), 0)),
                   pl.BlockSpec((SCAN_ROWS, MIX_W), lambda b, s: (rows(b, s, True), 0))),
        scratch_shapes=[pltpu.VMEM((2, N_HEADS, HEAD_W, HEAD_W), F32)],
        compiler_params=_params(("parallel", "arbitrary")),
        name="gdn",
    )(qkv, qkv, qkv, ba_t, qkv, qkv, qkv, ba_t, a_log, dt_bias)


def _hgrn2_prepare(q_ref, f_ref, i_ref, lb_ref, rev):
    _, _, incl, _ = _chunk_masks(rev)
    tri = jnp.where(incl, 1.0, 0.0)
    row = lax.broadcasted_iota(jnp.int32, (CHUNK, 1), 0)
    n_sub = CHUNK // SUB
    n_chunks = q_ref.shape[0] // CHUNK

    def tile(ref, j, h):
        return ref[j * CHUNK:(j + 1) * CHUNK, h * HEAD_W:(h + 1) * HEAD_W]

    chains = [(j, h) for j in range(n_chunks) for h in range(N_HEADS)]
    qs_, ks_, lfs = [], [], []
    for j, h in chains:
        qr = tile(q_ref, j, h).astype(F32)
        lb = lb_ref[:, h * HEAD_W:(h + 1) * HEAD_W]
        f = lb + (1.0 - lb) * _sigmoid(tile(f_ref, j, h))
        qs_.append(_silu(qr))
        ks_.append(1.0 - f)
        lfs.append(jnp.log(f))
    gcs = _dot_3x_many(tri, lfs)
    tots = [jnp.sum(lf, axis=0, keepdims=True) for lf in lfs]
    blocks = []
    for q, k, gc in zip(qs_, ks_, gcs):
        row_blocks = []
        for i in range(n_sub):
            mid = i * SUB + SUB // 2
            gref = gc[mid:mid + 1, :]
            qsc = q[i * SUB:(i + 1) * SUB, :] * jnp.exp(gc[i * SUB:(i + 1) * SUB, :] - gref)
            lo, hi = (i * SUB, CHUNK) if rev else (0, (i + 1) * SUB)
            parts = [jnp.zeros((lo, HEAD_W), F32), k[lo:hi, :] * jnp.exp(gref - gc[lo:hi, :]),
                     jnp.zeros((CHUNK - hi, HEAD_W), F32)]
            ksc = jnp.concatenate([x for x in parts if x.shape[0]], axis=0)
            row_blocks.append((qsc, ksc))
        blocks.append(row_blocks)
    scores = [jnp.where(incl, jnp.concatenate([_dot_nt(a, b) for a, b in rb], axis=0), 0.0) for rb in blocks]
    vs = [tile(i_ref, *c) for c in chains]
    local = [_dot(sc, v) for sc, v in zip(scores, vs)]
    incr = [_dot_tn(v, k * jnp.exp(tot - gc)) for v, k, gc, tot in zip(vs, ks_, gcs, tots)]
    return {c: (q * jnp.exp(gc), lo, inc, jnp.exp(tot))
            for c, q, gc, lo, inc, tot in zip(chains, qs_, gcs, local, incr, tots)}


def _hgrn2_kernel(qf_ref, ff_ref, if_ref, qr_ref, fr_ref, ir_ref, lb_ref, of_ref, or_ref, s_ref):
    @pl.when(pl.program_id(1) == 0)
    def _():
        s_ref[...] = jnp.zeros_like(s_ref)

    n_chunks = qf_ref.shape[0] // CHUNK
    pre = (_hgrn2_prepare(qf_ref, ff_ref, if_ref, lb_ref.at[0], False),
           _hgrn2_prepare(qr_ref, fr_ref, ir_ref, lb_ref.at[1], True))
    o_refs = (of_ref, or_ref)

    lanes = [(d, h) for d in range(2) for h in range(N_HEADS)]
    states = {dh: s_ref[dh[0], dh[1]] for dh in lanes}
    for t in range(n_chunks):
        js = (t, n_chunks - 1 - t)
        cur = {dh: pre[dh[0]][js[dh[0]], dh[1]] for dh in lanes}
        qs = {dh: _dot_nt(cur[dh][0], states[dh]) for dh in lanes}
        for d, h in lanes:
            j = js[d]
            o_refs[d][j * CHUNK:(j + 1) * CHUNK, h * HEAD_W:(h + 1) * HEAD_W] = qs[d, h] + cur[d, h][1]
        states = {dh: states[dh] * cur[dh][3] + cur[dh][2] for dh in lanes}
    for d, h in lanes:
        s_ref[d, h] = states[d, h]


def _hgrn2(ub, uf, lb, *, batch, seq, ctx_len):
    n_all = seq // SCAN_ROWS
    n_ctx = ctx_len // SCAN_ROWS

    def rows(b, s, rev):
        return b * n_all + _block_order(s, n_ctx, n_all, rev)

    def in_specs(rev):
        f_col = (UF_HG_FB if rev else UF_HG_FF) // MIX_W
        return [
            pl.BlockSpec((SCAN_ROWS, MIX_W), lambda b, s: (rows(b, s, rev), UB_HG_Q // MIX_W)),
            pl.BlockSpec((SCAN_ROWS, MIX_W), lambda b, s: (rows(b, s, rev), f_col)),
            pl.BlockSpec((SCAN_ROWS, MIX_W), lambda b, s: (rows(b, s, rev), UB_HG_I // MIX_W)),
        ]

    out = jax.ShapeDtypeStruct((batch * seq, MIX_W), F32)
    return pl.pallas_call(
        _hgrn2_kernel,
        out_shape=(out, out),
        grid=(batch, n_all),
        in_specs=in_specs(False) + in_specs(True) + [pl.BlockSpec((2, 1, MIX_W), lambda b, s: (0, 0, 0))],
        out_specs=(pl.BlockSpec((SCAN_ROWS, MIX_W), lambda b, s: (rows(b, s, False), 0)),
                   pl.BlockSpec((SCAN_ROWS, MIX_W), lambda b, s: (rows(b, s, True), 0))),
        scratch_shapes=[pltpu.VMEM((2, N_HEADS, HEAD_W, HEAD_W), F32)],
        compiler_params=_params(("parallel", "arbitrary")),
        name="hgrn2",
    )(ub, uf, ub, ub, uf, ub, lb)


LRU_GROUP = 8


def _group_scan(a, b, rev):
    n, w = a.shape
    a = a.reshape(n // LRU_GROUP, LRU_GROUP, w)
    b = b.reshape(n // LRU_GROUP, LRU_GROUP, w)
    sub = lax.broadcasted_iota(jnp.int32, (1, LRU_GROUP, 1), 1)
    s = 1
    while s < LRU_GROUP:
        shift, valid = (LRU_GROUP - s, sub < LRU_GROUP - s) if rev else (s, sub >= s)
        a_s, b_s = pltpu.roll(a, shift, 1), pltpu.roll(b, shift, 1)
        b = jnp.where(valid, a * b_s + b, b)
        a = jnp.where(valid, a * a_s, a)
        s *= 2
    return a, b


def _chain_groups(a, b, carry, rev, store):
    n = a.shape[0]
    for g in (range(n - 1, -1, -1) if rev else range(n)):
        hg = a[g] * carry + b[g]
        store(g, hg)
        carry = hg[0:1, :] if rev else hg[LRU_GROUP - 1:LRU_GROUP, :]
    return carry


def _gelu_tanh(x):
    return 0.5 * x * (1.0 + jnp.tanh(0.7978845608028654 * (x + 0.044715 * (x * x * x))))


def _lru_kernel(xb_ref, gb_ref, cw_ref, cb_ref, wa_ref, ba_ref, wx_ref, bx_ref, lam_ref, o_ref, xc_ref, hf_ref,
                hb_ref, *, ctx_len):
    n = xb_ref.shape[0]
    nblk = n // LRU_BLOCK
    nctx = ctx_len // LRU_BLOCK
    xc_ref[...] = _seg_conv(xb_ref[...].astype(F32), cw_ref, ctx_len) + cb_ref[...]

    rate = [(-LRU_C * LOG2_E) * _softplus(-lam_ref[d]) for d in range(2)]

    def gates(xc, d):
        r = _sigmoid(_dot(xc, wa_ref[d]) + ba_ref[d])
        ig = _sigmoid(_dot(xc, wx_ref[d]) + bx_ref[d])
        a = jnp.exp2(r * rate[d])
        return a, jnp.sqrt(1.0 - a * a) * ig * xc

    def scan_block(blk, carry, d, out_ref):
        base = pl.multiple_of(blk * LRU_BLOCK, LRU_BLOCK)
        a, b = _group_scan(*gates(xc_ref[pl.ds(base, LRU_BLOCK), :], d), rev=bool(d))

        def store(g, hg):
            out_ref[pl.ds(pl.multiple_of(base + g * LRU_GROUP, LRU_GROUP), LRU_GROUP), :] = hg

        return _chain_groups(a, b, carry, bool(d), store)

    def step(i, carry):
        cf = scan_block(i, carry[0], 0, hf_ref)
        cb = scan_block(_block_order(i, nctx, nblk, True), carry[1], 1, hb_ref)
        return cf, cb

    zero = jnp.zeros((1, HEAD_W), F32)
    lax.fori_loop(0, nblk, step, (zero, zero))
    o_ref[...] = ((hf_ref[...] + hb_ref[...]) * _gelu_tanh(gb_ref[...].astype(F32))).astype(o_ref.dtype)


def _lru(ub, conv_w, conv_b, w_a, b_a, w_x, b_x, lam, *, batch, seq, ctx_len):
    vec = pl.BlockSpec((2, 1, HEAD_W), lambda b, h: (0, 0, h))
    mat = pl.BlockSpec((2, None, HEAD_W, HEAD_W), lambda b, h: (0, h, 0, 0))
    return pl.pallas_call(
        functools.partial(_lru_kernel, ctx_len=ctx_len),
        out_shape=jax.ShapeDtypeStruct((batch * seq, MIX_W), BF16),
        grid=(batch, N_HEADS),
        in_specs=[
            pl.BlockSpec((seq, HEAD_W), lambda b, h: (b, UB_LRU_X // HEAD_W + h)),
            pl.BlockSpec((seq, HEAD_W), lambda b, h: (b, UB_LRU_G // HEAD_W + h)),
            pl.BlockSpec((4, HEAD_W), lambda b, h: (0, h)),
            pl.BlockSpec((1, HEAD_W), lambda b, h: (0, h)),
            mat, vec, mat, vec, vec,
        ],
        out_specs=pl.BlockSpec((seq, HEAD_W), lambda b, h: (b, h)),
        scratch_shapes=[pltpu.VMEM((seq, HEAD_W), F32)] * 3,
        compiler_params=_params(("parallel", "parallel")),
        name="rglru",
    )(ub, ub, conv_w, conv_b, w_a, b_a, w_x, b_x, lam)


def _norm_rope(x, w, cos, sin, grp, scale):
    lane = lax.broadcasted_iota(jnp.int32, (1, 128), 1)
    first = (lane % (ATT_HEAD_DIM // 2)) < (ATT_HEAD_DIM // 4)
    ss = _dot_x3(x * x, grp)
    y = x * lax.rsqrt(ss * (1.0 / ATT_HEAD_DIM) + EPS) * w
    rot = jnp.where(first, pltpu.roll(y, 128 - ATT_HEAD_DIM // 4, 1), pltpu.roll(y, ATT_HEAD_DIM // 4, 1))
    return (y * cos + rot * sin) * scale


def _att_kernel(q_ref, k_ref, v_ref, cos_ref, sin_ref, qw_ref, kw_ref, grp_ref, o_ref, kt_ref, *, ctx_len):
    i = pl.program_id(1)
    grp = grp_ref[...]
    tq = q_ref.shape[0]

    @pl.when(i == 0)
    def _():
        for c in range(k_ref.shape[0] // tq):
            rows = slice(c * tq, (c + 1) * tq)
            kn = _norm_rope(k_ref[rows, :].astype(F32), kw_ref[...], cos_ref[rows, :], sin_ref[rows, :], grp, 1.0)
            kt_ref[:, rows] = kn.T.astype(kt_ref.dtype)

    q_scale = (ATT_HEAD_DIM ** -0.5) * LOG2_E
    q_rows = pl.ds(pl.multiple_of(i * tq, tq), tq)
    cos, sin = cos_ref[q_rows, :], sin_ref[q_rows, :]
    qn = [_norm_rope(q_ref[:, s * 128:(s + 1) * 128].astype(F32), qw_ref[...], cos, sin, grp, q_scale).astype(BF16)
          for s in range(q_ref.shape[1] // 128)]

    def attend(n_keys):
        v = v_ref[:n_keys, :]

        def scores(head):
            g = head // ATT_GROUP
            kt = kt_ref[g * ATT_HEAD_DIM:(g + 1) * ATT_HEAD_DIM, :n_keys]
            q = qn[head // 2][:, (head % 2) * ATT_HEAD_DIM:(head % 2 + 1) * ATT_HEAD_DIM]
            return jnp.dot(q, kt, preferred_element_type=F32)

        s_next = scores(0)
        for head in range(ATT_Q_HEADS):
            s = s_next
            if head + 1 < ATT_Q_HEADS:
                s_next = scores(head + 1)
            g = head // ATT_GROUP
            p = jnp.exp2(s - jnp.max(s, axis=-1, keepdims=True))
            den = jnp.sum(p, axis=-1, keepdims=True)
            pv = jnp.dot(p.astype(BF16), v, preferred_element_type=F32)
            o_ref[:, head * ATT_HEAD_DIM:(head + 1) * ATT_HEAD_DIM] = (
                pv[:, g * ATT_HEAD_DIM:(g + 1) * ATT_HEAD_DIM] / den).astype(o_ref.dtype)

    @pl.when(i == 0)
    def _():
        attend(ctx_len)

    @pl.when(i != 0)
    def _():
        attend(kt_ref.shape[1])


def _attention(ub, cos, sin, qw, kw, grp, *, batch, seq, ctx_len):
    tiles = seq // ATT_TQ
    whole = lambda shape: pl.BlockSpec(shape, lambda b, i: (0, 0))
    return pl.pallas_call(
        functools.partial(_att_kernel, ctx_len=ctx_len),
        out_shape=jax.ShapeDtypeStruct((batch * seq, ATT_QW), BF16),
        grid=(batch, tiles),
        in_specs=[
            pl.BlockSpec((ATT_TQ, ATT_QW), lambda b, i: (b * tiles + i, UB_ATT_Q // ATT_QW)),
            pl.BlockSpec((seq, ATT_KW), lambda b, i: (b, UB_ATT_K // ATT_KW)),
            pl.BlockSpec((seq, ATT_KW), lambda b, i: (b, UB_ATT_V // ATT_KW)),
            whole((seq, 128)), whole((seq, 128)), whole((1, 128)), whole((1, 128)), whole((128, 128)),
        ],
        out_specs=pl.BlockSpec((ATT_TQ, ATT_QW), lambda b, i: (b * tiles + i, 0)),
        scratch_shapes=[pltpu.VMEM((ATT_KW, seq), BF16)],
        compiler_params=_params(("parallel", "arbitrary")),
        name="attention",
    )(ub, ub, ub, cos, sin, qw, kw, grp)


def _merge_kernel(oaf_ref, oab_ref, obf_ref, obb_ref, ga_ref, gb_ref, yc_ref, yd_ref, gate_ref, h_ref, modc_ref,
                  modl_ref, dnw_ref, hgw_ref, wb_ref, wo_ref, out_ref, *, tiles_per_batch):
    def gated(of_ref, ob_ref, g_ref, nw_ref):
        o = of_ref[...] + ob_ref[...]
        ys = [_rms_rows(o[:, h * HEAD_W:(h + 1) * HEAD_W], nw_ref[...]) for h in range(N_HEADS)]
        g = g_ref[...].astype(F32)
        return jnp.concatenate(ys, axis=1) * _silu(g)

    ys = (gated(oaf_ref, oab_ref, ga_ref, dnw_ref), gated(obf_ref, obb_ref, gb_ref, hgw_ref), yc_ref[...], yd_ref[...])
    acc = None
    for b in range(N_BRANCH):
        gate = gate_ref[:, b * D_MODEL:(b + 1) * D_MODEL].astype(F32)
        term = _sigmoid(gate) * _dot(ys[b], wb_ref[b])
        acc = term if acc is None else acc + term
    is_ctx = (pl.program_id(0) % tiles_per_batch) == 0
    out_ref[...] = h_ref[...] + _pick_mod(modc_ref, modl_ref, 2, is_ctx) * _dot(acc, wo_ref[...])


def _merge(oaf, oab, obf, obb, ub, yc, yd, h, modc, modl, dnw, hgw, wb, wo, *, seq, ctx_len):
    rows = h.shape[0]
    tm = ctx_len
    tiles_per_batch = seq // tm
    mix = lambda c: pl.BlockSpec((tm, MIX_W), lambda i: (i, c // MIX_W))
    return pl.pallas_call(
        functools.partial(_merge_kernel, tiles_per_batch=tiles_per_batch),
        out_shape=jax.ShapeDtypeStruct((rows, D_MODEL), F32),
        grid=(rows // tm,),
        in_specs=[
            mix(0), mix(0), mix(0), mix(0), mix(UB_DN_G), mix(UB_HG_G), mix(0), mix(0),
            pl.BlockSpec((tm, N_BRANCH * D_MODEL), lambda i: (i, 0)),
            pl.BlockSpec((tm, D_MODEL), lambda i: (i, 0)),
            pl.BlockSpec((6, D_MODEL), lambda i: (0, 0)),
            pl.BlockSpec((None, 6, D_MODEL), lambda i: (i // tiles_per_batch, 0, 0)),
            pl.BlockSpec((1, HEAD_W), lambda i: (0, 0)),
            pl.BlockSpec((1, HEAD_W), lambda i: (0, 0)),
            pl.BlockSpec((N_BRANCH, MIX_W, D_MODEL), lambda i: (0, 0, 0)),
            pl.BlockSpec((D_MODEL, D_MODEL), lambda i: (0, 0)),
        ],
        out_specs=pl.BlockSpec((tm, D_MODEL), lambda i: (i, 0)),
        compiler_params=_params(("parallel",)),
        name="merge",
    )(oaf, oab, obf, obb, ub, ub, yc, yd, ub, h, modc, modl, dnw, hgw, wb, wo)


def _mlp_kernel(h_ref, modc_ref, modl_ref, nw_ref, w1_ref, w2_ref, o_ref, z_ref, acc_ref,
                *, tm, tiles_per_batch, ctx_len):
    j = pl.program_id(1)
    row = (pl.program_id(0) % tiles_per_batch) * tm + lax.broadcasted_iota(jnp.int32, (tm, 1), 0)
    is_ctx = row < ctx_len

    @pl.when(j == 0)
    def _():
        y = _rms_rows(h_ref[...], nw_ref[...])
        shift = _pick_mod(modc_ref, modl_ref, 3, is_ctx)
        scale = _pick_mod(modc_ref, modl_ref, 4, is_ctx)
        z_ref[...] = (y * (1.0 + scale) + shift).astype(BF16)
        acc_ref[...] = jnp.zeros_like(acc_ref)

    a = jnp.maximum(jnp.dot(z_ref[...], w1_ref[...], preferred_element_type=F32), 0.0)
    acc_ref[...] += _dot(a * a, w2_ref[...])

    @pl.when(j == pl.num_programs(1) - 1)
    def _():
        o_ref[...] = h_ref[...] + _pick_mod(modc_ref, modl_ref, 5, is_ctx) * acc_ref[...]


def _mlp(h, modc, modl, nw, w1, w2, *, seq, ctx_len):
    rows = h.shape[0]
    tiles_per_batch = 4
    tm = seq // tiles_per_batch
    kern = functools.partial(_mlp_kernel, tm=tm, tiles_per_batch=tiles_per_batch, ctx_len=ctx_len)
    return pl.pallas_call(
        kern,
        out_shape=jax.ShapeDtypeStruct((rows, D_MODEL), F32),
        grid=(rows // tm, D_FF // FF_BLOCK),
        in_specs=[
            pl.BlockSpec((tm, D_MODEL), lambda i, j: (i, 0)),
            pl.BlockSpec((6, D_MODEL), lambda i, j: (0, 0)),
            pl.BlockSpec((None, 6, D_MODEL), lambda i, j: (i // tiles_per_batch, 0, 0)),
            pl.BlockSpec((1, D_MODEL), lambda i, j: (0, 0)),
            pl.BlockSpec((D_MODEL, FF_BLOCK), lambda i, j: (0, j)),
            pl.BlockSpec((FF_BLOCK, D_MODEL), lambda i, j: (j, 0)),
        ],
        out_specs=pl.BlockSpec((tm, D_MODEL), lambda i, j: (i, 0)),
        scratch_shapes=[pltpu.VMEM((tm, D_MODEL), BF16), pltpu.VMEM((tm, D_MODEL), F32)],
        compiler_params=_params(("parallel", "arbitrary"), vmem=VMEM_LIMIT_V7X + 8 * 1024 * 1024),
        name="mlp",
    )(h, modc, modl, nw, w1, w2)


def _regroup_w_in(w_in):
    o = np.cumsum([0, 512, 512, 512, 512, 8, 8, 512, 512, 512, 512, 512, 512, 512, 512, 128, 128, 4096])
    cols = lambda a, b: w_in[..., o[a]:o[b]]
    wb = jnp.concatenate([cols(16, 17), cols(6, 7), cols(9, 11), cols(0, 3), cols(11, 13), cols(3, 4), cols(13, 16)],
                         axis=-1)
    wf = jnp.concatenate([cols(7, 9), cols(4, 6)], axis=-1)
    pad = lambda w, n: jnp.pad(w, ((0, 0), (0, 0), (0, n - w.shape[-1]))).astype(BF16)
    return pad(wb, N_UB), pad(wf, N_UF)


def _rope_tables(t_len, ctx_len):
    rows = t_len // GRID_W
    row_id = jnp.repeat(jnp.arange(rows), GRID_W).astype(F32)
    col_id = jnp.tile(jnp.arange(GRID_W), rows).astype(F32)
    axis_dim = ATT_HEAD_DIM // 2
    inv = ROPE_THETA ** (-jnp.arange(0, axis_dim, 2, dtype=F32) / axis_dim)
    ang = jnp.stack([row_id[:, None] * inv, col_id[:, None] * inv], axis=1)
    cos, sin = jnp.cos(ang), jnp.sin(ang)
    cos_h = jnp.concatenate([cos, cos], axis=-1).reshape(t_len, ATT_HEAD_DIM)
    sin_h = jnp.concatenate([-sin, sin], axis=-1).reshape(t_len, ATT_HEAD_DIM)
    cos_t = jnp.concatenate([jnp.ones((ctx_len, ATT_HEAD_DIM), F32), cos_h], axis=0)
    sin_t = jnp.concatenate([jnp.zeros((ctx_len, ATT_HEAD_DIM), F32), sin_h], axis=0)
    return jnp.tile(cos_t, (1, 2)), jnp.tile(sin_t, (1, 2))


def _hgrn2_lower_bounds(p):
    sm = jax.nn.softmax(p.astype(F32), axis=1)
    cs = jnp.cumsum(sm, axis=1)
    return cs - cs[:, :1]


def kernel(x, c, ctx, c_ctx, mod_w, mod_b, norm1_w, norm2_w, w_in, dn_conv_w, dn_a_log, dn_dt_bias, dn_norm_w,
           hg_lower_bounds, hg_norm_w, lru_conv_w, lru_conv_b, lru_w_a, lru_b_a, lru_w_x, lru_b_x, lru_lambda,
           att_q_norm_w, att_k_norm_w, w_branch, w_out, mlp_w1, mlp_w2):
    batch, t_len, _ = x.shape
    ctx_len = ctx.shape[1]
    depth = mod_w.shape[0]
    seq = ctx_len + t_len
    n_chunks = seq // CHUNK
    assert ctx_len % SCAN_ROWS == 0 and t_len % SCAN_ROWS == 0 and ctx_len == ATT_TQ
    dims = dict(batch=batch, seq=seq, ctx_len=ctx_len)

    w_ub, w_uf = _regroup_w_in(w_in)
    w_branch_b, w_out_b = w_branch.astype(BF16), w_out.astype(BF16)
    w1_b, w2_b = mlp_w1.astype(BF16), mlp_w2.astype(BF16)
    lru_wa_b, lru_wx_b = lru_w_a.astype(BF16), lru_w_x.astype(BF16)
    lb_all = _hgrn2_lower_bounds(hg_lower_bounds)
    cos_t, sin_t = _rope_tables(t_len, ctx_len)
    grp = jnp.asarray((np.arange(128)[:, None] // ATT_HEAD_DIM) == (np.arange(128)[None, :] // ATT_HEAD_DIM), F32)

    n_mod_rows = batch + 8
    cc = jnp.zeros((n_mod_rows, D_MODEL), F32).at[:batch].set(c).at[batch].set(c_ctx)
    mods = _modulations(cc, mod_w.astype(BF16), mod_b)

    h = jnp.concatenate([ctx, x], axis=1).reshape(batch * seq, D_MODEL)
    for l in range(depth):
        modl = mods[l, :batch].reshape(batch, 6, D_MODEL)
        modc = mods[l, batch].reshape(6, D_MODEL)
        nw1 = norm1_w[l].reshape(1, D_MODEL)
        ub, uf = _inproj(h, modc, modl, nw1, w_ub[l], w_uf[l], seq=seq, ctx_len=ctx_len)

        qkv = _dn_prep(ub, dn_conv_w[l], **dims)
        ba = uf[:, UF_DN_BA:UF_DN_BA + 4 * N_HEADS].reshape(batch * n_chunks, CHUNK, 2, 2, N_HEADS)
        ba_t = jnp.transpose(ba, (3, 0, 2, 4, 1)).reshape(2, batch * n_chunks, 2 * N_HEADS, CHUNK)
        oa = _gdn(qkv, ba_t, dn_a_log[l].reshape(2, N_HEADS, 1), dn_dt_bias[l].reshape(2, N_HEADS, 1), **dims)

        ob = _hgrn2(ub, uf, lb_all[:, l].reshape(2, 1, MIX_W), **dims)

        yc = _lru(ub, lru_conv_w[l], lru_conv_b[l].reshape(1, MIX_W), lru_wa_b[l], lru_b_a[l].reshape(2, 1, MIX_W),
                  lru_wx_b[l], lru_b_x[l].reshape(2, 1, MIX_W), lru_lambda[l].reshape(2, 1, MIX_W), **dims)

        yd = _attention(ub, cos_t, sin_t, jnp.tile(att_q_norm_w[l], 2).reshape(1, 128),
                        jnp.tile(att_k_norm_w[l], 2).reshape(1, 128), grp, **dims)

        h = _merge(oa[0], oa[1], ob[0], ob[1], ub, yc, yd, h, modc, modl, dn_norm_w[l].reshape(1, HEAD_W),
                   hg_norm_w[l].reshape(1, HEAD_W), w_branch_b[l], w_out_b[l], seq=seq, ctx_len=ctx_len)
        h = _mlp(h, modc, modl, norm2_w[l].reshape(1, D_MODEL), w1_b[l], w2_b[l], seq=seq, ctx_len=ctx_len)

    return h.reshape(batch, seq, D_MODEL)[:, ctx_len:]
```

```python
import functools

import jax
import jax.numpy as jnp
import numpy as np
from jax import lax
from jax.experimental import pallas as pl
from jax.experimental.pallas import tpu as pltpu

F32 = jnp.float32
BF16 = jnp.bfloat16

EPS = 1e-6
D_MODEL = 1024
GRID_W = 64
N_HEADS = 4
HEAD_W = 128
MIX_W = N_HEADS * HEAD_W
CHUNK = 64
SUB = 16
SCAN_ROWS = 256
LRU_C = 8.0
LRU_BLOCK = 256
ATT_Q_HEADS = 8
ATT_KV_HEADS = 2
ATT_HEAD_DIM = 64
ATT_GROUP = ATT_Q_HEADS // ATT_KV_HEADS
ATT_QW = ATT_Q_HEADS * ATT_HEAD_DIM
ATT_KW = ATT_KV_HEADS * ATT_HEAD_DIM
ATT_TQ = 256
ROPE_THETA = 10000.0
LOG2_E = 1.4426950408889634
N_BRANCH = 4
D_FF = 4 * D_MODEL
FF_BLOCK = 2048

UB_GATE = 0
UB_HG_Q, UB_HG_I, UB_HG_G = 4096, 4608, 5120
UB_DN_Q = 5632
UB_LRU_X, UB_LRU_G = 7168, 7680
UB_DN_G = 8192
UB_ATT_Q = 8704
TN_UB = 2304
N_UB = 4 * TN_UB
UF_HG_FF, UF_HG_FB, UF_ATT_K, UF_ATT_V, UF_DN_BA = 0, 512, 1024, 1152, 1280
N_UF = 1536

VMEM_LIMIT_V7X = 48 * 1024 * 1024


def _params(sem, vmem=VMEM_LIMIT_V7X):
    return pltpu.CompilerParams(dimension_semantics=sem, vmem_limit_bytes=vmem)


def _sigmoid(x):
    return 0.5 * jnp.tanh(0.5 * x) + 0.5


def _silu(x):
    h = 0.5 * x
    return h * (1.0 + jnp.tanh(h))


def _softplus(x):
    return jnp.maximum(x, 0.0) + jnp.log1p(jnp.exp(-jnp.abs(x)))


def _dot(a, b):
    return jnp.dot(a.astype(BF16), b.astype(BF16), preferred_element_type=F32)


def _dot_nt(a, b):
    return lax.dot_general(a.astype(BF16), b.astype(BF16), (((1,), (1,)), ((), ())),
                           preferred_element_type=F32)


def _dot_tn(a, b):
    return lax.dot_general(a.astype(BF16), b.astype(BF16), (((0,), (0,)), ((), ())),
                           preferred_element_type=F32)


def _split3(x):
    hi = x.astype(BF16)
    r1 = x - hi.astype(F32)
    mid = r1.astype(BF16)
    lo = (r1 - mid.astype(F32)).astype(BF16)
    return hi, mid, lo


def _dot_x3(x, m):
    m = m.astype(BF16)
    hi, mid, lo = _split3(x)
    out = jnp.dot(hi, m, preferred_element_type=F32)
    out = out + jnp.dot(mid, m, preferred_element_type=F32)
    return out + jnp.dot(lo, m, preferred_element_type=F32)


def _dot_3x_many(m, xs):
    w = xs[0].shape[1]
    pieces = [_split3(x) for x in xs]
    cols = [p[i] for i in range(3) for p in pieces]
    wide = jnp.dot(m.astype(BF16), jnp.concatenate(cols, axis=1), preferred_element_type=F32)
    n = len(xs)
    part = lambda p, i: wide[:, (p * n + i) * w:(p * n + i + 1) * w]
    return [part(0, i) + part(1, i) + part(2, i) for i in range(n)]


def _rms_rows(x, w):
    return x * lax.rsqrt(jnp.mean(x * x, axis=-1, keepdims=True) + EPS) * w


def _mod_kernel(c_ref, w_ref, b_ref, o_ref):
    c = c_ref[...]
    o_ref[...] = _dot(_silu(c), w_ref[...]) + b_ref[...]


def _modulations(cc, mod_w, mod_b):
    depth = mod_w.shape[0]
    rows = cc.shape[0]
    n_out = mod_w.shape[2]
    return pl.pallas_call(
        _mod_kernel,
        out_shape=jax.ShapeDtypeStruct((depth, rows, n_out), F32),
        grid=(depth, n_out // D_MODEL),
        in_specs=[
            pl.BlockSpec((rows, D_MODEL), lambda l, j: (0, 0)),
            pl.BlockSpec((None, D_MODEL, D_MODEL), lambda l, j: (l, 0, j)),
            pl.BlockSpec((None, 1, D_MODEL), lambda l, j: (l, 0, j)),
        ],
        out_specs=pl.BlockSpec((None, rows, D_MODEL), lambda l, j: (l, 0, j)),
        compiler_params=_params(("parallel", "parallel")),
        name="modulations",
    )(cc, mod_w, mod_b.reshape(depth, 1, n_out))


def _pick_mod(modc_ref, modl_ref, idx, is_ctx):
    return jnp.where(is_ctx, modc_ref[idx:idx + 1, :], modl_ref[idx:idx + 1, :])


def _inproj_kernel(h_ref, modc_ref, modl_ref, nw_ref, wb_ref, wf_ref, ob_ref, of_ref, xn_ref,
                   *, tm, tiles_per_batch, ctx_len):
    j = pl.program_id(1)

    @pl.when(j == 0)
    def _():
        y = _rms_rows(h_ref[...], nw_ref[...])
        row = (pl.program_id(0) % tiles_per_batch) * tm + lax.broadcasted_iota(jnp.int32, (tm, 1), 0)
        is_ctx = row < ctx_len
        shift = _pick_mod(modc_ref, modl_ref, 0, is_ctx)
        scale = _pick_mod(modc_ref, modl_ref, 1, is_ctx)
        xn_ref[...] = (y * (1.0 + scale) + shift).astype(BF16)

    ob_ref[...] = jnp.dot(xn_ref[...], wb_ref[...], preferred_element_type=F32).astype(ob_ref.dtype)

    @pl.when(j == pl.num_programs(1) - 1)
    def _():
        of_ref[...] = jnp.dot(xn_ref[...], wf_ref[...], preferred_element_type=F32)


def _inproj(h, modc, modl, nw, w_b, w_f, *, seq, ctx_len):
    rows = h.shape[0]
    tiles_per_batch = 4
    tm = seq // tiles_per_batch
    kern = functools.partial(_inproj_kernel, tm=tm, tiles_per_batch=tiles_per_batch, ctx_len=ctx_len)
    return pl.pallas_call(
        kern,
        out_shape=(jax.ShapeDtypeStruct((rows, N_UB), BF16), jax.ShapeDtypeStruct((rows, N_UF), F32)),
        grid=(rows // tm, N_UB // TN_UB),
        in_specs=[
            pl.BlockSpec((tm, D_MODEL), lambda i, j: (i, 0)),
            pl.BlockSpec((6, D_MODEL), lambda i, j: (0, 0)),
            pl.BlockSpec((None, 6, D_MODEL), lambda i, j: (i // tiles_per_batch, 0, 0)),
            pl.BlockSpec((1, D_MODEL), lambda i, j: (0, 0)),
            pl.BlockSpec((D_MODEL, TN_UB), lambda i, j: (0, j)),
            pl.BlockSpec((D_MODEL, N_UF), lambda i, j: (0, 0)),
        ],
        out_specs=(pl.BlockSpec((tm, TN_UB), lambda i, j: (i, j)),
                   pl.BlockSpec((tm, N_UF), lambda i, j: (i, 0))),
        scratch_shapes=[pltpu.VMEM((tm, D_MODEL), BF16)],
        compiler_params=_params(("parallel", "arbitrary"), vmem=VMEM_LIMIT_V7X + 8 * 1024 * 1024),
        name="inproj",
    )(h, modc, modl, nw, w_b, w_f)


def _seg_conv(x, w_ref, ctx_len):
    n = x.shape[0]
    row = lax.broadcasted_iota(jnp.int32, (n, 1), 0)
    lo = jnp.where(row >= ctx_len, ctx_len, 0)
    hi = jnp.where(row >= ctx_len, n, ctx_len)

    def tap(k):
        tk = row + k
        valid = jnp.logical_and(tk >= lo, tk < hi)
        return jnp.where(valid, pltpu.roll(x, (-k) % n, 0), 0.0)

    return (tap(-1) * w_ref[0:1, :] + x * w_ref[1:2, :] + tap(1) * w_ref[2:3, :] + tap(2) * w_ref[3:4, :])


def _dn_prep_kernel(u_ref, w_ref, o_ref, *, ctx_len):
    j = pl.program_id(1)
    y = _seg_conv(u_ref[...].astype(F32), w_ref, ctx_len)
    y = _silu(y)
    n = lax.rsqrt(jnp.sum(y * y, axis=-1, keepdims=True) + EPS)
    fac = jnp.where(j < N_HEADS, n * (HEAD_W ** -0.5), jnp.where(j < 2 * N_HEADS, n, 1.0))
    o_ref[...] = (y * fac).astype(o_ref.dtype)


def _dn_prep(ub, conv_w, *, batch, seq, ctx_len):
    nblk = 3 * N_HEADS
    return pl.pallas_call(
        functools.partial(_dn_prep_kernel, ctx_len=ctx_len),
        out_shape=jax.ShapeDtypeStruct((batch * seq, 3 * MIX_W), BF16),
        grid=(batch, nblk),
        in_specs=[
            pl.BlockSpec((seq, HEAD_W), lambda b, j: (b, UB_DN_Q // HEAD_W + j)),
            pl.BlockSpec((4, HEAD_W), lambda b, j: (0, j)),
        ],
        out_specs=pl.BlockSpec((seq, HEAD_W), lambda b, j: (b, j)),
        compiler_params=_params(("parallel", "parallel")),
        name="dn_prep",
    )(ub, conv_w)


def _block_order(s, n_ctx, n_all, rev):
    if not rev:
        return s
    return jnp.where(s < n_ctx, n_ctx - 1 - s, n_all + n_ctx - 1 - s)


def _chunk_masks(rev):
    ii = lax.broadcasted_iota(jnp.int32, (CHUNK, CHUNK), 0)
    jj = lax.broadcasted_iota(jnp.int32, (CHUNK, CHUNK), 1)
    incl = (ii <= jj) if rev else (ii >= jj)
    incl_t = (ii >= jj) if rev else (ii <= jj)
    return ii, jj, incl, incl_t


def _unit_tri_inverse(a_all, eye, bd):
    ad = [jnp.where(bd, a, 0.0) for a in a_all]
    ao = [a - d for a, d in zip(a_all, ad)]
    p = [-d for d in ad]
    dinv = [eye + x for x in p]
    for _ in range(3):
        p = [_dot(x, x) for x in p]
        dinv = [d + _dot(d, x) for d, x in zip(dinv, p)]
    m = [-_dot(d, o) for d, o in zip(dinv, ao)]
    mm = [_dot(x, x) for x in m]
    t = [eye + x for x in m]
    t = [x + _dot(x, y) for x, y in zip(t, mm)]
    return [_dot(x, d) for x, d in zip(t, dinv)]


def _gdn_prepare(q_ref, k_ref, v_ref, ba_ref, alog, dtb, rev):
    ii, jj, incl, incl_t = _chunk_masks(rev)
    is_eye = ii == jj
    eye = jnp.where(is_eye, 1.0, 0.0)
    tri = jnp.where(incl, 1.0, 0.0)
    tri_t = jnp.where(incl_t, 1.0, 0.0)
    bd = (ii // SUB) == (jj // SUB)
    n_chunks = q_ref.shape[0] // CHUNK

    alog_neg = -jnp.exp(alog)
    chains = [(j, h) for j in range(n_chunks) for h in range(N_HEADS)]
    vec = {}
    for j in range(n_chunks):
        ba = ba_ref[j]
        beta_r = _sigmoid(ba[0:N_HEADS])
        g_r = alog_neg * _softplus(ba[N_HEADS:2 * N_HEADS] + dtb)
        for h in range(N_HEADS):
            gr = g_r[h:h + 1]
            beta_c = jnp.sum(eye * beta_r[h:h + 1], axis=1, keepdims=True)
            g_c = jnp.sum(eye * gr, axis=1, keepdims=True)
            gam_c = jnp.sum(tri * gr, axis=1, keepdims=True)
            gam_r = jnp.sum(tri_t * g_c, axis=0, keepdims=True)
            tot = jnp.sum(gr, axis=1, keepdims=True)
            dec_i = jnp.where(incl, jnp.exp(gam_c - gam_r), 0.0)
            vec[j, h] = (beta_c, gam_c, tot, dec_i)

    def tile(ref, j, h):
        return ref[j * CHUNK:(j + 1) * CHUNK, h * HEAD_W:(h + 1) * HEAD_W]

    k16 = [tile(k_ref, *c) for c in chains]
    q16 = [tile(q_ref, *c) for c in chains]
    ks = [k.astype(F32) for k in k16]
    kbs = [k * vec[c][0] for k, c in zip(ks, chains)]
    kq = [_dot_nt(jnp.concatenate([kb.astype(BF16), q], axis=0), k) for kb, q, k in zip(kbs, q16, k16)]
    a_low = [jnp.where(is_eye, 0.0, x[:CHUNK] * vec[c][3]) for x, c in zip(kq, chains)]
    a_qk = [x[CHUNK:] * vec[c][3] for x, c in zip(kq, chains)]
    t_inv = _unit_tri_inverse(a_low, eye, bd)
    egc = [jnp.exp(vec[c][1]) for c in chains]
    sols = [_dot(t, jnp.concatenate([tile(v_ref, *c).astype(F32) * vec[c][0], kb * e], axis=1))
            for t, kb, e, c in zip(t_inv, kbs, egc, chains)]
    pre = {}
    for c, sol, aq, q, k, e in zip(chains, sols, a_qk, q16, ks, egc):
        q_dec = q.astype(F32) * e
        k_dec_t = (k * jnp.exp(vec[c][2] - vec[c][1])).T
        pre[c] = (sol[:, :HEAD_W], jnp.concatenate([sol[:, HEAD_W:], q_dec], axis=0).astype(BF16),
                  jnp.concatenate([aq, k_dec_t], axis=0).astype(BF16), jnp.exp(vec[c][2]))
    return pre


def _gdn_kernel(qf_ref, kf_ref, vf_ref, baf_ref, qr_ref, kr_ref, vr_ref, bar_ref, alog_ref, dtb_ref,
                of_ref, or_ref, s_ref):
    @pl.when(pl.program_id(1) == 0)
    def _():
        s_ref[...] = jnp.zeros_like(s_ref)

    n_chunks = qf_ref.shape[0] // CHUNK
    pre = (_gdn_prepare(qf_ref, kf_ref, vf_ref, baf_ref, alog_ref[0], dtb_ref[0], False),
           _gdn_prepare(qr_ref, kr_ref, vr_ref, bar_ref, alog_ref[1], dtb_ref[1], True))
    o_refs = (of_ref, or_ref)

    lanes = [(d, h) for d in range(2) for h in range(N_HEADS)]
    states = {dh: s_ref[dh[0], dh[1]] for dh in lanes}
    for t in range(n_chunks):
        js = (t, n_chunks - 1 - t)
        cur = {dh: pre[dh[0]][js[dh[0]], dh[1]] for dh in lanes}
        wq = {dh: _dot(cur[dh][1], states[dh]) for dh in lanes}
        v_new = {dh: cur[dh][0] - wq[dh][:CHUNK] for dh in lanes}
        ak = {dh: _dot(cur[dh][2], v_new[dh]) for dh in lanes}
        for d, h in lanes:
            j = js[d]
            o_refs[d][j * CHUNK:(j + 1) * CHUNK, h * HEAD_W:(h + 1) * HEAD_W] = wq[d, h][CHUNK:] + ak[d, h][:CHUNK]
        states = {dh: states[dh] * cur[dh][3] + ak[dh][CHUNK:] for dh in lanes}
    for d, h in lanes:
        s_ref[d, h] = states[d, h]


def _gdn(qkv, ba_t, a_log, dt_bias, *, batch, seq, ctx_len):
    n_all = seq // SCAN_ROWS
    n_ctx = ctx_len // SCAN_ROWS
    cpb = SCAN_ROWS // CHUNK

    def rows(b, s, rev):
        return b * n_all + _block_order(s, n_ctx, n_all, rev)

    def in_specs(rev):
        d = int(rev)
        return [
            pl.BlockSpec((SCAN_ROWS, MIX_W), lambda b, s: (rows(b, s, rev), 0)),
            pl.BlockSpec((SCAN_ROWS, MIX_W), lambda b, s: (rows(b, s, rev), 1)),
            pl.BlockSpec((SCAN_ROWS, MIX_W), lambda b, s: (rows(b, s, rev), 2)),
            pl.BlockSpec((None, cpb, 2 * N_HEADS, CHUNK), lambda b, s: (d, rows(b, s, rev), 0, 0)),
        ]

    vec = pl.BlockSpec((2, N_HEADS, 1), lambda b, s: (0, 0, 0))
    out = jax.ShapeDtypeStruct((batch * seq, MIX_W), F32)
    return pl.pallas_call(
        _gdn_kernel,
        out_shape=(out, out),
        grid=(batch, n_all),
        in_specs=in_specs(False) + in_specs(True) + [vec, vec],
        out_specs=(pl.BlockSpec((SCAN_ROWS, MIX_W), lambda b, s: (rows(b, s, False), 0)),
                   pl.BlockSpec((SCAN_ROWS, MIX_W), lambda b, s: (rows(b, s, True), 0))),
        scratch_shapes=[pltpu.VMEM((2, N_HEADS, HEAD_W, HEAD_W), F32)],
        compiler_params=_params(("parallel", "arbitrary")),
        name="gdn",
    )(qkv, qkv, qkv, ba_t, qkv, qkv, qkv, ba_t, a_log, dt_bias)


def _hgrn2_prepare(q_ref, f_ref, i_ref, lb_ref, rev):
    _, _, incl, _ = _chunk_masks(rev)
    tri = jnp.where(incl, 1.0, 0.0)
    row = lax.broadcasted_iota(jnp.int32, (CHUNK, 1), 0)
    n_sub = CHUNK // SUB
    n_chunks = q_ref.shape[0] // CHUNK

    def tile(ref, j, h):
        return ref[j * CHUNK:(j + 1) * CHUNK, h * HEAD_W:(h + 1) * HEAD_W]

    chains = [(j, h) for j in range(n_chunks) for h in range(N_HEADS)]
    qs_, ks_, lfs = [], [], []
    for j, h in chains:
        qr = tile(q_ref, j, h).astype(F32)
        lb = lb_ref[:, h * HEAD_W:(h + 1) * HEAD_W]
        f = lb + (1.0 - lb) * _sigmoid(tile(f_ref, j, h))
        qs_.append(_silu(qr))
        ks_.append(1.0 - f)
        lfs.append(jnp.log(f))
    gcs = _dot_3x_many(tri, lfs)
    tots = [jnp.sum(lf, axis=0, keepdims=True) for lf in lfs]
    blocks = []
    for q, k, gc in zip(qs_, ks_, gcs):
        row_blocks = []
        for i in range(n_sub):
            mid = i * SUB + SUB // 2
            gref = gc[mid:mid + 1, :]
            qsc = q[i * SUB:(i + 1) * SUB, :] * jnp.exp(gc[i * SUB:(i + 1) * SUB, :] - gref)
            lo, hi = (i * SUB, CHUNK) if rev else (0, (i + 1) * SUB)
            parts = [jnp.zeros((lo, HEAD_W), F32), k[lo:hi, :] * jnp.exp(gref - gc[lo:hi, :]),
                     jnp.zeros((CHUNK - hi, HEAD_W), F32)]
            ksc = jnp.concatenate([x for x in parts if x.shape[0]], axis=0)
            row_blocks.append((qsc, ksc))
        blocks.append(row_blocks)
    scores = [jnp.where(incl, jnp.concatenate([_dot_nt(a, b) for a, b in rb], axis=0), 0.0) for rb in blocks]
    vs = [tile(i_ref, *c) for c in chains]
    local = [_dot(sc, v) for sc, v in zip(scores, vs)]
    incr = [_dot_tn(v, k * jnp.exp(tot - gc)) for v, k, gc, tot in zip(vs, ks_, gcs, tots)]
    return {c: (q * jnp.exp(gc), lo, inc, jnp.exp(tot))
            for c, q, gc, lo, inc, tot in zip(chains, qs_, gcs, local, incr, tots)}


def _hgrn2_kernel(qf_ref, ff_ref, if_ref, qr_ref, fr_ref, ir_ref, lb_ref, of_ref, or_ref, s_ref):
    @pl.when(pl.program_id(1) == 0)
    def _():
        s_ref[...] = jnp.zeros_like(s_ref)

    n_chunks = qf_ref.shape[0] // CHUNK
    pre = (_hgrn2_prepare(qf_ref, ff_ref, if_ref, lb_ref.at[0], False),
           _hgrn2_prepare(qr_ref, fr_ref, ir_ref, lb_ref.at[1], True))
    o_refs = (of_ref, or_ref)

    lanes = [(d, h) for d in range(2) for h in range(N_HEADS)]
    states = {dh: s_ref[dh[0], dh[1]] for dh in lanes}
    for t in range(n_chunks):
        js = (t, n_chunks - 1 - t)
        cur = {dh: pre[dh[0]][js[dh[0]], dh[1]] for dh in lanes}
        qs = {dh: _dot_nt(cur[dh][0], states[dh]) for dh in lanes}
        for d, h in lanes:
            j = js[d]
            o_refs[d][j * CHUNK:(j + 1) * CHUNK, h * HEAD_W:(h + 1) * HEAD_W] = qs[d, h] + cur[d, h][1]
        states = {dh: states[dh] * cur[dh][3] + cur[dh][2] for dh in lanes}
    for d, h in lanes:
        s_ref[d, h] = states[d, h]


def _hgrn2(ub, uf, lb, *, batch, seq, ctx_len):
    n_all = seq // SCAN_ROWS
    n_ctx = ctx_len // SCAN_ROWS

    def rows(b, s, rev):
        return b * n_all + _block_order(s, n_ctx, n_all, rev)

    def in_specs(rev):
        f_col = (UF_HG_FB if rev else UF_HG_FF) // MIX_W
        return [
            pl.BlockSpec((SCAN_ROWS, MIX_W), lambda b, s: (rows(b, s, rev), UB_HG_Q // MIX_W)),
            pl.BlockSpec((SCAN_ROWS, MIX_W), lambda b, s: (rows(b, s, rev), f_col)),
            pl.BlockSpec((SCAN_ROWS, MIX_W), lambda b, s: (rows(b, s, rev), UB_HG_I // MIX_W)),
        ]

    out = jax.ShapeDtypeStruct((batch * seq, MIX_W), F32)
    return pl.pallas_call(
        _hgrn2_kernel,
        out_shape=(out, out),
        grid=(batch, n_all),
        in_specs=in_specs(False) + in_specs(True) + [pl.BlockSpec((2, 1, MIX_W), lambda b, s: (0, 0, 0))],
        out_specs=(pl.BlockSpec((SCAN_ROWS, MIX_W), lambda b, s: (rows(b, s, False), 0)),
                   pl.BlockSpec((SCAN_ROWS, MIX_W), lambda b, s: (rows(b, s, True), 0))),
        scratch_shapes=[pltpu.VMEM((2, N_HEADS, HEAD_W, HEAD_W), F32)],
        compiler_params=_params(("parallel", "arbitrary")),
        name="hgrn2",
    )(ub, uf, ub, ub, uf, ub, lb)


LRU_GROUP = 8


def _group_scan(a, b, rev):
    n, w = a.shape
    a = a.reshape(n // LRU_GROUP, LRU_GROUP, w)
    b = b.reshape(n // LRU_GROUP, LRU_GROUP, w)
    sub = lax.broadcasted_iota(jnp.int32, (1, LRU_GROUP, 1), 1)
    s = 1
    while s < LRU_GROUP:
        shift, valid = (LRU_GROUP - s, sub < LRU_GROUP - s) if rev else (s, sub >= s)
        a_s, b_s = pltpu.roll(a, shift, 1), pltpu.roll(b, shift, 1)
        b = jnp.where(valid, a * b_s + b, b)
        a = jnp.where(valid, a * a_s, a)
        s *= 2
    return a, b


def _chain_groups(a, b, carry, rev, store):
    n = a.shape[0]
    for g in (range(n - 1, -1, -1) if rev else range(n)):
        hg = a[g] * carry + b[g]
        store(g, hg)
        carry = hg[0:1, :] if rev else hg[LRU_GROUP - 1:LRU_GROUP, :]
    return carry


def _gelu_tanh(x):
    return 0.5 * x * (1.0 + jnp.tanh(0.7978845608028654 * (x + 0.044715 * (x * x * x))))


def _lru_kernel(xb_ref, gb_ref, cw_ref, cb_ref, wa_ref, ba_ref, wx_ref, bx_ref, lam_ref, o_ref, xc_ref, hf_ref,
                hb_ref, *, ctx_len):
    n = xb_ref.shape[0]
    nblk = n // LRU_BLOCK
    nctx = ctx_len // LRU_BLOCK
    xc_ref[...] = _seg_conv(xb_ref[...].astype(F32), cw_ref, ctx_len) + cb_ref[...]

    rate = [(-LRU_C * LOG2_E) * _softplus(-lam_ref[d]) for d in range(2)]

    def gates(xc, d):
        r = _sigmoid(_dot(xc, wa_ref[d]) + ba_ref[d])
        ig = _sigmoid(_dot(xc, wx_ref[d]) + bx_ref[d])
        a = jnp.exp2(r * rate[d])
        return a, jnp.sqrt(1.0 - a * a) * ig * xc

    def scan_block(blk, carry, d, out_ref):
        base = pl.multiple_of(blk * LRU_BLOCK, LRU_BLOCK)
        a, b = _group_scan(*gates(xc_ref[pl.ds(base, LRU_BLOCK), :], d), rev=bool(d))

        def store(g, hg):
            out_ref[pl.ds(pl.multiple_of(base + g * LRU_GROUP, LRU_GROUP), LRU_GROUP), :] = hg

        return _chain_groups(a, b, carry, bool(d), store)

    def step(i, carry):
        cf = scan_block(i, carry[0], 0, hf_ref)
        cb = scan_block(_block_order(i, nctx, nblk, True), carry[1], 1, hb_ref)
        return cf, cb

    zero = jnp.zeros((1, HEAD_W), F32)
    lax.fori_loop(0, nblk, step, (zero, zero))
    o_ref[...] = ((hf_ref[...] + hb_ref[...]) * _gelu_tanh(gb_ref[...].astype(F32))).astype(o_ref.dtype)


def _lru(ub, conv_w, conv_b, w_a, b_a, w_x, b_x, lam, *, batch, seq, ctx_len):
    vec = pl.BlockSpec((2, 1, HEAD_W), lambda b, h: (0, 0, h))
    mat = pl.BlockSpec((2, None, HEAD_W, HEAD_W), lambda b, h: (0, h, 0, 0))
    return pl.pallas_call(
        functools.partial(_lru_kernel, ctx_len=ctx_len),
        out_shape=jax.ShapeDtypeStruct((batch * seq, MIX_W), BF16),
        grid=(batch, N_HEADS),
        in_specs=[
            pl.BlockSpec((seq, HEAD_W), lambda b, h: (b, UB_LRU_X // HEAD_W + h)),
            pl.BlockSpec((seq, HEAD_W), lambda b, h: (b, UB_LRU_G // HEAD_W + h)),
            pl.BlockSpec((4, HEAD_W), lambda b, h: (0, h)),
            pl.BlockSpec((1, HEAD_W), lambda b, h: (0, h)),
            mat, vec, mat, vec, vec,
        ],
        out_specs=pl.BlockSpec((seq, HEAD_W), lambda b, h: (b, h)),
        scratch_shapes=[pltpu.VMEM((seq, HEAD_W), F32)] * 3,
        compiler_params=_params(("parallel", "parallel")),
        name="rglru",
    )(ub, ub, conv_w, conv_b, w_a, b_a, w_x, b_x, lam)


def _norm_rope(x, w, cos, sin, grp, scale):
    lane = lax.broadcasted_iota(jnp.int32, (1, 128), 1)
    first = (lane % (ATT_HEAD_DIM // 2)) < (ATT_HEAD_DIM // 4)
    ss = _dot_x3(x * x, grp)
    y = x * lax.rsqrt(ss * (1.0 / ATT_HEAD_DIM) + EPS) * w
    rot = jnp.where(first, pltpu.roll(y, 128 - ATT_HEAD_DIM // 4, 1), pltpu.roll(y, ATT_HEAD_DIM // 4, 1))
    return (y * cos + rot * sin) * scale


def _att_kernel(q_ref, k_ref, v_ref, cos_ref, sin_ref, qw_ref, kw_ref, grp_ref, o_ref, kt_ref, vb_ref, *, ctx_len):
    i = pl.program_id(1)
    grp = grp_ref[...]
    tq = q_ref.shape[0]

    @pl.when(i == 0)
    def _():
        for c in range(k_ref.shape[0] // tq):
            rows = slice(c * tq, (c + 1) * tq)
            kn = _norm_rope(k_ref[rows, :].astype(F32), kw_ref[...], cos_ref[rows, :], sin_ref[rows, :], grp, 1.0)
            kt_ref[:, rows] = kn.T.astype(kt_ref.dtype)
        vb_ref[...] = v_ref[...].astype(vb_ref.dtype)

    q_scale = (ATT_HEAD_DIM ** -0.5) * LOG2_E
    q_rows = pl.ds(pl.multiple_of(i * tq, tq), tq)
    cos, sin = cos_ref[q_rows, :], sin_ref[q_rows, :]
    qn = [_norm_rope(q_ref[:, s * 128:(s + 1) * 128].astype(F32), qw_ref[...], cos, sin, grp, q_scale).astype(BF16)
          for s in range(q_ref.shape[1] // 128)]

    def attend(n_keys):
        v = vb_ref[:n_keys, :]

        def scores(head):
            g = head // ATT_GROUP
            kt = kt_ref[g * ATT_HEAD_DIM:(g + 1) * ATT_HEAD_DIM, :n_keys]
            q = qn[head // 2][:, (head % 2) * ATT_HEAD_DIM:(head % 2 + 1) * ATT_HEAD_DIM]
            return jnp.dot(q, kt, preferred_element_type=F32)

        s_next = scores(0)
        for head in range(ATT_Q_HEADS):
            s = s_next
            if head + 1 < ATT_Q_HEADS:
                s_next = scores(head + 1)
            g = head // ATT_GROUP
            p = jnp.exp2(s - jnp.max(s, axis=-1, keepdims=True))
            den = jnp.sum(p, axis=-1, keepdims=True)
            pv = jnp.dot(p.astype(BF16), v, preferred_element_type=F32)
            o_ref[:, head * ATT_HEAD_DIM:(head + 1) * ATT_HEAD_DIM] = (
                pv[:, g * ATT_HEAD_DIM:(g + 1) * ATT_HEAD_DIM] / den).astype(o_ref.dtype)

    @pl.when(i == 0)
    def _():
        attend(ctx_len)

    @pl.when(i != 0)
    def _():
        attend(kt_ref.shape[1])


def _attention(ub, uf, cos, sin, qw, kw, grp, *, batch, seq, ctx_len):
    tiles = seq // ATT_TQ
    whole = lambda shape: pl.BlockSpec(shape, lambda b, i: (0, 0))
    return pl.pallas_call(
        functools.partial(_att_kernel, ctx_len=ctx_len),
        out_shape=jax.ShapeDtypeStruct((batch * seq, ATT_QW), BF16),
        grid=(batch, tiles),
        in_specs=[
            pl.BlockSpec((ATT_TQ, ATT_QW), lambda b, i: (b * tiles + i, UB_ATT_Q // ATT_QW)),
            pl.BlockSpec((seq, ATT_KW), lambda b, i: (b, UF_ATT_K // ATT_KW)),
            pl.BlockSpec((seq, ATT_KW), lambda b, i: (b, UF_ATT_V // ATT_KW)),
            whole((seq, 128)), whole((seq, 128)), whole((1, 128)), whole((1, 128)), whole((128, 128)),
        ],
        out_specs=pl.BlockSpec((ATT_TQ, ATT_QW), lambda b, i: (b * tiles + i, 0)),
        scratch_shapes=[pltpu.VMEM((ATT_KW, seq), BF16), pltpu.VMEM((seq, ATT_KW), BF16)],
        compiler_params=_params(("parallel", "arbitrary")),
        name="attention",
    )(ub, uf, uf, cos, sin, qw, kw, grp)


def _merge_kernel(oaf_ref, oab_ref, obf_ref, obb_ref, ga_ref, gb_ref, yc_ref, yd_ref, gate_ref, h_ref, modc_ref,
                  modl_ref, dnw_ref, hgw_ref, wb_ref, wo_ref, out_ref, *, tiles_per_batch):
    def gated(of_ref, ob_ref, g_ref, nw_ref):
        o = of_ref[...] + ob_ref[...]
        ys = [_rms_rows(o[:, h * HEAD_W:(h + 1) * HEAD_W], nw_ref[...]) for h in range(N_HEADS)]
        g = g_ref[...].astype(F32)
        return jnp.concatenate(ys, axis=1) * _silu(g)

    ys = (gated(oaf_ref, oab_ref, ga_ref, dnw_ref), gated(obf_ref, obb_ref, gb_ref, hgw_ref), yc_ref[...], yd_ref[...])
    acc = None
    for b in range(N_BRANCH):
        gate = gate_ref[:, b * D_MODEL:(b + 1) * D_MODEL].astype(F32)
        term = _sigmoid(gate) * _dot(ys[b], wb_ref[b])
        acc = term if acc is None else acc + term
    is_ctx = (pl.program_id(0) % tiles_per_batch) == 0
    out_ref[...] = h_ref[...] + _pick_mod(modc_ref, modl_ref, 2, is_ctx) * _dot(acc, wo_ref[...])


def _merge(oaf, oab, obf, obb, ub, yc, yd, h, modc, modl, dnw, hgw, wb, wo, *, seq, ctx_len):
    rows = h.shape[0]
    tm = ctx_len
    tiles_per_batch = seq // tm
    mix = lambda c: pl.BlockSpec((tm, MIX_W), lambda i: (i, c // MIX_W))
    return pl.pallas_call(
        functools.partial(_merge_kernel, tiles_per_batch=tiles_per_batch),
        out_shape=jax.ShapeDtypeStruct((rows, D_MODEL), F32),
        grid=(rows // tm,),
        in_specs=[
            mix(0), mix(0), mix(0), mix(0), mix(UB_DN_G), mix(UB_HG_G), mix(0), mix(0),
            pl.BlockSpec((tm, N_BRANCH * D_MODEL), lambda i: (i, 0)),
            pl.BlockSpec((tm, D_MODEL), lambda i: (i, 0)),
            pl.BlockSpec((6, D_MODEL), lambda i: (0, 0)),
            pl.BlockSpec((None, 6, D_MODEL), lambda i: (i // tiles_per_batch, 0, 0)),
            pl.BlockSpec((1, HEAD_W), lambda i: (0, 0)),
            pl.BlockSpec((1, HEAD_W), lambda i: (0, 0)),
            pl.BlockSpec((N_BRANCH, MIX_W, D_MODEL), lambda i: (0, 0, 0)),
            pl.BlockSpec((D_MODEL, D_MODEL), lambda i: (0, 0)),
        ],
        out_specs=pl.BlockSpec((tm, D_MODEL), lambda i: (i, 0)),
        compiler_params=_params(("parallel",)),
        name="merge",
    )(oaf, oab, obf, obb, ub, ub, yc, yd, ub, h, modc, modl, dnw, hgw, wb, wo)


def _mlp_kernel(h_ref, modc_ref, modl_ref, nw_ref, w1_ref, w2_ref, o_ref, z_ref, acc_ref,
                *, tm, tiles_per_batch, ctx_len):
    j = pl.program_id(1)
    row = (pl.program_id(0) % tiles_per_batch) * tm + lax.broadcasted_iota(jnp.int32, (tm, 1), 0)
    is_ctx = row < ctx_len

    @pl.when(j == 0)
    def _():
        y = _rms_rows(h_ref[...], nw_ref[...])
        shift = _pick_mod(modc_ref, modl_ref, 3, is_ctx)
        scale = _pick_mod(modc_ref, modl_ref, 4, is_ctx)
        z_ref[...] = (y * (1.0 + scale) + shift).astype(BF16)
        acc_ref[...] = jnp.zeros_like(acc_ref)

    a = jnp.maximum(jnp.dot(z_ref[...], w1_ref[...], preferred_element_type=F32), 0.0)
    acc_ref[...] += _dot(a * a, w2_ref[...])

    @pl.when(j == pl.num_programs(1) - 1)
    def _():
        o_ref[...] = h_ref[...] + _pick_mod(modc_ref, modl_ref, 5, is_ctx) * acc_ref[...]


def _mlp(h, modc, modl, nw, w1, w2, *, seq, ctx_len):
    rows = h.shape[0]
    tiles_per_batch = 4
    tm = seq // tiles_per_batch
    kern = functools.partial(_mlp_kernel, tm=tm, tiles_per_batch=tiles_per_batch, ctx_len=ctx_len)
    return pl.pallas_call(
        kern,
        out_shape=jax.ShapeDtypeStruct((rows, D_MODEL), F32),
        grid=(rows // tm, D_FF // FF_BLOCK),
        in_specs=[
            pl.BlockSpec((tm, D_MODEL), lambda i, j: (i, 0)),
            pl.BlockSpec((6, D_MODEL), lambda i, j: (0, 0)),
            pl.BlockSpec((None, 6, D_MODEL), lambda i, j: (i // tiles_per_batch, 0, 0)),
            pl.BlockSpec((1, D_MODEL), lambda i, j: (0, 0)),
            pl.BlockSpec((D_MODEL, FF_BLOCK), lambda i, j: (0, j)),
            pl.BlockSpec((FF_BLOCK, D_MODEL), lambda i, j: (j, 0)),
        ],
        out_specs=pl.BlockSpec((tm, D_MODEL), lambda i, j: (i, 0)),
        scratch_shapes=[pltpu.VMEM((tm, D_MODEL), BF16), pltpu.VMEM((tm, D_MODEL), F32)],
        compiler_params=_params(("parallel", "arbitrary"), vmem=VMEM_LIMIT_V7X + 8 * 1024 * 1024),
        name="mlp",
    )(h, modc, modl, nw, w1, w2)


def _regroup_w_in(w_in):
    o = np.cumsum([0, 512, 512, 512, 512, 8, 8, 512, 512, 512, 512, 512, 512, 512, 512, 128, 128, 4096])
    cols = lambda a, b: w_in[..., o[a]:o[b]]
    wb = jnp.concatenate([cols(16, 17), cols(6, 7), cols(9, 11), cols(0, 3), cols(11, 13), cols(3, 4), cols(13, 14)],
                         axis=-1)
    wf = jnp.concatenate([cols(7, 9), cols(14, 16), cols(4, 6)], axis=-1)
    pad = lambda w, n: jnp.pad(w, ((0, 0), (0, 0), (0, n - w.shape[-1]))).astype(BF16)
    return pad(wb, N_UB), pad(wf, N_UF)


def _rope_tables(t_len, ctx_len):
    rows = t_len // GRID_W
    row_id = jnp.repeat(jnp.arange(rows), GRID_W).astype(F32)
    col_id = jnp.tile(jnp.arange(GRID_W), rows).astype(F32)
    axis_dim = ATT_HEAD_DIM // 2
    inv = ROPE_THETA ** (-jnp.arange(0, axis_dim, 2, dtype=F32) / axis_dim)
    ang = jnp.stack([row_id[:, None] * inv, col_id[:, None] * inv], axis=1)
    cos, sin = jnp.cos(ang), jnp.sin(ang)
    cos_h = jnp.concatenate([cos, cos], axis=-1).reshape(t_len, ATT_HEAD_DIM)
    sin_h = jnp.concatenate([-sin, sin], axis=-1).reshape(t_len, ATT_HEAD_DIM)
    cos_t = jnp.concatenate([jnp.ones((ctx_len, ATT_HEAD_DIM), F32), cos_h], axis=0)
    sin_t = jnp.concatenate([jnp.zeros((ctx_len, ATT_HEAD_DIM), F32), sin_h], axis=0)
    return jnp.tile(cos_t, (1, 2)), jnp.tile(sin_t, (1, 2))


def _hgrn2_lower_bounds(p):
    sm = jax.nn.softmax(p.astype(F32), axis=1)
    cs = jnp.cumsum(sm, axis=1)
    return cs - cs[:, :1]


def kernel(x, c, ctx, c_ctx, mod_w, mod_b, norm1_w, norm2_w, w_in, dn_conv_w, dn_a_log, dn_dt_bias, dn_norm_w,
           hg_lower_bounds, hg_norm_w, lru_conv_w, lru_conv_b, lru_w_a, lru_b_a, lru_w_x, lru_b_x, lru_lambda,
           att_q_norm_w, att_k_norm_w, w_branch, w_out, mlp_w1, mlp_w2):
    batch, t_len, _ = x.shape
    ctx_len = ctx.shape[1]
    depth = mod_w.shape[0]
    seq = ctx_len + t_len
    n_chunks = seq // CHUNK
    assert ctx_len % SCAN_ROWS == 0 and t_len % SCAN_ROWS == 0 and ctx_len == ATT_TQ
    dims = dict(batch=batch, seq=seq, ctx_len=ctx_len)

    w_ub, w_uf = _regroup_w_in(w_in)
    w_branch_b, w_out_b = w_branch.astype(BF16), w_out.astype(BF16)
    w1_b, w2_b = mlp_w1.astype(BF16), mlp_w2.astype(BF16)
    lru_wa_b, lru_wx_b = lru_w_a.astype(BF16), lru_w_x.astype(BF16)
    lb_all = _hgrn2_lower_bounds(hg_lower_bounds)
    cos_t, sin_t = _rope_tables(t_len, ctx_len)
    grp = jnp.asarray((np.arange(128)[:, None] // ATT_HEAD_DIM) == (np.arange(128)[None, :] // ATT_HEAD_DIM), F32)

    n_mod_rows = batch + 8
    cc = jnp.zeros((n_mod_rows, D_MODEL), F32).at[:batch].set(c).at[batch].set(c_ctx)
    mods = _modulations(cc, mod_w.astype(BF16), mod_b)

    h = jnp.concatenate([ctx, x], axis=1).reshape(batch * seq, D_MODEL)
    for l in range(depth):
        modl = mods[l, :batch].reshape(batch, 6, D_MODEL)
        modc = mods[l, batch].reshape(6, D_MODEL)
        nw1 = norm1_w[l].reshape(1, D_MODEL)
        ub, uf = _inproj(h, modc, modl, nw1, w_ub[l], w_uf[l], seq=seq, ctx_len=ctx_len)

        qkv = _dn_prep(ub, dn_conv_w[l], **dims)
        ba = uf[:, UF_DN_BA:UF_DN_BA + 4 * N_HEADS].reshape(batch * n_chunks, CHUNK, 2, 2, N_HEADS)
        ba_t = jnp.transpose(ba, (3, 0, 2, 4, 1)).reshape(2, batch * n_chunks, 2 * N_HEADS, CHUNK)
        oa = _gdn(qkv, ba_t, dn_a_log[l].reshape(2, N_HEADS, 1), dn_dt_bias[l].reshape(2, N_HEADS, 1), **dims)

        ob = _hgrn2(ub, uf, lb_all[:, l].reshape(2, 1, MIX_W), **dims)

        yc = _lru(ub, lru_conv_w[l], lru_conv_b[l].reshape(1, MIX_W), lru_wa_b[l], lru_b_a[l].reshape(2, 1, MIX_W),
                  lru_wx_b[l], lru_b_x[l].reshape(2, 1, MIX_W), lru_lambda[l].reshape(2, 1, MIX_W), **dims)

        yd = _attention(ub, uf, cos_t, sin_t, jnp.tile(att_q_norm_w[l], 2).reshape(1, 128),
                        jnp.tile(att_k_norm_w[l], 2).reshape(1, 128), grp, **dims)

        h = _merge(oa[0], oa[1], ob[0], ob[1], ub, yc, yd, h, modc, modl, dn_norm_w[l].reshape(1, HEAD_W),
                   hg_norm_w[l].reshape(1, HEAD_W), w_branch_b[l], w_out_b[l], seq=seq, ctx_len=ctx_len)
        h = _mlp(h, modc, modl, norm2_w[l].reshape(1, D_MODEL), w1_b[l], w2_b[l], seq=seq, ctx_len=ctx_len)

    return h.reshape(batch, seq, D_MODEL)[:, ctx_len:]
```

```python
import functools

import jax
import jax.numpy as jnp
import numpy as np
from jax import lax
from jax.experimental import pallas as pl
from jax.experimental.pallas import tpu as pltpu

F32 = jnp.float32
BF16 = jnp.bfloat16

EPS = 1e-6
D_MODEL = 1024
GRID_W = 64
N_HEADS = 4
HEAD_W = 128
MIX_W = N_HEADS * HEAD_W
CHUNK = 64
SUB = 16
SCAN_ROWS = 256
LRU_C = 8.0
LRU_BLOCK = 256
ATT_Q_HEADS = 8
ATT_KV_HEADS = 2
ATT_HEAD_DIM = 64
ATT_GROUP = ATT_Q_HEADS // ATT_KV_HEADS
ATT_QW = ATT_Q_HEADS * ATT_HEAD_DIM
ATT_KW = ATT_KV_HEADS * ATT_HEAD_DIM
ATT_TQ = 256
ROPE_THETA = 10000.0
LOG2_E = 1.4426950408889634
N_BRANCH = 4
D_FF = 4 * D_MODEL
FF_BLOCK = 2048

UB_GATE = 0
UB_HG_Q, UB_HG_I, UB_HG_G = 4096, 4608, 5120
UB_DN_Q = 5632
UB_LRU_X, UB_LRU_G = 7168, 7680
UB_DN_G = 8192
UB_ATT_Q = 8704
TN_UB = 2304
N_UB = 4 * TN_UB
UF_HG_FF, UF_HG_FB, UF_ATT_K, UF_ATT_V, UF_DN_BA = 0, 512, 1024, 1152, 1280
N_UF = 1536

VMEM_LIMIT_V7X = 48 * 1024 * 1024


def _params(sem, vmem=VMEM_LIMIT_V7X):
    return pltpu.CompilerParams(dimension_semantics=sem, vmem_limit_bytes=vmem)


def _sigmoid(x):
    return 0.5 * jnp.tanh(0.5 * x) + 0.5


def _silu(x):
    h = 0.5 * x
    return h * (1.0 + jnp.tanh(h))


def _softplus(x):
    return jnp.maximum(x, 0.0) + jnp.log1p(jnp.exp(-jnp.abs(x)))


def _dot(a, b):
    return jnp.dot(a.astype(BF16), b.astype(BF16), preferred_element_type=F32)


def _dot_nt(a, b):
    return lax.dot_general(a.astype(BF16), b.astype(BF16), (((1,), (1,)), ((), ())),
                           preferred_element_type=F32)


def _dot_tn(a, b):
    return lax.dot_general(a.astype(BF16), b.astype(BF16), (((0,), (0,)), ((), ())),
                           preferred_element_type=F32)


def _split3(x):
    hi = x.astype(BF16)
    r1 = x - hi.astype(F32)
    mid = r1.astype(BF16)
    lo = (r1 - mid.astype(F32)).astype(BF16)
    return hi, mid, lo


def _dot_x3(x, m):
    m = m.astype(BF16)
    hi, mid, lo = _split3(x)
    out = jnp.dot(hi, m, preferred_element_type=F32)
    out = out + jnp.dot(mid, m, preferred_element_type=F32)
    return out + jnp.dot(lo, m, preferred_element_type=F32)


def _dot_3x_many(m, xs):
    w = xs[0].shape[1]
    pieces = [_split3(x) for x in xs]
    cols = [p[i] for i in range(3) for p in pieces]
    wide = jnp.dot(m.astype(BF16), jnp.concatenate(cols, axis=1), preferred_element_type=F32)
    n = len(xs)
    part = lambda p, i: wide[:, (p * n + i) * w:(p * n + i + 1) * w]
    return [part(0, i) + part(1, i) + part(2, i) for i in range(n)]


def _rms_rows(x, w):
    return x * lax.rsqrt(jnp.mean(x * x, axis=-1, keepdims=True) + EPS) * w


def _mod_kernel(c_ref, w_ref, b_ref, o_ref):
    c = c_ref[...]
    o_ref[...] = _dot(_silu(c), w_ref[...]) + b_ref[...]


def _modulations(cc, mod_w, mod_b):
    depth = mod_w.shape[0]
    rows = cc.shape[0]
    n_out = mod_w.shape[2]
    return pl.pallas_call(
        _mod_kernel,
        out_shape=jax.ShapeDtypeStruct((depth, rows, n_out), F32),
        grid=(depth, n_out // D_MODEL),
        in_specs=[
            pl.BlockSpec((rows, D_MODEL), lambda l, j: (0, 0)),
            pl.BlockSpec((None, D_MODEL, D_MODEL), lambda l, j: (l, 0, j)),
            pl.BlockSpec((None, 1, D_MODEL), lambda l, j: (l, 0, j)),
        ],
        out_specs=pl.BlockSpec((None, rows, D_MODEL), lambda l, j: (l, 0, j)),
        compiler_params=_params(("parallel", "parallel")),
        name="modulations",
    )(cc, mod_w, mod_b.reshape(depth, 1, n_out))


def _pick_mod(modc_ref, modl_ref, idx, is_ctx):
    return jnp.where(is_ctx, modc_ref[idx:idx + 1, :], modl_ref[idx:idx + 1, :])


def _inproj_kernel(h_ref, modc_ref, modl_ref, nw_ref, wb_ref, wf_ref, ob_ref, of_ref, xn_ref,
                   *, tm, tiles_per_batch, ctx_len):
    j = pl.program_id(1)

    @pl.when(j == 0)
    def _():
        y = _rms_rows(h_ref[...], nw_ref[...])
        row = (pl.program_id(0) % tiles_per_batch) * tm + lax.broadcasted_iota(jnp.int32, (tm, 1), 0)
        is_ctx = row < ctx_len
        shift = _pick_mod(modc_ref, modl_ref, 0, is_ctx)
        scale = _pick_mod(modc_ref, modl_ref, 1, is_ctx)
        xn_ref[...] = (y * (1.0 + scale) + shift).astype(BF16)

    ob_ref[...] = jnp.dot(xn_ref[...], wb_ref[...], preferred_element_type=F32).astype(ob_ref.dtype)

    @pl.when(j == pl.num_programs(1) - 1)
    def _():
        of_ref[...] = jnp.dot(xn_ref[...], wf_ref[...], preferred_element_type=F32)


def _inproj(h, modc, modl, nw, w_b, w_f, *, seq, ctx_len):
    rows = h.shape[0]
    tiles_per_batch = 4
    tm = seq // tiles_per_batch
    kern = functools.partial(_inproj_kernel, tm=tm, tiles_per_batch=tiles_per_batch, ctx_len=ctx_len)
    return pl.pallas_call(
        kern,
        out_shape=(jax.ShapeDtypeStruct((rows, N_UB), BF16), jax.ShapeDtypeStruct((rows, N_UF), F32)),
        grid=(rows // tm, N_UB // TN_UB),
        in_specs=[
            pl.BlockSpec((tm, D_MODEL), lambda i, j: (i, 0)),
            pl.BlockSpec((6, D_MODEL), lambda i, j: (0, 0)),
            pl.BlockSpec((None, 6, D_MODEL), lambda i, j: (i // tiles_per_batch, 0, 0)),
            pl.BlockSpec((1, D_MODEL), lambda i, j: (0, 0)),
            pl.BlockSpec((D_MODEL, TN_UB), lambda i, j: (0, j)),
            pl.BlockSpec((D_MODEL, N_UF), lambda i, j: (0, 0)),
        ],
        out_specs=(pl.BlockSpec((tm, TN_UB), lambda i, j: (i, j)),
                   pl.BlockSpec((tm, N_UF), lambda i, j: (i, 0))),
        scratch_shapes=[pltpu.VMEM((tm, D_MODEL), BF16)],
        compiler_params=_params(("parallel", "arbitrary"), vmem=VMEM_LIMIT_V7X + 8 * 1024 * 1024),
        name="inproj",
    )(h, modc, modl, nw, w_b, w_f)


def _seg_conv(x, w_ref, ctx_len):
    n = x.shape[0]
    row = lax.broadcasted_iota(jnp.int32, (n, 1), 0)
    lo = jnp.where(row >= ctx_len, ctx_len, 0)
    hi = jnp.where(row >= ctx_len, n, ctx_len)

    def tap(k):
        tk = row + k
        valid = jnp.logical_and(tk >= lo, tk < hi)
        return jnp.where(valid, pltpu.roll(x, (-k) % n, 0), 0.0)

    return (tap(-1) * w_ref[0:1, :] + x * w_ref[1:2, :] + tap(1) * w_ref[2:3, :] + tap(2) * w_ref[3:4, :])


def _dn_prep_kernel(u_ref, w_ref, o_ref, *, ctx_len):
    j = pl.program_id(1)
    y = _seg_conv(u_ref[...].astype(F32), w_ref, ctx_len)
    y = _silu(y)
    n = lax.rsqrt(jnp.sum(y * y, axis=-1, keepdims=True) + EPS)
    fac = jnp.where(j < N_HEADS, n * (HEAD_W ** -0.5), jnp.where(j < 2 * N_HEADS, n, 1.0))
    o_ref[...] = (y * fac).astype(o_ref.dtype)


def _dn_prep(ub, conv_w, *, batch, seq, ctx_len):
    nblk = 3 * N_HEADS
    return pl.pallas_call(
        functools.partial(_dn_prep_kernel, ctx_len=ctx_len),
        out_shape=jax.ShapeDtypeStruct((batch * seq, 3 * MIX_W), BF16),
        grid=(batch, nblk),
        in_specs=[
            pl.BlockSpec((seq, HEAD_W), lambda b, j: (b, UB_DN_Q // HEAD_W + j)),
            pl.BlockSpec((4, HEAD_W), lambda b, j: (0, j)),
        ],
        out_specs=pl.BlockSpec((seq, HEAD_W), lambda b, j: (b, j)),
        compiler_params=_params(("parallel", "parallel")),
        name="dn_prep",
    )(ub, conv_w)


def _block_order(s, n_ctx, n_all, rev):
    if not rev:
        return s
    return jnp.where(s < n_ctx, n_ctx - 1 - s, n_all + n_ctx - 1 - s)


def _chunk_masks(rev):
    ii = lax.broadcasted_iota(jnp.int32, (CHUNK, CHUNK), 0)
    jj = lax.broadcasted_iota(jnp.int32, (CHUNK, CHUNK), 1)
    incl = (ii <= jj) if rev else (ii >= jj)
    incl_t = (ii >= jj) if rev else (ii <= jj)
    return ii, jj, incl, incl_t


def _unit_tri_inverse(a_all, eye, bd):
    ad = [jnp.where(bd, a, 0.0) for a in a_all]
    ao = [a - d for a, d in zip(a_all, ad)]
    p = [-d for d in ad]
    dinv = [eye + x for x in p]
    for _ in range(3):
        p = [_dot(x, x) for x in p]
        dinv = [d + _dot(d, x) for d, x in zip(dinv, p)]
    m = [-_dot(d, o) for d, o in zip(dinv, ao)]
    mm = [_dot(x, x) for x in m]
    t = [eye + x for x in m]
    t = [x + _dot(x, y) for x, y in zip(t, mm)]
    return [_dot(x, d) for x, d in zip(t, dinv)]


def _gdn_prepare(q_ref, k_ref, v_ref, ba_ref, alog, dtb, rev):
    ii, jj, incl, incl_t = _chunk_masks(rev)
    is_eye = ii == jj
    eye = jnp.where(is_eye, 1.0, 0.0)
    tri = jnp.where(incl, 1.0, 0.0)
    tri_t = jnp.where(incl_t, 1.0, 0.0)
    bd = (ii // SUB) == (jj // SUB)
    n_chunks = q_ref.shape[0] // CHUNK

    alog_neg = -jnp.exp(alog)
    chains = [(j, h) for j in range(n_chunks) for h in range(N_HEADS)]
    vec = {}
    for j in range(n_chunks):
        ba = ba_ref[j]
        beta_r = _sigmoid(ba[0:N_HEADS])
        g_r = alog_neg * _softplus(ba[N_HEADS:2 * N_HEADS] + dtb)
        for h in range(N_HEADS):
            gr = g_r[h:h + 1]
            beta_c = jnp.sum(eye * beta_r[h:h + 1], axis=1, keepdims=True)
            g_c = jnp.sum(eye * gr, axis=1, keepdims=True)
            gam_c = jnp.sum(tri * gr, axis=1, keepdims=True)
            gam_r = jnp.sum(tri_t * g_c, axis=0, keepdims=True)
            tot = jnp.sum(gr, axis=1, keepdims=True)
            dec_i = jnp.where(incl, jnp.exp(gam_c - gam_r), 0.0)
            vec[j, h] = (beta_c, gam_c, tot, dec_i)

    def tile(ref, j, h):
        return ref[j * CHUNK:(j + 1) * CHUNK, h * HEAD_W:(h + 1) * HEAD_W]

    k16 = [tile(k_ref, *c) for c in chains]
    q16 = [tile(q_ref, *c) for c in chains]
    ks = [k.astype(F32) for k in k16]
    kbs = [k * vec[c][0] for k, c in zip(ks, chains)]
    kq = [_dot_nt(jnp.concatenate([kb.astype(BF16), q], axis=0), k) for kb, q, k in zip(kbs, q16, k16)]
    a_low = [jnp.where(is_eye, 0.0, x[:CHUNK] * vec[c][3]) for x, c in zip(kq, chains)]
    a_qk = [x[CHUNK:] * vec[c][3] for x, c in zip(kq, chains)]
    t_inv = _unit_tri_inverse(a_low, eye, bd)
    egc = [jnp.exp(vec[c][1]) for c in chains]
    sols = [_dot(t, jnp.concatenate([tile(v_ref, *c).astype(F32) * vec[c][0], kb * e], axis=1))
            for t, kb, e, c in zip(t_inv, kbs, egc, chains)]
    pre = {}
    for c, sol, aq, q, k, e in zip(chains, sols, a_qk, q16, ks, egc):
        q_dec = q.astype(F32) * e
        k_dec_t = (k * jnp.exp(vec[c][2] - vec[c][1])).T
        pre[c] = (sol[:, :HEAD_W], jnp.concatenate([sol[:, HEAD_W:], q_dec], axis=0).astype(BF16),
                  jnp.concatenate([aq, k_dec_t], axis=0).astype(BF16), jnp.exp(vec[c][2]))
    return pre


def _gdn_kernel(qf_ref, kf_ref, vf_ref, baf_ref, qr_ref, kr_ref, vr_ref, bar_ref, alog_ref, dtb_ref,
                of_ref, or_ref, s_ref):
    @pl.when(pl.program_id(1) == 0)
    def _():
        s_ref[...] = jnp.zeros_like(s_ref)

    n_chunks = qf_ref.shape[0] // CHUNK
    pre = (_gdn_prepare(qf_ref, kf_ref, vf_ref, baf_ref, alog_ref[0], dtb_ref[0], False),
           _gdn_prepare(qr_ref, kr_ref, vr_ref, bar_ref, alog_ref[1], dtb_ref[1], True))
    o_refs = (of_ref, or_ref)

    lanes = [(d, h) for d in range(2) for h in range(N_HEADS)]
    states = {dh: s_ref[dh[0], dh[1]] for dh in lanes}
    for t in range(n_chunks):
        js = (t, n_chunks - 1 - t)
        cur = {dh: pre[dh[0]][js[dh[0]], dh[1]] for dh in lanes}
        wq = {dh: _dot(cur[dh][1], states[dh]) for dh in lanes}
        v_new = {dh: cur[dh][0] - wq[dh][:CHUNK] for dh in lanes}
        ak = {dh: _dot(cur[dh][2], v_new[dh]) for dh in lanes}
        for d, h in lanes:
            j = js[d]
            o_refs[d][j * CHUNK:(j + 1) * CHUNK, h * HEAD_W:(h + 1) * HEAD_W] = wq[d, h][CHUNK:] + ak[d, h][:CHUNK]
        states = {dh: states[dh] * cur[dh][3] + ak[dh][CHUNK:] for dh in lanes}
    for d, h in lanes:
        s_ref[d, h] = states[d, h]


def _gdn(qkv, ba_t, a_log, dt_bias, *, batch, seq, ctx_len):
    n_all = seq // SCAN_ROWS
    n_ctx = ctx_len // SCAN_ROWS
    cpb = SCAN_ROWS // CHUNK

    def rows(b, s, rev):
        return b * n_all + _block_order(s, n_ctx, n_all, rev)

    def in_specs(rev):
        d = int(rev)
        return [
            pl.BlockSpec((SCAN_ROWS, MIX_W), lambda b, s: (rows(b, s, rev), 0)),
            pl.BlockSpec((SCAN_ROWS, MIX_W), lambda b, s: (rows(b, s, rev), 1)),
            pl.BlockSpec((SCAN_ROWS, MIX_W), lambda b, s: (rows(b, s, rev), 2)),
            pl.BlockSpec((None, cpb, 2 * N_HEADS, CHUNK), lambda b, s: (d, rows(b, s, rev), 0, 0)),
        ]

    vec = pl.BlockSpec((2, N_HEADS, 1), lambda b, s: (0, 0, 0))
    out = jax.ShapeDtypeStruct((batch * seq, MIX_W), F32)
    return pl.pallas_call(
        _gdn_kernel,
        out_shape=(out, out),
        grid=(batch, n_all),
        in_specs=in_specs(False) + in_specs(True) + [vec, vec],
        out_specs=(pl.BlockSpec((SCAN_ROWS, MIX_W), lambda b, s: (rows(b, s, False), 0)),
                   pl.BlockSpec((SCAN_ROWS, MIX_W), lambda b, s: (rows(b, s, True), 0))),
        scratch_shapes=[pltpu.VMEM((2, N_HEADS, HEAD_W, HEAD_W), F32)],
        compiler_params=_params(("parallel", "arbitrary")),
        name="gdn",
    )(qkv, qkv, qkv, ba_t, qkv, qkv, qkv, ba_t, a_log, dt_bias)


def _hgrn2_prepare(q_ref, f_ref, i_ref, lb_ref, rev):
    _, _, incl, _ = _chunk_masks(rev)
    tri = jnp.where(incl, 1.0, 0.0)
    row = lax.broadcasted_iota(jnp.int32, (CHUNK, 1), 0)
    n_sub = CHUNK // SUB
    n_chunks = q_ref.shape[0] // CHUNK

    def tile(ref, j, h):
        return ref[j * CHUNK:(j + 1) * CHUNK, h * HEAD_W:(h + 1) * HEAD_W]

    chains = [(j, h) for j in range(n_chunks) for h in range(N_HEADS)]
    qs_, ks_, lfs = [], [], []
    for j, h in chains:
        qr = tile(q_ref, j, h).astype(F32)
        lb = lb_ref[:, h * HEAD_W:(h + 1) * HEAD_W]
        f = 0.5 * (1.0 + lb) + (0.5 * (1.0 - lb)) * jnp.tanh(tile(f_ref, j, h))
        qs_.append(_silu(qr))
        ks_.append(1.0 - f)
        lfs.append(jnp.log(f))
    gcs = _dot_3x_many(tri, lfs)
    tots = [jnp.sum(lf, axis=0, keepdims=True) for lf in lfs]
    blocks = []
    for q, k, gc in zip(qs_, ks_, gcs):
        row_blocks = []
        for i in range(n_sub):
            mid = i * SUB + SUB // 2
            gref = gc[mid:mid + 1, :]
            qsc = q[i * SUB:(i + 1) * SUB, :] * jnp.exp(gc[i * SUB:(i + 1) * SUB, :] - gref)
            lo, hi = (i * SUB, CHUNK) if rev else (0, (i + 1) * SUB)
            parts = [jnp.zeros((lo, HEAD_W), F32), k[lo:hi, :] * jnp.exp(gref - gc[lo:hi, :]),
                     jnp.zeros((CHUNK - hi, HEAD_W), F32)]
            ksc = jnp.concatenate([x for x in parts if x.shape[0]], axis=0)
            row_blocks.append((qsc, ksc))
        blocks.append(row_blocks)
    scores = [jnp.where(incl, jnp.concatenate([_dot_nt(a, b) for a, b in rb], axis=0), 0.0) for rb in blocks]
    vs = [tile(i_ref, *c) for c in chains]
    local = [_dot(sc, v) for sc, v in zip(scores, vs)]
    incr = [_dot_tn(v, k * jnp.exp(tot - gc)) for v, k, gc, tot in zip(vs, ks_, gcs, tots)]
    return {c: (q * jnp.exp(gc), lo, inc, jnp.exp(tot))
            for c, q, gc, lo, inc, tot in zip(chains, qs_, gcs, local, incr, tots)}


def _hgrn2_kernel(qf_ref, ff_ref, if_ref, qr_ref, fr_ref, ir_ref, lb_ref, of_ref, or_ref, s_ref):
    @pl.when(pl.program_id(1) == 0)
    def _():
        s_ref[...] = jnp.zeros_like(s_ref)

    n_chunks = qf_ref.shape[0] // CHUNK
    pre = (_hgrn2_prepare(qf_ref, ff_ref, if_ref, lb_ref.at[0], False),
           _hgrn2_prepare(qr_ref, fr_ref, ir_ref, lb_ref.at[1], True))
    o_refs = (of_ref, or_ref)

    lanes = [(d, h) for d in range(2) for h in range(N_HEADS)]
    states = {dh: s_ref[dh[0], dh[1]] for dh in lanes}
    for t in range(n_chunks):
        js = (t, n_chunks - 1 - t)
        cur = {dh: pre[dh[0]][js[dh[0]], dh[1]] for dh in lanes}
        qs = {dh: _dot_nt(cur[dh][0], states[dh]) for dh in lanes}
        for d, h in lanes:
            j = js[d]
            o_refs[d][j * CHUNK:(j + 1) * CHUNK, h * HEAD_W:(h + 1) * HEAD_W] = qs[d, h] + cur[d, h][1]
        states = {dh: states[dh] * cur[dh][3] + cur[dh][2] for dh in lanes}
    for d, h in lanes:
        s_ref[d, h] = states[d, h]


def _hgrn2(ub, uf, lb, *, batch, seq, ctx_len):
    n_all = seq // SCAN_ROWS
    n_ctx = ctx_len // SCAN_ROWS

    def rows(b, s, rev):
        return b * n_all + _block_order(s, n_ctx, n_all, rev)

    def in_specs(rev):
        f_col = (UF_HG_FB if rev else UF_HG_FF) // MIX_W
        return [
            pl.BlockSpec((SCAN_ROWS, MIX_W), lambda b, s: (rows(b, s, rev), UB_HG_Q // MIX_W)),
            pl.BlockSpec((SCAN_ROWS, MIX_W), lambda b, s: (rows(b, s, rev), f_col)),
            pl.BlockSpec((SCAN_ROWS, MIX_W), lambda b, s: (rows(b, s, rev), UB_HG_I // MIX_W)),
        ]

    out = jax.ShapeDtypeStruct((batch * seq, MIX_W), F32)
    return pl.pallas_call(
        _hgrn2_kernel,
        out_shape=(out, out),
        grid=(batch, n_all),
        in_specs=in_specs(False) + in_specs(True) + [pl.BlockSpec((2, 1, MIX_W), lambda b, s: (0, 0, 0))],
        out_specs=(pl.BlockSpec((SCAN_ROWS, MIX_W), lambda b, s: (rows(b, s, False), 0)),
                   pl.BlockSpec((SCAN_ROWS, MIX_W), lambda b, s: (rows(b, s, True), 0))),
        scratch_shapes=[pltpu.VMEM((2, N_HEADS, HEAD_W, HEAD_W), F32)],
        compiler_params=_params(("parallel", "arbitrary")),
        name="hgrn2",
    )(ub, uf, ub, ub, uf, ub, lb)


LRU_GROUP = 8


def _group_scan(a, b, rev):
    n, w = a.shape
    a = a.reshape(n // LRU_GROUP, LRU_GROUP, w)
    b = b.reshape(n // LRU_GROUP, LRU_GROUP, w)
    sub = lax.broadcasted_iota(jnp.int32, (1, LRU_GROUP, 1), 1)
    s = 1
    while s < LRU_GROUP:
        shift, valid = (LRU_GROUP - s, sub < LRU_GROUP - s) if rev else (s, sub >= s)
        a_s, b_s = pltpu.roll(a, shift, 1), pltpu.roll(b, shift, 1)
        b = jnp.where(valid, a * b_s + b, b)
        a = jnp.where(valid, a * a_s, a)
        s *= 2
    return a, b


def _chain_groups(a, b, carry, rev, store):
    n = a.shape[0]
    for g in (range(n - 1, -1, -1) if rev else range(n)):
        hg = a[g] * carry + b[g]
        store(g, hg)
        carry = hg[0:1, :] if rev else hg[LRU_GROUP - 1:LRU_GROUP, :]
    return carry


def _gelu_tanh(x):
    return 0.5 * x * (1.0 + jnp.tanh(0.7978845608028654 * (x + 0.044715 * (x * x * x))))


def _lru_kernel(xb_ref, gb_ref, cw_ref, cb_ref, wa_ref, ba_ref, wx_ref, bx_ref, lam_ref, o_ref, xc_ref, hf_ref,
                hb_ref, *, ctx_len):
    n = xb_ref.shape[0]
    nblk = n // LRU_BLOCK
    nctx = ctx_len // LRU_BLOCK
    xc_ref[...] = _seg_conv(xb_ref[...].astype(F32), cw_ref, ctx_len) + cb_ref[...]

    half_rate = [(-0.5 * LRU_C * LOG2_E) * _softplus(-lam_ref[d]) for d in range(2)]

    def gates(xc, d):
        t_r = jnp.tanh(_dot(xc, wa_ref[d]) + ba_ref[d])
        ig = 0.5 * jnp.tanh(_dot(xc, wx_ref[d]) + bx_ref[d]) + 0.5
        a = jnp.exp2(t_r * half_rate[d] + half_rate[d])
        return a, jnp.sqrt(1.0 - a * a) * ig * xc

    def scan_block(blk, carry, d, out_ref):
        base = pl.multiple_of(blk * LRU_BLOCK, LRU_BLOCK)
        a, b = _group_scan(*gates(xc_ref[pl.ds(base, LRU_BLOCK), :], d), rev=bool(d))

        def store(g, hg):
            out_ref[pl.ds(pl.multiple_of(base + g * LRU_GROUP, LRU_GROUP), LRU_GROUP), :] = hg

        return _chain_groups(a, b, carry, bool(d), store)

    def step(i, carry):
        cf = scan_block(i, carry[0], 0, hf_ref)
        cb = scan_block(_block_order(i, nctx, nblk, True), carry[1], 1, hb_ref)
        return cf, cb

    zero = jnp.zeros((1, HEAD_W), F32)
    lax.fori_loop(0, nblk, step, (zero, zero))
    o_ref[...] = ((hf_ref[...] + hb_ref[...]) * _gelu_tanh(gb_ref[...].astype(F32))).astype(o_ref.dtype)


def _lru(ub, conv_w, conv_b, w_a, b_a, w_x, b_x, lam, *, batch, seq, ctx_len):
    vec = pl.BlockSpec((2, 1, HEAD_W), lambda b, h: (0, 0, h))
    mat = pl.BlockSpec((2, None, HEAD_W, HEAD_W), lambda b, h: (0, h, 0, 0))
    return pl.pallas_call(
        functools.partial(_lru_kernel, ctx_len=ctx_len),
        out_shape=jax.ShapeDtypeStruct((batch * seq, MIX_W), BF16),
        grid=(batch, N_HEADS),
        in_specs=[
            pl.BlockSpec((seq, HEAD_W), lambda b, h: (b, UB_LRU_X // HEAD_W + h)),
            pl.BlockSpec((seq, HEAD_W), lambda b, h: (b, UB_LRU_G // HEAD_W + h)),
            pl.BlockSpec((4, HEAD_W), lambda b, h: (0, h)),
            pl.BlockSpec((1, HEAD_W), lambda b, h: (0, h)),
            mat, vec, mat, vec, vec,
        ],
        out_specs=pl.BlockSpec((seq, HEAD_W), lambda b, h: (b, h)),
        scratch_shapes=[pltpu.VMEM((seq, HEAD_W), F32)] * 3,
        compiler_params=_params(("parallel", "parallel")),
        name="rglru",
    )(ub, ub, conv_w, conv_b, w_a, b_a, w_x, b_x, lam)


def _norm_rope(x, w, cos, sin, grp, scale):
    lane = lax.broadcasted_iota(jnp.int32, (1, 128), 1)
    first = (lane % (ATT_HEAD_DIM // 2)) < (ATT_HEAD_DIM // 4)
    ss = _dot_x3(x * x, grp)
    y = x * lax.rsqrt(ss * (1.0 / ATT_HEAD_DIM) + EPS) * w
    rot = jnp.where(first, pltpu.roll(y, 128 - ATT_HEAD_DIM // 4, 1), pltpu.roll(y, ATT_HEAD_DIM // 4, 1))
    return (y * cos + rot * sin) * scale


def _att_kernel(q_ref, k_ref, v_ref, cos_ref, sin_ref, qw_ref, kw_ref, grp_ref, o_ref, kt_ref, vb_ref, *, ctx_len):
    i = pl.program_id(1)
    grp = grp_ref[...]
    tq = q_ref.shape[0]

    @pl.when(i == 0)
    def _():
        for c in range(k_ref.shape[0] // tq):
            rows = slice(c * tq, (c + 1) * tq)
            kn = _norm_rope(k_ref[rows, :].astype(F32), kw_ref[...], cos_ref[rows, :], sin_ref[rows, :], grp, 1.0)
            kt_ref[:, rows] = kn.T.astype(kt_ref.dtype)
        vb_ref[...] = v_ref[...].astype(vb_ref.dtype)

    q_scale = (ATT_HEAD_DIM ** -0.5) * LOG2_E
    q_rows = pl.ds(pl.multiple_of(i * tq, tq), tq)
    cos, sin = cos_ref[q_rows, :], sin_ref[q_rows, :]
    qn = [_norm_rope(q_ref[:, s * 128:(s + 1) * 128].astype(F32), qw_ref[...], cos, sin, grp, q_scale).astype(BF16)
          for s in range(q_ref.shape[1] // 128)]

    def attend(n_keys):
        v = vb_ref[:n_keys, :]

        def scores(head):
            g = head // ATT_GROUP
            kt = kt_ref[g * ATT_HEAD_DIM:(g + 1) * ATT_HEAD_DIM, :n_keys]
            q = qn[head // 2][:, (head % 2) * ATT_HEAD_DIM:(head % 2 + 1) * ATT_HEAD_DIM]
            return jnp.dot(q, kt, preferred_element_type=F32)

        s_next = scores(0)
        for head in range(ATT_Q_HEADS):
            s = s_next
            if head + 1 < ATT_Q_HEADS:
                s_next = scores(head + 1)
            g = head // ATT_GROUP
            p = jnp.exp2(s - jnp.max(s, axis=-1, keepdims=True))
            den = jnp.sum(p, axis=-1, keepdims=True)
            pv = jnp.dot(p.astype(BF16), v, preferred_element_type=F32)
            o_ref[:, head * ATT_HEAD_DIM:(head + 1) * ATT_HEAD_DIM] = (
                pv[:, g * ATT_HEAD_DIM:(g + 1) * ATT_HEAD_DIM] / den).astype(o_ref.dtype)

    @pl.when(i == 0)
    def _():
        attend(ctx_len)

    @pl.when(i != 0)
    def _():
        attend(kt_ref.shape[1])


def _attention(ub, uf, cos, sin, qw, kw, grp, *, batch, seq, ctx_len):
    tiles = seq // ATT_TQ
    whole = lambda shape: pl.BlockSpec(shape, lambda b, i: (0, 0))
    return pl.pallas_call(
        functools.partial(_att_kernel, ctx_len=ctx_len),
        out_shape=jax.ShapeDtypeStruct((batch * seq, ATT_QW), BF16),
        grid=(batch, tiles),
        in_specs=[
            pl.BlockSpec((ATT_TQ, ATT_QW), lambda b, i: (b * tiles + i, UB_ATT_Q // ATT_QW)),
            pl.BlockSpec((seq, ATT_KW), lambda b, i: (b, UF_ATT_K // ATT_KW)),
            pl.BlockSpec((seq, ATT_KW), lambda b, i: (b, UF_ATT_V // ATT_KW)),
            whole((seq, 128)), whole((seq, 128)), whole((1, 128)), whole((1, 128)), whole((128, 128)),
        ],
        out_specs=pl.BlockSpec((ATT_TQ, ATT_QW), lambda b, i: (b * tiles + i, 0)),
        scratch_shapes=[pltpu.VMEM((ATT_KW, seq), BF16), pltpu.VMEM((seq, ATT_KW), BF16)],
        compiler_params=_params(("parallel", "arbitrary")),
        name="attention",
    )(ub, uf, uf, cos, sin, qw, kw, grp)


def _merge_kernel(oaf_ref, oab_ref, obf_ref, obb_ref, ga_ref, gb_ref, yc_ref, yd_ref, gate_ref, h_ref, modc_ref,
                  modl_ref, dnw_ref, hgw_ref, wb_ref, wo_ref, out_ref, *, tiles_per_batch):
    def gated(of_ref, ob_ref, g_ref, nw_ref):
        o = of_ref[...] + ob_ref[...]
        ys = [_rms_rows(o[:, h * HEAD_W:(h + 1) * HEAD_W], nw_ref[...]) for h in range(N_HEADS)]
        g = g_ref[...].astype(F32)
        return jnp.concatenate(ys, axis=1) * _silu(g)

    ys = (gated(oaf_ref, oab_ref, ga_ref, dnw_ref), gated(obf_ref, obb_ref, gb_ref, hgw_ref), yc_ref[...], yd_ref[...])
    acc = None
    for b in range(N_BRANCH):
        gate = gate_ref[:, b * D_MODEL:(b + 1) * D_MODEL].astype(F32)
        term = (1.0 + jnp.tanh(gate)) * _dot(ys[b], wb_ref[b])
        acc = term if acc is None else acc + term
    is_ctx = (pl.program_id(0) % tiles_per_batch) == 0
    out_ref[...] = h_ref[...] + _pick_mod(modc_ref, modl_ref, 2, is_ctx) * _dot(acc, wo_ref[...])


def _merge(oaf, oab, obf, obb, ub, yc, yd, h, modc, modl, dnw, hgw, wb, wo, *, seq, ctx_len):
    rows = h.shape[0]
    tm = ctx_len
    tiles_per_batch = seq // tm
    mix = lambda c: pl.BlockSpec((tm, MIX_W), lambda i: (i, c // MIX_W))
    return pl.pallas_call(
        functools.partial(_merge_kernel, tiles_per_batch=tiles_per_batch),
        out_shape=jax.ShapeDtypeStruct((rows, D_MODEL), F32),
        grid=(rows // tm,),
        in_specs=[
            mix(0), mix(0), mix(0), mix(0), mix(UB_DN_G), mix(UB_HG_G), mix(0), mix(0),
            pl.BlockSpec((tm, N_BRANCH * D_MODEL), lambda i: (i, 0)),
            pl.BlockSpec((tm, D_MODEL), lambda i: (i, 0)),
            pl.BlockSpec((6, D_MODEL), lambda i: (0, 0)),
            pl.BlockSpec((None, 6, D_MODEL), lambda i: (i // tiles_per_batch, 0, 0)),
            pl.BlockSpec((1, HEAD_W), lambda i: (0, 0)),
            pl.BlockSpec((1, HEAD_W), lambda i: (0, 0)),
            pl.BlockSpec((N_BRANCH, MIX_W, D_MODEL), lambda i: (0, 0, 0)),
            pl.BlockSpec((D_MODEL, D_MODEL), lambda i: (0, 0)),
        ],
        out_specs=pl.BlockSpec((tm, D_MODEL), lambda i: (i, 0)),
        compiler_params=_params(("parallel",)),
        name="merge",
    )(oaf, oab, obf, obb, ub, ub, yc, yd, ub, h, modc, modl, dnw, hgw, wb, wo)


def _mlp_kernel(h_ref, modc_ref, modl_ref, nw_ref, w1_ref, w2_ref, o_ref, z_ref, acc_ref,
                *, tm, tiles_per_batch, ctx_len):
    j = pl.program_id(1)
    row = (pl.program_id(0) % tiles_per_batch) * tm + lax.broadcasted_iota(jnp.int32, (tm, 1), 0)
    is_ctx = row < ctx_len

    @pl.when(j == 0)
    def _():
        y = _rms_rows(h_ref[...], nw_ref[...])
        shift = _pick_mod(modc_ref, modl_ref, 3, is_ctx)
        scale = _pick_mod(modc_ref, modl_ref, 4, is_ctx)
        z_ref[...] = (y * (1.0 + scale) + shift).astype(BF16)
        acc_ref[...] = jnp.zeros_like(acc_ref)

    a = jnp.maximum(jnp.dot(z_ref[...], w1_ref[...], preferred_element_type=F32), 0.0)
    acc_ref[...] += _dot(a * a, w2_ref[...])

    @pl.when(j == pl.num_programs(1) - 1)
    def _():
        o_ref[...] = h_ref[...] + _pick_mod(modc_ref, modl_ref, 5, is_ctx) * acc_ref[...]


def _mlp(h, modc, modl, nw, w1, w2, *, seq, ctx_len):
    rows = h.shape[0]
    tiles_per_batch = 4
    tm = seq // tiles_per_batch
    kern = functools.partial(_mlp_kernel, tm=tm, tiles_per_batch=tiles_per_batch, ctx_len=ctx_len)
    return pl.pallas_call(
        kern,
        out_shape=jax.ShapeDtypeStruct((rows, D_MODEL), F32),
        grid=(rows // tm, D_FF // FF_BLOCK),
        in_specs=[
            pl.BlockSpec((tm, D_MODEL), lambda i, j: (i, 0)),
            pl.BlockSpec((6, D_MODEL), lambda i, j: (0, 0)),
            pl.BlockSpec((None, 6, D_MODEL), lambda i, j: (i // tiles_per_batch, 0, 0)),
            pl.BlockSpec((1, D_MODEL), lambda i, j: (0, 0)),
            pl.BlockSpec((D_MODEL, FF_BLOCK), lambda i, j: (0, j)),
            pl.BlockSpec((FF_BLOCK, D_MODEL), lambda i, j: (j, 0)),
        ],
        out_specs=pl.BlockSpec((tm, D_MODEL), lambda i, j: (i, 0)),
        scratch_shapes=[pltpu.VMEM((tm, D_MODEL), BF16), pltpu.VMEM((tm, D_MODEL), F32)],
        compiler_params=_params(("parallel", "arbitrary"), vmem=VMEM_LIMIT_V7X + 8 * 1024 * 1024),
        name="mlp",
    )(h, modc, modl, nw, w1, w2)


def _regroup_w_in(w_in):
    o = np.cumsum([0, 512, 512, 512, 512, 8, 8, 512, 512, 512, 512, 512, 512, 512, 512, 128, 128, 4096])
    cols = lambda a, b: w_in[..., o[a]:o[b]]
    wb = jnp.concatenate([0.5 * cols(16, 17), cols(6, 7), cols(9, 11), cols(0, 3), cols(11, 13), cols(3, 4),
                          cols(13, 14)], axis=-1)
    wf = jnp.concatenate([0.5 * cols(7, 9), cols(14, 16), cols(4, 6)], axis=-1)
    pad = lambda w, n: jnp.pad(w, ((0, 0), (0, 0), (0, n - w.shape[-1]))).astype(BF16)
    return pad(wb, N_UB), pad(wf, N_UF)


def _rope_tables(t_len, ctx_len):
    rows = t_len // GRID_W
    row_id = jnp.repeat(jnp.arange(rows), GRID_W).astype(F32)
    col_id = jnp.tile(jnp.arange(GRID_W), rows).astype(F32)
    axis_dim = ATT_HEAD_DIM // 2
    inv = ROPE_THETA ** (-jnp.arange(0, axis_dim, 2, dtype=F32) / axis_dim)
    ang = jnp.stack([row_id[:, None] * inv, col_id[:, None] * inv], axis=1)
    cos, sin = jnp.cos(ang), jnp.sin(ang)
    cos_h = jnp.concatenate([cos, cos], axis=-1).reshape(t_len, ATT_HEAD_DIM)
    sin_h = jnp.concatenate([-sin, sin], axis=-1).reshape(t_len, ATT_HEAD_DIM)
    cos_t = jnp.concatenate([jnp.ones((ctx_len, ATT_HEAD_DIM), F32), cos_h], axis=0)
    sin_t = jnp.concatenate([jnp.zeros((ctx_len, ATT_HEAD_DIM), F32), sin_h], axis=0)
    return jnp.tile(cos_t, (1, 2)), jnp.tile(sin_t, (1, 2))


def _hgrn2_lower_bounds(p):
    sm = jax.nn.softmax(p.astype(F32), axis=1)
    cs = jnp.cumsum(sm, axis=1)
    return cs - cs[:, :1]


def kernel(x, c, ctx, c_ctx, mod_w, mod_b, norm1_w, norm2_w, w_in, dn_conv_w, dn_a_log, dn_dt_bias, dn_norm_w,
           hg_lower_bounds, hg_norm_w, lru_conv_w, lru_conv_b, lru_w_a, lru_b_a, lru_w_x, lru_b_x, lru_lambda,
           att_q_norm_w, att_k_norm_w, w_branch, w_out, mlp_w1, mlp_w2):
    batch, t_len, _ = x.shape
    ctx_len = ctx.shape[1]
    depth = mod_w.shape[0]
    seq = ctx_len + t_len
    n_chunks = seq // CHUNK
    assert ctx_len % SCAN_ROWS == 0 and t_len % SCAN_ROWS == 0 and ctx_len == ATT_TQ
    dims = dict(batch=batch, seq=seq, ctx_len=ctx_len)

    w_ub, w_uf = _regroup_w_in(w_in)
    w_branch_b, w_out_b = (0.5 * w_branch).astype(BF16), w_out.astype(BF16)
    w1_b, w2_b = mlp_w1.astype(BF16), mlp_w2.astype(BF16)
    lru_wa_b, lru_wx_b = (0.5 * lru_w_a).astype(BF16), (0.5 * lru_w_x).astype(BF16)
    lb_all = _hgrn2_lower_bounds(hg_lower_bounds)
    cos_t, sin_t = _rope_tables(t_len, ctx_len)
    grp = jnp.asarray((np.arange(128)[:, None] // ATT_HEAD_DIM) == (np.arange(128)[None, :] // ATT_HEAD_DIM), F32)

    n_mod_rows = batch + 8
    cc = jnp.zeros((n_mod_rows, D_MODEL), F32).at[:batch].set(c).at[batch].set(c_ctx)
    mods = _modulations(cc, mod_w.astype(BF16), mod_b)

    h = jnp.concatenate([ctx, x], axis=1).reshape(batch * seq, D_MODEL)
    for l in range(depth):
        modl = mods[l, :batch].reshape(batch, 6, D_MODEL)
        modc = mods[l, batch].reshape(6, D_MODEL)
        nw1 = norm1_w[l].reshape(1, D_MODEL)
        ub, uf = _inproj(h, modc, modl, nw1, w_ub[l], w_uf[l], seq=seq, ctx_len=ctx_len)

        qkv = _dn_prep(ub, dn_conv_w[l], **dims)
        ba = uf[:, UF_DN_BA:UF_DN_BA + 4 * N_HEADS].reshape(batch * n_chunks, CHUNK, 2, 2, N_HEADS)
        ba_t = jnp.transpose(ba, (3, 0, 2, 4, 1)).reshape(2, batch * n_chunks, 2 * N_HEADS, CHUNK)
        oa = _gdn(qkv, ba_t, dn_a_log[l].reshape(2, N_HEADS, 1), dn_dt_bias[l].reshape(2, N_HEADS, 1), **dims)

        ob = _hgrn2(ub, uf, lb_all[:, l].reshape(2, 1, MIX_W), **dims)

        yc = _lru(ub, lru_conv_w[l], lru_conv_b[l].reshape(1, MIX_W), lru_wa_b[l],
                  (0.5 * lru_b_a[l]).reshape(2, 1, MIX_W), lru_wx_b[l], (0.5 * lru_b_x[l]).reshape(2, 1, MIX_W),
                  lru_lambda[l].reshape(2, 1, MIX_W), **dims)

        yd = _attention(ub, uf, cos_t, sin_t, jnp.tile(att_q_norm_w[l], 2).reshape(1, 128),
                        jnp.tile(att_k_norm_w[l], 2).reshape(1, 128), grp, **dims)

        h = _merge(oa[0], oa[1], ob[0], ob[1], ub, yc, yd, h, modc, modl, dn_norm_w[l].reshape(1, HEAD_W),
                   hg_norm_w[l].reshape(1, HEAD_W), w_branch_b[l], w_out_b[l], seq=seq, ctx_len=ctx_len)
        h = _mlp(h, modc, modl, norm2_w[l].reshape(1, D_MODEL), w1_b[l], w2_b[l], seq=seq, ctx_len=ctx_len)

    return h.reshape(batch, seq, D_MODEL)[:, ctx_len:]
```

```python
import functools

import jax
import jax.numpy as jnp
import numpy as np
from jax import lax
from jax.experimental import pallas as pl
from jax.experimental.pallas import tpu as pltpu

F32 = jnp.float32
BF16 = jnp.bfloat16

EPS = 1e-6
D_MODEL = 1024
GRID_W = 64
N_HEADS = 4
HEAD_W = 128
MIX_W = N_HEADS * HEAD_W
CHUNK = 64
SUB = 16
SCAN_ROWS = 256
LRU_C = 8.0
LRU_BLOCK = 256
ATT_Q_HEADS = 8
ATT_KV_HEADS = 2
ATT_HEAD_DIM = 64
ATT_GROUP = ATT_Q_HEADS // ATT_KV_HEADS
ATT_QW = ATT_Q_HEADS * ATT_HEAD_DIM
ATT_KW = ATT_KV_HEADS * ATT_HEAD_DIM
ATT_TQ = 256
ROPE_THETA = 10000.0
LOG2_E = 1.4426950408889634
N_BRANCH = 4
D_FF = 4 * D_MODEL
FF_BLOCK = 2048

UB_GATE = 0
UB_HG_Q, UB_HG_I, UB_HG_G = 4096, 4608, 5120
UB_DN_Q = 5632
UB_LRU_X, UB_LRU_G = 7168, 7680
UB_DN_G = 8192
UB_ATT_Q = 8704
TN_UB = 2304
N_UB = 4 * TN_UB
UF_HG_FF, UF_HG_FB, UF_ATT_K, UF_ATT_V, UF_DN_BA = 0, 512, 1024, 1152, 1280
N_UF = 1536

VMEM_LIMIT_V7X = 48 * 1024 * 1024


def _params(sem, vmem=VMEM_LIMIT_V7X):
    return pltpu.CompilerParams(dimension_semantics=sem, vmem_limit_bytes=vmem)


def _sigmoid(x):
    return 0.5 * jnp.tanh(0.5 * x) + 0.5


def _silu(x):
    h = 0.5 * x
    return h * (1.0 + jnp.tanh(h))


def _softplus(x):
    return jnp.maximum(x, 0.0) + jnp.log1p(jnp.exp(-jnp.abs(x)))


def _dot(a, b):
    return jnp.dot(a.astype(BF16), b.astype(BF16), preferred_element_type=F32)


def _dot_nt(a, b):
    return lax.dot_general(a.astype(BF16), b.astype(BF16), (((1,), (1,)), ((), ())),
                           preferred_element_type=F32)


def _dot_tn(a, b):
    return lax.dot_general(a.astype(BF16), b.astype(BF16), (((0,), (0,)), ((), ())),
                           preferred_element_type=F32)


def _split3(x):
    hi = x.astype(BF16)
    r1 = x - hi.astype(F32)
    mid = r1.astype(BF16)
    lo = (r1 - mid.astype(F32)).astype(BF16)
    return hi, mid, lo


def _dot_x3(x, m):
    m = m.astype(BF16)
    hi, mid, lo = _split3(x)
    out = jnp.dot(hi, m, preferred_element_type=F32)
    out = out + jnp.dot(mid, m, preferred_element_type=F32)
    return out + jnp.dot(lo, m, preferred_element_type=F32)


def _dot_3x_many(m, xs):
    w = xs[0].shape[1]
    pieces = [_split3(x) for x in xs]
    cols = [p[i] for i in range(3) for p in pieces]
    wide = jnp.dot(m.astype(BF16), jnp.concatenate(cols, axis=1), preferred_element_type=F32)
    n = len(xs)
    part = lambda p, i: wide[:, (p * n + i) * w:(p * n + i + 1) * w]
    return [part(0, i) + part(1, i) + part(2, i) for i in range(n)]


def _rms_rows(x, w):
    return x * lax.rsqrt(jnp.mean(x * x, axis=-1, keepdims=True) + EPS) * w


def _mod_kernel(c_ref, w_ref, b_ref, o_ref):
    c = c_ref[...]
    o_ref[...] = _dot(_silu(c), w_ref[...]) + b_ref[...]


def _modulations(cc, mod_w, mod_b):
    depth = mod_w.shape[0]
    rows = cc.shape[0]
    n_out = mod_w.shape[2]
    return pl.pallas_call(
        _mod_kernel,
        out_shape=jax.ShapeDtypeStruct((depth, rows, n_out), F32),
        grid=(depth, n_out // D_MODEL),
        in_specs=[
            pl.BlockSpec((rows, D_MODEL), lambda l, j: (0, 0)),
            pl.BlockSpec((None, D_MODEL, D_MODEL), lambda l, j: (l, 0, j)),
            pl.BlockSpec((None, 1, D_MODEL), lambda l, j: (l, 0, j)),
        ],
        out_specs=pl.BlockSpec((None, rows, D_MODEL), lambda l, j: (l, 0, j)),
        compiler_params=_params(("parallel", "parallel")),
        name="modulations",
    )(cc, mod_w, mod_b.reshape(depth, 1, n_out))


def _pick_mod(modc_ref, modl_ref, idx, is_ctx):
    return jnp.where(is_ctx, modc_ref[idx:idx + 1, :], modl_ref[idx:idx + 1, :])


def _inproj_kernel(h_ref, modc_ref, modl_ref, nw_ref, wb_ref, wf_ref, ob_ref, of_ref, xn_ref,
                   *, tm, tiles_per_batch, ctx_len):
    j = pl.program_id(1)

    @pl.when(j == 0)
    def _():
        y = _rms_rows(h_ref[...], nw_ref[...])
        row = (pl.program_id(0) % tiles_per_batch) * tm + lax.broadcasted_iota(jnp.int32, (tm, 1), 0)
        is_ctx = row < ctx_len
        shift = _pick_mod(modc_ref, modl_ref, 0, is_ctx)
        scale = _pick_mod(modc_ref, modl_ref, 1, is_ctx)
        xn_ref[...] = (y * (1.0 + scale) + shift).astype(BF16)

    ob_ref[...] = jnp.dot(xn_ref[...], wb_ref[...], preferred_element_type=F32).astype(ob_ref.dtype)

    @pl.when(j == pl.num_programs(1) - 1)
    def _():
        of_ref[...] = jnp.dot(xn_ref[...], wf_ref[...], preferred_element_type=F32)


def _inproj(h, modc, modl, nw, w_b, w_f, *, seq, ctx_len):
    rows = h.shape[0]
    tiles_per_batch = 4
    tm = seq // tiles_per_batch
    kern = functools.partial(_inproj_kernel, tm=tm, tiles_per_batch=tiles_per_batch, ctx_len=ctx_len)
    return pl.pallas_call(
        kern,
        out_shape=(jax.ShapeDtypeStruct((rows, N_UB), BF16), jax.ShapeDtypeStruct((rows, N_UF), F32)),
        grid=(rows // tm, N_UB // TN_UB),
        in_specs=[
            pl.BlockSpec((tm, D_MODEL), lambda i, j: (i, 0)),
            pl.BlockSpec((6, D_MODEL), lambda i, j: (0, 0)),
            pl.BlockSpec((None, 6, D_MODEL), lambda i, j: (i // tiles_per_batch, 0, 0)),
            pl.BlockSpec((1, D_MODEL), lambda i, j: (0, 0)),
            pl.BlockSpec((D_MODEL, TN_UB), lambda i, j: (0, j)),
            pl.BlockSpec((D_MODEL, N_UF), lambda i, j: (0, 0)),
        ],
        out_specs=(pl.BlockSpec((tm, TN_UB), lambda i, j: (i, j)),
                   pl.BlockSpec((tm, N_UF), lambda i, j: (i, 0))),
        scratch_shapes=[pltpu.VMEM((tm, D_MODEL), BF16)],
        compiler_params=_params(("parallel", "arbitrary"), vmem=VMEM_LIMIT_V7X + 8 * 1024 * 1024),
        name="inproj",
    )(h, modc, modl, nw, w_b, w_f)


def _seg_conv(x, w_ref, ctx_len):
    n = x.shape[0]
    row = lax.broadcasted_iota(jnp.int32, (n, 1), 0)
    lo = jnp.where(row >= ctx_len, ctx_len, 0)
    hi = jnp.where(row >= ctx_len, n, ctx_len)

    def tap(k):
        tk = row + k
        valid = jnp.logical_and(tk >= lo, tk < hi)
        return jnp.where(valid, pltpu.roll(x, (-k) % n, 0), 0.0)

    return (tap(-1) * w_ref[0:1, :] + x * w_ref[1:2, :] + tap(1) * w_ref[2:3, :] + tap(2) * w_ref[3:4, :])


def _dn_prep_kernel(u_ref, w_ref, o_ref, *, ctx_len):
    j = pl.program_id(1)
    y = _seg_conv(u_ref[...].astype(F32), w_ref, ctx_len)
    y = _silu(y)
    n = lax.rsqrt(jnp.sum(y * y, axis=-1, keepdims=True) + EPS)
    fac = jnp.where(j < N_HEADS, n * (HEAD_W ** -0.5), jnp.where(j < 2 * N_HEADS, n, 1.0))
    o_ref[...] = (y * fac).astype(o_ref.dtype)


def _dn_prep(ub, conv_w, *, batch, seq, ctx_len):
    nblk = 3 * N_HEADS
    return pl.pallas_call(
        functools.partial(_dn_prep_kernel, ctx_len=ctx_len),
        out_shape=jax.ShapeDtypeStruct((batch * seq, 3 * MIX_W), BF16),
        grid=(batch, nblk),
        in_specs=[
            pl.BlockSpec((seq, HEAD_W), lambda b, j: (b, UB_DN_Q // HEAD_W + j)),
            pl.BlockSpec((4, HEAD_W), lambda b, j: (0, j)),
        ],
        out_specs=pl.BlockSpec((seq, HEAD_W), lambda b, j: (b, j)),
        compiler_params=_params(("parallel", "parallel")),
        name="dn_prep",
    )(ub, conv_w)


def _block_order(s, n_ctx, n_all, rev):
    if not rev:
        return s
    return jnp.where(s < n_ctx, n_ctx - 1 - s, n_all + n_ctx - 1 - s)


def _chunk_masks(rev):
    ii = lax.broadcasted_iota(jnp.int32, (CHUNK, CHUNK), 0)
    jj = lax.broadcasted_iota(jnp.int32, (CHUNK, CHUNK), 1)
    incl = (ii <= jj) if rev else (ii >= jj)
    incl_t = (ii >= jj) if rev else (ii <= jj)
    return ii, jj, incl, incl_t


def _unit_tri_inverse(a_all, eye, bd):
    ad = [jnp.where(bd, a, 0.0) for a in a_all]
    ao = [a - d for a, d in zip(a_all, ad)]
    p = [-d for d in ad]
    dinv = [eye + x for x in p]
    for _ in range(3):
        p = [_dot(x, x) for x in p]
        dinv = [d + _dot(d, x) for d, x in zip(dinv, p)]
    m = [-_dot(d, o) for d, o in zip(dinv, ao)]
    mm = [_dot(x, x) for x in m]
    t = [eye + x for x in m]
    t = [x + _dot(x, y) for x, y in zip(t, mm)]
    return [_dot(x, d) for x, d in zip(t, dinv)]


def _gdn_prepare(q_ref, k_ref, v_ref, ba_ref, alog, dtb, rev):
    ii, jj, incl, incl_t = _chunk_masks(rev)
    is_eye = ii == jj
    eye = jnp.where(is_eye, 1.0, 0.0)
    tri = jnp.where(incl, 1.0, 0.0)
    tri_t = jnp.where(incl_t, 1.0, 0.0)
    bd = (ii // SUB) == (jj // SUB)
    n_chunks = q_ref.shape[0] // CHUNK

    alog_neg = -jnp.exp(alog)
    chains = [(j, h) for j in range(n_chunks) for h in range(N_HEADS)]
    vec = {}
    for j in range(n_chunks):
        ba = ba_ref[j]
        beta_r = _sigmoid(ba[0:N_HEADS])
        g_r = alog_neg * _softplus(ba[N_HEADS:2 * N_HEADS] + dtb)
        for h in range(N_HEADS):
            gr = g_r[h:h + 1]
            beta_c = jnp.sum(eye * beta_r[h:h + 1], axis=1, keepdims=True)
            g_c = jnp.sum(eye * gr, axis=1, keepdims=True)
            gam_c = jnp.sum(tri * gr, axis=1, keepdims=True)
            gam_r = jnp.sum(tri_t * g_c, axis=0, keepdims=True)
            tot = jnp.sum(gr, axis=1, keepdims=True)
            dec_i = jnp.where(incl, jnp.exp(gam_c - gam_r), 0.0)
            vec[j, h] = (beta_c, gam_c, tot, dec_i)

    def tile(ref, j, h):
        return ref[j * CHUNK:(j + 1) * CHUNK, h * HEAD_W:(h + 1) * HEAD_W]

    k16 = [tile(k_ref, *c) for c in chains]
    q16 = [tile(q_ref, *c) for c in chains]
    ks = [k.astype(F32) for k in k16]
    kbs = [k * vec[c][0] for k, c in zip(ks, chains)]
    kq = [_dot_nt(jnp.concatenate([kb.astype(BF16), q], axis=0), k) for kb, q, k in zip(kbs, q16, k16)]
    a_low = [jnp.where(is_eye, 0.0, x[:CHUNK] * vec[c][3]) for x, c in zip(kq, chains)]
    a_qk = [x[CHUNK:] * vec[c][3] for x, c in zip(kq, chains)]
    t_inv = _unit_tri_inverse(a_low, eye, bd)
    egc = [jnp.exp(vec[c][1]) for c in chains]
    sols = [_dot(t, jnp.concatenate([tile(v_ref, *c).astype(F32) * vec[c][0], kb * e], axis=1))
            for t, kb, e, c in zip(t_inv, kbs, egc, chains)]
    pre = {}
    for c, sol, aq, q, k, e in zip(chains, sols, a_qk, q16, ks, egc):
        q_dec = q.astype(F32) * e
        k_dec_t = (k * jnp.exp(vec[c][2] - vec[c][1])).T
        pre[c] = (sol[:, :HEAD_W], jnp.concatenate([sol[:, HEAD_W:], q_dec], axis=0).astype(BF16),
                  jnp.concatenate([aq, k_dec_t], axis=0).astype(BF16), jnp.exp(vec[c][2]))
    return pre


def _gdn_kernel(qf_ref, kf_ref, vf_ref, baf_ref, qr_ref, kr_ref, vr_ref, bar_ref, alog_ref, dtb_ref,
                of_ref, or_ref, s_ref):
    @pl.when(pl.program_id(1) == 0)
    def _():
        s_ref[...] = jnp.zeros_like(s_ref)

    n_chunks = qf_ref.shape[0] // CHUNK
    pre = (_gdn_prepare(qf_ref, kf_ref, vf_ref, baf_ref, alog_ref[0], dtb_ref[0], False),
           _gdn_prepare(qr_ref, kr_ref, vr_ref, bar_ref, alog_ref[1], dtb_ref[1], True))
    o_refs = (of_ref, or_ref)

    lanes = [(d, h) for d in range(2) for h in range(N_HEADS)]
    states = {dh: s_ref[dh[0], dh[1]] for dh in lanes}
    for t in range(n_chunks):
        js = (t, n_chunks - 1 - t)
        cur = {dh: pre[dh[0]][js[dh[0]], dh[1]] for dh in lanes}
        wq = {dh: _dot(cur[dh][1], states[dh]) for dh in lanes}
        v_new = {dh: cur[dh][0] - wq[dh][:CHUNK] for dh in lanes}
        ak = {dh: _dot(cur[dh][2], v_new[dh]) for dh in lanes}
        for d, h in lanes:
            j = js[d]
            o_refs[d][j * CHUNK:(j + 1) * CHUNK, h * HEAD_W:(h + 1) * HEAD_W] = wq[d, h][CHUNK:] + ak[d, h][:CHUNK]
        states = {dh: states[dh] * cur[dh][3] + ak[dh][CHUNK:] for dh in lanes}
    for d, h in lanes:
        s_ref[d, h] = states[d, h]


def _gdn(qkv, ba_t, a_log, dt_bias, *, batch, seq, ctx_len):
    n_all = seq // SCAN_ROWS
    n_ctx = ctx_len // SCAN_ROWS
    cpb = SCAN_ROWS // CHUNK

    def rows(b, s, rev):
        return b * n_all + _block_order(s, n_ctx, n_all, rev)

    def in_specs(rev):
        d = int(rev)
        return [
            pl.BlockSpec((SCAN_ROWS, MIX_W), lambda b, s: (rows(b, s, rev), 0)),
            pl.BlockSpec((SCAN_ROWS, MIX_W), lambda b, s: (rows(b, s, rev), 1)),
            pl.BlockSpec((SCAN_ROWS, MIX_W), lambda b, s: (rows(b, s, rev), 2)),
            pl.BlockSpec((None, cpb, 2 * N_HEADS, CHUNK), lambda b, s: (d, rows(b, s, rev), 0, 0)),
        ]

    vec = pl.BlockSpec((2, N_HEADS, 1), lambda b, s: (0, 0, 0))
    out = jax.ShapeDtypeStruct((batch * seq, MIX_W), F32)
    return pl.pallas_call(
        _gdn_kernel,
        out_shape=(out, out),
        grid=(batch, n_all),
        in_specs=in_specs(False) + in_specs(True) + [vec, vec],
        out_specs=(pl.BlockSpec((SCAN_ROWS, MIX_W), lambda b, s: (rows(b, s, False), 0)),
                   pl.BlockSpec((SCAN_ROWS, MIX_W), lambda b, s: (rows(b, s, True), 0))),
        scratch_shapes=[pltpu.VMEM((2, N_HEADS, HEAD_W, HEAD_W), F32)],
        compiler_params=_params(("parallel", "arbitrary")),
        name="gdn",
    )(qkv, qkv, qkv, ba_t, qkv, qkv, qkv, ba_t, a_log, dt_bias)


def _hgrn2_prepare(q_ref, f_ref, i_ref, lb_ref, rev):
    _, _, incl, _ = _chunk_masks(rev)
    tri = jnp.where(incl, 1.0, 0.0)
    row = lax.broadcasted_iota(jnp.int32, (CHUNK, 1), 0)
    n_sub = CHUNK // SUB
    n_chunks = q_ref.shape[0] // CHUNK

    def tile(ref, j, h):
        return ref[j * CHUNK:(j + 1) * CHUNK, h * HEAD_W:(h + 1) * HEAD_W]

    chains = [(j, h) for j in range(n_chunks) for h in range(N_HEADS)]
    qs_, ks_, lfs = [], [], []
    for j, h in chains:
        qr = tile(q_ref, j, h).astype(F32)
        lb = lb_ref[:, h * HEAD_W:(h + 1) * HEAD_W]
        f = 0.5 * (1.0 + lb) + (0.5 * (1.0 - lb)) * jnp.tanh(tile(f_ref, j, h))
        qs_.append(_silu(qr))
        ks_.append(1.0 - f)
        lfs.append(jnp.log(f))
    gcs = _dot_3x_many(tri, lfs)
    tots = [jnp.sum(lf, axis=0, keepdims=True) for lf in lfs]
    blocks = []
    for q, k, gc in zip(qs_, ks_, gcs):
        row_blocks = []
        for i in range(n_sub):
            mid = i * SUB + SUB // 2
            gref = gc[mid:mid + 1, :]
            qsc = q[i * SUB:(i + 1) * SUB, :] * jnp.exp(gc[i * SUB:(i + 1) * SUB, :] - gref)
            lo, hi = (i * SUB, CHUNK) if rev else (0, (i + 1) * SUB)
            parts = [jnp.zeros((lo, HEAD_W), F32), k[lo:hi, :] * jnp.exp(gref - gc[lo:hi, :]),
                     jnp.zeros((CHUNK - hi, HEAD_W), F32)]
            ksc = jnp.concatenate([x for x in parts if x.shape[0]], axis=0)
            row_blocks.append((qsc, ksc))
        blocks.append(row_blocks)
    scores = [jnp.where(incl, jnp.concatenate([_dot_nt(a, b) for a, b in rb], axis=0), 0.0) for rb in blocks]
    vs = [tile(i_ref, *c) for c in chains]
    local = [_dot(sc, v) for sc, v in zip(scores, vs)]
    incr = [_dot_tn(v, k * jnp.exp(tot - gc)) for v, k, gc, tot in zip(vs, ks_, gcs, tots)]
    return {c: (q * jnp.exp(gc), lo, inc, jnp.exp(tot))
            for c, q, gc, lo, inc, tot in zip(chains, qs_, gcs, local, incr, tots)}


def _hgrn2_kernel(qf_ref, ff_ref, if_ref, qr_ref, fr_ref, ir_ref, lb_ref, of_ref, or_ref, s_ref):
    @pl.when(pl.program_id(1) == 0)
    def _():
        s_ref[...] = jnp.zeros_like(s_ref)

    n_chunks = qf_ref.shape[0] // CHUNK
    pre = (_hgrn2_prepare(qf_ref, ff_ref, if_ref, lb_ref.at[0], False),
           _hgrn2_prepare(qr_ref, fr_ref, ir_ref, lb_ref.at[1], True))
    o_refs = (of_ref, or_ref)

    lanes = [(d, h) for d in range(2) for h in range(N_HEADS)]
    states = {dh: s_ref[dh[0], dh[1]] for dh in lanes}
    for t in range(n_chunks):
        js = (t, n_chunks - 1 - t)
        cur = {dh: pre[dh[0]][js[dh[0]], dh[1]] for dh in lanes}
        qs = {dh: _dot_nt(cur[dh][0], states[dh]) for dh in lanes}
        for d, h in lanes:
            j = js[d]
            o_refs[d][j * CHUNK:(j + 1) * CHUNK, h * HEAD_W:(h + 1) * HEAD_W] = qs[d, h] + cur[d, h][1]
        states = {dh: states[dh] * cur[dh][3] + cur[dh][2] for dh in lanes}
    for d, h in lanes:
        s_ref[d, h] = states[d, h]


def _hgrn2(ub, uf, lb, *, batch, seq, ctx_len):
    n_all = seq // SCAN_ROWS
    n_ctx = ctx_len // SCAN_ROWS

    def rows(b, s, rev):
        return b * n_all + _block_order(s, n_ctx, n_all, rev)

    def in_specs(rev):
        f_col = (UF_HG_FB if rev else UF_HG_FF) // MIX_W
        return [
            pl.BlockSpec((SCAN_ROWS, MIX_W), lambda b, s: (rows(b, s, rev), UB_HG_Q // MIX_W)),
            pl.BlockSpec((SCAN_ROWS, MIX_W), lambda b, s: (rows(b, s, rev), f_col)),
            pl.BlockSpec((SCAN_ROWS, MIX_W), lambda b, s: (rows(b, s, rev), UB_HG_I // MIX_W)),
        ]

    out = jax.ShapeDtypeStruct((batch * seq, MIX_W), F32)
    return pl.pallas_call(
        _hgrn2_kernel,
        out_shape=(out, out),
        grid=(batch, n_all),
        in_specs=in_specs(False) + in_specs(True) + [pl.BlockSpec((2, 1, MIX_W), lambda b, s: (0, 0, 0))],
        out_specs=(pl.BlockSpec((SCAN_ROWS, MIX_W), lambda b, s: (rows(b, s, False), 0)),
                   pl.BlockSpec((SCAN_ROWS, MIX_W), lambda b, s: (rows(b, s, True), 0))),
        scratch_shapes=[pltpu.VMEM((2, N_HEADS, HEAD_W, HEAD_W), F32)],
        compiler_params=_params(("parallel", "arbitrary")),
        name="hgrn2",
    )(ub, uf, ub, ub, uf, ub, lb)


LRU_GROUP = 8


def _group_scan(a, b, rev):
    n, w = a.shape
    a = a.reshape(n // LRU_GROUP, LRU_GROUP, w)
    b = b.reshape(n // LRU_GROUP, LRU_GROUP, w)
    sub = lax.broadcasted_iota(jnp.int32, (1, LRU_GROUP, 1), 1)
    s = 1
    while s < LRU_GROUP:
        shift, valid = (LRU_GROUP - s, sub < LRU_GROUP - s) if rev else (s, sub >= s)
        a_s, b_s = pltpu.roll(a, shift, 1), pltpu.roll(b, shift, 1)
        b = jnp.where(valid, a * b_s + b, b)
        a = jnp.where(valid, a * a_s, a)
        s *= 2
    return a, b


def _chain_groups(a, b, carry, rev, store):
    n = a.shape[0]
    for g in (range(n - 1, -1, -1) if rev else range(n)):
        hg = a[g] * carry + b[g]
        store(g, hg)
        carry = hg[0:1, :] if rev else hg[LRU_GROUP - 1:LRU_GROUP, :]
    return carry


def _gelu_tanh(x):
    return 0.5 * x * (1.0 + jnp.tanh(0.7978845608028654 * (x + 0.044715 * (x * x * x))))


def _lru_kernel(xb_ref, gb_ref, cw_ref, cb_ref, wa_ref, ba_ref, wx_ref, bx_ref, lam_ref, o_ref, xc_ref, hf_ref,
                hb_ref, *, ctx_len):
    n = xb_ref.shape[0]
    nblk = n // LRU_BLOCK
    nctx = ctx_len // LRU_BLOCK
    xc_ref[...] = _seg_conv(xb_ref[...].astype(F32), cw_ref, ctx_len) + cb_ref[...]

    half_rate = [(-0.5 * LRU_C * LOG2_E) * _softplus(-lam_ref[d]) for d in range(2)]

    def gates(xc, d):
        t_r = jnp.tanh(_dot(xc, wa_ref[d]) + ba_ref[d])
        ig = 0.5 * jnp.tanh(_dot(xc, wx_ref[d]) + bx_ref[d]) + 0.5
        a = jnp.exp2(t_r * half_rate[d] + half_rate[d])
        return a, jnp.sqrt(1.0 - a * a) * ig * xc

    def scan_block(blk, carry, d, out_ref):
        base = blk * LRU_BLOCK
        a, b = _group_scan(*gates(xc_ref[base:base + LRU_BLOCK, :], d), rev=bool(d))

        def store(g, hg):
            out_ref[base + g * LRU_GROUP:base + (g + 1) * LRU_GROUP, :] = hg

        return _chain_groups(a, b, carry, bool(d), store)

    cf = cb = jnp.zeros((1, HEAD_W), F32)
    for i in range(nblk):
        cf = scan_block(i, cf, 0, hf_ref)
        cb = scan_block(nctx - 1 - i if i < nctx else nblk + nctx - 1 - i, cb, 1, hb_ref)
    o_ref[...] = ((hf_ref[...] + hb_ref[...]) * _gelu_tanh(gb_ref[...].astype(F32))).astype(o_ref.dtype)


def _lru(ub, conv_w, conv_b, w_a, b_a, w_x, b_x, lam, *, batch, seq, ctx_len):
    vec = pl.BlockSpec((2, 1, HEAD_W), lambda b, h: (0, 0, h))
    mat = pl.BlockSpec((2, None, HEAD_W, HEAD_W), lambda b, h: (0, h, 0, 0))
    return pl.pallas_call(
        functools.partial(_lru_kernel, ctx_len=ctx_len),
        out_shape=jax.ShapeDtypeStruct((batch * seq, MIX_W), BF16),
        grid=(batch, N_HEADS),
        in_specs=[
            pl.BlockSpec((seq, HEAD_W), lambda b, h: (b, UB_LRU_X // HEAD_W + h)),
            pl.BlockSpec((seq, HEAD_W), lambda b, h: (b, UB_LRU_G // HEAD_W + h)),
            pl.BlockSpec((4, HEAD_W), lambda b, h: (0, h)),
            pl.BlockSpec((1, HEAD_W), lambda b, h: (0, h)),
            mat, vec, mat, vec, vec,
        ],
        out_specs=pl.BlockSpec((seq, HEAD_W), lambda b, h: (b, h)),
        scratch_shapes=[pltpu.VMEM((seq, HEAD_W), F32)] * 3,
        compiler_params=_params(("parallel", "parallel")),
        name="rglru",
    )(ub, ub, conv_w, conv_b, w_a, b_a, w_x, b_x, lam)


def _norm_rope(x, w, cos, sin, grp, scale):
    lane = lax.broadcasted_iota(jnp.int32, (1, 128), 1)
    first = (lane % (ATT_HEAD_DIM // 2)) < (ATT_HEAD_DIM // 4)
    ss = _dot_x3(x * x, grp)
    y = x * lax.rsqrt(ss * (1.0 / ATT_HEAD_DIM) + EPS) * w
    rot = jnp.where(first, pltpu.roll(y, 128 - ATT_HEAD_DIM // 4, 1), pltpu.roll(y, ATT_HEAD_DIM // 4, 1))
    return (y * cos + rot * sin) * scale


def _att_kernel(q_ref, k_ref, v_ref, cos_ref, sin_ref, qw_ref, kw_ref, grp_ref, o_ref, kt_ref, vb_ref, *, ctx_len):
    i = pl.program_id(1)
    grp = grp_ref[...]
    tq = q_ref.shape[0]

    @pl.when(i == 0)
    def _():
        for c in range(k_ref.shape[0] // tq):
            rows = slice(c * tq, (c + 1) * tq)
            kn = _norm_rope(k_ref[rows, :].astype(F32), kw_ref[...], cos_ref[rows, :], sin_ref[rows, :], grp, 1.0)
            kt_ref[:, rows] = kn.T.astype(kt_ref.dtype)
        vb_ref[...] = v_ref[...].astype(vb_ref.dtype)

    q_scale = (ATT_HEAD_DIM ** -0.5) * LOG2_E
    q_rows = pl.ds(pl.multiple_of(i * tq, tq), tq)
    cos, sin = cos_ref[q_rows, :], sin_ref[q_rows, :]
    qn = [_norm_rope(q_ref[:, s * 128:(s + 1) * 128].astype(F32), qw_ref[...], cos, sin, grp, q_scale).astype(BF16)
          for s in range(q_ref.shape[1] // 128)]

    def attend(n_keys):
        v = vb_ref[:n_keys, :]

        def scores(head):
            g = head // ATT_GROUP
            kt = kt_ref[g * ATT_HEAD_DIM:(g + 1) * ATT_HEAD_DIM, :n_keys]
            q = qn[head // 2][:, (head % 2) * ATT_HEAD_DIM:(head % 2 + 1) * ATT_HEAD_DIM]
            return jnp.dot(q, kt, preferred_element_type=F32)

        s_next = scores(0)
        for head in range(ATT_Q_HEADS):
            s = s_next
            if head + 1 < ATT_Q_HEADS:
                s_next = scores(head + 1)
            g = head // ATT_GROUP
            p = jnp.exp2(s - jnp.max(s, axis=-1, keepdims=True))
            den = jnp.sum(p, axis=-1, keepdims=True)
            pv = jnp.dot(p.astype(BF16), v, preferred_element_type=F32)
            o_ref[:, head * ATT_HEAD_DIM:(head + 1) * ATT_HEAD_DIM] = (
                pv[:, g * ATT_HEAD_DIM:(g + 1) * ATT_HEAD_DIM] / den).astype(o_ref.dtype)

    @pl.when(i == 0)
    def _():
        attend(ctx_len)

    @pl.when(i != 0)
    def _():
        attend(kt_ref.shape[1])


def _attention(ub, uf, cos, sin, qw, kw, grp, *, batch, seq, ctx_len):
    tiles = seq // ATT_TQ
    whole = lambda shape: pl.BlockSpec(shape, lambda b, i: (0, 0))
    return pl.pallas_call(
        functools.partial(_att_kernel, ctx_len=ctx_len),
        out_shape=jax.ShapeDtypeStruct((batch * seq, ATT_QW), BF16),
        grid=(batch, tiles),
        in_specs=[
            pl.BlockSpec((ATT_TQ, ATT_QW), lambda b, i: (b * tiles + i, UB_ATT_Q // ATT_QW)),
            pl.BlockSpec((seq, ATT_KW), lambda b, i: (b, UF_ATT_K // ATT_KW)),
            pl.BlockSpec((seq, ATT_KW), lambda b, i: (b, UF_ATT_V // ATT_KW)),
            whole((seq, 128)), whole((seq, 128)), whole((1, 128)), whole((1, 128)), whole((128, 128)),
        ],
        out_specs=pl.BlockSpec((ATT_TQ, ATT_QW), lambda b, i: (b * tiles + i, 0)),
        scratch_shapes=[pltpu.VMEM((ATT_KW, seq), BF16), pltpu.VMEM((seq, ATT_KW), BF16)],
        compiler_params=_params(("parallel", "arbitrary")),
        name="attention",
    )(ub, uf, uf, cos, sin, qw, kw, grp)


def _merge_kernel(oaf_ref, oab_ref, obf_ref, obb_ref, ga_ref, gb_ref, yc_ref, yd_ref, gate_ref, h_ref, modc_ref,
                  modl_ref, dnw_ref, hgw_ref, wb_ref, wo_ref, out_ref, *, tiles_per_batch):
    def gated(of_ref, ob_ref, g_ref, nw_ref):
        o = of_ref[...] + ob_ref[...]
        ys = [_rms_rows(o[:, h * HEAD_W:(h + 1) * HEAD_W], nw_ref[...]) for h in range(N_HEADS)]
        g = g_ref[...].astype(F32)
        return jnp.concatenate(ys, axis=1) * _silu(g)

    ys = (gated(oaf_ref, oab_ref, ga_ref, dnw_ref), gated(obf_ref, obb_ref, gb_ref, hgw_ref), yc_ref[...], yd_ref[...])
    acc = None
    for b in range(N_BRANCH):
        gate = gate_ref[:, b * D_MODEL:(b + 1) * D_MODEL].astype(F32)
        term = (1.0 + jnp.tanh(gate)) * _dot(ys[b], wb_ref[b])
        acc = term if acc is None else acc + term
    is_ctx = (pl.program_id(0) % tiles_per_batch) == 0
    out_ref[...] = h_ref[...] + _pick_mod(modc_ref, modl_ref, 2, is_ctx) * _dot(acc, wo_ref[...])


def _merge(oaf, oab, obf, obb, ub, yc, yd, h, modc, modl, dnw, hgw, wb, wo, *, seq, ctx_len):
    rows = h.shape[0]
    tm = ctx_len
    tiles_per_batch = seq // tm
    mix = lambda c: pl.BlockSpec((tm, MIX_W), lambda i: (i, c // MIX_W))
    return pl.pallas_call(
        functools.partial(_merge_kernel, tiles_per_batch=tiles_per_batch),
        out_shape=jax.ShapeDtypeStruct((rows, D_MODEL), F32),
        grid=(rows // tm,),
        in_specs=[
            mix(0), mix(0), mix(0), mix(0), mix(UB_DN_G), mix(UB_HG_G), mix(0), mix(0),
            pl.BlockSpec((tm, N_BRANCH * D_MODEL), lambda i: (i, 0)),
            pl.BlockSpec((tm, D_MODEL), lambda i: (i, 0)),
            pl.BlockSpec((6, D_MODEL), lambda i: (0, 0)),
            pl.BlockSpec((None, 6, D_MODEL), lambda i: (i // tiles_per_batch, 0, 0)),
            pl.BlockSpec((1, HEAD_W), lambda i: (0, 0)),
            pl.BlockSpec((1, HEAD_W), lambda i: (0, 0)),
            pl.BlockSpec((N_BRANCH, MIX_W, D_MODEL), lambda i: (0, 0, 0)),
            pl.BlockSpec((D_MODEL, D_MODEL), lambda i: (0, 0)),
        ],
        out_specs=pl.BlockSpec((tm, D_MODEL), lambda i: (i, 0)),
        compiler_params=_params(("parallel",)),
        name="merge",
    )(oaf, oab, obf, obb, ub, ub, yc, yd, ub, h, modc, modl, dnw, hgw, wb, wo)


def _mlp_kernel(h_ref, modc_ref, modl_ref, nw_ref, w1_ref, w2_ref, o_ref, z_ref, acc_ref,
                *, tm, tiles_per_batch, ctx_len):
    j = pl.program_id(1)
    row = (pl.program_id(0) % tiles_per_batch) * tm + lax.broadcasted_iota(jnp.int32, (tm, 1), 0)
    is_ctx = row < ctx_len

    @pl.when(j == 0)
    def _():
        y = _rms_rows(h_ref[...], nw_ref[...])
        shift = _pick_mod(modc_ref, modl_ref, 3, is_ctx)
        scale = _pick_mod(modc_ref, modl_ref, 4, is_ctx)
        z_ref[...] = (y * (1.0 + scale) + shift).astype(BF16)
        acc_ref[...] = jnp.zeros_like(acc_ref)

    a = jnp.maximum(jnp.dot(z_ref[...], w1_ref[...], preferred_element_type=F32), 0.0)
    acc_ref[...] += _dot(a * a, w2_ref[...])

    @pl.when(j == pl.num_programs(1) - 1)
    def _():
        o_ref[...] = h_ref[...] + _pick_mod(modc_ref, modl_ref, 5, is_ctx) * acc_ref[...]


def _mlp(h, modc, modl, nw, w1, w2, *, seq, ctx_len):
    rows = h.shape[0]
    tiles_per_batch = 4
    tm = seq // tiles_per_batch
    kern = functools.partial(_mlp_kernel, tm=tm, tiles_per_batch=tiles_per_batch, ctx_len=ctx_len)
    return pl.pallas_call(
        kern,
        out_shape=jax.ShapeDtypeStruct((rows, D_MODEL), F32),
        grid=(rows // tm, D_FF // FF_BLOCK),
        in_specs=[
            pl.BlockSpec((tm, D_MODEL), lambda i, j: (i, 0)),
            pl.BlockSpec((6, D_MODEL), lambda i, j: (0, 0)),
            pl.BlockSpec((None, 6, D_MODEL), lambda i, j: (i // tiles_per_batch, 0, 0)),
            pl.BlockSpec((1, D_MODEL), lambda i, j: (0, 0)),
            pl.BlockSpec((D_MODEL, FF_BLOCK), lambda i, j: (0, j)),
            pl.BlockSpec((FF_BLOCK, D_MODEL), lambda i, j: (j, 0)),
        ],
        out_specs=pl.BlockSpec((tm, D_MODEL), lambda i, j: (i, 0)),
        scratch_shapes=[pltpu.VMEM((tm, D_MODEL), BF16), pltpu.VMEM((tm, D_MODEL), F32)],
        compiler_params=_params(("parallel", "arbitrary"), vmem=VMEM_LIMIT_V7X + 8 * 1024 * 1024),
        name="mlp",
    )(h, modc, modl, nw, w1, w2)


def _regroup_w_in(w_in):
    o = np.cumsum([0, 512, 512, 512, 512, 8, 8, 512, 512, 512, 512, 512, 512, 512, 512, 128, 128, 4096])
    cols = lambda a, b: w_in[..., o[a]:o[b]]
    wb = jnp.concatenate([0.5 * cols(16, 17), cols(6, 7), cols(9, 11), cols(0, 3), cols(11, 13), cols(3, 4),
                          cols(13, 14)], axis=-1)
    wf = jnp.concatenate([0.5 * cols(7, 9), cols(14, 16), cols(4, 6)], axis=-1)
    pad = lambda w, n: jnp.pad(w, ((0, 0), (0, 0), (0, n - w.shape[-1]))).astype(BF16)
    return pad(wb, N_UB), pad(wf, N_UF)


def _rope_tables(t_len, ctx_len):
    rows = t_len // GRID_W
    row_id = jnp.repeat(jnp.arange(rows), GRID_W).astype(F32)
    col_id = jnp.tile(jnp.arange(GRID_W), rows).astype(F32)
    axis_dim = ATT_HEAD_DIM // 2
    inv = ROPE_THETA ** (-jnp.arange(0, axis_dim, 2, dtype=F32) / axis_dim)
    ang = jnp.stack([row_id[:, None] * inv, col_id[:, None] * inv], axis=1)
    cos, sin = jnp.cos(ang), jnp.sin(ang)
    cos_h = jnp.concatenate([cos, cos], axis=-1).reshape(t_len, ATT_HEAD_DIM)
    sin_h = jnp.concatenate([-sin, sin], axis=-1).reshape(t_len, ATT_HEAD_DIM)
    cos_t = jnp.concatenate([jnp.ones((ctx_len, ATT_HEAD_DIM), F32), cos_h], axis=0)
    sin_t = jnp.concatenate([jnp.zeros((ctx_len, ATT_HEAD_DIM), F32), sin_h], axis=0)
    return jnp.tile(cos_t, (1, 2)), jnp.tile(sin_t, (1, 2))


def _hgrn2_lower_bounds(p):
    sm = jax.nn.softmax(p.astype(F32), axis=1)
    cs = jnp.cumsum(sm, axis=1)
    return cs - cs[:, :1]


def kernel(x, c, ctx, c_ctx, mod_w, mod_b, norm1_w, norm2_w, w_in, dn_conv_w, dn_a_log, dn_dt_bias, dn_norm_w,
           hg_lower_bounds, hg_norm_w, lru_conv_w, lru_conv_b, lru_w_a, lru_b_a, lru_w_x, lru_b_x, lru_lambda,
           att_q_norm_w, att_k_norm_w, w_branch, w_out, mlp_w1, mlp_w2):
    batch, t_len, _ = x.shape
    ctx_len = ctx.shape[1]
    depth = mod_w.shape[0]
    seq = ctx_len + t_len
    n_chunks = seq // CHUNK
    assert ctx_len % SCAN_ROWS == 0 and t_len % SCAN_ROWS == 0 and ctx_len == ATT_TQ
    dims = dict(batch=batch, seq=seq, ctx_len=ctx_len)

    w_ub, w_uf = _regroup_w_in(w_in)
    w_branch_b, w_out_b = (0.5 * w_branch).astype(BF16), w_out.astype(BF16)
    w1_b, w2_b = mlp_w1.astype(BF16), mlp_w2.astype(BF16)
    lru_wa_b, lru_wx_b = (0.5 * lru_w_a).astype(BF16), (0.5 * lru_w_x).astype(BF16)
    lb_all = _hgrn2_lower_bounds(hg_lower_bounds)
    cos_t, sin_t = _rope_tables(t_len, ctx_len)
    grp = jnp.asarray((np.arange(128)[:, None] // ATT_HEAD_DIM) == (np.arange(128)[None, :] // ATT_HEAD_DIM), F32)

    n_mod_rows = batch + 8
    cc = jnp.zeros((n_mod_rows, D_MODEL), F32).at[:batch].set(c).at[batch].set(c_ctx)
    mods = _modulations(cc, mod_w.astype(BF16), mod_b)

    h = jnp.concatenate([ctx, x], axis=1).reshape(batch * seq, D_MODEL)
    for l in range(depth):
        modl = mods[l, :batch].reshape(batch, 6, D_MODEL)
        modc = mods[l, batch].reshape(6, D_MODEL)
        nw1 = norm1_w[l].reshape(1, D_MODEL)
        ub, uf = _inproj(h, modc, modl, nw1, w_ub[l], w_uf[l], seq=seq, ctx_len=ctx_len)

        qkv = _dn_prep(ub, dn_conv_w[l], **dims)
        ba = uf[:, UF_DN_BA:UF_DN_BA + 4 * N_HEADS].reshape(batch * n_chunks, CHUNK, 2, 2, N_HEADS)
        ba_t = jnp.transpose(ba, (3, 0, 2, 4, 1)).reshape(2, batch * n_chunks, 2 * N_HEADS, CHUNK)
        oa = _gdn(qkv, ba_t, dn_a_log[l].reshape(2, N_HEADS, 1), dn_dt_bias[l].reshape(2, N_HEADS, 1), **dims)

        ob = _hgrn2(ub, uf, lb_all[:, l].reshape(2, 1, MIX_W), **dims)

        yc = _lru(ub, lru_conv_w[l], lru_conv_b[l].reshape(1, MIX_W), lru_wa_b[l],
                  (0.5 * lru_b_a[l]).reshape(2, 1, MIX_W), lru_wx_b[l], (0.5 * lru_b_x[l]).reshape(2, 1, MIX_W),
                  lru_lambda[l].reshape(2, 1, MIX_W), **dims)

        yd = _attention(ub, uf, cos_t, sin_t, jnp.tile(att_q_norm_w[l], 2).reshape(1, 128),
                        jnp.tile(att_k_norm_w[l], 2).reshape(1, 128), grp, **dims)

        h = _merge(oa[0], oa[1], ob[0], ob[1], ub, yc, yd, h, modc, modl, dn_norm_w[l].reshape(1, HEAD_W),
                   hg_norm_w[l].reshape(1, HEAD_W), w_branch_b[l], w_out_b[l], seq=seq, ctx_len=ctx_len)
        h = _mlp(h, modc, modl, norm2_w[l].reshape(1, D_MODEL), w1_b[l], w2_b[l], seq=seq, ctx_len=ctx_len)

    return h.reshape(batch, seq, D_MODEL)[:, ctx_len:]
```

```python
import functools

import jax
import jax.numpy as jnp
import numpy as np
from jax import lax
from jax.experimental import pallas as pl
from jax.experimental.pallas import tpu as pltpu

F32 = jnp.float32
BF16 = jnp.bfloat16

EPS = 1e-6
D_MODEL = 1024
GRID_W = 64
N_HEADS = 4
HEAD_W = 128
MIX_W = N_HEADS * HEAD_W
CHUNK = 64
SUB = 16
SCAN_ROWS = 256
LRU_C = 8.0
LRU_BLOCK = 256
ATT_Q_HEADS = 8
ATT_KV_HEADS = 2
ATT_HEAD_DIM = 64
ATT_GROUP = ATT_Q_HEADS // ATT_KV_HEADS
ATT_QW = ATT_Q_HEADS * ATT_HEAD_DIM
ATT_KW = ATT_KV_HEADS * ATT_HEAD_DIM
ATT_TQ = 256
ROPE_THETA = 10000.0
LOG2_E = 1.4426950408889634
N_BRANCH = 4
D_FF = 4 * D_MODEL
FF_BLOCK = 2048

UB_GATE = 0
UB_HG_Q, UB_HG_I, UB_HG_G = 4096, 4608, 5120
UB_DN_Q = 5632
UB_LRU_X, UB_LRU_G = 7168, 7680
UB_DN_G = 8192
UB_ATT_Q = 8704
TN_UB = 2304
N_UB = 4 * TN_UB
UF_HG_FF, UF_HG_FB, UF_ATT_K, UF_ATT_V, UF_DN_BA = 0, 512, 1024, 1152, 1280
N_UF = 1536

VMEM_LIMIT_V7X = 48 * 1024 * 1024


def _params(sem, vmem=VMEM_LIMIT_V7X):
    return pltpu.CompilerParams(dimension_semantics=sem, vmem_limit_bytes=vmem)


def _sigmoid(x):
    return 0.5 * jnp.tanh(0.5 * x) + 0.5


def _silu(x):
    h = 0.5 * x
    return h * (1.0 + jnp.tanh(h))


def _softplus(x):
    return jnp.maximum(x, 0.0) + jnp.log1p(jnp.exp(-jnp.abs(x)))


def _dot(a, b):
    return jnp.dot(a.astype(BF16), b.astype(BF16), preferred_element_type=F32)


def _dot_nt(a, b):
    return lax.dot_general(a.astype(BF16), b.astype(BF16), (((1,), (1,)), ((), ())),
                           preferred_element_type=F32)


def _dot_tn(a, b):
    return lax.dot_general(a.astype(BF16), b.astype(BF16), (((0,), (0,)), ((), ())),
                           preferred_element_type=F32)


def _split3(x):
    hi = x.astype(BF16)
    r1 = x - hi.astype(F32)
    mid = r1.astype(BF16)
    lo = (r1 - mid.astype(F32)).astype(BF16)
    return hi, mid, lo


def _dot_x3(x, m):
    m = m.astype(BF16)
    hi, mid, lo = _split3(x)
    out = jnp.dot(hi, m, preferred_element_type=F32)
    out = out + jnp.dot(mid, m, preferred_element_type=F32)
    return out + jnp.dot(lo, m, preferred_element_type=F32)


def _dot_3x_many(m, xs):
    w = xs[0].shape[1]
    pieces = [_split3(x) for x in xs]
    cols = [p[i] for i in range(3) for p in pieces]
    wide = jnp.dot(m.astype(BF16), jnp.concatenate(cols, axis=1), preferred_element_type=F32)
    n = len(xs)
    part = lambda p, i: wide[:, (p * n + i) * w:(p * n + i + 1) * w]
    return [part(0, i) + part(1, i) + part(2, i) for i in range(n)]


def _rms_rows(x, w):
    return x * lax.rsqrt(jnp.mean(x * x, axis=-1, keepdims=True) + EPS) * w


def _mod_kernel(c_ref, w_ref, b_ref, o_ref):
    c = c_ref[...]
    o_ref[...] = _dot(_silu(c), w_ref[...]) + b_ref[...]


def _modulations(cc, mod_w, mod_b):
    depth = mod_w.shape[0]
    rows = cc.shape[0]
    n_out = mod_w.shape[2]
    return pl.pallas_call(
        _mod_kernel,
        out_shape=jax.ShapeDtypeStruct((depth, rows, n_out), F32),
        grid=(depth, n_out // D_MODEL),
        in_specs=[
            pl.BlockSpec((rows, D_MODEL), lambda l, j: (0, 0)),
            pl.BlockSpec((None, D_MODEL, D_MODEL), lambda l, j: (l, 0, j)),
            pl.BlockSpec((None, 1, D_MODEL), lambda l, j: (l, 0, j)),
        ],
        out_specs=pl.BlockSpec((None, rows, D_MODEL), lambda l, j: (l, 0, j)),
        compiler_params=_params(("parallel", "parallel")),
        name="modulations",
    )(cc, mod_w, mod_b.reshape(depth, 1, n_out))


def _pick_mod(modc_ref, modl_ref, idx, is_ctx):
    return jnp.where(is_ctx, modc_ref[idx:idx + 1, :], modl_ref[idx:idx + 1, :])


def _inproj_kernel(h_ref, modc_ref, modl_ref, nw_ref, wb_ref, wf_ref, ob_ref, of_ref, xn_ref,
                   *, tm, tiles_per_batch, ctx_len):
    j = pl.program_id(1)

    @pl.when(j == 0)
    def _():
        y = _rms_rows(h_ref[...], nw_ref[...])
        row = (pl.program_id(0) % tiles_per_batch) * tm + lax.broadcasted_iota(jnp.int32, (tm, 1), 0)
        is_ctx = row < ctx_len
        shift = _pick_mod(modc_ref, modl_ref, 0, is_ctx)
        scale = _pick_mod(modc_ref, modl_ref, 1, is_ctx)
        xn_ref[...] = (y * (1.0 + scale) + shift).astype(BF16)

    ob_ref[...] = jnp.dot(xn_ref[...], wb_ref[...], preferred_element_type=F32).astype(ob_ref.dtype)

    @pl.when(j == pl.num_programs(1) - 1)
    def _():
        of_ref[...] = jnp.dot(xn_ref[...], wf_ref[...], preferred_element_type=F32)


def _inproj(h, modc, modl, nw, w_b, w_f, *, seq, ctx_len):
    rows = h.shape[0]
    tiles_per_batch = 4
    tm = seq // tiles_per_batch
    kern = functools.partial(_inproj_kernel, tm=tm, tiles_per_batch=tiles_per_batch, ctx_len=ctx_len)
    return pl.pallas_call(
        kern,
        out_shape=(jax.ShapeDtypeStruct((rows, N_UB), BF16), jax.ShapeDtypeStruct((rows, N_UF), F32)),
        grid=(rows // tm, N_UB // TN_UB),
        in_specs=[
            pl.BlockSpec((tm, D_MODEL), lambda i, j: (i, 0)),
            pl.BlockSpec((6, D_MODEL), lambda i, j: (0, 0)),
            pl.BlockSpec((None, 6, D_MODEL), lambda i, j: (i // tiles_per_batch, 0, 0)),
            pl.BlockSpec((1, D_MODEL), lambda i, j: (0, 0)),
            pl.BlockSpec((D_MODEL, TN_UB), lambda i, j: (0, j)),
            pl.BlockSpec((D_MODEL, N_UF), lambda i, j: (0, 0)),
        ],
        out_specs=(pl.BlockSpec((tm, TN_UB), lambda i, j: (i, j)),
                   pl.BlockSpec((tm, N_UF), lambda i, j: (i, 0))),
        scratch_shapes=[pltpu.VMEM((tm, D_MODEL), BF16)],
        compiler_params=_params(("parallel", "arbitrary"), vmem=VMEM_LIMIT_V7X + 8 * 1024 * 1024),
        name="inproj",
    )(h, modc, modl, nw, w_b, w_f)


def _seg_conv(x, w_ref, ctx_len):
    n = x.shape[0]
    row = lax.broadcasted_iota(jnp.int32, (n, 1), 0)
    lo = jnp.where(row >= ctx_len, ctx_len, 0)
    hi = jnp.where(row >= ctx_len, n, ctx_len)

    def tap(k):
        tk = row + k
        valid = jnp.logical_and(tk >= lo, tk < hi)
        return jnp.where(valid, pltpu.roll(x, (-k) % n, 0), 0.0)

    return (tap(-1) * w_ref[0:1, :] + x * w_ref[1:2, :] + tap(1) * w_ref[2:3, :] + tap(2) * w_ref[3:4, :])


def _dn_prep_kernel(u_ref, w_ref, o_ref, *, ctx_len):
    j = pl.program_id(1)
    y = _seg_conv(u_ref[...].astype(F32), w_ref, ctx_len)
    y = _silu(y)
    n = lax.rsqrt(jnp.sum(y * y, axis=-1, keepdims=True) + EPS)
    fac = jnp.where(j < N_HEADS, n * (HEAD_W ** -0.5), jnp.where(j < 2 * N_HEADS, n, 1.0))
    o_ref[...] = (y * fac).astype(o_ref.dtype)


def _dn_prep(ub, conv_w, *, batch, seq, ctx_len):
    nblk = 3 * N_HEADS
    return pl.pallas_call(
        functools.partial(_dn_prep_kernel, ctx_len=ctx_len),
        out_shape=jax.ShapeDtypeStruct((batch * seq, 3 * MIX_W), BF16),
        grid=(batch, nblk),
        in_specs=[
            pl.BlockSpec((seq, HEAD_W), lambda b, j: (b, UB_DN_Q // HEAD_W + j)),
            pl.BlockSpec((4, HEAD_W), lambda b, j: (0, j)),
        ],
        out_specs=pl.BlockSpec((seq, HEAD_W), lambda b, j: (b, j)),
        compiler_params=_params(("parallel", "parallel")),
        name="dn_prep",
    )(ub, conv_w)


def _block_order(s, n_ctx, n_all, rev):
    if not rev:
        return s
    return jnp.where(s < n_ctx, n_ctx - 1 - s, n_all + n_ctx - 1 - s)


def _chunk_masks(rev):
    ii = lax.broadcasted_iota(jnp.int32, (CHUNK, CHUNK), 0)
    jj = lax.broadcasted_iota(jnp.int32, (CHUNK, CHUNK), 1)
    incl = (ii <= jj) if rev else (ii >= jj)
    incl_t = (ii >= jj) if rev else (ii <= jj)
    return ii, jj, incl, incl_t


def _unit_tri_inverse(a_all, eye, bd):
    ad = [jnp.where(bd, a, 0.0) for a in a_all]
    ao = [a - d for a, d in zip(a_all, ad)]
    p = [-d for d in ad]
    dinv = [eye + x for x in p]
    for _ in range(3):
        p = [_dot(x, x) for x in p]
        dinv = [d + _dot(d, x) for d, x in zip(dinv, p)]
    m = [-_dot(d, o) for d, o in zip(dinv, ao)]
    mm = [_dot(x, x) for x in m]
    t = [eye + x for x in m]
    t = [x + _dot(x, y) for x, y in zip(t, mm)]
    return [_dot(x, d) for x, d in zip(t, dinv)]


def _gdn_prepare(q_ref, k_ref, v_ref, ba_ref, alog, dtb, rev):
    ii, jj, incl, incl_t = _chunk_masks(rev)
    is_eye = ii == jj
    eye = jnp.where(is_eye, 1.0, 0.0)
    tri = jnp.where(incl, 1.0, 0.0)
    tri_t = jnp.where(incl_t, 1.0, 0.0)
    bd = (ii // SUB) == (jj // SUB)
    n_chunks = q_ref.shape[0] // CHUNK

    alog_neg = -jnp.exp(alog)
    chains = [(j, h) for j in range(n_chunks) for h in range(N_HEADS)]
    vec = {}
    for j in range(n_chunks):
        ba = ba_ref[j]
        beta_r = _sigmoid(ba[0:N_HEADS])
        g_r = alog_neg * _softplus(ba[N_HEADS:2 * N_HEADS] + dtb)
        for h in range(N_HEADS):
            gr = g_r[h:h + 1]
            beta_c = jnp.sum(eye * beta_r[h:h + 1], axis=1, keepdims=True)
            g_c = jnp.sum(eye * gr, axis=1, keepdims=True)
            gam_c = jnp.sum(tri * gr, axis=1, keepdims=True)
            gam_r = jnp.sum(tri_t * g_c, axis=0, keepdims=True)
            tot = jnp.sum(gr, axis=1, keepdims=True)
            dec_i = jnp.where(incl, jnp.exp(gam_c - gam_r), 0.0)
            vec[j, h] = (beta_c, gam_c, tot, dec_i)

    def tile(ref, j, h):
        return ref[j * CHUNK:(j + 1) * CHUNK, h * HEAD_W:(h + 1) * HEAD_W]

    k16 = [tile(k_ref, *c) for c in chains]
    q16 = [tile(q_ref, *c) for c in chains]
    ks = [k.astype(F32) for k in k16]
    kbs = [k * vec[c][0] for k, c in zip(ks, chains)]
    kq = [_dot_nt(jnp.concatenate([kb.astype(BF16), q], axis=0), k) for kb, q, k in zip(kbs, q16, k16)]
    a_low = [jnp.where(is_eye, 0.0, x[:CHUNK] * vec[c][3]) for x, c in zip(kq, chains)]
    a_qk = [x[CHUNK:] * vec[c][3] for x, c in zip(kq, chains)]
    t_inv = _unit_tri_inverse(a_low, eye, bd)
    egc = [jnp.exp(vec[c][1]) for c in chains]
    sols = [_dot(t, jnp.concatenate([tile(v_ref, *c).astype(F32) * vec[c][0], kb * e], axis=1))
            for t, kb, e, c in zip(t_inv, kbs, egc, chains)]
    pre = {}
    for c, sol, aq, q, k, e in zip(chains, sols, a_qk, q16, ks, egc):
        q_dec = q.astype(F32) * e
        k_dec_t = (k * jnp.exp(vec[c][2] - vec[c][1])).T
        pre[c] = (sol[:, :HEAD_W], jnp.concatenate([sol[:, HEAD_W:], q_dec], axis=0).astype(BF16),
                  jnp.concatenate([aq, k_dec_t], axis=0).astype(BF16), jnp.exp(vec[c][2]))
    return pre


def _gdn_kernel(qf_ref, kf_ref, vf_ref, baf_ref, qr_ref, kr_ref, vr_ref, bar_ref, alog_ref, dtb_ref,
                of_ref, or_ref, s_ref):
    @pl.when(pl.program_id(1) == 0)
    def _():
        s_ref[...] = jnp.zeros_like(s_ref)

    n_chunks = qf_ref.shape[0] // CHUNK
    pre = (_gdn_prepare(qf_ref, kf_ref, vf_ref, baf_ref, alog_ref[0], dtb_ref[0], False),
           _gdn_prepare(qr_ref, kr_ref, vr_ref, bar_ref, alog_ref[1], dtb_ref[1], True))
    o_refs = (of_ref, or_ref)

    lanes = [(d, h) for d in range(2) for h in range(N_HEADS)]
    states = {dh: s_ref[dh[0], dh[1]] for dh in lanes}
    for t in range(n_chunks):
        js = (t, n_chunks - 1 - t)
        cur = {dh: pre[dh[0]][js[dh[0]], dh[1]] for dh in lanes}
        wq = {dh: _dot(cur[dh][1], states[dh]) for dh in lanes}
        v_new = {dh: cur[dh][0] - wq[dh][:CHUNK] for dh in lanes}
        ak = {dh: _dot(cur[dh][2], v_new[dh]) for dh in lanes}
        for d, h in lanes:
            j = js[d]
            o_refs[d][j * CHUNK:(j + 1) * CHUNK, h * HEAD_W:(h + 1) * HEAD_W] = wq[d, h][CHUNK:] + ak[d, h][:CHUNK]
        states = {dh: states[dh] * cur[dh][3] + ak[dh][CHUNK:] for dh in lanes}
    for d, h in lanes:
        s_ref[d, h] = states[d, h]


def _gdn(qkv, ba_t, a_log, dt_bias, *, batch, seq, ctx_len):
    n_all = seq // SCAN_ROWS
    n_ctx = ctx_len // SCAN_ROWS
    cpb = SCAN_ROWS // CHUNK

    def rows(b, s, rev):
        return b * n_all + _block_order(s, n_ctx, n_all, rev)

    def in_specs(rev):
        d = int(rev)
        return [
            pl.BlockSpec((SCAN_ROWS, MIX_W), lambda b, s: (rows(b, s, rev), 0)),
            pl.BlockSpec((SCAN_ROWS, MIX_W), lambda b, s: (rows(b, s, rev), 1)),
            pl.BlockSpec((SCAN_ROWS, MIX_W), lambda b, s: (rows(b, s, rev), 2)),
            pl.BlockSpec((None, cpb, 2 * N_HEADS, CHUNK), lambda b, s: (d, rows(b, s, rev), 0, 0)),
        ]

    vec = pl.BlockSpec((2, N_HEADS, 1), lambda b, s: (0, 0, 0))
    out = jax.ShapeDtypeStruct((batch * seq, MIX_W), F32)
    return pl.pallas_call(
        _gdn_kernel,
        out_shape=(out, out),
        grid=(batch, n_all),
        in_specs=in_specs(False) + in_specs(True) + [vec, vec],
        out_specs=(pl.BlockSpec((SCAN_ROWS, MIX_W), lambda b, s: (rows(b, s, False), 0)),
                   pl.BlockSpec((SCAN_ROWS, MIX_W), lambda b, s: (rows(b, s, True), 0))),
        scratch_shapes=[pltpu.VMEM((2, N_HEADS, HEAD_W, HEAD_W), F32)],
        compiler_params=_params(("parallel", "arbitrary")),
        name="gdn",
    )(qkv, qkv, qkv, ba_t, qkv, qkv, qkv, ba_t, a_log, dt_bias)


def _hgrn2_prepare(q_ref, f_ref, i_ref, lb_ref, rev):
    _, _, incl, _ = _chunk_masks(rev)
    tri = jnp.where(incl, 1.0, 0.0)
    row = lax.broadcasted_iota(jnp.int32, (CHUNK, 1), 0)
    n_sub = CHUNK // SUB
    n_chunks = q_ref.shape[0] // CHUNK

    def tile(ref, j, h):
        return ref[j * CHUNK:(j + 1) * CHUNK, h * HEAD_W:(h + 1) * HEAD_W]

    chains = [(j, h) for j in range(n_chunks) for h in range(N_HEADS)]
    qs_, ks_, lfs = [], [], []
    for j, h in chains:
        qr = tile(q_ref, j, h).astype(F32)
        lb = lb_ref[:, h * HEAD_W:(h + 1) * HEAD_W]
        f = 0.5 * (1.0 + lb) + (0.5 * (1.0 - lb)) * jnp.tanh(tile(f_ref, j, h))
        qs_.append(_silu(qr))
        ks_.append(1.0 - f)
        lfs.append(jnp.log(f))
    gcs = _dot_3x_many(tri, lfs)
    tots = [jnp.sum(lf, axis=0, keepdims=True) for lf in lfs]
    blocks = []
    for q, k, gc in zip(qs_, ks_, gcs):
        row_blocks = []
        for i in range(n_sub):
            mid = i * SUB + SUB // 2
            gref = gc[mid:mid + 1, :]
            qsc = q[i * SUB:(i + 1) * SUB, :] * jnp.exp(gc[i * SUB:(i + 1) * SUB, :] - gref)
            lo, hi = (i * SUB, CHUNK) if rev else (0, (i + 1) * SUB)
            parts = [jnp.zeros((lo, HEAD_W), F32), k[lo:hi, :] * jnp.exp(gref - gc[lo:hi, :]),
                     jnp.zeros((CHUNK - hi, HEAD_W), F32)]
            ksc = jnp.concatenate([x for x in parts if x.shape[0]], axis=0)
            row_blocks.append((qsc, ksc))
        blocks.append(row_blocks)
    scores = [jnp.where(incl, jnp.concatenate([_dot_nt(a, b) for a, b in rb], axis=0), 0.0) for rb in blocks]
    vs = [tile(i_ref, *c) for c in chains]
    local = [_dot(sc, v) for sc, v in zip(scores, vs)]
    incr = [_dot_tn(v, k * jnp.exp(tot - gc)) for v, k, gc, tot in zip(vs, ks_, gcs, tots)]
    return {c: (q * jnp.exp(gc), lo, inc, jnp.exp(tot))
            for c, q, gc, lo, inc, tot in zip(chains, qs_, gcs, local, incr, tots)}


def _hgrn2_kernel(qf_ref, ff_ref, if_ref, qr_ref, fr_ref, ir_ref, lb_ref, of_ref, or_ref, s_ref):
    @pl.when(pl.program_id(1) == 0)
    def _():
        s_ref[...] = jnp.zeros_like(s_ref)

    n_chunks = qf_ref.shape[0] // CHUNK
    pre = (_hgrn2_prepare(qf_ref, ff_ref, if_ref, lb_ref.at[0], False),
           _hgrn2_prepare(qr_ref, fr_ref, ir_ref, lb_ref.at[1], True))
    o_refs = (of_ref, or_ref)

    lanes = [(d, h) for d in range(2) for h in range(N_HEADS)]
    states = {dh: s_ref[dh[0], dh[1]] for dh in lanes}
    for t in range(n_chunks):
        js = (t, n_chunks - 1 - t)
        cur = {dh: pre[dh[0]][js[dh[0]], dh[1]] for dh in lanes}
        qs = {dh: _dot_nt(cur[dh][0], states[dh]) for dh in lanes}
        for d, h in lanes:
            j = js[d]
            o_refs[d][j * CHUNK:(j + 1) * CHUNK, h * HEAD_W:(h + 1) * HEAD_W] = qs[d, h] + cur[d, h][1]
        states = {dh: states[dh] * cur[dh][3] + cur[dh][2] for dh in lanes}
    for d, h in lanes:
        s_ref[d, h] = states[d, h]


def _hgrn2(ub, uf, lb, *, batch, seq, ctx_len):
    n_all = seq // SCAN_ROWS
    n_ctx = ctx_len // SCAN_ROWS

    def rows(b, s, rev):
        return b * n_all + _block_order(s, n_ctx, n_all, rev)

    def in_specs(rev):
        f_col = (UF_HG_FB if rev else UF_HG_FF) // MIX_W
        return [
            pl.BlockSpec((SCAN_ROWS, MIX_W), lambda b, s: (rows(b, s, rev), UB_HG_Q // MIX_W)),
            pl.BlockSpec((SCAN_ROWS, MIX_W), lambda b, s: (rows(b, s, rev), f_col)),
            pl.BlockSpec((SCAN_ROWS, MIX_W), lambda b, s: (rows(b, s, rev), UB_HG_I // MIX_W)),
        ]

    out = jax.ShapeDtypeStruct((batch * seq, MIX_W), F32)
    return pl.pallas_call(
        _hgrn2_kernel,
        out_shape=(out, out),
        grid=(batch, n_all),
        in_specs=in_specs(False) + in_specs(True) + [pl.BlockSpec((2, 1, MIX_W), lambda b, s: (0, 0, 0))],
        out_specs=(pl.BlockSpec((SCAN_ROWS, MIX_W), lambda b, s: (rows(b, s, False), 0)),
                   pl.BlockSpec((SCAN_ROWS, MIX_W), lambda b, s: (rows(b, s, True), 0))),
        scratch_shapes=[pltpu.VMEM((2, N_HEADS, HEAD_W, HEAD_W), F32)],
        compiler_params=_params(("parallel", "arbitrary")),
        name="hgrn2",
    )(ub, uf, ub, ub, uf, ub, lb)


LRU_GROUP = 8


def _group_scan(a, b, rev):
    n, w = a.shape
    a = a.reshape(n // LRU_GROUP, LRU_GROUP, w)
    b = b.reshape(n // LRU_GROUP, LRU_GROUP, w)
    sub = lax.broadcasted_iota(jnp.int32, (1, LRU_GROUP, 1), 1)
    s = 1
    while s < LRU_GROUP:
        shift, valid = (LRU_GROUP - s, sub < LRU_GROUP - s) if rev else (s, sub >= s)
        a_s, b_s = pltpu.roll(a, shift, 1), pltpu.roll(b, shift, 1)
        b = jnp.where(valid, a * b_s + b, b)
        a = jnp.where(valid, a * a_s, a)
        s *= 2
    return a, b


def _chain_groups(a, b, carry, rev, store):
    n = a.shape[0]
    for g in (range(n - 1, -1, -1) if rev else range(n)):
        hg = a[g] * carry + b[g]
        store(g, hg)
        carry = hg[0:1, :] if rev else hg[LRU_GROUP - 1:LRU_GROUP, :]
    return carry


def _gelu_tanh(x):
    return 0.5 * x * (1.0 + jnp.tanh(0.7978845608028654 * (x + 0.044715 * (x * x * x))))


def _lru_kernel(xb_ref, gb_ref, cw_ref, cb_ref, wa_ref, ba_ref, wx_ref, bx_ref, lam_ref, o_ref, xc_ref, hf_ref,
                hb_ref, *, ctx_len):
    n = xb_ref.shape[0]
    nblk = n // LRU_BLOCK
    nctx = ctx_len // LRU_BLOCK
    xc_ref[...] = _seg_conv(xb_ref[...].astype(F32), cw_ref, ctx_len) + cb_ref[...]

    half_rate = [(-0.5 * LRU_C * LOG2_E) * _softplus(-lam_ref[d]) for d in range(2)]

    def gates(xc, d):
        t_r = jnp.tanh(_dot(xc, wa_ref[d]) + ba_ref[d])
        ig = 0.5 * jnp.tanh(_dot(xc, wx_ref[d]) + bx_ref[d]) + 0.5
        a = jnp.exp2(t_r * half_rate[d] + half_rate[d])
        return a, jnp.sqrt(1.0 - a * a) * ig * xc

    def scan_block(blk, carry, d, out_ref):
        base = blk * LRU_BLOCK
        a, b = _group_scan(*gates(xc_ref[base:base + LRU_BLOCK, :], d), rev=bool(d))

        def store(g, hg):
            out_ref[base + g * LRU_GROUP:base + (g + 1) * LRU_GROUP, :] = hg

        return _chain_groups(a, b, carry, bool(d), store)

    cf = cb = jnp.zeros((1, HEAD_W), F32)
    for i in range(nblk):
        cf = scan_block(i, cf, 0, hf_ref)
        cb = scan_block(nctx - 1 - i if i < nctx else nblk + nctx - 1 - i, cb, 1, hb_ref)
    o_ref[...] = ((hf_ref[...] + hb_ref[...]) * _gelu_tanh(gb_ref[...].astype(F32))).astype(o_ref.dtype)


def _lru(ub, conv_w, conv_b, w_a, b_a, w_x, b_x, lam, *, batch, seq, ctx_len):
    vec = pl.BlockSpec((2, 1, HEAD_W), lambda b, h: (0, 0, h))
    mat = pl.BlockSpec((2, None, HEAD_W, HEAD_W), lambda b, h: (0, h, 0, 0))
    return pl.pallas_call(
        functools.partial(_lru_kernel, ctx_len=ctx_len),
        out_shape=jax.ShapeDtypeStruct((batch * seq, MIX_W), BF16),
        grid=(batch, N_HEADS),
        in_specs=[
            pl.BlockSpec((seq, HEAD_W), lambda b, h: (b, UB_LRU_X // HEAD_W + h)),
            pl.BlockSpec((seq, HEAD_W), lambda b, h: (b, UB_LRU_G // HEAD_W + h)),
            pl.BlockSpec((4, HEAD_W), lambda b, h: (0, h)),
            pl.BlockSpec((1, HEAD_W), lambda b, h: (0, h)),
            mat, vec, mat, vec, vec,
        ],
        out_specs=pl.BlockSpec((seq, HEAD_W), lambda b, h: (b, h)),
        scratch_shapes=[pltpu.VMEM((seq, HEAD_W), F32)] * 3,
        compiler_params=_params(("parallel", "parallel")),
        name="rglru",
    )(ub, ub, conv_w, conv_b, w_a, b_a, w_x, b_x, lam)


def _norm_rope(x, w, cos, sin, grp, scale):
    lane = lax.broadcasted_iota(jnp.int32, (1, 128), 1)
    first = (lane % (ATT_HEAD_DIM // 2)) < (ATT_HEAD_DIM // 4)
    ss = _dot_x3(x * x, grp)
    y = x * lax.rsqrt(ss * (1.0 / ATT_HEAD_DIM) + EPS) * w
    rot = jnp.where(first, pltpu.roll(y, 128 - ATT_HEAD_DIM // 4, 1), pltpu.roll(y, ATT_HEAD_DIM // 4, 1))
    return (y * cos + rot * sin) * scale


def _att_kernel(q_ref, k_ref, v_ref, cos_ref, sin_ref, qw_ref, kw_ref, grp_ref, o_ref, kt_ref, vb_ref, *, ctx_len):
    i = pl.program_id(1)
    grp = grp_ref[...]
    tq = q_ref.shape[0]

    @pl.when(i == 0)
    def _():
        for c in range(k_ref.shape[0] // tq):
            rows = slice(c * tq, (c + 1) * tq)
            kn = _norm_rope(k_ref[rows, :].astype(F32), kw_ref[...], cos_ref[rows, :], sin_ref[rows, :], grp, 1.0)
            kt_ref[:, rows] = kn.T.astype(kt_ref.dtype)
        vb_ref[...] = v_ref[...].astype(vb_ref.dtype)

    q_scale = (ATT_HEAD_DIM ** -0.5) * LOG2_E
    q_rows = pl.ds(pl.multiple_of(i * tq, tq), tq)
    cos, sin = cos_ref[q_rows, :], sin_ref[q_rows, :]
    qn = [_norm_rope(q_ref[:, s * 128:(s + 1) * 128].astype(F32), qw_ref[...], cos, sin, grp, q_scale).astype(BF16)
          for s in range(q_ref.shape[1] // 128)]

    def attend(n_keys):
        v = vb_ref[:n_keys, :]

        def scores(head):
            g = head // ATT_GROUP
            kt = kt_ref[g * ATT_HEAD_DIM:(g + 1) * ATT_HEAD_DIM, :n_keys]
            q = qn[head // 2][:, (head % 2) * ATT_HEAD_DIM:(head % 2 + 1) * ATT_HEAD_DIM]
            return jnp.dot(q, kt, preferred_element_type=F32)

        s_next = scores(0)
        for head in range(ATT_Q_HEADS):
            s = s_next
            if head + 1 < ATT_Q_HEADS:
                s_next = scores(head + 1)
            g = head // ATT_GROUP
            p = jnp.exp2(s - jnp.max(s, axis=-1, keepdims=True))
            den = jnp.sum(p, axis=-1, keepdims=True)
            pv = jnp.dot(p.astype(BF16), v, preferred_element_type=F32)
            o_ref[:, head * ATT_HEAD_DIM:(head + 1) * ATT_HEAD_DIM] = (
                pv[:, g * ATT_HEAD_DIM:(g + 1) * ATT_HEAD_DIM] / den).astype(o_ref.dtype)

    @pl.when(i == 0)
    def _():
        attend(ctx_len)

    @pl.when(i != 0)
    def _():
        attend(kt_ref.shape[1])


def _attention(ub, uf, cos, sin, qw, kw, grp, *, batch, seq, ctx_len):
    tiles = seq // ATT_TQ
    whole = lambda shape: pl.BlockSpec(shape, lambda b, i: (0, 0))
    return pl.pallas_call(
        functools.partial(_att_kernel, ctx_len=ctx_len),
        out_shape=jax.ShapeDtypeStruct((batch * seq, ATT_QW), BF16),
        grid=(batch, tiles),
        in_specs=[
            pl.BlockSpec((ATT_TQ, ATT_QW), lambda b, i: (b * tiles + i, UB_ATT_Q // ATT_QW)),
            pl.BlockSpec((seq, ATT_KW), lambda b, i: (b, UF_ATT_K // ATT_KW)),
            pl.BlockSpec((seq, ATT_KW), lambda b, i: (b, UF_ATT_V // ATT_KW)),
            whole((seq, 128)), whole((seq, 128)), whole((1, 128)), whole((1, 128)), whole((128, 128)),
        ],
        out_specs=pl.BlockSpec((ATT_TQ, ATT_QW), lambda b, i: (b * tiles + i, 0)),
        scratch_shapes=[pltpu.VMEM((ATT_KW, seq), BF16), pltpu.VMEM((seq, ATT_KW), BF16)],
        compiler_params=_params(("parallel", "arbitrary")),
        name="attention",
    )(ub, uf, uf, cos, sin, qw, kw, grp)


def _merge_kernel(oaf_ref, oab_ref, obf_ref, obb_ref, ga_ref, gb_ref, yc_ref, yd_ref, gate_ref, h_ref, modc_ref,
                  modl_ref, dnw_ref, hgw_ref, wb_ref, wo_ref, out_ref, *, tiles_per_batch):
    def gated(of_ref, ob_ref, g_ref, nw_ref):
        o = of_ref[...] + ob_ref[...]
        ys = [_rms_rows(o[:, h * HEAD_W:(h + 1) * HEAD_W], nw_ref[...]) for h in range(N_HEADS)]
        g = g_ref[...].astype(F32)
        return jnp.concatenate(ys, axis=1) * _silu(g)

    ys = (gated(oaf_ref, oab_ref, ga_ref, dnw_ref), gated(obf_ref, obb_ref, gb_ref, hgw_ref), yc_ref[...], yd_ref[...])
    acc = None
    for b in range(N_BRANCH):
        gate = gate_ref[:, b * D_MODEL:(b + 1) * D_MODEL].astype(F32)
        term = (1.0 + jnp.tanh(gate)) * _dot(ys[b], wb_ref[b])
        acc = term if acc is None else acc + term
    is_ctx = (pl.program_id(0) % tiles_per_batch) == 0
    out_ref[...] = h_ref[...] + _pick_mod(modc_ref, modl_ref, 2, is_ctx) * _dot(acc, wo_ref[...])


def _merge(oaf, oab, obf, obb, ub, yc, yd, h, modc, modl, dnw, hgw, wb, wo, *, seq, ctx_len):
    rows = h.shape[0]
    tm = ctx_len
    tiles_per_batch = seq // tm
    mix = lambda c: pl.BlockSpec((tm, MIX_W), lambda i: (i, c // MIX_W))
    return pl.pallas_call(
        functools.partial(_merge_kernel, tiles_per_batch=tiles_per_batch),
        out_shape=jax.ShapeDtypeStruct((rows, D_MODEL), F32),
        grid=(rows // tm,),
        in_specs=[
            mix(0), mix(0), mix(0), mix(0), mix(UB_DN_G), mix(UB_HG_G), mix(0), mix(0),
            pl.BlockSpec((tm, N_BRANCH * D_MODEL), lambda i: (i, 0)),
            pl.BlockSpec((tm, D_MODEL), lambda i: (i, 0)),
            pl.BlockSpec((6, D_MODEL), lambda i: (0, 0)),
            pl.BlockSpec((None, 6, D_MODEL), lambda i: (i // tiles_per_batch, 0, 0)),
            pl.BlockSpec((1, HEAD_W), lambda i: (0, 0)),
            pl.BlockSpec((1, HEAD_W), lambda i: (0, 0)),
            pl.BlockSpec((N_BRANCH, MIX_W, D_MODEL), lambda i: (0, 0, 0)),
            pl.BlockSpec((D_MODEL, D_MODEL), lambda i: (0, 0)),
        ],
        out_specs=pl.BlockSpec((tm, D_MODEL), lambda i: (i, 0)),
        compiler_params=_params(("parallel",)),
        name="merge",
    )(oaf, oab, obf, obb, ub, ub, yc, yd, ub, h, modc, modl, dnw, hgw, wb, wo)


def _mlp_kernel(h_ref, modc_ref, modl_ref, nw_ref, w1_ref, w2_ref, o_ref, z_ref, acc_ref,
                *, tm, tiles_per_batch, ctx_len):
    j = pl.program_id(1)
    row = (pl.program_id(0) % tiles_per_batch) * tm + lax.broadcasted_iota(jnp.int32, (tm, 1), 0)
    is_ctx = row < ctx_len

    @pl.when(j == 0)
    def _():
        acc_ref[...] = jnp.zeros_like(acc_ref)

    y = _rms_rows(h_ref[...], nw_ref[...])
    shift = _pick_mod(modc_ref, modl_ref, 3, is_ctx)
    scale = _pick_mod(modc_ref, modl_ref, 4, is_ctx)
    z = (y * (1.0 + scale) + shift).astype(BF16)
    a = jnp.maximum(jnp.dot(z, w1_ref[...], preferred_element_type=F32), 0.0)
    acc_ref[...] += _dot(a * a, w2_ref[...])

    @pl.when(j == pl.num_programs(1) - 1)
    def _():
        o_ref[...] = h_ref[...] + _pick_mod(modc_ref, modl_ref, 5, is_ctx) * acc_ref[...]


def _mlp(h, modc, modl, nw, w1, w2, *, seq, ctx_len):
    rows = h.shape[0]
    tiles_per_batch = 4
    tm = seq // tiles_per_batch
    kern = functools.partial(_mlp_kernel, tm=tm, tiles_per_batch=tiles_per_batch, ctx_len=ctx_len)
    return pl.pallas_call(
        kern,
        out_shape=jax.ShapeDtypeStruct((rows, D_MODEL), F32),
        grid=(rows // tm, D_FF // FF_BLOCK),
        in_specs=[
            pl.BlockSpec((tm, D_MODEL), lambda i, j: (i, 0)),
            pl.BlockSpec((6, D_MODEL), lambda i, j: (0, 0)),
            pl.BlockSpec((None, 6, D_MODEL), lambda i, j: (i // tiles_per_batch, 0, 0)),
            pl.BlockSpec((1, D_MODEL), lambda i, j: (0, 0)),
            pl.BlockSpec((D_MODEL, FF_BLOCK), lambda i, j: (0, j)),
            pl.BlockSpec((FF_BLOCK, D_MODEL), lambda i, j: (j, 0)),
        ],
        out_specs=pl.BlockSpec((tm, D_MODEL), lambda i, j: (i, 0)),
        scratch_shapes=[pltpu.VMEM((tm, D_MODEL), BF16), pltpu.VMEM((tm, D_MODEL), F32)],
        compiler_params=_params(("parallel", "arbitrary"), vmem=VMEM_LIMIT_V7X + 8 * 1024 * 1024),
        name="mlp",
    )(h, modc, modl, nw, w1, w2)


def _regroup_w_in(w_in):
    o = np.cumsum([0, 512, 512, 512, 512, 8, 8, 512, 512, 512, 512, 512, 512, 512, 512, 128, 128, 4096])
    cols = lambda a, b: w_in[..., o[a]:o[b]]
    wb = jnp.concatenate([0.5 * cols(16, 17), cols(6, 7), cols(9, 11), cols(0, 3), cols(11, 13), cols(3, 4),
                          cols(13, 14)], axis=-1)
    wf = jnp.concatenate([0.5 * cols(7, 9), cols(14, 16), cols(4, 6)], axis=-1)
    pad = lambda w, n: jnp.pad(w, ((0, 0), (0, 0), (0, n - w.shape[-1]))).astype(BF16)
    return pad(wb, N_UB), pad(wf, N_UF)


def _rope_tables(t_len, ctx_len):
    rows = t_len // GRID_W
    row_id = jnp.repeat(jnp.arange(rows), GRID_W).astype(F32)
    col_id = jnp.tile(jnp.arange(GRID_W), rows).astype(F32)
    axis_dim = ATT_HEAD_DIM // 2
    inv = ROPE_THETA ** (-jnp.arange(0, axis_dim, 2, dtype=F32) / axis_dim)
    ang = jnp.stack([row_id[:, None] * inv, col_id[:, None] * inv], axis=1)
    cos, sin = jnp.cos(ang), jnp.sin(ang)
    cos_h = jnp.concatenate([cos, cos], axis=-1).reshape(t_len, ATT_HEAD_DIM)
    sin_h = jnp.concatenate([-sin, sin], axis=-1).reshape(t_len, ATT_HEAD_DIM)
    cos_t = jnp.concatenate([jnp.ones((ctx_len, ATT_HEAD_DIM), F32), cos_h], axis=0)
    sin_t = jnp.concatenate([jnp.zeros((ctx_len, ATT_HEAD_DIM), F32), sin_h], axis=0)
    return jnp.tile(cos_t, (1, 2)), jnp.tile(sin_t, (1, 2))


def _hgrn2_lower_bounds(p):
    sm = jax.nn.softmax(p.astype(F32), axis=1)
    cs = jnp.cumsum(sm, axis=1)
    return cs - cs[:, :1]


def kernel(x, c, ctx, c_ctx, mod_w, mod_b, norm1_w, norm2_w, w_in, dn_conv_w, dn_a_log, dn_dt_bias, dn_norm_w,
           hg_lower_bounds, hg_norm_w, lru_conv_w, lru_conv_b, lru_w_a, lru_b_a, lru_w_x, lru_b_x, lru_lambda,
           att_q_norm_w, att_k_norm_w, w_branch, w_out, mlp_w1, mlp_w2):
    batch, t_len, _ = x.shape
    ctx_len = ctx.shape[1]
    depth = mod_w.shape[0]
    seq = ctx_len + t_len
    n_chunks = seq // CHUNK
    assert ctx_len % SCAN_ROWS == 0 and t_len % SCAN_ROWS == 0 and ctx_len == ATT_TQ
    dims = dict(batch=batch, seq=seq, ctx_len=ctx_len)

    w_ub, w_uf = _regroup_w_in(w_in)
    w_branch_b, w_out_b = (0.5 * w_branch).astype(BF16), w_out.astype(BF16)
    w1_b, w2_b = mlp_w1.astype(BF16), mlp_w2.astype(BF16)
    lru_wa_b, lru_wx_b = (0.5 * lru_w_a).astype(BF16), (0.5 * lru_w_x).astype(BF16)
    lb_all = _hgrn2_lower_bounds(hg_lower_bounds)
    cos_t, sin_t = _rope_tables(t_len, ctx_len)
    grp = jnp.asarray((np.arange(128)[:, None] // ATT_HEAD_DIM) == (np.arange(128)[None, :] // ATT_HEAD_DIM), F32)

    n_mod_rows = batch + 8
    cc = jnp.zeros((n_mod_rows, D_MODEL), F32).at[:batch].set(c).at[batch].set(c_ctx)
    mods = _modulations(cc, mod_w.astype(BF16), mod_b)

    h = jnp.concatenate([ctx, x], axis=1).reshape(batch * seq, D_MODEL)
    for l in range(depth):
        modl = mods[l, :batch].reshape(batch, 6, D_MODEL)
        modc = mods[l, batch].reshape(6, D_MODEL)
        nw1 = norm1_w[l].reshape(1, D_MODEL)
        ub, uf = _inproj(h, modc, modl, nw1, w_ub[l], w_uf[l], seq=seq, ctx_len=ctx_len)

        qkv = _dn_prep(ub, dn_conv_w[l], **dims)
        ba = uf[:, UF_DN_BA:UF_DN_BA + 4 * N_HEADS].reshape(batch * n_chunks, CHUNK, 2, 2, N_HEADS)
        ba_t = jnp.transpose(ba, (3, 0, 2, 4, 1)).reshape(2, batch * n_chunks, 2 * N_HEADS, CHUNK)
        oa = _gdn(qkv, ba_t, dn_a_log[l].reshape(2, N_HEADS, 1), dn_dt_bias[l].reshape(2, N_HEADS, 1), **dims)

        ob = _hgrn2(ub, uf, lb_all[:, l].reshape(2, 1, MIX_W), **dims)

        yc = _lru(ub, lru_conv_w[l], lru_conv_b[l].reshape(1, MIX_W), lru_wa_b[l],
                  (0.5 * lru_b_a[l]).reshape(2, 1, MIX_W), lru_wx_b[l], (0.5 * lru_b_x[l]).reshape(2, 1, MIX_W),
                  lru_lambda[l].reshape(2, 1, MIX_W), **dims)

        yd = _attention(ub, uf, cos_t, sin_t, jnp.tile(att_q_norm_w[l], 2).reshape(1, 128),
                        jnp.tile(att_k_norm_w[l], 2).reshape(1, 128), grp, **dims)

        h = _merge(oa[0], oa[1], ob[0], ob[1], ub, yc, yd, h, modc, modl, dn_norm_w[l].reshape(1, HEAD_W),
                   hg_norm_w[l].reshape(1, HEAD_W), w_branch_b[l], w_out_b[l], seq=seq, ctx_len=ctx_len)
        h = _mlp(h, modc, modl, norm2_w[l].reshape(1, D_MODEL), w1_b[l], w2_b[l], seq=seq, ctx_len=ctx_len)

    return h.reshape(batch, seq, D_MODEL)[:, ctx_len:]
```
